```python
import math
import jax, jax.numpy as jnp
from jax import lax
import numpy as np

D_MODEL = 1024
BATCH = 16
SEQ = 2048
DEPTH = 2

N_EVEN = (DEPTH + 1) // 2
N_ODD = DEPTH // 2
DN_ALPHA = (2.0 * DEPTH) ** 0.25
DN_BETA = (8.0 * DEPTH) ** -0.25
LN_EPS = 1e-5
NEG = -1e30
Q_BLOCK = 128

GM_WIDTH = D_MODEL // 2
GM_GROUPS = 4
GM_GROUP_DIM = GM_WIDTH // GM_GROUPS
GM_CHUNK = 128

MLA_HEADS = 8
MLA_NOPE = 64
MLA_ROPE = 32
MLA_V = 64
MLA_Q_RANK = D_MODEL // 4
MLA_KV_RANK = D_MODEL // 8
ROPE_BASE = 10000.0
AB_IN = 2 * GM_WIDTH + MLA_Q_RANK + MLA_KV_RANK + MLA_ROPE
AB_MIX = GM_WIDTH + MLA_HEADS * MLA_V
AB_SPLITS = (GM_WIDTH, 2 * GM_WIDTH, 2 * GM_WIDTH + MLA_Q_RANK,
             2 * GM_WIDTH + MLA_Q_RANK + MLA_KV_RANK)

NSA_HEADS = 16
NSA_GROUPS = 2
NSA_HPG = NSA_HEADS // NSA_GROUPS
NSA_DH = 64
NSA_CMP_LEN = 32
NSA_CMP_STRIDE = 16
NSA_CMP_HIDDEN = 256
NSA_SEL_LEN = 64
NSA_TOPK = 8
NSA_SEL_QBLOCK = 64
NSA_WINDOW = 512
NSA_BRANCHES = 3
NSA_KV = NSA_GROUPS * NSA_DH
C_MIX = NSA_HEADS * NSA_DH
C_IN = C_MIX + 6 * NSA_KV + NSA_BRANCHES * NSA_HEADS
NSA_SPLITS = tuple(C_MIX + i * NSA_KV for i in range(7))
NSA_FORCE = 1e4

MOE_GROUPS = 4
MOE_EPG = 8
MOE_EXPERTS = MOE_GROUPS * MOE_EPG
MOE_TOPK = 2
MOE_HIDDEN = D_MODEL // 4

kernel_name = 'hybrid_gmlp_mla_nsa_hmoe_deepnorm'


def layer_norm(x, g, b):
    xf = x.astype(jnp.float32)
    mu = xf.mean(-1, keepdims=True)
    var = jnp.square(xf - mu).mean(-1, keepdims=True)
    return ((xf - mu) * lax.rsqrt(var + LN_EPS) * g + b).astype(x.dtype)


def rms_norm(x, g):
    xf = x.astype(jnp.float32)
    return (xf * lax.rsqrt(jnp.mean(xf * xf, -1, keepdims=True) + LN_EPS) * g).astype(x.dtype)


def rope(x, pos):
    half = x.shape[-1] // 2
    freq = jnp.exp(-math.log(ROPE_BASE) * jnp.arange(half, dtype=jnp.float32) / half)
    ang = pos.astype(jnp.float32)[:, :, None, None] * freq
    cos, sin = jnp.cos(ang), jnp.sin(ang)
    x1 = x[..., :half].astype(jnp.float32)
    x2 = x[..., half:].astype(jnp.float32)
    return jnp.concatenate([x1 * cos - x2 * sin, x1 * sin + x2 * cos], -1).astype(x.dtype)


def gmlp_mix(u, v, ln_g, ln_b, w_s, b_s):
    B, S, _ = u.shape
    nc = S // GM_CHUNK
    vn = layer_norm(v, ln_g, ln_b).reshape(B, nc, GM_CHUNK, GM_GROUPS, GM_GROUP_DIM)
    causal = jnp.tril(jnp.ones((GM_CHUNK, GM_CHUNK), dtype=bool))
    ws = jnp.where(causal, w_s, 0.0).astype(vn.dtype)
    s = jnp.einsum('gts,bnsgc->bntgc', ws, vn) + jnp.transpose(b_s)[:, :, None]
    return u * s.reshape(B, S, GM_WIDTH)


def mla_mix(c_q, c_kv, k_r, pos, q_norm_g, kv_norm_g, w_uq, w_uk, w_uv):
    B, S, _ = c_q.shape
    q = (rms_norm(c_q, q_norm_g) @ w_uq).reshape(B, S, MLA_HEADS, MLA_NOPE + MLA_ROPE)
    q_nope = q[..., :MLA_NOPE]
    q_rope = rope(q[..., MLA_NOPE:], pos)
    ckv = rms_norm(c_kv, kv_norm_g)
    k_nope = (ckv @ w_uk).reshape(B, S, MLA_HEADS, MLA_NOPE)
    v = (ckv @ w_uv).reshape(B, S, MLA_HEADS, MLA_V)
    k_rope = rope(k_r[:, :, None, :], pos)[:, :, 0]
    scale = (MLA_NOPE + MLA_ROPE) ** -0.5
    outs = []
    for i in range(S // Q_BLOCK):
        q0, q1 = i * Q_BLOCK, (i + 1) * Q_BLOCK
        s = (jnp.einsum('bqhd,bkhd->bhqk', q_nope[:, q0:q1], k_nope[:, :q1])
             + jnp.einsum('bqhr,bkr->bhqk', q_rope[:, q0:q1], k_rope[:, :q1]))
        mask = (q0 + jnp.arange(Q_BLOCK))[:, None] >= jnp.arange(q1)[None, :]
        s = jnp.where(mask, s.astype(jnp.float32) * scale, NEG)
        p = jax.nn.softmax(s, axis=-1).astype(v.dtype)
        outs.append(jnp.einsum('bhqk,bkhd->bqhd', p, v[:, :q1]))
    return jnp.concatenate(outs, axis=1).reshape(B, S, MLA_HEADS * MLA_V)


def ab_mixer(x, pos, w_in, gm_ln_g, gm_ln_b, gm_ws, gm_bs, q_norm_g, kv_norm_g,
             w_uq, w_uk, w_uv, w_o):
    h = x @ w_in
    u, v, c_q, c_kv, k_r = jnp.split(h, AB_SPLITS, axis=-1)
    y_a = gmlp_mix(jax.nn.gelu(u), jax.nn.gelu(v), gm_ln_g, gm_ln_b, gm_ws, gm_bs)
    y_b = mla_mix(c_q, c_kv, k_r, pos, q_norm_g, kv_norm_g, w_uq, w_uk, w_uv)
    return jnp.concatenate([y_a, y_b], axis=-1) @ w_o


def compress_kv(k, pos_emb, w1, w2):
    B, S, G, D = k.shape
    r = NSA_CMP_LEN // NSA_CMP_STRIDE
    sub = k.reshape(B, S // NSA_CMP_STRIDE, NSA_CMP_STRIDE, G, D)
    nc = S // NSA_CMP_STRIDE - r + 1
    blocks = jnp.concatenate([sub[:, j:j + nc] for j in range(r)], axis=2) + pos_emb[:, None, :]
    flat = blocks.transpose(0, 1, 3, 2, 4).reshape(B, nc, G, NSA_CMP_LEN * D)
    return jax.nn.gelu(flat @ w1) @ w2


def nsa_compressed(q, k_cmp, v_cmp):
    S, nc = q.shape[1], k_cmp.shape[1]
    s = jnp.einsum('btgid,bngd->bgitn', q, k_cmp).astype(jnp.float32) * NSA_DH ** -0.5
    end = jnp.arange(nc) * NSA_CMP_STRIDE + NSA_CMP_LEN - 1
    mask = jnp.arange(S)[:, None] >= end[None, :]
    p = jnp.where(mask, jax.nn.softmax(jnp.where(mask, s, NEG), axis=-1), 0.0)
    o = jnp.einsum('bgitn,bngd->btgid', p.astype(v_cmp.dtype), v_cmp)
    return o, p


def nsa_select_blocks(p_cmp):
    S, nc = p_cmp.shape[3], p_cmp.shape[4]
    nsel = S // NSA_SEL_LEN
    c0 = jnp.arange(nc) * NSA_CMP_STRIDE
    s0 = jnp.arange(nsel) * NSA_SEL_LEN
    cover = ((c0[:, None] < s0[None, :] + NSA_SEL_LEN)
             & (c0[:, None] + NSA_CMP_LEN > s0[None, :])).astype(jnp.float32)
    imp = jnp.einsum('bgitn,nj->bgtj', p_cmp, cover)
    jj = jnp.arange(nsel)[None, :]
    tb = (jnp.arange(S) // NSA_SEL_LEN)[:, None]
    forced = (jj == 0) | (jj == tb) | (jj == tb - 1)
    score = jnp.where(forced, NSA_FORCE, jnp.where(jj <= tb, imp, -NSA_FORCE))
    _, idx = lax.top_k(score, min(NSA_TOPK, nsel))
    return idx


def nsa_selected(q, k_s, v_s, sel_idx):
    B, S, G, HPG, D = q.shape
    K = sel_idx.shape[-1]
    nsel = S // NSA_SEL_LEN
    kb = k_s.reshape(B, nsel, NSA_SEL_LEN, G, D).transpose(0, 3, 1, 2, 4)
    vb = v_s.reshape(B, nsel, NSA_SEL_LEN, G, D).transpose(0, 3, 1, 2, 4)
    nqb = S // NSA_SEL_QBLOCK
    q_b = q.reshape(B, nqb, NSA_SEL_QBLOCK, G, HPG, D).transpose(1, 0, 2, 3, 4, 5)
    idx_b = sel_idx.reshape(B, G, nqb, NSA_SEL_QBLOCK, K).transpose(2, 0, 1, 3, 4)
    bi = jnp.arange(B)[:, None, None, None]
    gi = jnp.arange(G)[None, :, None, None]

    def one_block(args):
        i, qi, idx = args
        tq = i * NSA_SEL_QBLOCK + jnp.arange(NSA_SEL_QBLOCK)
        kg = kb[bi, gi, idx]
        vg = vb[bi, gi, idx]
        kpos = idx[..., None] * NSA_SEL_LEN + jnp.arange(NSA_SEL_LEN)
        mask = kpos <= tq[None, None, :, None, None]
        s = jnp.einsum('bqgid,bgqkld->bgiqkl', qi, kg).astype(jnp.float32) * NSA_DH ** -0.5
        s = jnp.where(mask[:, :, None], s, NEG)
        p = jax.nn.softmax(s.reshape(B, G, HPG, NSA_SEL_QBLOCK, K * NSA_SEL_LEN), axis=-1)
        p = p.reshape(s.shape).astype(vg.dtype)
        return jnp.einsum('bgiqkl,bgqkld->bqgid', p, vg)

    o = lax.map(one_block, (jnp.arange(nqb), q_b, idx_b))
    return o.transpose(1, 0, 2, 3, 4, 5).reshape(B, S, G, HPG, D)


def nsa_window(q, k_w, v_w):
    B, S, G, HPG, D = q.shape
    nb = S // Q_BLOCK
    span = NSA_WINDOW + Q_BLOCK
    kp = jnp.pad(k_w, ((0, 0), (NSA_WINDOW, 0), (0, 0), (0, 0)))
    vp = jnp.pad(v_w, ((0, 0), (NSA_WINDOW, 0), (0, 0), (0, 0)))
    q_b = q.reshape(B, nb, Q_BLOCK, G, HPG, D).transpose(1, 0, 2, 3, 4, 5)

    def one_block(args):
        i, qi = args
        start = i * Q_BLOCK
        kb = lax.dynamic_slice_in_dim(kp, start, span, axis=1)
        vb = lax.dynamic_slice_in_dim(vp, start, span, axis=1)
        tq = (start + jnp.arange(Q_BLOCK))[:, None]
        kpos = (start - NSA_WINDOW + jnp.arange(span))[None, :]
        mask = (kpos <= tq) & (kpos > tq - NSA_WINDOW) & (kpos >= 0)
        s = jnp.einsum('bqgid,bkgd->bgiqk', qi, kb).astype(jnp.float32) * NSA_DH ** -0.5
        p = jax.nn.softmax(jnp.where(mask, s, NEG), axis=-1).astype(vb.dtype)
        return jnp.einsum('bgiqk,bkgd->bqgid', p, vb)

    o = lax.map(one_block, (jnp.arange(nb), q_b))
    return o.transpose(1, 0, 2, 3, 4, 5).reshape(B, S, G, HPG, D)


def nsa_mixer(x, w_in, cmp_pos, w_ck1, w_ck2, w_cv1, w_cv2, gate_b, w_o):
    B, S, _ = x.shape
    h = x @ w_in
    q, kc, vc, ks, vs, kw, vw, g = jnp.split(h, NSA_SPLITS, axis=-1)
    q = q.reshape(B, S, NSA_GROUPS, NSA_HPG, NSA_DH)

    def kv(t):
        return t.reshape(B, S, NSA_GROUPS, NSA_DH)

    k_cmp = compress_kv(kv(kc), cmp_pos[0], w_ck1, w_ck2)
    v_cmp = compress_kv(kv(vc), cmp_pos[1], w_cv1, w_cv2)
    o_cmp, p_cmp = nsa_compressed(q, k_cmp, v_cmp)
    sel_idx = nsa_select_blocks(p_cmp)
    o_slc = nsa_selected(q, kv(ks), kv(vs), sel_idx)
    o_win = nsa_window(q, kv(kw), kv(vw))
    gates = jax.nn.sigmoid(g + gate_b).reshape(B, S, NSA_BRANCHES, NSA_GROUPS, NSA_HPG, 1)
    o = gates[:, :, 0] * o_cmp + gates[:, :, 1] * o_slc + gates[:, :, 2] * o_win
    return o.reshape(B, S, C_MIX) @ w_o


def hier_moe(x, w_rg, b_rg, w_re, b_re, w_gate, w_up, w_down):
    B, S, D = x.shape
    T = B * S
    xt = x.reshape(T, D)
    g_logits = (xt @ w_rg + b_rg).astype(jnp.float32)
    g_prob = jax.nn.softmax(g_logits, axis=-1)
    g_sel = jnp.argmax(g_logits, axis=-1)
    g_w = jnp.take_along_axis(g_prob, g_sel[:, None], axis=-1)
    e_logits = (xt @ w_re + b_re).astype(jnp.float32).reshape(T, MOE_GROUPS, MOE_EPG)
    e_sel = jnp.take_along_axis(e_logits, g_sel[:, None, None], axis=1)[:, 0]
    top_p, top_i = lax.top_k(jax.nn.softmax(e_sel, axis=-1), MOE_TOPK)
    top_p = top_p / top_p.sum(-1, keepdims=True)
    w_grp = (jax.nn.one_hot(top_i, MOE_EPG, dtype=jnp.float32) * top_p[..., None]).sum(1) * g_w
    y = jnp.zeros_like(xt)
    for gi in range(MOE_GROUPS):
        e0 = gi * MOE_EPG
        wg = jnp.where(g_sel[:, None] == gi, w_grp, 0.0).astype(x.dtype)
        hg = (jax.nn.silu(jnp.einsum('td,edf->tef', xt, w_gate[e0:e0 + MOE_EPG]))
              * jnp.einsum('td,edf->tef', xt, w_up[e0:e0 + MOE_EPG]))
        y = y + jnp.einsum('tef,efd->td', hg * wg[..., None], w_down[e0:e0 + MOE_EPG])
    return y.reshape(B, S, D)


def setup_inputs(seed: int = 0) -> dict:
    key = jax.random.key(seed)
    keys = iter(jax.random.split(key, 48))

    def nrm(shape, scale):
        return jax.random.normal(next(keys), shape, jnp.float32) * scale

    def gain(shape):
        return 1.0 + nrm(shape, 0.1)

    x = nrm((BATCH, SEQ, D_MODEL), 1.0)
    offset = jax.random.randint(next(keys), (BATCH, 1), 0, 4096, dtype=jnp.int32)
    positions = (offset + jnp.arange(SEQ, dtype=jnp.int32)[None, :]).astype(jnp.int32)
    return {
        'x': x,
        'positions': positions,
        'ab_w_in': nrm((N_EVEN, D_MODEL, AB_IN), D_MODEL ** -0.5),
        'ab_gm_ln_g': gain((N_EVEN, GM_WIDTH)),
        'ab_gm_ln_b': nrm((N_EVEN, GM_WIDTH), 0.02),
        'ab_gm_ws': nrm((N_EVEN, GM_GROUPS, GM_CHUNK, GM_CHUNK), GM_CHUNK ** -0.5),
        'ab_gm_bs': gain((N_EVEN, GM_GROUPS, GM_CHUNK)),
        'ab_mla_q_norm': gain((N_EVEN, MLA_Q_RANK)),
        'ab_mla_kv_norm': gain((N_EVEN, MLA_KV_RANK)),
        'ab_mla_w_uq': nrm((N_EVEN, MLA_Q_RANK, MLA_HEADS * (MLA_NOPE + MLA_ROPE)), MLA_Q_RANK ** -0.5),
        'ab_mla_w_uk': nrm((N_EVEN, MLA_KV_RANK, MLA_HEADS * MLA_NOPE), MLA_KV_RANK ** -0.5),
        'ab_mla_w_uv': nrm((N_EVEN, MLA_KV_RANK, MLA_HEADS * MLA_V), MLA_KV_RANK ** -0.5),
        'ab_w_o': nrm((N_EVEN, AB_MIX, D_MODEL), AB_MIX ** -0.5 * DN_BETA),
        'c_w_in': nrm((N_ODD, D_MODEL, C_IN), D_MODEL ** -0.5),
        'c_cmp_pos': nrm((N_ODD, 2, NSA_CMP_LEN, NSA_DH), 0.1),
        'c_w_ck1': nrm((N_ODD, NSA_CMP_LEN * NSA_DH, NSA_CMP_HIDDEN), (NSA_CMP_LEN * NSA_DH) ** -0.5),
        'c_w_ck2': nrm((N_ODD, NSA_CMP_HIDDEN, NSA_DH), NSA_CMP_HIDDEN ** -0.5),
        'c_w_cv1': nrm((N_ODD, NSA_CMP_LEN * NSA_DH, NSA_CMP_HIDDEN), (NSA_CMP_LEN * NSA_DH) ** -0.5),
        'c_w_cv2': nrm((N_ODD, NSA_CMP_HIDDEN, NSA_DH), NSA_CMP_HIDDEN ** -0.5),
        'c_gate_b': nrm((N_ODD, NSA_BRANCHES * NSA_HEADS), 0.1),
        'c_w_o': nrm((N_ODD, C_MIX, D_MODEL), C_MIX ** -0.5 * DN_BETA),
        'moe_w_rg': nrm((DEPTH, D_MODEL, MOE_GROUPS), D_MODEL ** -0.5),
        'moe_b_rg': nrm((DEPTH, MOE_GROUPS), 0.01),
        'moe_w_re': nrm((DEPTH, D_MODEL, MOE_EXPERTS), D_MODEL ** -0.5),
        'moe_b_re': nrm((DEPTH, MOE_EXPERTS), 0.01),
        'moe_w_gate': nrm((DEPTH, MOE_EXPERTS, D_MODEL, MOE_HIDDEN), D_MODEL ** -0.5),
        'moe_w_up': nrm((DEPTH, MOE_EXPERTS, D_MODEL, MOE_HIDDEN), D_MODEL ** -0.5),
        'moe_w_down': nrm((DEPTH, MOE_EXPERTS, MOE_HIDDEN, D_MODEL), MOE_HIDDEN ** -0.5 * DN_BETA),
        'ln1_g': gain((DEPTH, D_MODEL)),
        'ln1_b': nrm((DEPTH, D_MODEL), 0.02),
        'ln2_g': gain((DEPTH, D_MODEL)),
        'ln2_b': nrm((DEPTH, D_MODEL), 0.02),
    }


def reference(x, positions, ab_w_in, ab_gm_ln_g, ab_gm_ln_b, ab_gm_ws, ab_gm_bs,
              ab_mla_q_norm, ab_mla_kv_norm, ab_mla_w_uq, ab_mla_w_uk, ab_mla_w_uv, ab_w_o,
              c_w_in, c_cmp_pos, c_w_ck1, c_w_ck2, c_w_cv1, c_w_cv2, c_gate_b, c_w_o,
              moe_w_rg, moe_b_rg, moe_w_re, moe_b_re, moe_w_gate, moe_w_up, moe_w_down,
              ln1_g, ln1_b, ln2_g, ln2_b):
    for layer in range(DEPTH):
        j = layer // 2
        if layer % 2 == 0:
            mix = ab_mixer(x, positions, ab_w_in[j], ab_gm_ln_g[j], ab_gm_ln_b[j], ab_gm_ws[j],
                           ab_gm_bs[j], ab_mla_q_norm[j], ab_mla_kv_norm[j], ab_mla_w_uq[j],
                           ab_mla_w_uk[j], ab_mla_w_uv[j], ab_w_o[j])
        else:
            mix = nsa_mixer(x, c_w_in[j], c_cmp_pos[j], c_w_ck1[j], c_w_ck2[j], c_w_cv1[j],
                            c_w_cv2[j], c_gate_b[j], c_w_o[j])
        x = layer_norm(DN_ALPHA * x + mix, ln1_g[layer], ln1_b[layer])
        ffn = hier_moe(x, moe_w_rg[layer], moe_b_rg[layer], moe_w_re[layer], moe_b_re[layer],
                       moe_w_gate[layer], moe_w_up[layer], moe_w_down[layer])
        x = layer_norm(DN_ALPHA * x + ffn, ln2_g[layer], ln2_b[layer])
    return x
```

```python
import functools
import math

import jax
import jax.numpy as jnp
from jax import lax
from jax.experimental import pallas as pl
from jax.experimental.pallas import tpu as pltpu

F32 = jnp.float32
BF16 = jnp.bfloat16

D_MODEL = 1024
BATCH = 16
SEQ = 2048
TOKENS = BATCH * SEQ
DEPTH = 2
DN_ALPHA = (2.0 * DEPTH) ** 0.25
LN_EPS = 1e-5
NEG = -1e30
LANES = 128
HALF = LANES // 2

GM_WIDTH = 512
GM_GROUPS = 4
GM_CHUNK = 128

MLA_HEADS = 8
MLA_NOPE = 64
MLA_ROPE = 32
MLA_V = 64
MLA_Q_RANK = 256
MLA_KV_RANK = 128
ROPE_BASE = 10000.0
MLA_SCALE = (MLA_NOPE + MLA_ROPE) ** -0.5

NSA_HEADS = 16
NSA_GROUPS = 2
NSA_HPG = 8
NSA_DH = 64
NSA_CMP_LEN = 32
NSA_CMP_STRIDE = 16
NSA_CMP_HIDDEN = 256
NSA_SEL_LEN = 64
NSA_TOPK = 8
NSA_WINDOW = 512
NSA_NSEL = SEQ // NSA_SEL_LEN
NSA_NCMP = SEQ // NSA_CMP_STRIDE
NSA_FORCE = 1e4
NSA_SCALE = NSA_DH ** -0.5
C_MIX = NSA_HEADS * NSA_DH

MOE_GROUPS = 4
MOE_EPG = 8
MOE_EXPERTS = 32
MOE_HIDDEN = 256

TM_PROJ = 512
TQ_MLA = 256
TQ_NSA = 128
TK_SEL = 256
WIN_SPAN = NSA_WINDOW + TQ_NSA
TM_MOE = 512


def _dot(a, b):
    return jnp.dot(a, b, preferred_element_type=F32)


def _dot_nt(a, b):
    return lax.dot_general(a, b, (((1,), (1,)), ((), ())), preferred_element_type=F32)


def _gelu(x):
    return 0.5 * x * (1.0 + jnp.tanh(math.sqrt(2.0 / math.pi) * (x + 0.044715 * (x * x * x))))


def _layer_norm(x, g, b):
    mu = jnp.mean(x, axis=-1, keepdims=True)
    xc = x - mu
    var = jnp.mean(xc * xc, axis=-1, keepdims=True)
    return xc * lax.rsqrt(var + LN_EPS) * g + b


def _rms_norm(x, g):
    return x * lax.rsqrt(jnp.mean(x * x, axis=-1, keepdims=True) + LN_EPS) * g


def _ab_in_kernel(x_ref, pos_ref, win_ref, lng_ref, lnb_ref, qg_ref, kvg_ref, wq_ref, wkv_ref,
                  fc_ref, sg_ref, gu_ref, vn_ref, q_ref, k_ref, v_ref):
    h = _dot(x_ref[...].astype(BF16), win_ref[...])
    gu_ref[...] = _gelu(h[:, 0:512]).astype(BF16)
    vn_ref[...] = _layer_norm(_gelu(h[:, 512:1024]), lng_ref[...], lnb_ref[...]).astype(BF16)

    ang = pos_ref[...].astype(F32) * fc_ref[...]
    cc = jnp.cos(ang)
    ss = jnp.sin(ang) * sg_ref[...]

    cqn = _rms_norm(h[:, 1024:1280], qg_ref[...]).astype(BF16)
    qq = _dot(cqn, wq_ref[...])
    for hd in range(MLA_HEADS):
        lo, hi = hd * LANES, (hd + 1) * LANES
        q_ref[:, lo:hi] = ((qq[:, lo:hi] * cc + qq[:, 1024 + lo:1024 + hi] * ss) * MLA_SCALE).astype(BF16)

    ckvn = _rms_norm(h[:, 1280:1408], kvg_ref[...]).astype(BF16)
    kv = _dot(ckvn, wkv_ref[...])
    k_rope = h[:, 1408:1536] * cc + h[:, 1536:1664] * ss
    for hd in range(MLA_HEADS):
        lo, hi = hd * LANES, (hd + 1) * LANES
        k_ref[:, lo:hi] = (kv[:, lo:hi] + k_rope).astype(BF16)
    v_ref[...] = kv[:, 1024:1536].astype(BF16)


def _ab_in(x2, pos2, win, lng, lnb, qg, kvg, wq, wkv, fc, sg):
    tm = TM_PROJ
    row = lambda n: pl.BlockSpec((tm, n), lambda i: (i, 0))
    full = lambda a: pl.BlockSpec(a.shape, lambda i: (0,) * a.ndim)
    return pl.pallas_call(
        _ab_in_kernel,
        grid=(TOKENS // tm,),
        in_specs=[row(D_MODEL), row(1), full(win), full(lng), full(lnb), full(qg), full(kvg),
                  full(wq), full(wkv), full(fc), full(sg)],
        out_specs=[row(512), row(512), row(1024), row(1024), row(512)],
        out_shape=[jax.ShapeDtypeStruct((TOKENS, 512), BF16), jax.ShapeDtypeStruct((TOKENS, 512), BF16),
                   jax.ShapeDtypeStruct((TOKENS, 1024), BF16), jax.ShapeDtypeStruct((TOKENS, 1024), BF16),
                   jax.ShapeDtypeStruct((TOKENS, 512), BF16)],
        compiler_params=pltpu.CompilerParams(dimension_semantics=("arbitrary",)),
        name="ab_in",
    )(x2, pos2, win, lng, lnb, qg, kvg, wq, wkv, fc, sg)


def _mla_attn_kernel(q_ref, k_ref, v_ref, o_ref):
    tq = TQ_MLA
    qi = pl.program_id(2)
    lane = lax.broadcasted_iota(jnp.int32, (tq, LANES), 1)
    row = lax.broadcasted_iota(jnp.int32, (tq, tq), 0)
    col = lax.broadcasted_iota(jnp.int32, (tq, tq), 1)
    outs = []
    for hh in range(2):
        q = q_ref[0, :, hh * LANES:(hh + 1) * LANES]

        def step(j, carry, masked, hh=hh, q=q):
            m, l, acc = carry
            r0 = pl.multiple_of(j * tq, tq)
            k = k_ref[0, pl.ds(r0, tq), hh * LANES:(hh + 1) * LANES]
            v = v_ref[0, pl.ds(r0, tq), :]
            s = _dot_nt(q, k)
            if masked:
                s = jnp.where(row >= col, s, NEG)
            m_new = jnp.maximum(m, jnp.max(s, axis=-1, keepdims=True))
            a = jnp.exp(m - m_new)
            p = jnp.exp(s - m_new)
            l = a * l + jnp.sum(p, axis=-1, keepdims=True)
            acc = a * acc + _dot(p.astype(BF16), v)
            return m_new, l, acc

        carry = (jnp.full((tq, 1), NEG, F32), jnp.zeros((tq, 1), F32), jnp.zeros((tq, LANES), F32))
        carry = lax.fori_loop(0, qi, functools.partial(step, masked=False), carry)
        m, l, acc = step(qi, carry, True)
        outs.append(acc / l)
    o_ref[0] = jnp.where(lane < HALF, outs[0], outs[1]).astype(BF16)


def _mla_attn(q3, k3, v3):
    tq = TQ_MLA
    return pl.pallas_call(
        _mla_attn_kernel,
        grid=(BATCH, MLA_HEADS // 2, SEQ // tq),
        in_specs=[pl.BlockSpec((1, tq, 2 * LANES), lambda b, p, i: (b, i, p)),
                  pl.BlockSpec((1, SEQ, 2 * LANES), lambda b, p, i: (b, 0, p)),
                  pl.BlockSpec((1, SEQ, LANES), lambda b, p, i: (b, 0, p))],
        out_specs=pl.BlockSpec((1, tq, LANES), lambda b, p, i: (b, i, p)),
        out_shape=jax.ShapeDtypeStruct((BATCH, SEQ, MLA_HEADS * MLA_V), BF16),
        compiler_params=pltpu.CompilerParams(dimension_semantics=("arbitrary",) * 3),
        name="mla_attn",
    )(q3, k3, v3)


def _router(x1, wr, br):
    tm = x1.shape[0]
    logits = jnp.dot(x1, wr, preferred_element_type=F32, precision=lax.Precision.HIGHEST) + br
    lane = lax.broadcasted_iota(jnp.int32, (tm, LANES), 1).astype(F32)
    big = 1e6
    is_g = (lane >= MOE_EXPERTS) & (lane < MOE_EXPERTS + MOE_GROUPS)
    gl = jnp.where(is_g, logits, NEG)
    gmax = jnp.max(gl, axis=-1, keepdims=True)
    g_sel = jnp.min(jnp.where(is_g & (gl == gmax), lane, big), axis=-1, keepdims=True) - MOE_EXPERTS
    g_w = 1.0 / jnp.sum(jnp.where(is_g, jnp.exp(gl - gmax), 0.0), axis=-1, keepdims=True)
    in_grp = (lane >= g_sel * MOE_EPG) & (lane < (g_sel + 1) * MOE_EPG)
    el = jnp.where(in_grp, logits, NEG)
    emax = jnp.max(el, axis=-1, keepdims=True)
    ee = jnp.where(in_grp, jnp.exp(el - emax), 0.0)
    pe = ee / jnp.sum(ee, axis=-1, keepdims=True)
    p1 = jnp.max(pe, axis=-1, keepdims=True)
    i1 = jnp.min(jnp.where(in_grp & (pe == p1), lane, big), axis=-1, keepdims=True)
    rest = in_grp & (lane != i1)
    pr = jnp.where(rest, pe, -1.0)
    p2 = jnp.max(pr, axis=-1, keepdims=True)
    i2 = jnp.min(jnp.where(rest & (pr == p2), lane, big), axis=-1, keepdims=True)
    tot = p1 + p2
    return jnp.where(lane == i1, p1 / tot * g_w, jnp.where(lane == i2, p2 / tot * g_w, 0.0))


def _mix_out_kernel(*refs, gmlp):
    if gmlp:
        (x_ref, gu_ref, vn_ref, ws_ref, bs_ref, yb_ref, wo_ref, g_ref, b_ref, wr_ref, br_ref,
         x1_ref, x1b_ref, wf_ref, ya_ref) = refs
        tm = x_ref.shape[0]
        r = lax.broadcasted_iota(jnp.int32, (GM_CHUNK, GM_CHUNK), 0)
        c = lax.broadcasted_iota(jnp.int32, (GM_CHUNK, GM_CHUNK), 1)
        for g in range(GM_GROUPS):
            ws = jnp.where(r >= c, ws_ref[g], 0.0).astype(BF16)
            bias = bs_ref[:, g:g + 1]
            for ch in range(tm // GM_CHUNK):
                rows = slice(ch * GM_CHUNK, (ch + 1) * GM_CHUNK)
                cols = slice(g * LANES, (g + 1) * LANES)
                s = _dot(ws, vn_ref[rows, cols]) + bias
                ya_ref[rows, cols] = (gu_ref[rows, cols].astype(F32) * s).astype(BF16)
        mix = _dot(ya_ref[...], wo_ref[0:GM_WIDTH, :]) + _dot(yb_ref[...], wo_ref[GM_WIDTH:, :])
    else:
        x_ref, y_ref, wo_ref, g_ref, b_ref, wr_ref, br_ref, x1_ref, x1b_ref, wf_ref = refs
        mix = _dot(y_ref[...], wo_ref[...])
    x1 = _layer_norm(DN_ALPHA * x_ref[...] + mix, g_ref[...], b_ref[...])
    x1_ref[...] = x1
    x1b_ref[...] = x1.astype(BF16)
    wf_ref[...] = _router(x1, wr_ref[...], br_ref[...])


def _mix_out(x2, ys, wo, g, b, wr, br, gm=None):
    tm = TM_PROJ
    row = lambda n: pl.BlockSpec((tm, n), lambda i: (i, 0))
    full = lambda a: pl.BlockSpec(a.shape, lambda i: (0,) * a.ndim)
    if gm is not None:
        gu, vn, ws, bs = gm
        args = (x2, gu, vn, ws, bs, ys, wo, g, b, wr, br)
        in_specs = [row(D_MODEL), row(512), row(512), full(ws), full(bs), row(512), full(wo),
                    full(g), full(b), full(wr), full(br)]
        scratch = [pltpu.VMEM((tm, GM_WIDTH), BF16)]
    else:
        args = (x2, ys, wo, g, b, wr, br)
        in_specs = [row(D_MODEL), row(C_MIX), full(wo), full(g), full(b), full(wr), full(br)]
        scratch = []
    return pl.pallas_call(
        functools.partial(_mix_out_kernel, gmlp=gm is not None),
        grid=(TOKENS // tm,),
        in_specs=in_specs,
        out_specs=[row(D_MODEL), row(D_MODEL), row(LANES)],
        out_shape=[jax.ShapeDtypeStruct((TOKENS, D_MODEL), F32), jax.ShapeDtypeStruct((TOKENS, D_MODEL), BF16),
                   jax.ShapeDtypeStruct((TOKENS, LANES), F32)],
        scratch_shapes=scratch,
        compiler_params=pltpu.CompilerParams(dimension_semantics=("arbitrary",)),
        name="mix_out_gmlp" if gm is not None else "mix_out",
    )(*args)


def _moe_kernel(xb_ref, x1_ref, wf_ref, wg_ref, wu_ref, wd_ref, g_ref, b_ref, o_ref, acc_ref):
    e = pl.program_id(1)

    @pl.when(e == 0)
    def _():
        acc_ref[...] = jnp.zeros_like(acc_ref)

    x = xb_ref[...]
    hg = _dot(x, wg_ref[0])
    hu = _dot(x, wu_ref[0])
    lane = lax.broadcasted_iota(jnp.int32, wf_ref.shape, 1)
    w_tok = jnp.sum(jnp.where(lane == e, wf_ref[...], 0.0), axis=-1, keepdims=True)
    hid = hg * jax.nn.sigmoid(hg) * hu * w_tok
    acc_ref[...] += _dot(hid.astype(BF16), wd_ref[0])

    @pl.when(e == pl.num_programs(1) - 1)
    def _():
        o_ref[...] = _layer_norm(DN_ALPHA * x1_ref[...] + acc_ref[...], g_ref[...], b_ref[...])


def _moe(x1b, x1, wf, wg, wu, wd, g, b):
    tm = TM_MOE
    row = lambda n: pl.BlockSpec((tm, n), lambda i, e: (i, 0))
    full = lambda a: pl.BlockSpec(a.shape, lambda i, e: (0,) * a.ndim)
    return pl.pallas_call(
        _moe_kernel,
        grid=(TOKENS // tm, MOE_EXPERTS),
        in_specs=[row(D_MODEL), row(D_MODEL), row(LANES),
                  pl.BlockSpec((1, D_MODEL, MOE_HIDDEN), lambda i, e: (e, 0, 0)),
                  pl.BlockSpec((1, D_MODEL, MOE_HIDDEN), lambda i, e: (e, 0, 0)),
                  pl.BlockSpec((1, MOE_HIDDEN, D_MODEL), lambda i, e: (e, 0, 0)),
                  full(g), full(b)],
        out_specs=row(D_MODEL),
        out_shape=jax.ShapeDtypeStruct((TOKENS, D_MODEL), F32),
        scratch_shapes=[pltpu.VMEM((tm, D_MODEL), F32)],
        compiler_params=pltpu.CompilerParams(dimension_semantics=("arbitrary", "arbitrary")),
        name="moe_experts",
    )(x1b, x1, wf, wg, wu, wd, g, b)


def _dup_halves(t):
    lane = lax.broadcasted_iota(jnp.int32, t.shape, 1)
    r = pltpu.roll(t, HALF, 1)
    return jnp.where(lane < HALF, t, r), jnp.where(lane < HALF, r, t)


def _nsa_in_kernel(x_ref, w_ref, gb_ref, q_ref, kc_ref, vc_ref, ks_ref, vs_ref, kw_ref, vw_ref, gate_ref):
    h = _dot(x_ref[...].astype(BF16), w_ref[...])
    q_ref[...] = (h[:, 0:C_MIX] * NSA_SCALE).astype(BF16)
    kc_ref[...] = h[:, 1024:1152].astype(BF16)
    vc_ref[...] = h[:, 1152:1280].astype(BF16)
    for idx, ref in enumerate((ks_ref, vs_ref, kw_ref, vw_ref)):
        d0, d1 = _dup_halves(h[:, 1280 + idx * LANES:1280 + (idx + 1) * LANES])
        ref[:, 0:LANES] = d0.astype(BF16)
        ref[:, LANES:2 * LANES] = d1.astype(BF16)
    for g in range(NSA_GROUPS):
        gate_ref[g] = jax.nn.sigmoid(h[:, 1792 + g * LANES:1792 + (g + 1) * LANES] + gb_ref[g])


def _nsa_in(x2, w, gb):
    tm = TM_PROJ
    row = lambda n: pl.BlockSpec((tm, n), lambda i: (i, 0))
    full = lambda a: pl.BlockSpec(a.shape, lambda i: (0,) * a.ndim)
    sd = jax.ShapeDtypeStruct
    return pl.pallas_call(
        _nsa_in_kernel,
        grid=(TOKENS // tm,),
        in_specs=[row(D_MODEL), full(w), full(gb)],
        out_specs=[row(C_MIX), row(LANES), row(LANES), row(2 * LANES), row(2 * LANES), row(2 * LANES),
                   row(2 * LANES), pl.BlockSpec((NSA_GROUPS, tm, LANES), lambda i: (0, i, 0))],
        out_shape=[sd((TOKENS, C_MIX), BF16), sd((TOKENS, LANES), BF16), sd((TOKENS, LANES), BF16),
                   sd((TOKENS, 2 * LANES), BF16), sd((TOKENS, 2 * LANES), BF16), sd((TOKENS, 2 * LANES), BF16),
                   sd((TOKENS, 2 * LANES), BF16), sd((NSA_GROUPS, TOKENS, LANES), F32)],
        compiler_params=pltpu.CompilerParams(dimension_semantics=("arbitrary",)),
        name="nsa_in",
    )(x2, w, gb)


def _compress_kernel(kc_ref, vc_ref, pk_ref, pv_ref, wk1_ref, wv1_ref, wk2_ref, wv2_ref, ko_ref, vo_ref):
    for a_ref, p_ref, w1_ref, w2_ref, o_ref in ((kc_ref, pk_ref, wk1_ref, wk2_ref, ko_ref),
                                                (vc_ref, pv_ref, wv1_ref, wv2_ref, vo_ref)):
        a = a_ref[0].astype(F32)
        a0 = (a + p_ref[0]).astype(BF16)
        a1 = (a + p_ref[1]).astype(BF16)
        outs = []
        for g in range(NSA_GROUPS):
            first = _dot(a0, w1_ref[g])
            second = _dot(a1, w1_ref[NSA_GROUPS + g])
            hid = first + pltpu.roll(second, NSA_NCMP - 1, 0)
            outs.append(_dot(_gelu(hid).astype(BF16), w2_ref[...]))
        o_ref[0] = jnp.concatenate(outs, axis=1).astype(BF16)


def _compress(kc_r, vc_r, pk, pv, wk1, wv1, wk2, wv2):
    blk = pl.BlockSpec((1, NSA_NCMP, NSA_CMP_STRIDE * LANES), lambda b: (b, 0, 0))
    full = lambda a: pl.BlockSpec(a.shape, lambda b: (0,) * a.ndim)
    out = pl.BlockSpec((1, NSA_NCMP, 2 * LANES), lambda b: (b, 0, 0))
    sd = jax.ShapeDtypeStruct((BATCH, NSA_NCMP, 2 * LANES), BF16)
    return pl.pallas_call(
        _compress_kernel,
        grid=(BATCH,),
        in_specs=[blk, blk, full(pk), full(pv), full(wk1), full(wv1), full(wk2), full(wv2)],
        out_specs=[out, out],
        out_shape=[sd, sd],
        compiler_params=pltpu.CompilerParams(dimension_semantics=("arbitrary",)),
        name="nsa_compress",
    )(kc_r, vc_r, pk, pv, wk1, wv1, wk2, wv2)


def _nsa_attn_kernel(q_ref, kc_ref, vc_ref, ks_ref, vs_ref, kw_ref, vw_ref, gate_ref, cover_ref, exp_ref,
                     o_ref, mem_ref):
    tq, hpg = TQ_NSA, NSA_HPG
    rows = hpg * tq
    qi = pl.program_id(2)
    q0 = qi * tq
    lane = lax.broadcasted_iota(jnp.int32, (tq, LANES), 1)
    lo = lane < HALF
    t_tok = q0 + lax.broadcasted_iota(jnp.int32, (tq, 1), 0)

    parts = []
    for p in range(hpg // 2):
        qp = q_ref[0, :, p * LANES:(p + 1) * LANES]
        zero = jnp.zeros_like(qp)
        parts.append(jnp.where(lo, qp, zero))
        parts.append(jnp.where(lo, zero, qp))
    qs = jnp.concatenate(parts, axis=0)

    def softmax_masked(s, mask):
        s3 = jnp.where(mask[None], s.reshape(hpg, tq, s.shape[-1]), NEG)
        m = jnp.max(s3, axis=-1, keepdims=True)
        e = jnp.exp(s3 - m)
        return e / jnp.sum(e, axis=-1, keepdims=True)

    s_c = _dot_nt(qs, kc_ref[0])
    vis = t_tok >= lane * NSA_CMP_STRIDE + (NSA_CMP_LEN - 1)
    p_c = jnp.where(vis[None], softmax_masked(s_c, vis), 0.0)
    o_c = _dot(p_c.reshape(rows, LANES).astype(BF16), vc_ref[0])

    imp = jnp.dot(jnp.sum(p_c, axis=0), cover_ref[...], preferred_element_type=F32,
                  precision=lax.Precision.HIGHEST)
    tb = t_tok // NSA_SEL_LEN
    forced = (lane == 0) | (lane == tb) | (lane == tb - 1)
    score = jnp.where(forced, NSA_FORCE, jnp.where(lane <= tb, imp, -NSA_FORCE))
    rank = jnp.zeros((tq, LANES), jnp.int32)
    for i in range(NSA_NSEL):
        si = score[:, i:i + 1]
        beats = (si > score) | ((si == score) & (lane > i))
        rank = rank + beats.astype(jnp.int32)
    sel = ((rank < NSA_TOPK) & (lane < NSA_NSEL)).astype(BF16)
    n_sel_tiles = SEQ // TK_SEL
    for kt in range(n_sel_tiles):
        mem_ref[kt] = _dot(sel, exp_ref[:, kt * TK_SEL:(kt + 1) * TK_SEL])

    kcol = lax.broadcasted_iota(jnp.int32, (1, TK_SEL), 1)

    def sel_step(kt, carry):
        m, l, acc = carry
        r0 = pl.multiple_of(kt * TK_SEL, TK_SEL)
        s = _dot_nt(qs, ks_ref[0, pl.ds(r0, TK_SEL), :])
        mask = (mem_ref[kt] > 0.5) & (r0 + kcol <= t_tok)
        s3 = jnp.where(mask[None], s.reshape(hpg, tq, TK_SEL), NEG)
        m_new = jnp.maximum(m, jnp.max(s3, axis=-1, keepdims=True))
        a = jnp.exp(m - m_new)
        p = jnp.exp(s3 - m_new)
        l = a * l + jnp.sum(p, axis=-1, keepdims=True)
        pv = _dot(p.reshape(rows, TK_SEL).astype(BF16), vs_ref[0, pl.ds(r0, TK_SEL), :])
        acc = a * acc + pv.reshape(hpg, tq, LANES)
        return m_new, l, acc

    carry = (jnp.full((hpg, tq, 1), NEG, F32), jnp.zeros((hpg, tq, 1), F32),
             jnp.zeros((hpg, tq, LANES), F32))
    n_tiles = (q0 + tq - 1) // TK_SEL + 1
    m, l, acc = lax.fori_loop(0, n_tiles, sel_step, carry)
    o_s = (acc / l).reshape(rows, LANES)

    start = pl.multiple_of(jnp.maximum(q0 - NSA_WINDOW, 0), tq)
    kpos = start + lax.broadcasted_iota(jnp.int32, (1, WIN_SPAN), 1)
    s_w = _dot_nt(qs, kw_ref[0, pl.ds(start, WIN_SPAN), :])
    in_win = (kpos <= t_tok) & (kpos > t_tok - NSA_WINDOW)
    p_w = softmax_masked(s_w, in_win)
    o_w = _dot(p_w.reshape(rows, WIN_SPAN).astype(BF16), vw_ref[0, pl.ds(start, WIN_SPAN), :])

    gt = gate_ref[0, 0]

    def gate_col(br):
        return jnp.concatenate([gt[:, br * hpg + i:br * hpg + i + 1] for i in range(hpg)], axis=0)

    o = gate_col(0) * o_c + gate_col(1) * o_s + gate_col(2) * o_w
    outs = []
    for p in range(hpg // 2):
        even = o[(2 * p) * tq:(2 * p + 1) * tq]
        odd = o[(2 * p + 1) * tq:(2 * p + 2) * tq]
        outs.append(jnp.where(lo, even, odd))
    o_ref[0] = jnp.concatenate(outs, axis=1).astype(BF16)


def _nsa_attn(q3, kc2, vc2, ks3, vs3, kw3, vw3, gates4, cover, expand):
    tq = TQ_NSA
    half_w = NSA_HPG * NSA_DH
    kv = pl.BlockSpec((1, SEQ, LANES), lambda b, g, i: (b, 0, g))
    cmp_spec = pl.BlockSpec((1, NSA_NCMP, LANES), lambda b, g, i: (b, 0, g))
    full = lambda a: pl.BlockSpec(a.shape, lambda b, g, i: (0,) * a.ndim)
    return pl.pallas_call(
        _nsa_attn_kernel,
        grid=(BATCH, NSA_GROUPS, SEQ // tq),
        in_specs=[pl.BlockSpec((1, tq, half_w), lambda b, g, i: (b, i, g)),
                  cmp_spec, cmp_spec, kv, kv, kv, kv,
                  pl.BlockSpec((1, 1, tq, LANES), lambda b, g, i: (g, b, i, 0)),
                  full(cover), full(expand)],
        out_specs=pl.BlockSpec((1, tq, half_w), lambda b, g, i: (b, i, g)),
        out_shape=jax.ShapeDtypeStruct((BATCH, SEQ, C_MIX), BF16),
        scratch_shapes=[pltpu.VMEM((SEQ // TK_SEL, tq, TK_SEL), F32)],
        compiler_params=pltpu.CompilerParams(dimension_semantics=("arbitrary",) * 3),
        name="nsa_attn",
    )(q3, kc2, vc2, ks3, vs3, kw3, vw3, gates4, cover, expand)


def _rope_tables():
    half = MLA_ROPE // 2
    freq = jnp.exp(-math.log(ROPE_BASE) * jnp.arange(half, dtype=F32) / half)
    zeros64 = jnp.zeros((MLA_NOPE,), F32)
    zeros32 = jnp.zeros((LANES - MLA_NOPE - MLA_ROPE,), F32)
    fc = jnp.concatenate([zeros64, freq, freq, zeros32])[None, :]
    sg = jnp.concatenate([zeros64, -jnp.ones((half,), F32), jnp.ones((half,), F32), zeros32])[None, :]
    return fc, sg


def _swap_halves(w):
    half = w.shape[-1] // 2
    return jnp.concatenate([w[..., half:], w[..., :half]], axis=-1)


def _pad_last(w, n):
    return jnp.pad(w, [(0, 0)] * (w.ndim - 1) + [(0, n - w.shape[-1])])


def _ab_weights(w_in, w_uq, w_uk, w_uv):
    w_kr = w_in[:, 1408:1440]
    place = lambda w: jnp.pad(w, ((0, 0), (MLA_NOPE, LANES - MLA_NOPE - MLA_ROPE)))
    win = jnp.concatenate([w_in[:, :1408], place(w_kr), place(_swap_halves(w_kr))], axis=1).astype(BF16)
    uq = w_uq.reshape(MLA_Q_RANK, MLA_HEADS, MLA_NOPE + MLA_ROPE)
    nope, rp = uq[..., :MLA_NOPE], uq[..., MLA_NOPE:]
    q_pad = _pad_last(jnp.concatenate([nope, rp], -1), LANES).reshape(MLA_Q_RANK, MLA_HEADS * LANES)
    q_sw = _pad_last(jnp.concatenate([jnp.zeros_like(nope), _swap_halves(rp)], -1), LANES)
    wq = jnp.concatenate([q_pad, q_sw.reshape(MLA_Q_RANK, MLA_HEADS * LANES)], axis=1).astype(BF16)
    k_pad = _pad_last(w_uk.reshape(MLA_KV_RANK, MLA_HEADS, MLA_NOPE), LANES).reshape(MLA_KV_RANK, -1)
    wkv = jnp.concatenate([k_pad, w_uv], axis=1).astype(BF16)
    return win, wq, wkv


def _router_weights(w_rg, b_rg, w_re, b_re):
    wr = _pad_last(jnp.concatenate([w_re, w_rg], axis=1), LANES)
    br = _pad_last(jnp.concatenate([b_re, b_rg])[None, :], LANES)
    return wr, br


def _nsa_in_weights(w_in, gate_b):
    g_cols = w_in[:, C_MIX + 768:].reshape(D_MODEL, 3, NSA_GROUPS, NSA_HPG)
    g_blocks = [_pad_last(g_cols[:, :, g, :].reshape(D_MODEL, 3 * NSA_HPG), LANES) for g in range(NSA_GROUPS)]
    w = jnp.concatenate([w_in[:, :C_MIX + 768]] + g_blocks, axis=1).astype(BF16)
    gb = gate_b.reshape(3, NSA_GROUPS, NSA_HPG)
    gb = jnp.stack([_pad_last(gb[:, g, :].reshape(1, 3 * NSA_HPG), LANES) for g in range(NSA_GROUPS)])
    return w, gb


def _compress_weights(pos, w1, w2):
    w1r = w1.reshape(2, NSA_CMP_STRIDE, NSA_DH, NSA_CMP_HIDDEN)
    zero = jnp.zeros_like(w1r)
    per_g = []
    for g in range(NSA_GROUPS):
        parts = [w1r if gg == g else zero for gg in range(NSA_GROUPS)]
        per_g.append(jnp.stack(parts, axis=2).reshape(2, NSA_CMP_STRIDE * LANES, NSA_CMP_HIDDEN))
    w1x = jnp.stack(per_g, axis=1).reshape(2 * NSA_GROUPS, NSA_CMP_STRIDE * LANES, NSA_CMP_HIDDEN)
    posr = pos.reshape(2, NSA_CMP_STRIDE, 1, NSA_DH)
    posx = jnp.broadcast_to(posr, (2, NSA_CMP_STRIDE, NSA_GROUPS, NSA_DH)).reshape(2, 1, NSA_CMP_STRIDE * LANES)
    w2x = jnp.concatenate([w2, w2], axis=1)
    return posx, w1x.astype(BF16), w2x.astype(BF16)


def _selection_tables():
    n = jnp.arange(LANES)[:, None]
    j = jnp.arange(LANES)[None, :]
    c0 = n * NSA_CMP_STRIDE
    s0 = j * NSA_SEL_LEN
    cover = ((c0 < s0 + NSA_SEL_LEN) & (c0 + NSA_CMP_LEN > s0) & (n < NSA_NCMP - 1) & (j < NSA_NSEL))
    kk = jnp.arange(SEQ)[None, :]
    expand = (kk // NSA_SEL_LEN == n)
    return cover.astype(F32), expand.astype(BF16)


def kernel(x, positions, ab_w_in, ab_gm_ln_g, ab_gm_ln_b, ab_gm_ws, ab_gm_bs, ab_mla_q_norm,
           ab_mla_kv_norm, ab_mla_w_uq, ab_mla_w_uk, ab_mla_w_uv, ab_w_o, c_w_in, c_cmp_pos, c_w_ck1,
           c_w_ck2, c_w_cv1, c_w_cv2, c_gate_b, c_w_o, moe_w_rg, moe_b_rg, moe_w_re, moe_b_re,
           moe_w_gate, moe_w_up, moe_w_down, ln1_g, ln1_b, ln2_g, ln2_b):
    x2 = x.reshape(TOKENS, D_MODEL)
    pos2 = positions.reshape(TOKENS, 1)
    vec = lambda a: a[None, :]

    def moe_layer(layer, x1b, x1, wf):
        return _moe(x1b, x1, wf, moe_w_gate[layer].astype(BF16), moe_w_up[layer].astype(BF16),
                    moe_w_down[layer].astype(BF16), vec(ln2_g[layer]), vec(ln2_b[layer]))

    win, wq, wkv = _ab_weights(ab_w_in[0], ab_mla_w_uq[0], ab_mla_w_uk[0], ab_mla_w_uv[0])
    fc, sg = _rope_tables()
    gu, vn, q, k, v = _ab_in(x2, pos2, win, vec(ab_gm_ln_g[0]), vec(ab_gm_ln_b[0]), vec(ab_mla_q_norm[0]),
                             vec(ab_mla_kv_norm[0]), wq, wkv, fc, sg)
    yb = _mla_attn(q.reshape(BATCH, SEQ, -1), k.reshape(BATCH, SEQ, -1), v.reshape(BATCH, SEQ, -1))
    wr, br = _router_weights(moe_w_rg[0], moe_b_rg[0], moe_w_re[0], moe_b_re[0])
    x1, x1b, wf = _mix_out(x2, yb.reshape(TOKENS, -1), ab_w_o[0].astype(BF16), vec(ln1_g[0]), vec(ln1_b[0]),
                           wr, br, gm=(gu, vn, ab_gm_ws[0], jnp.transpose(ab_gm_bs[0])))
    x2 = moe_layer(0, x1b, x1, wf)

    w_nsa, gb = _nsa_in_weights(c_w_in[0], c_gate_b[0])
    q, kc, vc, ks, vs, kw, vw, gates = _nsa_in(x2, w_nsa, gb)
    pk, wk1, wk2 = _compress_weights(c_cmp_pos[0, 0], c_w_ck1[0], c_w_ck2[0])
    pv, wv1, wv2 = _compress_weights(c_cmp_pos[0, 1], c_w_cv1[0], c_w_cv2[0])
    blocks = lambda a: a.reshape(BATCH, NSA_NCMP, NSA_CMP_STRIDE * LANES)
    kc2, vc2 = _compress(blocks(kc), blocks(vc), pk, pv, wk1, wv1, wk2, wv2)
    cover, expand = _selection_tables()
    b3 = lambda a: a.reshape(BATCH, SEQ, -1)
    o = _nsa_attn(b3(q), kc2, vc2, b3(ks), b3(vs), b3(kw), b3(vw),
                  gates.reshape(NSA_GROUPS, BATCH, SEQ, LANES), cover, expand)
    wr, br = _router_weights(moe_w_rg[1], moe_b_rg[1], moe_w_re[1], moe_b_re[1])
    x1, x1b, wf = _mix_out(x2, o.reshape(TOKENS, -1), c_w_o[0].astype(BF16), vec(ln1_g[1]), vec(ln1_b[1]), wr, br)
    x2 = moe_layer(1, x1b, x1, wf)
    return x2.reshape(BATCH, SEQ, D_MODEL)
```

```python
import functools
import math

import jax
import jax.numpy as jnp
from jax import lax
from jax.experimental import pallas as pl
from jax.experimental.pallas import tpu as pltpu

F32 = jnp.float32
BF16 = jnp.bfloat16

D_MODEL = 1024
BATCH = 16
SEQ = 2048
TOKENS = BATCH * SEQ
DEPTH = 2
DN_ALPHA = (2.0 * DEPTH) ** 0.25
LN_EPS = 1e-5
NEG = -1e30
LANES = 128
HALF = LANES // 2

GM_WIDTH = 512
GM_GROUPS = 4
GM_CHUNK = 128

MLA_HEADS = 8
MLA_NOPE = 64
MLA_ROPE = 32
MLA_V = 64
MLA_Q_RANK = 256
MLA_KV_RANK = 128
ROPE_BASE = 10000.0
MLA_SCALE = (MLA_NOPE + MLA_ROPE) ** -0.5

NSA_HEADS = 16
NSA_GROUPS = 2
NSA_HPG = 8
NSA_DH = 64
NSA_CMP_LEN = 32
NSA_CMP_STRIDE = 16
NSA_CMP_HIDDEN = 256
NSA_SEL_LEN = 64
NSA_TOPK = 8
NSA_WINDOW = 512
NSA_NSEL = SEQ // NSA_SEL_LEN
NSA_NCMP = SEQ // NSA_CMP_STRIDE
NSA_FORCE = 1e4
NSA_SCALE = NSA_DH ** -0.5
C_MIX = NSA_HEADS * NSA_DH

MOE_GROUPS = 4
MOE_EPG = 8
MOE_EXPERTS = 32
MOE_HIDDEN = 256

TM_PROJ = 512
TQ_MLA = 256
TQ_NSA = 128
TK_SEL = 256
WIN_SPAN = NSA_WINDOW + TQ_NSA
TM_MOE = 512

GSEL_LANE = MOE_EXPERTS
GRAN = 16
TBL_W = 3 * MOE_GROUPS
ROWS_LOCAL = 640
ROWS_SORTED = TM_MOE * (TOKENS // TM_MOE + MOE_GROUPS
                        + -(-(TOKENS // TM_PROJ) * MOE_GROUPS * (GRAN - 1) // TM_MOE))


def _dot(a, b):
    return jnp.dot(a, b, preferred_element_type=F32)


def _dot_nt(a, b):
    return lax.dot_general(a, b, (((1,), (1,)), ((), ())), preferred_element_type=F32)


def _gelu(x):
    return 0.5 * x * (1.0 + jnp.tanh(math.sqrt(2.0 / math.pi) * (x + 0.044715 * (x * x * x))))


def _layer_norm(x, g, b):
    mu = jnp.mean(x, axis=-1, keepdims=True)
    xc = x - mu
    var = jnp.mean(xc * xc, axis=-1, keepdims=True)
    return xc * lax.rsqrt(var + LN_EPS) * g + b


def _rms_norm(x, g):
    return x * lax.rsqrt(jnp.mean(x * x, axis=-1, keepdims=True) + LN_EPS) * g


def _ab_in_kernel(x_ref, pos_ref, win_ref, lng_ref, lnb_ref, qg_ref, kvg_ref, wq_ref, wkv_ref,
                  fc_ref, sg_ref, gu_ref, vn_ref, q_ref, k_ref, v_ref):
    h = _dot(x_ref[...].astype(BF16), win_ref[...])
    gu_ref[...] = _gelu(h[:, 0:512]).astype(BF16)
    vn_ref[...] = _layer_norm(_gelu(h[:, 512:1024]), lng_ref[...], lnb_ref[...]).astype(BF16)

    ang = pos_ref[...].astype(F32) * fc_ref[...]
    cc = jnp.cos(ang)
    ss = jnp.sin(ang) * sg_ref[...]

    cqn = _rms_norm(h[:, 1024:1280], qg_ref[...]).astype(BF16)
    qq = _dot(cqn, wq_ref[...])
    for hd in range(MLA_HEADS):
        lo, hi = hd * LANES, (hd + 1) * LANES
        q_ref[:, lo:hi] = ((qq[:, lo:hi] * cc + qq[:, 1024 + lo:1024 + hi] * ss) * MLA_SCALE).astype(BF16)

    ckvn = _rms_norm(h[:, 1280:1408], kvg_ref[...]).astype(BF16)
    kv = _dot(ckvn, wkv_ref[...])
    k_rope = h[:, 1408:1536] * cc + h[:, 1536:1664] * ss
    for hd in range(MLA_HEADS):
        lo, hi = hd * LANES, (hd + 1) * LANES
        k_ref[:, lo:hi] = (kv[:, lo:hi] + k_rope).astype(BF16)
    v_ref[...] = kv[:, 1024:1536].astype(BF16)


def _ab_in(x2, pos2, win, lng, lnb, qg, kvg, wq, wkv, fc, sg):
    tm = TM_PROJ
    row = lambda n: pl.BlockSpec((tm, n), lambda i: (i, 0))
    full = lambda a: pl.BlockSpec(a.shape, lambda i: (0,) * a.ndim)
    return pl.pallas_call(
        _ab_in_kernel,
        grid=(TOKENS // tm,),
        in_specs=[row(D_MODEL), row(1), full(win), full(lng), full(lnb), full(qg), full(kvg),
                  full(wq), full(wkv), full(fc), full(sg)],
        out_specs=[row(512), row(512), row(1024), row(1024), row(512)],
        out_shape=[jax.ShapeDtypeStruct((TOKENS, 512), BF16), jax.ShapeDtypeStruct((TOKENS, 512), BF16),
                   jax.ShapeDtypeStruct((TOKENS, 1024), BF16), jax.ShapeDtypeStruct((TOKENS, 1024), BF16),
                   jax.ShapeDtypeStruct((TOKENS, 512), BF16)],
        compiler_params=pltpu.CompilerParams(dimension_semantics=("arbitrary",)),
        name="ab_in",
    )(x2, pos2, win, lng, lnb, qg, kvg, wq, wkv, fc, sg)


def _mla_attn_kernel(q_ref, k_ref, v_ref, o_ref, m_ref, l_ref, acc_ref):
    tq = TQ_MLA
    qi = pl.program_id(1)
    lane = lax.broadcasted_iota(jnp.int32, (tq, LANES), 1)
    row = lax.broadcasted_iota(jnp.int32, (tq, tq), 0)
    col = lax.broadcasted_iota(jnp.int32, (tq, tq), 1)
    m_ref[...] = jnp.full(m_ref.shape, NEG, F32)
    l_ref[...] = jnp.zeros(l_ref.shape, F32)
    acc_ref[...] = jnp.zeros(acc_ref.shape, F32)

    lo = lane < HALF

    def tile(j, masked):
        r0 = pl.multiple_of(j * tq, tq)
        scores = []
        for h in range(MLA_HEADS):
            s = _dot_nt(q_ref[0, :, h * LANES:(h + 1) * LANES], k_ref[0, pl.ds(r0, tq), h * LANES:(h + 1) * LANES])
            scores.append(jnp.where(row >= col, s, NEG) if masked else s)
        probs, alphas = [], []
        for h in range(MLA_HEADS):
            m_old = m_ref[h]
            m_new = jnp.maximum(m_old, jnp.max(scores[h], axis=-1, keepdims=True))
            a = jnp.exp(m_old - m_new)
            p = jnp.exp(scores[h] - jnp.concatenate([m_new, m_new], axis=1))
            l_ref[h] = a * l_ref[h] + jnp.sum(p, axis=-1, keepdims=True)
            m_ref[h] = m_new
            probs.append(p.astype(BF16))
            alphas.append(a)
        for pr in range(MLA_HEADS // 2):
            v = v_ref[0, pl.ds(r0, tq), pr * LANES:(pr + 1) * LANES]
            pv = _dot(jnp.concatenate([probs[2 * pr], probs[2 * pr + 1]], axis=0), v)
            a = jnp.where(lo, alphas[2 * pr], alphas[2 * pr + 1])
            acc_ref[pr] = a * acc_ref[pr] + jnp.where(lo, pv[:tq], pv[tq:])

    def body(j, c):
        tile(j, False)
        return c

    lax.fori_loop(0, qi, body, 0)
    tile(qi, True)
    for pr in range(MLA_HEADS // 2):
        l = jnp.where(lo, l_ref[2 * pr], l_ref[2 * pr + 1])
        o_ref[0, :, pr * LANES:(pr + 1) * LANES] = (acc_ref[pr] / l).astype(BF16)


def _mla_attn(q3, k3, v3):
    tq = TQ_MLA
    return pl.pallas_call(
        _mla_attn_kernel,
        grid=(BATCH, SEQ // tq),
        in_specs=[pl.BlockSpec((1, tq, MLA_HEADS * LANES), lambda b, i: (b, i, 0)),
                  pl.BlockSpec((1, SEQ, MLA_HEADS * LANES), lambda b, i: (b, 0, 0)),
                  pl.BlockSpec((1, SEQ, MLA_HEADS * MLA_V), lambda b, i: (b, 0, 0))],
        out_specs=pl.BlockSpec((1, tq, MLA_HEADS * MLA_V), lambda b, i: (b, i, 0)),
        out_shape=jax.ShapeDtypeStruct((BATCH, SEQ, MLA_HEADS * MLA_V), BF16),
        scratch_shapes=[pltpu.VMEM((MLA_HEADS, tq, LANES), F32), pltpu.VMEM((MLA_HEADS, tq, LANES), F32),
                        pltpu.VMEM((MLA_HEADS // 2, tq, LANES), F32)],
        compiler_params=pltpu.CompilerParams(dimension_semantics=("arbitrary",) * 2),
        name="mla_attn",
    )(q3, k3, v3)


def _router(x1, wr, br):
    tm = x1.shape[0]
    logits = jnp.dot(x1, wr, preferred_element_type=F32, precision=lax.Precision.HIGHEST) + br
    lane = lax.broadcasted_iota(jnp.int32, (tm, LANES), 1).astype(F32)
    big = 1e6
    is_g = (lane >= MOE_EXPERTS) & (lane < MOE_EXPERTS + MOE_GROUPS)
    gl = jnp.where(is_g, logits, NEG)
    gmax = jnp.max(gl, axis=-1, keepdims=True)
    g_sel = jnp.min(jnp.where(is_g & (gl == gmax), lane, big), axis=-1, keepdims=True) - MOE_EXPERTS
    g_w = 1.0 / jnp.sum(jnp.where(is_g, jnp.exp(gl - gmax), 0.0), axis=-1, keepdims=True)
    in_grp = (lane >= g_sel * MOE_EPG) & (lane < (g_sel + 1) * MOE_EPG)
    el = jnp.where(in_grp, logits, NEG)
    emax = jnp.max(el, axis=-1, keepdims=True)
    ee = jnp.where(in_grp, jnp.exp(el - emax), 0.0)
    pe = ee / jnp.sum(ee, axis=-1, keepdims=True)
    p1 = jnp.max(pe, axis=-1, keepdims=True)
    i1 = jnp.min(jnp.where(in_grp & (pe == p1), lane, big), axis=-1, keepdims=True)
    rest = in_grp & (lane != i1)
    pr = jnp.where(rest, pe, -1.0)
    p2 = jnp.max(pr, axis=-1, keepdims=True)
    i2 = jnp.min(jnp.where(rest & (pr == p2), lane, big), axis=-1, keepdims=True)
    tot = p1 + p2
    wf = jnp.where(lane == i1, p1 / tot * g_w, jnp.where(lane == i2, p2 / tot * g_w, 0.0))
    wf = jnp.where(lane == GSEL_LANE, g_sel, wf)
    cnt = jnp.sum(jnp.where(lane == g_sel, 1.0, 0.0), axis=0, keepdims=True)
    return wf, cnt


def _mix_out_kernel(*refs, gmlp):
    if gmlp:
        (x_ref, gu_ref, vn_ref, ws_ref, bs_ref, yb_ref, wo_ref, g_ref, b_ref, wr_ref, br_ref,
         x1_ref, x1b_ref, wf_ref, cnt_ref, ya_ref) = refs
        tm = x_ref.shape[0]
        r = lax.broadcasted_iota(jnp.int32, (GM_CHUNK, GM_CHUNK), 0)
        c = lax.broadcasted_iota(jnp.int32, (GM_CHUNK, GM_CHUNK), 1)
        for g in range(GM_GROUPS):
            ws = jnp.where(r >= c, ws_ref[g], 0.0).astype(BF16)
            bias = bs_ref[:, g:g + 1]
            for ch in range(tm // GM_CHUNK):
                rows = slice(ch * GM_CHUNK, (ch + 1) * GM_CHUNK)
                cols = slice(g * LANES, (g + 1) * LANES)
                s = _dot(ws, vn_ref[rows, cols]) + bias
                ya_ref[rows, cols] = (gu_ref[rows, cols].astype(F32) * s).astype(BF16)
        mix = _dot(ya_ref[...], wo_ref[0:GM_WIDTH, :]) + _dot(yb_ref[...], wo_ref[GM_WIDTH:, :])
    else:
        x_ref, y_ref, wo_ref, g_ref, b_ref, wr_ref, br_ref, x1_ref, x1b_ref, wf_ref, cnt_ref = refs
        mix = _dot(y_ref[...], wo_ref[...])
    x1 = _layer_norm(DN_ALPHA * x_ref[...] + mix, g_ref[...], b_ref[...])
    x1_ref[...] = x1
    x1b_ref[...] = x1.astype(BF16)
    wf, cnt = _router(x1, wr_ref[...], br_ref[...])
    wf_ref[...] = wf
    cnt_ref[0] = jnp.broadcast_to(cnt, cnt_ref.shape[1:])


def _mix_out(x2, ys, wo, g, b, wr, br, gm=None):
    tm = TM_PROJ
    row = lambda n: pl.BlockSpec((tm, n), lambda i: (i, 0))
    full = lambda a: pl.BlockSpec(a.shape, lambda i: (0,) * a.ndim)
    if gm is not None:
        gu, vn, ws, bs = gm
        args = (x2, gu, vn, ws, bs, ys, wo, g, b, wr, br)
        in_specs = [row(D_MODEL), row(512), row(512), full(ws), full(bs), row(512), full(wo),
                    full(g), full(b), full(wr), full(br)]
        scratch = [pltpu.VMEM((tm, GM_WIDTH), BF16)]
    else:
        args = (x2, ys, wo, g, b, wr, br)
        in_specs = [row(D_MODEL), row(C_MIX), full(wo), full(g), full(b), full(wr), full(br)]
        scratch = []
    return pl.pallas_call(
        functools.partial(_mix_out_kernel, gmlp=gm is not None),
        grid=(TOKENS // tm,),
        in_specs=in_specs,
        out_specs=[row(D_MODEL), row(D_MODEL), row(LANES), pl.BlockSpec((1, 8, LANES), lambda i: (i, 0, 0))],
        out_shape=[jax.ShapeDtypeStruct((TOKENS, D_MODEL), F32), jax.ShapeDtypeStruct((TOKENS, D_MODEL), BF16),
                   jax.ShapeDtypeStruct((TOKENS, LANES), F32),
                   jax.ShapeDtypeStruct((TOKENS // tm, 8, LANES), F32)],
        scratch_shapes=scratch,
        compiler_params=pltpu.CompilerParams(dimension_semantics=("arbitrary",)),
        name="mix_out_gmlp" if gm is not None else "mix_out",
    )(*args)


def _group_dest(wf, ltri_ref, ustr_ref):
    tm = wf.shape[0]
    lane = lax.broadcasted_iota(jnp.int32, (tm, LANES), 1).astype(F32)
    onehot = jnp.where(lane == wf[:, GSEL_LANE:GSEL_LANE + 1], 1.0, 0.0)
    before = _dot(ltri_ref[...], onehot.astype(BF16))
    cnt = jnp.sum(onehot, axis=0, keepdims=True)
    gran = jnp.floor((cnt + (GRAN - 1)) * (1.0 / GRAN))
    start = _dot(jnp.broadcast_to(gran, (8, LANES)).astype(BF16), ustr_ref[...])[0:1]
    return jnp.sum(onehot * (GRAN * start + before), axis=-1, keepdims=True)


def _granule_copies(tbl_ref, tile, vmem_bufs, hbm_refs, sems, to_hbm, act):
    for g in range(MOE_GROUPS):
        n = tbl_ref[tile * TBL_W + g]
        loc = tbl_ref[tile * TBL_W + MOE_GROUPS + g]
        glb = tbl_ref[tile * TBL_W + 2 * MOE_GROUPS + g]

        def body(k, c, loc=loc, glb=glb):
            lo = pl.multiple_of((loc + k) * GRAN, GRAN)
            hi = pl.multiple_of((glb + k) * GRAN, GRAN)
            for idx, (vb, hb) in enumerate(zip(vmem_bufs, hbm_refs)):
                v_sl, h_sl = vb.at[pl.ds(lo, GRAN)], hb.at[pl.ds(hi, GRAN)]
                src, dst = (v_sl, h_sl) if to_hbm else (h_sl, v_sl)
                act(pltpu.make_async_copy(src, dst, sems.at[idx]))
            return c

        lax.fori_loop(0, n, body, 0)


def _dispatch_kernel(tbl_ref, xb_ref, wf_ref, ltri_ref, ustr_ref, xs_in, ws_in, xs_out, ws_out, xbuf, wbuf, sems):
    del xs_in, ws_in
    tile = pl.program_id(0)
    tm = xb_ref.shape[0]
    wf = wf_ref[...]
    dest = _group_dest(wf, ltri_ref, ustr_ref)
    dest_row = jnp.broadcast_to(dest, (tm, LANES)).T[0:1]
    r = lax.broadcasted_iota(jnp.int32, (ROWS_LOCAL, tm), 0).astype(F32)
    perm = jnp.where(r == dest_row, 1.0, 0.0).astype(BF16)
    xbuf[...] = _dot(perm, xb_ref[...]).astype(BF16)
    hi = wf.astype(BF16)
    r1 = wf - hi.astype(F32)
    mid = r1.astype(BF16)
    lo = (r1 - mid.astype(F32)).astype(BF16)
    wbuf[...] = _dot(perm, hi) + _dot(perm, mid) + _dot(perm, lo)
    bufs, outs = (xbuf, wbuf), (xs_out, ws_out)
    _granule_copies(tbl_ref, tile, bufs, outs, sems, True, lambda c: c.start())
    _granule_copies(tbl_ref, tile, bufs, outs, sems, True, lambda c: c.wait())


def _dispatch(tbl, x1b, wf, ltri, ustr):
    tm = TM_PROJ
    row = lambda n: pl.BlockSpec((tm, n), lambda i, t: (i, 0))
    full = lambda a: pl.BlockSpec(a.shape, lambda i, t: (0,) * a.ndim)
    anyspace = pl.BlockSpec(memory_space=pl.ANY)
    xs0 = jnp.zeros((ROWS_SORTED, D_MODEL), BF16)
    ws0 = jnp.zeros((ROWS_SORTED, LANES), F32)
    return pl.pallas_call(
        _dispatch_kernel,
        grid_spec=pltpu.PrefetchScalarGridSpec(
            num_scalar_prefetch=1, grid=(TOKENS // tm,),
            in_specs=[row(D_MODEL), row(LANES), full(ltri), full(ustr), anyspace, anyspace],
            out_specs=[anyspace, anyspace],
            scratch_shapes=[pltpu.VMEM((ROWS_LOCAL, D_MODEL), BF16), pltpu.VMEM((ROWS_LOCAL, LANES), F32),
                            pltpu.SemaphoreType.DMA((2,))]),
        out_shape=[jax.ShapeDtypeStruct((ROWS_SORTED, D_MODEL), BF16),
                   jax.ShapeDtypeStruct((ROWS_SORTED, LANES), F32)],
        input_output_aliases={5: 0, 6: 1},
        compiler_params=pltpu.CompilerParams(dimension_semantics=("arbitrary",)),
        name="moe_dispatch",
    )(tbl, x1b, wf, ltri, ustr, xs0, ws0)


def _experts_kernel(gid_ref, valid_ref, xs_ref, ws_ref, wg_ref, wu_ref, wd_ref, y_ref, acc_ref):
    i, e = pl.program_id(0), pl.program_id(1)
    last = pl.num_programs(1) - 1

    @pl.when(valid_ref[i] == 1)
    def _():
        @pl.when(e == 0)
        def _():
            acc_ref[...] = jnp.zeros_like(acc_ref)

        x = xs_ref[...]
        hg = _dot(x, wg_ref[0])
        hu = _dot(x, wu_ref[0])
        lane = lax.broadcasted_iota(jnp.int32, ws_ref.shape, 1)
        w_tok = jnp.sum(jnp.where(lane == gid_ref[i] * MOE_EPG + e, ws_ref[...], 0.0), axis=-1, keepdims=True)
        hid = hg * jax.nn.sigmoid(hg) * hu * w_tok
        acc_ref[...] += _dot(hid.astype(BF16), wd_ref[0])

        @pl.when(e == last)
        def _():
            y_ref[...] = acc_ref[...].astype(BF16)

    @pl.when((valid_ref[i] == 0) & (e == last))
    def _():
        y_ref[...] = jnp.zeros_like(y_ref)


def _experts(gid, valid, xs, ws, wg, wu, wd):
    tm = TM_MOE
    row = lambda n: pl.BlockSpec((tm, n), lambda i, e, gid, valid: (i, 0))
    wspec = lambda a, b: pl.BlockSpec((1, a, b), lambda i, e, gid, valid: (gid[i] * MOE_EPG + e, 0, 0))
    return pl.pallas_call(
        _experts_kernel,
        grid_spec=pltpu.PrefetchScalarGridSpec(
            num_scalar_prefetch=2, grid=(ROWS_SORTED // tm, MOE_EPG),
            in_specs=[row(D_MODEL), row(LANES), wspec(D_MODEL, MOE_HIDDEN), wspec(D_MODEL, MOE_HIDDEN),
                      wspec(MOE_HIDDEN, D_MODEL)],
            out_specs=row(D_MODEL),
            scratch_shapes=[pltpu.VMEM((tm, D_MODEL), F32)]),
        out_shape=jax.ShapeDtypeStruct((ROWS_SORTED, D_MODEL), BF16),
        compiler_params=pltpu.CompilerParams(dimension_semantics=("arbitrary", "arbitrary")),
        name="moe_experts",
    )(gid, valid, xs, ws, wg, wu, wd)


def _combine_kernel(tbl_ref, wf_ref, x1_ref, ltri_ref, ustr_ref, g_ref, b_ref, y_hbm, o_ref, ybuf, sems):
    tile = pl.program_id(0)
    tm = x1_ref.shape[0]
    ybuf[...] = jnp.zeros_like(ybuf)
    _granule_copies(tbl_ref, tile, (ybuf,), (y_hbm,), sems, False, lambda c: c.start())
    dest = _group_dest(wf_ref[...], ltri_ref, ustr_ref)
    c = lax.broadcasted_iota(jnp.int32, (tm, ROWS_LOCAL), 1).astype(F32)
    unperm = jnp.where(c == dest, 1.0, 0.0).astype(BF16)
    _granule_copies(tbl_ref, tile, (ybuf,), (y_hbm,), sems, False, lambda c: c.wait())
    ffn = _dot(unperm, ybuf[...])
    o_ref[...] = _layer_norm(DN_ALPHA * x1_ref[...] + ffn, g_ref[...], b_ref[...])


def _combine(tbl, wf, x1, ltri, ustr, g, b, y):
    tm = TM_PROJ
    row = lambda n: pl.BlockSpec((tm, n), lambda i, t: (i, 0))
    full = lambda a: pl.BlockSpec(a.shape, lambda i, t: (0,) * a.ndim)
    return pl.pallas_call(
        _combine_kernel,
        grid_spec=pltpu.PrefetchScalarGridSpec(
            num_scalar_prefetch=1, grid=(TOKENS // tm,),
            in_specs=[row(LANES), row(D_MODEL), full(ltri), full(ustr), full(g), full(b),
                      pl.BlockSpec(memory_space=pl.ANY)],
            out_specs=row(D_MODEL),
            scratch_shapes=[pltpu.VMEM((ROWS_LOCAL, D_MODEL), BF16), pltpu.SemaphoreType.DMA((1,))]),
        out_shape=jax.ShapeDtypeStruct((TOKENS, D_MODEL), F32),
        compiler_params=pltpu.CompilerParams(dimension_semantics=("arbitrary",)),
        name="moe_combine",
    )(tbl, wf, x1, ltri, ustr, g, b, y)


def _routing_tables(cnt):
    n_tiles = cnt.shape[0]
    c = cnt[:, 0, :MOE_GROUPS].astype(jnp.int32)
    gran = (c + GRAN - 1) // GRAN
    local = jnp.cumsum(gran, axis=1) - gran
    per_tile = TM_MOE // GRAN
    tiles_g = (jnp.sum(gran, axis=0) + per_tile - 1) // per_tile
    ends = jnp.cumsum(tiles_g)
    base = (ends - tiles_g) * per_tile
    glob = base[None, :] + jnp.cumsum(gran, axis=0) - gran
    tbl = jnp.concatenate([gran, local, glob], axis=1).reshape(n_tiles * TBL_W)
    idx = jnp.arange(ROWS_SORTED // TM_MOE)
    gid = jnp.minimum(jnp.sum(idx[:, None] >= ends[None, :], axis=1), MOE_GROUPS - 1).astype(jnp.int32)
    valid = (idx < ends[-1]).astype(jnp.int32)
    return tbl, gid, valid


def _sort_tables():
    t = jnp.arange(TM_PROJ)
    ltri = (t[None, :] < t[:, None]).astype(BF16)
    l = jnp.arange(LANES)
    ustr = (l[:, None] < l[None, :]).astype(BF16)
    return ltri, ustr


def _dup_halves(t):
    lane = lax.broadcasted_iota(jnp.int32, t.shape, 1)
    r = pltpu.roll(t, HALF, 1)
    return jnp.where(lane < HALF, t, r), jnp.where(lane < HALF, r, t)


def _nsa_in_kernel(x_ref, w_ref, gb_ref, q_ref, kc_ref, vc_ref, ks_ref, vs_ref, kw_ref, vw_ref, gate_ref):
    h = _dot(x_ref[...].astype(BF16), w_ref[...])
    q_ref[...] = (h[:, 0:C_MIX] * NSA_SCALE).astype(BF16)
    kc_ref[...] = h[:, 1024:1152].astype(BF16)
    vc_ref[...] = h[:, 1152:1280].astype(BF16)
    for idx, ref in enumerate((ks_ref, vs_ref, kw_ref, vw_ref)):
        d0, d1 = _dup_halves(h[:, 1280 + idx * LANES:1280 + (idx + 1) * LANES])
        ref[:, 0:LANES] = d0.astype(BF16)
        ref[:, LANES:2 * LANES] = d1.astype(BF16)
    for g in range(NSA_GROUPS):
        gate_ref[g] = jax.nn.sigmoid(h[:, 1792 + g * LANES:1792 + (g + 1) * LANES] + gb_ref[g])


def _nsa_in(x2, w, gb):
    tm = TM_PROJ
    row = lambda n: pl.BlockSpec((tm, n), lambda i: (i, 0))
    full = lambda a: pl.BlockSpec(a.shape, lambda i: (0,) * a.ndim)
    sd = jax.ShapeDtypeStruct
    return pl.pallas_call(
        _nsa_in_kernel,
        grid=(TOKENS // tm,),
        in_specs=[row(D_MODEL), full(w), full(gb)],
        out_specs=[row(C_MIX), row(LANES), row(LANES), row(2 * LANES), row(2 * LANES), row(2 * LANES),
                   row(2 * LANES), pl.BlockSpec((NSA_GROUPS, tm, LANES), lambda i: (0, i, 0))],
        out_shape=[sd((TOKENS, C_MIX), BF16), sd((TOKENS, LANES), BF16), sd((TOKENS, LANES), BF16),
                   sd((TOKENS, 2 * LANES), BF16), sd((TOKENS, 2 * LANES), BF16), sd((TOKENS, 2 * LANES), BF16),
                   sd((TOKENS, 2 * LANES), BF16), sd((NSA_GROUPS, TOKENS, LANES), F32)],
        compiler_params=pltpu.CompilerParams(dimension_semantics=("arbitrary",)),
        name="nsa_in",
    )(x2, w, gb)


def _compress_kernel(kc_ref, vc_ref, pk_ref, pv_ref, wk1_ref, wv1_ref, wk2_ref, wv2_ref, ko_ref, vo_ref):
    for a_ref, p_ref, w1_ref, w2_ref, o_ref in ((kc_ref, pk_ref, wk1_ref, wk2_ref, ko_ref),
                                                (vc_ref, pv_ref, wv1_ref, wv2_ref, vo_ref)):
        a = a_ref[0].astype(F32)
        a0 = (a + p_ref[0]).astype(BF16)
        a1 = (a + p_ref[1]).astype(BF16)
        outs = []
        for g in range(NSA_GROUPS):
            first = _dot(a0, w1_ref[g])
            second = _dot(a1, w1_ref[NSA_GROUPS + g])
            hid = first + pltpu.roll(second, NSA_NCMP - 1, 0)
            outs.append(_dot(_gelu(hid).astype(BF16), w2_ref[...]))
        o_ref[0] = jnp.concatenate(outs, axis=1).astype(BF16)


def _compress(kc_r, vc_r, pk, pv, wk1, wv1, wk2, wv2):
    blk = pl.BlockSpec((1, NSA_NCMP, NSA_CMP_STRIDE * LANES), lambda b: (b, 0, 0))
    full = lambda a: pl.BlockSpec(a.shape, lambda b: (0,) * a.ndim)
    out = pl.BlockSpec((1, NSA_NCMP, 2 * LANES), lambda b: (b, 0, 0))
    sd = jax.ShapeDtypeStruct((BATCH, NSA_NCMP, 2 * LANES), BF16)
    return pl.pallas_call(
        _compress_kernel,
        grid=(BATCH,),
        in_specs=[blk, blk, full(pk), full(pv), full(wk1), full(wv1), full(wk2), full(wv2)],
        out_specs=[out, out],
        out_shape=[sd, sd],
        compiler_params=pltpu.CompilerParams(dimension_semantics=("arbitrary",)),
        name="nsa_compress",
    )(kc_r, vc_r, pk, pv, wk1, wv1, wk2, wv2)


def _nsa_attn_kernel(q_ref, kc_ref, vc_ref, ks_ref, vs_ref, kw_ref, vw_ref, gate_ref, cover_ref, exp_ref,
                     o_ref, mem_ref):
    tq, hpg = TQ_NSA, NSA_HPG
    rows = hpg * tq
    qi = pl.program_id(2)
    q0 = qi * tq
    lane = lax.broadcasted_iota(jnp.int32, (tq, LANES), 1)
    lo = lane < HALF
    t_tok = q0 + lax.broadcasted_iota(jnp.int32, (tq, 1), 0)

    parts = []
    for p in range(hpg // 2):
        qp = q_ref[0, :, p * LANES:(p + 1) * LANES]
        zero = jnp.zeros_like(qp)
        parts.append(jnp.where(lo, qp, zero))
        parts.append(jnp.where(lo, zero, qp))
    qs = jnp.concatenate(parts, axis=0)

    def softmax_masked(s, mask):
        s3 = jnp.where(mask[None], s.reshape(hpg, tq, s.shape[-1]), NEG)
        m = jnp.max(s3, axis=-1, keepdims=True)
        e = jnp.exp(s3 - m)
        return e / jnp.sum(e, axis=-1, keepdims=True)

    s_c = _dot_nt(qs, kc_ref[0])
    vis = t_tok >= lane * NSA_CMP_STRIDE + (NSA_CMP_LEN - 1)
    p_c = jnp.where(vis[None], softmax_masked(s_c, vis), 0.0)
    o_c = _dot(p_c.reshape(rows, LANES).astype(BF16), vc_ref[0])

    imp = jnp.dot(jnp.sum(p_c, axis=0), cover_ref[...], preferred_element_type=F32,
                  precision=lax.Precision.HIGHEST)
    tb = t_tok // NSA_SEL_LEN
    forced = (lane == 0) | (lane == tb) | (lane == tb - 1)
    score = jnp.where(forced, NSA_FORCE, jnp.where(lane <= tb, imp, -NSA_FORCE))
    rank = jnp.zeros((tq, LANES), jnp.int32)
    for i in range(NSA_NSEL):
        si = score[:, i:i + 1]
        beats = (si > score) | ((si == score) & (lane > i))
        rank = rank + beats.astype(jnp.int32)
    sel = ((rank < NSA_TOPK) & (lane < NSA_NSEL)).astype(BF16)
    n_sel_tiles = SEQ // TK_SEL
    for kt in range(n_sel_tiles):
        mem_ref[kt] = _dot(sel, exp_ref[:, kt * TK_SEL:(kt + 1) * TK_SEL])

    kcol = lax.broadcasted_iota(jnp.int32, (1, TK_SEL), 1)

    def sel_step(kt, carry):
        m, l, acc = carry
        r0 = pl.multiple_of(kt * TK_SEL, TK_SEL)
        s = _dot_nt(qs, ks_ref[0, pl.ds(r0, TK_SEL), :])
        mask = (mem_ref[kt] > 0.5) & (r0 + kcol <= t_tok)
        s3 = jnp.where(mask[None], s.reshape(hpg, tq, TK_SEL), NEG)
        m_new = jnp.maximum(m, jnp.max(s3, axis=-1, keepdims=True))
        a = jnp.exp(m - m_new)
        p = jnp.exp(s3 - m_new)
        l = a * l + jnp.sum(p, axis=-1, keepdims=True)
        pv = _dot(p.reshape(rows, TK_SEL).astype(BF16), vs_ref[0, pl.ds(r0, TK_SEL), :])
        acc = a * acc + pv.reshape(hpg, tq, LANES)
        return m_new, l, acc

    carry = (jnp.full((hpg, tq, 1), NEG, F32), jnp.zeros((hpg, tq, 1), F32),
             jnp.zeros((hpg, tq, LANES), F32))
    n_tiles = (q0 + tq - 1) // TK_SEL + 1
    m, l, acc = lax.fori_loop(0, n_tiles, sel_step, carry)
    o_s = (acc / l).reshape(rows, LANES)

    start = pl.multiple_of(jnp.maximum(q0 - NSA_WINDOW, 0), tq)
    kpos = start + lax.broadcasted_iota(jnp.int32, (1, WIN_SPAN), 1)
    s_w = _dot_nt(qs, kw_ref[0, pl.ds(start, WIN_SPAN), :])
    in_win = (kpos <= t_tok) & (kpos > t_tok - NSA_WINDOW)
    p_w = softmax_masked(s_w, in_win)
    o_w = _dot(p_w.reshape(rows, WIN_SPAN).astype(BF16), vw_ref[0, pl.ds(start, WIN_SPAN), :])

    gt = gate_ref[0, 0]

    def gate_col(br):
        return jnp.concatenate([gt[:, br * hpg + i:br * hpg + i + 1] for i in range(hpg)], axis=0)

    o = gate_col(0) * o_c + gate_col(1) * o_s + gate_col(2) * o_w
    outs = []
    for p in range(hpg // 2):
        even = o[(2 * p) * tq:(2 * p + 1) * tq]
        odd = o[(2 * p + 1) * tq:(2 * p + 2) * tq]
        outs.append(jnp.where(lo, even, odd))
    o_ref[0] = jnp.concatenate(outs, axis=1).astype(BF16)


def _nsa_attn(q3, kc2, vc2, ks3, vs3, kw3, vw3, gates4, cover, expand):
    tq = TQ_NSA
    half_w = NSA_HPG * NSA_DH
    kv = pl.BlockSpec((1, SEQ, LANES), lambda b, g, i: (b, 0, g))
    cmp_spec = pl.BlockSpec((1, NSA_NCMP, LANES), lambda b, g, i: (b, 0, g))
    full = lambda a: pl.BlockSpec(a.shape, lambda b, g, i: (0,) * a.ndim)
    return pl.pallas_call(
        _nsa_attn_kernel,
        grid=(BATCH, NSA_GROUPS, SEQ // tq),
        in_specs=[pl.BlockSpec((1, tq, half_w), lambda b, g, i: (b, i, g)),
                  cmp_spec, cmp_spec, kv, kv, kv, kv,
                  pl.BlockSpec((1, 1, tq, LANES), lambda b, g, i: (g, b, i, 0)),
                  full(cover), full(expand)],
        out_specs=pl.BlockSpec((1, tq, half_w), lambda b, g, i: (b, i, g)),
        out_shape=jax.ShapeDtypeStruct((BATCH, SEQ, C_MIX), BF16),
        scratch_shapes=[pltpu.VMEM((SEQ // TK_SEL, tq, TK_SEL), F32)],
        compiler_params=pltpu.CompilerParams(dimension_semantics=("arbitrary",) * 3),
        name="nsa_attn",
    )(q3, kc2, vc2, ks3, vs3, kw3, vw3, gates4, cover, expand)


def _rope_tables():
    half = MLA_ROPE // 2
    freq = jnp.exp(-math.log(ROPE_BASE) * jnp.arange(half, dtype=F32) / half)
    zeros64 = jnp.zeros((MLA_NOPE,), F32)
    zeros32 = jnp.zeros((LANES - MLA_NOPE - MLA_ROPE,), F32)
    fc = jnp.concatenate([zeros64, freq, freq, zeros32])[None, :]
    sg = jnp.concatenate([zeros64, -jnp.ones((half,), F32), jnp.ones((half,), F32), zeros32])[None, :]
    return fc, sg


def _swap_halves(w):
    half = w.shape[-1] // 2
    return jnp.concatenate([w[..., half:], w[..., :half]], axis=-1)


def _pad_last(w, n):
    return jnp.pad(w, [(0, 0)] * (w.ndim - 1) + [(0, n - w.shape[-1])])


def _ab_weights(w_in, w_uq, w_uk, w_uv):
    w_kr = w_in[:, 1408:1440]
    place = lambda w: jnp.pad(w, ((0, 0), (MLA_NOPE, LANES - MLA_NOPE - MLA_ROPE)))
    win = jnp.concatenate([w_in[:, :1408], place(w_kr), place(_swap_halves(w_kr))], axis=1).astype(BF16)
    uq = w_uq.reshape(MLA_Q_RANK, MLA_HEADS, MLA_NOPE + MLA_ROPE)
    nope, rp = uq[..., :MLA_NOPE], uq[..., MLA_NOPE:]
    q_pad = _pad_last(jnp.concatenate([nope, rp], -1), LANES).reshape(MLA_Q_RANK, MLA_HEADS * LANES)
    q_sw = _pad_last(jnp.concatenate([jnp.zeros_like(nope), _swap_halves(rp)], -1), LANES)
    wq = jnp.concatenate([q_pad, q_sw.reshape(MLA_Q_RANK, MLA_HEADS * LANES)], axis=1).astype(BF16)
    k_pad = _pad_last(w_uk.reshape(MLA_KV_RANK, MLA_HEADS, MLA_NOPE), LANES).reshape(MLA_KV_RANK, -1)
    wkv = jnp.concatenate([k_pad, w_uv], axis=1).astype(BF16)
    return win, wq, wkv


def _router_weights(w_rg, b_rg, w_re, b_re):
    wr = _pad_last(jnp.concatenate([w_re, w_rg], axis=1), LANES)
    br = _pad_last(jnp.concatenate([b_re, b_rg])[None, :], LANES)
    return wr, br


def _nsa_in_weights(w_in, gate_b):
    g_cols = w_in[:, C_MIX + 768:].reshape(D_MODEL, 3, NSA_GROUPS, NSA_HPG)
    g_blocks = [_pad_last(g_cols[:, :, g, :].reshape(D_MODEL, 3 * NSA_HPG), LANES) for g in range(NSA_GROUPS)]
    w = jnp.concatenate([w_in[:, :C_MIX + 768]] + g_blocks, axis=1).astype(BF16)
    gb = gate_b.reshape(3, NSA_GROUPS, NSA_HPG)
    gb = jnp.stack([_pad_last(gb[:, g, :].reshape(1, 3 * NSA_HPG), LANES) for g in range(NSA_GROUPS)])
    return w, gb


def _compress_weights(pos, w1, w2):
    w1r = w1.reshape(2, NSA_CMP_STRIDE, NSA_DH, NSA_CMP_HIDDEN)
    zero = jnp.zeros_like(w1r)
    per_g = []
    for g in range(NSA_GROUPS):
        parts = [w1r if gg == g else zero for gg in range(NSA_GROUPS)]
        per_g.append(jnp.stack(parts, axis=2).reshape(2, NSA_CMP_STRIDE * LANES, NSA_CMP_HIDDEN))
    w1x = jnp.stack(per_g, axis=1).reshape(2 * NSA_GROUPS, NSA_CMP_STRIDE * LANES, NSA_CMP_HIDDEN)
    posr = pos.reshape(2, NSA_CMP_STRIDE, 1, NSA_DH)
    posx = jnp.broadcast_to(posr, (2, NSA_CMP_STRIDE, NSA_GROUPS, NSA_DH)).reshape(2, 1, NSA_CMP_STRIDE * LANES)
    w2x = jnp.concatenate([w2, w2], axis=1)
    return posx, w1x.astype(BF16), w2x.astype(BF16)


def _selection_tables():
    n = jnp.arange(LANES)[:, None]
    j = jnp.arange(LANES)[None, :]
    c0 = n * NSA_CMP_STRIDE
    s0 = j * NSA_SEL_LEN
    cover = ((c0 < s0 + NSA_SEL_LEN) & (c0 + NSA_CMP_LEN > s0) & (n < NSA_NCMP - 1) & (j < NSA_NSEL))
    kk = jnp.arange(SEQ)[None, :]
    expand = (kk // NSA_SEL_LEN == n)
    return cover.astype(F32), expand.astype(BF16)


def kernel(x, positions, ab_w_in, ab_gm_ln_g, ab_gm_ln_b, ab_gm_ws, ab_gm_bs, ab_mla_q_norm,
           ab_mla_kv_norm, ab_mla_w_uq, ab_mla_w_uk, ab_mla_w_uv, ab_w_o, c_w_in, c_cmp_pos, c_w_ck1,
           c_w_ck2, c_w_cv1, c_w_cv2, c_gate_b, c_w_o, moe_w_rg, moe_b_rg, moe_w_re, moe_b_re,
           moe_w_gate, moe_w_up, moe_w_down, ln1_g, ln1_b, ln2_g, ln2_b):
    x2 = x.reshape(TOKENS, D_MODEL)
    pos2 = positions.reshape(TOKENS, 1)
    vec = lambda a: a[None, :]

    ltri, ustr = _sort_tables()

    def moe_layer(layer, x1b, x1, wf, cnt):
        tbl, gid, valid = _routing_tables(cnt)
        xs, ws = _dispatch(tbl, x1b, wf, ltri, ustr)
        y = _experts(gid, valid, xs, ws, moe_w_gate[layer].astype(BF16), moe_w_up[layer].astype(BF16),
                     moe_w_down[layer].astype(BF16))
        return _combine(tbl, wf, x1, ltri, ustr, vec(ln2_g[layer]), vec(ln2_b[layer]), y)

    win, wq, wkv = _ab_weights(ab_w_in[0], ab_mla_w_uq[0], ab_mla_w_uk[0], ab_mla_w_uv[0])
    fc, sg = _rope_tables()
    gu, vn, q, k, v = _ab_in(x2, pos2, win, vec(ab_gm_ln_g[0]), vec(ab_gm_ln_b[0]), vec(ab_mla_q_norm[0]),
                             vec(ab_mla_kv_norm[0]), wq, wkv, fc, sg)
    yb = _mla_attn(q.reshape(BATCH, SEQ, -1), k.reshape(BATCH, SEQ, -1), v.reshape(BATCH, SEQ, -1))
    wr, br = _router_weights(moe_w_rg[0], moe_b_rg[0], moe_w_re[0], moe_b_re[0])
    x1, x1b, wf, cnt = _mix_out(x2, yb.reshape(TOKENS, -1), ab_w_o[0].astype(BF16), vec(ln1_g[0]),
                                vec(ln1_b[0]), wr, br, gm=(gu, vn, ab_gm_ws[0], jnp.transpose(ab_gm_bs[0])))
    x2 = moe_layer(0, x1b, x1, wf, cnt)

    w_nsa, gb = _nsa_in_weights(c_w_in[0], c_gate_b[0])
    q, kc, vc, ks, vs, kw, vw, gates = _nsa_in(x2, w_nsa, gb)
    pk, wk1, wk2 = _compress_weights(c_cmp_pos[0, 0], c_w_ck1[0], c_w_ck2[0])
    pv, wv1, wv2 = _compress_weights(c_cmp_pos[0, 1], c_w_cv1[0], c_w_cv2[0])
    blocks = lambda a: a.reshape(BATCH, NSA_NCMP, NSA_CMP_STRIDE * LANES)
    kc2, vc2 = _compress(blocks(kc), blocks(vc), pk, pv, wk1, wv1, wk2, wv2)
    cover, expand = _selection_tables()
    b3 = lambda a: a.reshape(BATCH, SEQ, -1)
    o = _nsa_attn(b3(q), kc2, vc2, b3(ks), b3(vs), b3(kw), b3(vw),
                  gates.reshape(NSA_GROUPS, BATCH, SEQ, LANES), cover, expand)
    wr, br = _router_weights(moe_w_rg[1], moe_b_rg[1], moe_w_re[1], moe_b_re[1])
    x1, x1b, wf, cnt = _mix_out(x2, o.reshape(TOKENS, -1), c_w_o[0].astype(BF16), vec(ln1_g[1]), vec(ln1_b[1]),
                                wr, br)
    x2 = moe_layer(1, x1b, x1, wf, cnt)
    return x2.reshape(BATCH, SEQ, D_MODEL)
```

```python
import functools
import math

import jax
import jax.numpy as jnp
from jax import lax
from jax.experimental import pallas as pl
from jax.experimental.pallas import tpu as pltpu

F32 = jnp.float32
BF16 = jnp.bfloat16

D_MODEL = 1024
BATCH = 16
SEQ = 2048
TOKENS = BATCH * SEQ
DEPTH = 2
DN_ALPHA = (2.0 * DEPTH) ** 0.25
LN_EPS = 1e-5
NEG = -1e30
LANES = 128
HALF = LANES // 2

GM_WIDTH = 512
GM_GROUPS = 4
GM_CHUNK = 128

MLA_HEADS = 8
MLA_NOPE = 64
MLA_ROPE = 32
MLA_V = 64
MLA_Q_RANK = 256
MLA_KV_RANK = 128
ROPE_BASE = 10000.0
MLA_SCALE = (MLA_NOPE + MLA_ROPE) ** -0.5

NSA_HEADS = 16
NSA_GROUPS = 2
NSA_HPG = 8
NSA_DH = 64
NSA_CMP_LEN = 32
NSA_CMP_STRIDE = 16
NSA_CMP_HIDDEN = 256
NSA_SEL_LEN = 64
NSA_TOPK = 8
NSA_WINDOW = 512
NSA_NSEL = SEQ // NSA_SEL_LEN
NSA_NCMP = SEQ // NSA_CMP_STRIDE
NSA_FORCE = 1e4
NSA_SCALE = NSA_DH ** -0.5
C_MIX = NSA_HEADS * NSA_DH

MOE_GROUPS = 4
MOE_EPG = 8
MOE_EXPERTS = 32
MOE_HIDDEN = 256

TM_PROJ = 512
TQ_MLA = 256
TQ_NSA = 256
TK_ATT = 256
WIN_CHUNKS = (NSA_WINDOW + TQ_NSA) // TK_ATT
TM_MOE = 512

GSEL_LANE = MOE_EXPERTS
GRAN = 16
TBL_W = 3 * MOE_GROUPS
ROWS_LOCAL = 640
ROWS_SORTED = TM_MOE * (TOKENS // TM_MOE + MOE_GROUPS
                        + -(-(TOKENS // TM_PROJ) * MOE_GROUPS * (GRAN - 1) // TM_MOE))


def _dot(a, b):
    return jnp.dot(a, b, preferred_element_type=F32)


def _dot_nt(a, b):
    return lax.dot_general(a, b, (((1,), (1,)), ((), ())), preferred_element_type=F32)


def _gelu(x):
    return 0.5 * x * (1.0 + jnp.tanh(math.sqrt(2.0 / math.pi) * (x + 0.044715 * (x * x * x))))


def _layer_norm(x, g, b):
    mu = jnp.mean(x, axis=-1, keepdims=True)
    xc = x - mu
    var = jnp.mean(xc * xc, axis=-1, keepdims=True)
    return xc * lax.rsqrt(var + LN_EPS) * g + b


def _rms_norm(x, g):
    return x * lax.rsqrt(jnp.mean(x * x, axis=-1, keepdims=True) + LN_EPS) * g


def _store_transposed(ref, blocks):
    for n, blk in enumerate(blocks):
        t = blk.T
        for c in range(t.shape[1] // TK_ATT):
            ref[n, c] = t[:, c * TK_ATT:(c + 1) * TK_ATT].astype(ref.dtype)


def _ab_in_kernel(x_ref, pos_ref, win_ref, lng_ref, lnb_ref, qg_ref, kvg_ref, wq_ref, wkv_ref,
                  fc_ref, sg_ref, gu_ref, vn_ref, q_ref, k_ref, vt_ref):
    h = _dot(x_ref[...].astype(BF16), win_ref[...])
    gu_ref[...] = _gelu(h[:, 0:512]).astype(BF16)
    vn_ref[...] = _layer_norm(_gelu(h[:, 512:1024]), lng_ref[...], lnb_ref[...]).astype(BF16)

    ang = pos_ref[...].astype(F32) * fc_ref[...]
    cc = jnp.cos(ang)
    ss = jnp.sin(ang) * sg_ref[...]

    cqn = _rms_norm(h[:, 1024:1280], qg_ref[...]).astype(BF16)
    qq = _dot(cqn, wq_ref[...])
    for hd in range(MLA_HEADS):
        lo, hi = hd * LANES, (hd + 1) * LANES
        q_ref[:, lo:hi] = ((qq[:, lo:hi] * cc + qq[:, 1024 + lo:1024 + hi] * ss) * MLA_SCALE).astype(BF16)

    ckvn = _rms_norm(h[:, 1280:1408], kvg_ref[...]).astype(BF16)
    kv = _dot(ckvn, wkv_ref[...])
    k_rope = h[:, 1408:1536] * cc + h[:, 1536:1664] * ss
    for hd in range(MLA_HEADS):
        lo, hi = hd * LANES, (hd + 1) * LANES
        k_ref[:, lo:hi] = (kv[:, lo:hi] + k_rope).astype(BF16)
    _store_transposed(vt_ref, [kv[:, 1024 + p * LANES:1024 + (p + 1) * LANES] for p in range(MLA_HEADS // 2)])


def _ab_in(x2, pos2, win, lng, lnb, qg, kvg, wq, wkv, fc, sg):
    tm = TM_PROJ
    row = lambda n: pl.BlockSpec((tm, n), lambda i: (i, 0))
    full = lambda a: pl.BlockSpec(a.shape, lambda i: (0,) * a.ndim)
    return pl.pallas_call(
        _ab_in_kernel,
        grid=(TOKENS // tm,),
        in_specs=[row(D_MODEL), row(1), full(win), full(lng), full(lnb), full(qg), full(kvg),
                  full(wq), full(wkv), full(fc), full(sg)],
        out_specs=[row(512), row(512), row(1024), row(1024),
                   pl.BlockSpec((MLA_HEADS // 2, tm // TK_ATT, LANES, TK_ATT), lambda i: (0, i, 0, 0))],
        out_shape=[jax.ShapeDtypeStruct((TOKENS, 512), BF16), jax.ShapeDtypeStruct((TOKENS, 512), BF16),
                   jax.ShapeDtypeStruct((TOKENS, 1024), BF16), jax.ShapeDtypeStruct((TOKENS, 1024), BF16),
                   jax.ShapeDtypeStruct((MLA_HEADS // 2, TOKENS // TK_ATT, LANES, TK_ATT), BF16)],
        compiler_params=pltpu.CompilerParams(dimension_semantics=("arbitrary",)),
        name="ab_in",
    )(x2, pos2, win, lng, lnb, qg, kvg, wq, wkv, fc, sg)


def _mla_attn_kernel(q_ref, k_ref, vt_ref, o_ref, m_ref, l_ref, acc_ref):
    tq, tk = TQ_MLA, TK_ATT
    qi = pl.program_id(1)
    krow = lax.broadcasted_iota(jnp.int32, (tk, tq), 0)
    qcol = lax.broadcasted_iota(jnp.int32, (tk, tq), 1)
    top = lax.broadcasted_iota(jnp.int32, (LANES, tq), 0) < HALF
    m_ref[...] = jnp.full(m_ref.shape, NEG, F32)
    l_ref[...] = jnp.zeros(l_ref.shape, F32)
    acc_ref[...] = jnp.zeros(acc_ref.shape, F32)

    def tile(j, masked):
        r0 = pl.multiple_of(j * tk, tk)
        scores = []
        for h in range(MLA_HEADS):
            s = _dot_nt(k_ref[0, pl.ds(r0, tk), h * LANES:(h + 1) * LANES], q_ref[0, :, h * LANES:(h + 1) * LANES])
            scores.append(jnp.where(krow <= qcol, s, NEG) if masked else s)
        probs, alphas = [], []
        for h in range(MLA_HEADS):
            m_old = m_ref[h]
            m_new = jnp.maximum(m_old, jnp.max(scores[h], axis=0, keepdims=True))
            a = jnp.exp(m_old - m_new)
            p = jnp.exp(scores[h] - m_new)
            l_ref[h] = a * l_ref[h] + jnp.sum(p, axis=0, keepdims=True)
            m_ref[h] = m_new
            probs.append(p.astype(BF16))
            alphas.append(a)
        for pr in range(MLA_HEADS // 2):
            pv = _dot(vt_ref[pr, j], jnp.concatenate([probs[2 * pr], probs[2 * pr + 1]], axis=1))
            a = jnp.where(top, alphas[2 * pr], alphas[2 * pr + 1])
            acc_ref[pr] = a * acc_ref[pr] + jnp.where(top, pv[:, :tq], pv[:, tq:])

    def body(j, c):
        tile(j, False)
        return c

    lax.fori_loop(0, qi, body, 0)
    tile(qi, True)
    for pr in range(MLA_HEADS // 2):
        l = jnp.where(top, l_ref[2 * pr], l_ref[2 * pr + 1])
        o_ref[0, :, pr * LANES:(pr + 1) * LANES] = (acc_ref[pr] / l).T.astype(BF16)


def _mla_attn(q3, k3, vt):
    tq = TQ_MLA
    n_chunks = SEQ // TK_ATT
    return pl.pallas_call(
        _mla_attn_kernel,
        grid=(BATCH, SEQ // tq),
        in_specs=[pl.BlockSpec((1, tq, MLA_HEADS * LANES), lambda b, i: (b, i, 0)),
                  pl.BlockSpec((1, SEQ, MLA_HEADS * LANES), lambda b, i: (b, 0, 0)),
                  pl.BlockSpec((MLA_HEADS // 2, n_chunks, LANES, TK_ATT), lambda b, i: (0, b, 0, 0))],
        out_specs=pl.BlockSpec((1, tq, MLA_HEADS * MLA_V), lambda b, i: (b, i, 0)),
        out_shape=jax.ShapeDtypeStruct((BATCH, SEQ, MLA_HEADS * MLA_V), BF16),
        scratch_shapes=[pltpu.VMEM((MLA_HEADS, 1, tq), F32), pltpu.VMEM((MLA_HEADS, 1, tq), F32),
                        pltpu.VMEM((MLA_HEADS // 2, LANES, tq), F32)],
        compiler_params=pltpu.CompilerParams(dimension_semantics=("arbitrary",) * 2),
        name="mla_attn",
    )(q3, k3, vt)


def _router(x1, wr, br):
    tm = x1.shape[0]
    x_hi = x1.astype(BF16)
    x_lo = (x1 - x_hi.astype(F32)).astype(BF16)
    logits = _dot(x_hi, wr[0]) + (_dot(x_lo, wr[0]) + _dot(x_hi, wr[1])) + br
    lane = lax.broadcasted_iota(jnp.int32, (tm, LANES), 1).astype(F32)
    big = 1e6
    is_g = (lane >= MOE_EXPERTS) & (lane < MOE_EXPERTS + MOE_GROUPS)
    gl = jnp.where(is_g, logits, NEG)
    gmax = jnp.max(gl, axis=-1, keepdims=True)
    g_sel = jnp.min(jnp.where(is_g & (gl == gmax), lane, big), axis=-1, keepdims=True) - MOE_EXPERTS
    g_w = 1.0 / jnp.sum(jnp.where(is_g, jnp.exp(gl - gmax), 0.0), axis=-1, keepdims=True)
    in_grp = (lane >= g_sel * MOE_EPG) & (lane < (g_sel + 1) * MOE_EPG)
    el = jnp.where(in_grp, logits, NEG)
    emax = jnp.max(el, axis=-1, keepdims=True)
    ee = jnp.where(in_grp, jnp.exp(el - emax), 0.0)
    pe = ee / jnp.sum(ee, axis=-1, keepdims=True)
    p1 = jnp.max(pe, axis=-1, keepdims=True)
    i1 = jnp.min(jnp.where(in_grp & (pe == p1), lane, big), axis=-1, keepdims=True)
    rest = in_grp & (lane != i1)
    pr = jnp.where(rest, pe, -1.0)
    p2 = jnp.max(pr, axis=-1, keepdims=True)
    i2 = jnp.min(jnp.where(rest & (pr == p2), lane, big), axis=-1, keepdims=True)
    tot = p1 + p2
    wf = jnp.where(lane == i1, p1 / tot * g_w, jnp.where(lane == i2, p2 / tot * g_w, 0.0))
    wf = jnp.where(lane == GSEL_LANE, g_sel, wf)
    cnt = jnp.sum(jnp.where(lane == g_sel, 1.0, 0.0), axis=0, keepdims=True)
    return wf, cnt


def _mix_out_kernel(*refs, gmlp):
    if gmlp:
        (x_ref, gu_ref, vn_ref, ws_ref, bs_ref, yb_ref, wo_ref, g_ref, b_ref, wr_ref, br_ref,
         x1_ref, x1b_ref, wf_ref, cnt_ref, ya_ref) = refs
        tm = x_ref.shape[0]
        r = lax.broadcasted_iota(jnp.int32, (GM_CHUNK, GM_CHUNK), 0)
        c = lax.broadcasted_iota(jnp.int32, (GM_CHUNK, GM_CHUNK), 1)
        for g in range(GM_GROUPS):
            ws = jnp.where(r >= c, ws_ref[g], 0.0).astype(BF16)
            bias = bs_ref[:, g:g + 1]
            for ch in range(tm // GM_CHUNK):
                rows = slice(ch * GM_CHUNK, (ch + 1) * GM_CHUNK)
                cols = slice(g * LANES, (g + 1) * LANES)
                s = _dot(ws, vn_ref[rows, cols]) + bias
                ya_ref[rows, cols] = (gu_ref[rows, cols].astype(F32) * s).astype(BF16)
        mix = _dot(ya_ref[...], wo_ref[0:GM_WIDTH, :]) + _dot(yb_ref[...], wo_ref[GM_WIDTH:, :])
    else:
        x_ref, y_ref, wo_ref, g_ref, b_ref, wr_ref, br_ref, x1_ref, x1b_ref, wf_ref, cnt_ref = refs
        mix = _dot(y_ref[...], wo_ref[...])
    x1 = _layer_norm(DN_ALPHA * x_ref[...] + mix, g_ref[...], b_ref[...])
    x1_ref[...] = x1
    x1b_ref[...] = x1.astype(BF16)
    wf, cnt = _router(x1, wr_ref[...], br_ref[...])
    wf_ref[...] = wf
    cnt_ref[0] = jnp.broadcast_to(cnt, cnt_ref.shape[1:])


def _mix_out(x2, ys, wo, g, b, wr, br, gm=None):
    tm = TM_PROJ
    row = lambda n: pl.BlockSpec((tm, n), lambda i: (i, 0))
    full = lambda a: pl.BlockSpec(a.shape, lambda i: (0,) * a.ndim)
    if gm is not None:
        gu, vn, ws, bs = gm
        args = (x2, gu, vn, ws, bs, ys, wo, g, b, wr, br)
        in_specs = [row(D_MODEL), row(512), row(512), full(ws), full(bs), row(512), full(wo),
                    full(g), full(b), full(wr), full(br)]
        scratch = [pltpu.VMEM((tm, GM_WIDTH), BF16)]
    else:
        args = (x2, ys, wo, g, b, wr, br)
        in_specs = [row(D_MODEL), row(C_MIX), full(wo), full(g), full(b), full(wr), full(br)]
        scratch = []
    return pl.pallas_call(
        functools.partial(_mix_out_kernel, gmlp=gm is not None),
        grid=(TOKENS // tm,),
        in_specs=in_specs,
        out_specs=[row(D_MODEL), row(D_MODEL), row(LANES), pl.BlockSpec((1, 8, LANES), lambda i: (i, 0, 0))],
        out_shape=[jax.ShapeDtypeStruct((TOKENS, D_MODEL), F32), jax.ShapeDtypeStruct((TOKENS, D_MODEL), BF16),
                   jax.ShapeDtypeStruct((TOKENS, LANES), F32),
                   jax.ShapeDtypeStruct((TOKENS // tm, 8, LANES), F32)],
        scratch_shapes=scratch,
        compiler_params=pltpu.CompilerParams(dimension_semantics=("arbitrary",)),
        name="mix_out_gmlp" if gm is not None else "mix_out",
    )(*args)


def _group_dest(wf, ltri_ref, ustr_ref):
    tm = wf.shape[0]
    lane = lax.broadcasted_iota(jnp.int32, (tm, LANES), 1).astype(F32)
    onehot = jnp.where(lane == wf[:, GSEL_LANE:GSEL_LANE + 1], 1.0, 0.0)
    before = _dot(ltri_ref[...], onehot.astype(BF16))
    cnt = jnp.sum(onehot, axis=0, keepdims=True)
    gran = jnp.floor((cnt + (GRAN - 1)) * (1.0 / GRAN))
    start = _dot(jnp.broadcast_to(gran, (8, LANES)).astype(BF16), ustr_ref[...])[0:1]
    return jnp.sum(onehot * (GRAN * start + before), axis=-1, keepdims=True)


def _granule_copies(tbl_ref, tile, vmem_bufs, hbm_refs, sems, to_hbm, act):
    for g in range(MOE_GROUPS):
        n = tbl_ref[tile * TBL_W + g]
        loc = tbl_ref[tile * TBL_W + MOE_GROUPS + g]
        glb = tbl_ref[tile * TBL_W + 2 * MOE_GROUPS + g]

        def body(k, c, loc=loc, glb=glb):
            lo = pl.multiple_of((loc + k) * GRAN, GRAN)
            hi = pl.multiple_of((glb + k) * GRAN, GRAN)
            for idx, (vb, hb) in enumerate(zip(vmem_bufs, hbm_refs)):
                v_sl, h_sl = vb.at[pl.ds(lo, GRAN)], hb.at[pl.ds(hi, GRAN)]
                src, dst = (v_sl, h_sl) if to_hbm else (h_sl, v_sl)
                act(pltpu.make_async_copy(src, dst, sems.at[idx]))
            return c

        lax.fori_loop(0, n, body, 0)


def _dispatch_kernel(tbl_ref, xb_ref, wf_ref, ltri_ref, ustr_ref, xs_in, ws_in, xs_out, ws_out, xbuf, wbuf, sems):
    del xs_in, ws_in
    tile = pl.program_id(0)
    tm = xb_ref.shape[0]
    wf = wf_ref[...]
    dest = _group_dest(wf, ltri_ref, ustr_ref)
    dest_row = jnp.broadcast_to(dest, (tm, LANES)).T[0:1]
    r = lax.broadcasted_iota(jnp.int32, (ROWS_LOCAL, tm), 0).astype(F32)
    perm = jnp.where(r == dest_row, 1.0, 0.0).astype(BF16)
    xbuf[...] = _dot(perm, xb_ref[...]).astype(BF16)
    hi = wf.astype(BF16)
    r1 = wf - hi.astype(F32)
    mid = r1.astype(BF16)
    lo = (r1 - mid.astype(F32)).astype(BF16)
    wbuf[...] = _dot(perm, hi) + _dot(perm, mid) + _dot(perm, lo)
    bufs, outs = (xbuf, wbuf), (xs_out, ws_out)
    _granule_copies(tbl_ref, tile, bufs, outs, sems, True, lambda c: c.start())
    _granule_copies(tbl_ref, tile, bufs, outs, sems, True, lambda c: c.wait())


def _dispatch(tbl, x1b, wf, ltri, ustr):
    tm = TM_PROJ
    row = lambda n: pl.BlockSpec((tm, n), lambda i, t: (i, 0))
    full = lambda a: pl.BlockSpec(a.shape, lambda i, t: (0,) * a.ndim)
    anyspace = pl.BlockSpec(memory_space=pl.ANY)
    xs0 = jnp.zeros((ROWS_SORTED, D_MODEL), BF16)
    ws0 = jnp.zeros((ROWS_SORTED, LANES), F32)
    return pl.pallas_call(
        _dispatch_kernel,
        grid_spec=pltpu.PrefetchScalarGridSpec(
            num_scalar_prefetch=1, grid=(TOKENS // tm,),
            in_specs=[row(D_MODEL), row(LANES), full(ltri), full(ustr), anyspace, anyspace],
            out_specs=[anyspace, anyspace],
            scratch_shapes=[pltpu.VMEM((ROWS_LOCAL, D_MODEL), BF16), pltpu.VMEM((ROWS_LOCAL, LANES), F32),
                            pltpu.SemaphoreType.DMA((2,))]),
        out_shape=[jax.ShapeDtypeStruct((ROWS_SORTED, D_MODEL), BF16),
                   jax.ShapeDtypeStruct((ROWS_SORTED, LANES), F32)],
        input_output_aliases={5: 0, 6: 1},
        compiler_params=pltpu.CompilerParams(dimension_semantics=("arbitrary",)),
        name="moe_dispatch",
    )(tbl, x1b, wf, ltri, ustr, xs0, ws0)


def _experts_kernel(gid_ref, valid_ref, xs_ref, ws_ref, wg_ref, wu_ref, wd_ref, y_ref, acc_ref):
    i, e = pl.program_id(0), pl.program_id(1)
    last = pl.num_programs(1) - 1

    @pl.when(valid_ref[i] == 1)
    def _():
        @pl.when(e == 0)
        def _():
            acc_ref[...] = jnp.zeros_like(acc_ref)

        x = xs_ref[...]
        hg = _dot(x, wg_ref[0])
        hu = _dot(x, wu_ref[0])
        lane = lax.broadcasted_iota(jnp.int32, ws_ref.shape, 1)
        w_tok = jnp.sum(jnp.where(lane == gid_ref[i] * MOE_EPG + e, ws_ref[...], 0.0), axis=-1, keepdims=True)
        hid = hg * jax.nn.sigmoid(hg) * hu * w_tok
        acc_ref[...] += _dot(hid.astype(BF16), wd_ref[0])

        @pl.when(e == last)
        def _():
            y_ref[...] = acc_ref[...].astype(BF16)

    @pl.when((valid_ref[i] == 0) & (e == last))
    def _():
        y_ref[...] = jnp.zeros_like(y_ref)


def _experts(gid, valid, xs, ws, wg, wu, wd):
    tm = TM_MOE
    row = lambda n: pl.BlockSpec((tm, n), lambda i, e, gid, valid: (i, 0))
    wspec = lambda a, b: pl.BlockSpec((1, a, b), lambda i, e, gid, valid: (gid[i] * MOE_EPG + e, 0, 0))
    return pl.pallas_call(
        _experts_kernel,
        grid_spec=pltpu.PrefetchScalarGridSpec(
            num_scalar_prefetch=2, grid=(ROWS_SORTED // tm, MOE_EPG),
            in_specs=[row(D_MODEL), row(LANES), wspec(D_MODEL, MOE_HIDDEN), wspec(D_MODEL, MOE_HIDDEN),
                      wspec(MOE_HIDDEN, D_MODEL)],
            out_specs=row(D_MODEL),
            scratch_shapes=[pltpu.VMEM((tm, D_MODEL), F32)]),
        out_shape=jax.ShapeDtypeStruct((ROWS_SORTED, D_MODEL), BF16),
        compiler_params=pltpu.CompilerParams(dimension_semantics=("arbitrary", "arbitrary")),
        name="moe_experts",
    )(gid, valid, xs, ws, wg, wu, wd)


def _combine_kernel(tbl_ref, wf_ref, x1_ref, ltri_ref, ustr_ref, g_ref, b_ref, y_hbm, o_ref, ybuf, sems):
    tile = pl.program_id(0)
    tm = x1_ref.shape[0]
    ybuf[...] = jnp.zeros_like(ybuf)
    _granule_copies(tbl_ref, tile, (ybuf,), (y_hbm,), sems, False, lambda c: c.start())
    dest = _group_dest(wf_ref[...], ltri_ref, ustr_ref)
    c = lax.broadcasted_iota(jnp.int32, (tm, ROWS_LOCAL), 1).astype(F32)
    unperm = jnp.where(c == dest, 1.0, 0.0).astype(BF16)
    _granule_copies(tbl_ref, tile, (ybuf,), (y_hbm,), sems, False, lambda c: c.wait())
    ffn = _dot(unperm, ybuf[...])
    o_ref[...] = _layer_norm(DN_ALPHA * x1_ref[...] + ffn, g_ref[...], b_ref[...])


def _combine(tbl, wf, x1, ltri, ustr, g, b, y):
    tm = TM_PROJ
    row = lambda n: pl.BlockSpec((tm, n), lambda i, t: (i, 0))
    full = lambda a: pl.BlockSpec(a.shape, lambda i, t: (0,) * a.ndim)
    return pl.pallas_call(
        _combine_kernel,
        grid_spec=pltpu.PrefetchScalarGridSpec(
            num_scalar_prefetch=1, grid=(TOKENS // tm,),
            in_specs=[row(LANES), row(D_MODEL), full(ltri), full(ustr), full(g), full(b),
                      pl.BlockSpec(memory_space=pl.ANY)],
            out_specs=row(D_MODEL),
            scratch_shapes=[pltpu.VMEM((ROWS_LOCAL, D_MODEL), BF16), pltpu.SemaphoreType.DMA((1,))]),
        out_shape=jax.ShapeDtypeStruct((TOKENS, D_MODEL), F32),
        compiler_params=pltpu.CompilerParams(dimension_semantics=("arbitrary",)),
        name="moe_combine",
    )(tbl, wf, x1, ltri, ustr, g, b, y)


def _routing_tables(cnt):
    n_tiles = cnt.shape[0]
    c = cnt[:, 0, :MOE_GROUPS].astype(jnp.int32)
    gran = (c + GRAN - 1) // GRAN
    local = jnp.cumsum(gran, axis=1) - gran
    per_tile = TM_MOE // GRAN
    tiles_g = (jnp.sum(gran, axis=0) + per_tile - 1) // per_tile
    ends = jnp.cumsum(tiles_g)
    base = (ends - tiles_g) * per_tile
    glob = base[None, :] + jnp.cumsum(gran, axis=0) - gran
    tbl = jnp.concatenate([gran, local, glob], axis=1).reshape(n_tiles * TBL_W)
    idx = jnp.arange(ROWS_SORTED // TM_MOE)
    gid = jnp.minimum(jnp.sum(idx[:, None] >= ends[None, :], axis=1), MOE_GROUPS - 1).astype(jnp.int32)
    valid = (idx < ends[-1]).astype(jnp.int32)
    return tbl, gid, valid


def _sort_tables():
    t = jnp.arange(TM_PROJ)
    ltri = (t[None, :] < t[:, None]).astype(BF16)
    l = jnp.arange(LANES)
    ustr = (l[:, None] < l[None, :]).astype(BF16)
    return ltri, ustr


def _dup_halves(t):
    lane = lax.broadcasted_iota(jnp.int32, t.shape, 1)
    r = pltpu.roll(t, HALF, 1)
    return jnp.where(lane < HALF, t, r), jnp.where(lane < HALF, r, t)


def _nsa_in_kernel(x_ref, w_ref, gb_ref, q_ref, kc_ref, vc_ref, ks_ref, vst_ref, kw_ref, vwt_ref, gate_ref):
    h = _dot(x_ref[...].astype(BF16), w_ref[...])
    q_ref[...] = (h[:, 0:C_MIX] * NSA_SCALE).astype(BF16)
    kc_ref[...] = h[:, 1024:1152].astype(BF16)
    vc_ref[...] = h[:, 1152:1280].astype(BF16)
    for idx, ref in ((0, ks_ref), (2, kw_ref)):
        d0, d1 = _dup_halves(h[:, 1280 + idx * LANES:1280 + (idx + 1) * LANES])
        ref[:, 0:LANES] = d0.astype(BF16)
        ref[:, LANES:2 * LANES] = d1.astype(BF16)
    for idx, ref in ((1, vst_ref), (3, vwt_ref)):
        _store_transposed(ref, _dup_halves(h[:, 1280 + idx * LANES:1280 + (idx + 1) * LANES]))
    for g in range(NSA_GROUPS):
        gate_ref[g] = jax.nn.sigmoid(h[:, 1792 + g * LANES:1792 + (g + 1) * LANES] + gb_ref[g])


def _nsa_in(x2, w, gb):
    tm = TM_PROJ
    row = lambda n: pl.BlockSpec((tm, n), lambda i: (i, 0))
    full = lambda a: pl.BlockSpec(a.shape, lambda i: (0,) * a.ndim)
    sd = jax.ShapeDtypeStruct
    vt_spec = pl.BlockSpec((NSA_GROUPS, tm // TK_ATT, LANES, TK_ATT), lambda i: (0, i, 0, 0))
    vt_shape = sd((NSA_GROUPS, TOKENS // TK_ATT, LANES, TK_ATT), BF16)
    return pl.pallas_call(
        _nsa_in_kernel,
        grid=(TOKENS // tm,),
        in_specs=[row(D_MODEL), full(w), full(gb)],
        out_specs=[row(C_MIX), row(LANES), row(LANES), row(2 * LANES), vt_spec, row(2 * LANES), vt_spec,
                   pl.BlockSpec((NSA_GROUPS, tm, LANES), lambda i: (0, i, 0))],
        out_shape=[sd((TOKENS, C_MIX), BF16), sd((TOKENS, LANES), BF16), sd((TOKENS, LANES), BF16),
                   sd((TOKENS, 2 * LANES), BF16), vt_shape, sd((TOKENS, 2 * LANES), BF16), vt_shape,
                   sd((NSA_GROUPS, TOKENS, LANES), F32)],
        compiler_params=pltpu.CompilerParams(dimension_semantics=("arbitrary",)),
        name="nsa_in",
    )(x2, w, gb)


def _compress_kernel(kc_ref, vc_ref, pk_ref, pv_ref, wk1_ref, wv1_ref, wk2_ref, wv2_ref, ko_ref, vo_ref):
    for a_ref, p_ref, w1_ref, w2_ref, o_ref in ((kc_ref, pk_ref, wk1_ref, wk2_ref, ko_ref),
                                                (vc_ref, pv_ref, wv1_ref, wv2_ref, vo_ref)):
        a = a_ref[0].astype(F32)
        a0 = (a + p_ref[0]).astype(BF16)
        a1 = (a + p_ref[1]).astype(BF16)
        outs = []
        for g in range(NSA_GROUPS):
            first = _dot(a0, w1_ref[g])
            second = _dot(a1, w1_ref[NSA_GROUPS + g])
            hid = first + pltpu.roll(second, NSA_NCMP - 1, 0)
            outs.append(_dot(_gelu(hid).astype(BF16), w2_ref[...]))
        if o_ref is ko_ref:
            o_ref[0] = jnp.concatenate(outs, axis=1).astype(BF16)
        else:
            for g in range(NSA_GROUPS):
                o_ref[0, g] = outs[g].T.astype(BF16)


def _compress(kc_r, vc_r, pk, pv, wk1, wv1, wk2, wv2):
    blk = pl.BlockSpec((1, NSA_NCMP, NSA_CMP_STRIDE * LANES), lambda b: (b, 0, 0))
    full = lambda a: pl.BlockSpec(a.shape, lambda b: (0,) * a.ndim)
    sd = jax.ShapeDtypeStruct
    return pl.pallas_call(
        _compress_kernel,
        grid=(BATCH,),
        in_specs=[blk, blk, full(pk), full(pv), full(wk1), full(wv1), full(wk2), full(wv2)],
        out_specs=[pl.BlockSpec((1, NSA_NCMP, 2 * LANES), lambda b: (b, 0, 0)),
                   pl.BlockSpec((1, NSA_GROUPS, LANES, NSA_NCMP), lambda b: (b, 0, 0, 0))],
        out_shape=[sd((BATCH, NSA_NCMP, 2 * LANES), BF16), sd((BATCH, NSA_GROUPS, LANES, NSA_NCMP), BF16)],
        compiler_params=pltpu.CompilerParams(dimension_semantics=("arbitrary",)),
        name="nsa_compress",
    )(kc_r, vc_r, pk, pv, wk1, wv1, wk2, wv2)


def _nsa_attn_kernel(q_ref, kc_ref, vct_ref, ks_ref, vst_ref, kw_ref, vwt_ref, gate_ref, cover_ref,
                     o_ref, sel_ref, m_ref, l_ref, acc_ref):
    tq, tk, hpg = TQ_NSA, TK_ATT, NSA_HPG
    qi = pl.program_id(2)
    q0 = qi * tq
    lane = lax.broadcasted_iota(jnp.int32, (tq, LANES), 1)
    t_tok = q0 + lax.broadcasted_iota(jnp.int32, (1, tq), 1)
    head = lambda x, i: x[:, i * tq:(i + 1) * tq]
    heads = range(hpg)

    parts = []
    for p in range(hpg // 2):
        qp = q_ref[0, :, p * LANES:(p + 1) * LANES]
        zero = jnp.zeros_like(qp)
        parts.append(jnp.where(lane < HALF, qp, zero))
        parts.append(jnp.where(lane < HALF, zero, qp))
    qs = jnp.concatenate(parts, axis=0)

    s_c = _dot_nt(kc_ref[0], qs)
    n_sub = lax.broadcasted_iota(jnp.int32, (NSA_NCMP, 1), 0)
    vis = t_tok >= n_sub * NSA_CMP_STRIDE + (NSA_CMP_LEN - 1)
    p_sum = jnp.zeros((NSA_NCMP, tq), F32)
    p_c = []
    for i in heads:
        sm = jnp.where(vis, head(s_c, i), NEG)
        e = jnp.exp(sm - jnp.max(sm, axis=0, keepdims=True))
        p = jnp.where(vis, e / jnp.sum(e, axis=0, keepdims=True), 0.0)
        p_sum = p_sum + p
        p_c.append(p.astype(BF16))
    o_c = _dot(vct_ref[0, 0], jnp.concatenate(p_c, axis=1))

    imp = jnp.dot(cover_ref[...], p_sum, preferred_element_type=F32,
                  precision=lax.Precision.HIGHEST)[0:NSA_NSEL]
    jj = lax.broadcasted_iota(jnp.int32, (NSA_NSEL, 1), 0)
    tb = t_tok // NSA_SEL_LEN
    forced = (jj == 0) | (jj == tb) | (jj == tb - 1)
    score = jnp.where(forced, NSA_FORCE, jnp.where(jj <= tb, imp, -NSA_FORCE))
    rank = jnp.zeros((NSA_NSEL, tq), jnp.int32)
    for i in range(NSA_NSEL):
        si = score[i:i + 1, :]
        beats = (si > score) | ((si == score) & (jj > i))
        rank = rank + beats.astype(jnp.int32)
    sel_ref[...] = jnp.where(rank < NSA_TOPK, 1.0, 0.0)

    m_ref[...] = jnp.full(m_ref.shape, NEG, F32)
    l_ref[...] = jnp.zeros(l_ref.shape, F32)
    acc_ref[...] = jnp.zeros(acc_ref.shape, F32)
    k_sub = lax.broadcasted_iota(jnp.int32, (tk, 1), 0)
    blocks_per_tile = tk // NSA_SEL_LEN

    def sel_step(kt, c):
        r0 = pl.multiple_of(kt * tk, tk)
        s = _dot_nt(ks_ref[0, pl.ds(r0, tk), :], qs)
        member = jnp.concatenate(
            [jnp.broadcast_to(sel_ref[pl.ds(kt * blocks_per_tile + a, 1), :], (NSA_SEL_LEN, tq))
             for a in range(blocks_per_tile)], axis=0)
        mask = (member > 0.5) & (r0 + k_sub <= t_tok)
        probs, alphas = [], []
        for i in heads:
            sm = jnp.where(mask, head(s, i), NEG)
            m_old = m_ref[i]
            m_new = jnp.maximum(m_old, jnp.max(sm, axis=0, keepdims=True))
            a = jnp.exp(m_old - m_new)
            p = jnp.exp(sm - m_new)
            l_ref[i] = a * l_ref[i] + jnp.sum(p, axis=0, keepdims=True)
            m_ref[i] = m_new
            probs.append(p.astype(BF16))
            alphas.append(a)
        pv = _dot(vst_ref[0, kt], jnp.concatenate(probs, axis=1))
        acc_ref[...] = jnp.concatenate(alphas, axis=1) * acc_ref[...] + pv
        return c

    lax.fori_loop(0, (q0 + tq - 1) // tk + 1, sel_step, 0)

    c0 = jnp.maximum(q0 + tq - WIN_CHUNKS * tk, 0) // tk
    s_w, in_win = [], []
    for c in range(WIN_CHUNKS):
        r0 = pl.multiple_of((c0 + c) * tk, tk)
        s_w.append(_dot_nt(kw_ref[0, pl.ds(r0, tk), :], qs))
        kpos = r0 + k_sub
        in_win.append((kpos <= t_tok) & (kpos > t_tok - NSA_WINDOW))
    e_w = [[] for _ in range(WIN_CHUNKS)]
    l_w = []
    for i in heads:
        sm = [jnp.where(in_win[c], head(s_w[c], i), NEG) for c in range(WIN_CHUNKS)]
        m = functools.reduce(jnp.maximum, [jnp.max(x, axis=0, keepdims=True) for x in sm])
        e = [jnp.exp(x - m) for x in sm]
        l_w.append(functools.reduce(jnp.add, [jnp.sum(x, axis=0, keepdims=True) for x in e]))
        for c in range(WIN_CHUNKS):
            e_w[c].append(e[c].astype(BF16))
    o_w = functools.reduce(jnp.add, [_dot(vwt_ref[0, c0 + c], jnp.concatenate(e_w[c], axis=1))
                                     for c in range(WIN_CHUNKS)])

    gt = gate_ref[0, 0].T
    top = lax.broadcasted_iota(jnp.int32, (LANES, tq), 0) < HALF
    outs = []
    for i in heads:
        outs.append(gt[i:i + 1] * head(o_c, i)
                    + gt[hpg + i:hpg + i + 1] / l_ref[i] * head(acc_ref[...], i)
                    + gt[2 * hpg + i:2 * hpg + i + 1] / l_w[i] * head(o_w, i))
    for p in range(hpg // 2):
        pair = jnp.where(top, outs[2 * p], outs[2 * p + 1])
        o_ref[0, :, p * LANES:(p + 1) * LANES] = pair.T.astype(BF16)


def _nsa_attn(q3, kc2, vct, ks3, vst, kw3, vwt, gates4, cover_t):
    tq = TQ_NSA
    half_w = NSA_HPG * NSA_DH
    n_chunks = SEQ // TK_ATT
    kv = pl.BlockSpec((1, SEQ, LANES), lambda b, g, i: (b, 0, g))
    vt = pl.BlockSpec((1, n_chunks, LANES, TK_ATT), lambda b, g, i: (g, b, 0, 0))
    full = lambda a: pl.BlockSpec(a.shape, lambda b, g, i: (0,) * a.ndim)
    return pl.pallas_call(
        _nsa_attn_kernel,
        grid=(BATCH, NSA_GROUPS, SEQ // tq),
        in_specs=[pl.BlockSpec((1, tq, half_w), lambda b, g, i: (b, i, g)),
                  pl.BlockSpec((1, NSA_NCMP, LANES), lambda b, g, i: (b, 0, g)),
                  pl.BlockSpec((1, 1, LANES, NSA_NCMP), lambda b, g, i: (b, g, 0, 0)),
                  kv, vt, kv, vt,
                  pl.BlockSpec((1, 1, tq, LANES), lambda b, g, i: (g, b, i, 0)),
                  full(cover_t)],
        out_specs=pl.BlockSpec((1, tq, half_w), lambda b, g, i: (b, i, g)),
        out_shape=jax.ShapeDtypeStruct((BATCH, SEQ, C_MIX), BF16),
        scratch_shapes=[pltpu.VMEM((NSA_NSEL, tq), F32), pltpu.VMEM((NSA_HPG, 1, tq), F32),
                        pltpu.VMEM((NSA_HPG, 1, tq), F32), pltpu.VMEM((LANES, NSA_HPG * tq), F32)],
        compiler_params=pltpu.CompilerParams(dimension_semantics=("arbitrary",) * 3),
        name="nsa_attn",
    )(q3, kc2, vct, ks3, vst, kw3, vwt, gates4, cover_t)


def _rope_tables():
    half = MLA_ROPE // 2
    freq = jnp.exp(-math.log(ROPE_BASE) * jnp.arange(half, dtype=F32) / half)
    zeros64 = jnp.zeros((MLA_NOPE,), F32)
    zeros32 = jnp.zeros((LANES - MLA_NOPE - MLA_ROPE,), F32)
    fc = jnp.concatenate([zeros64, freq, freq, zeros32])[None, :]
    sg = jnp.concatenate([zeros64, -jnp.ones((half,), F32), jnp.ones((half,), F32), zeros32])[None, :]
    return fc, sg


def _swap_halves(w):
    half = w.shape[-1] // 2
    return jnp.concatenate([w[..., half:], w[..., :half]], axis=-1)


def _pad_last(w, n):
    return jnp.pad(w, [(0, 0)] * (w.ndim - 1) + [(0, n - w.shape[-1])])


def _ab_weights(w_in, w_uq, w_uk, w_uv):
    w_kr = w_in[:, 1408:1440]
    place = lambda w: jnp.pad(w, ((0, 0), (MLA_NOPE, LANES - MLA_NOPE - MLA_ROPE)))
    win = jnp.concatenate([w_in[:, :1408], place(w_kr), place(_swap_halves(w_kr))], axis=1).astype(BF16)
    uq = w_uq.reshape(MLA_Q_RANK, MLA_HEADS, MLA_NOPE + MLA_ROPE)
    nope, rp = uq[..., :MLA_NOPE], uq[..., MLA_NOPE:]
    q_pad = _pad_last(jnp.concatenate([nope, rp], -1), LANES).reshape(MLA_Q_RANK, MLA_HEADS * LANES)
    q_sw = _pad_last(jnp.concatenate([jnp.zeros_like(nope), _swap_halves(rp)], -1), LANES)
    wq = jnp.concatenate([q_pad, q_sw.reshape(MLA_Q_RANK, MLA_HEADS * LANES)], axis=1).astype(BF16)
    k_pad = _pad_last(w_uk.reshape(MLA_KV_RANK, MLA_HEADS, MLA_NOPE), LANES).reshape(MLA_KV_RANK, -1)
    wkv = jnp.concatenate([k_pad, w_uv], axis=1).astype(BF16)
    return win, wq, wkv


def _router_weights(w_rg, b_rg, w_re, b_re):
    wr = _pad_last(jnp.concatenate([w_re, w_rg], axis=1), LANES)
    wr_hi = wr.astype(BF16)
    wr_lo = (wr - wr_hi.astype(F32)).astype(BF16)
    br = _pad_last(jnp.concatenate([b_re, b_rg])[None, :], LANES)
    return jnp.stack([wr_hi, wr_lo]), br


def _nsa_in_weights(w_in, gate_b):
    g_cols = w_in[:, C_MIX + 768:].reshape(D_MODEL, 3, NSA_GROUPS, NSA_HPG)
    g_blocks = [_pad_last(g_cols[:, :, g, :].reshape(D_MODEL, 3 * NSA_HPG), LANES) for g in range(NSA_GROUPS)]
    w = jnp.concatenate([w_in[:, :C_MIX + 768]] + g_blocks, axis=1).astype(BF16)
    gb = gate_b.reshape(3, NSA_GROUPS, NSA_HPG)
    gb = jnp.stack([_pad_last(gb[:, g, :].reshape(1, 3 * NSA_HPG), LANES) for g in range(NSA_GROUPS)])
    return w, gb


def _compress_weights(pos, w1, w2):
    w1r = w1.reshape(2, NSA_CMP_STRIDE, NSA_DH, NSA_CMP_HIDDEN)
    zero = jnp.zeros_like(w1r)
    per_g = []
    for g in range(NSA_GROUPS):
        parts = [w1r if gg == g else zero for gg in range(NSA_GROUPS)]
        per_g.append(jnp.stack(parts, axis=2).reshape(2, NSA_CMP_STRIDE * LANES, NSA_CMP_HIDDEN))
    w1x = jnp.stack(per_g, axis=1).reshape(2 * NSA_GROUPS, NSA_CMP_STRIDE * LANES, NSA_CMP_HIDDEN)
    posr = pos.reshape(2, NSA_CMP_STRIDE, 1, NSA_DH)
    posx = jnp.broadcast_to(posr, (2, NSA_CMP_STRIDE, NSA_GROUPS, NSA_DH)).reshape(2, 1, NSA_CMP_STRIDE * LANES)
    w2x = jnp.concatenate([w2, w2], axis=1)
    return posx, w1x.astype(BF16), w2x.astype(BF16)


def _selection_tables():
    n = jnp.arange(LANES)[:, None]
    j = jnp.arange(LANES)[None, :]
    c0 = n * NSA_CMP_STRIDE
    s0 = j * NSA_SEL_LEN
    cover = ((c0 < s0 + NSA_SEL_LEN) & (c0 + NSA_CMP_LEN > s0) & (n < NSA_NCMP - 1) & (j < NSA_NSEL))
    return jnp.transpose(cover).astype(F32)


def kernel(x, positions, ab_w_in, ab_gm_ln_g, ab_gm_ln_b, ab_gm_ws, ab_gm_bs, ab_mla_q_norm,
           ab_mla_kv_norm, ab_mla_w_uq, ab_mla_w_uk, ab_mla_w_uv, ab_w_o, c_w_in, c_cmp_pos, c_w_ck1,
           c_w_ck2, c_w_cv1, c_w_cv2, c_gate_b, c_w_o, moe_w_rg, moe_b_rg, moe_w_re, moe_b_re,
           moe_w_gate, moe_w_up, moe_w_down, ln1_g, ln1_b, ln2_g, ln2_b):
    x2 = x.reshape(TOKENS, D_MODEL)
    pos2 = positions.reshape(TOKENS, 1)
    vec = lambda a: a[None, :]

    ltri, ustr = _sort_tables()

    def moe_layer(layer, x1b, x1, wf, cnt):
        tbl, gid, valid = _routing_tables(cnt)
        xs, ws = _dispatch(tbl, x1b, wf, ltri, ustr)
        y = _experts(gid, valid, xs, ws, moe_w_gate[layer].astype(BF16), moe_w_up[layer].astype(BF16),
                     moe_w_down[layer].astype(BF16))
        return _combine(tbl, wf, x1, ltri, ustr, vec(ln2_g[layer]), vec(ln2_b[layer]), y)

    win, wq, wkv = _ab_weights(ab_w_in[0], ab_mla_w_uq[0], ab_mla_w_uk[0], ab_mla_w_uv[0])
    fc, sg = _rope_tables()
    gu, vn, q, k, vt = _ab_in(x2, pos2, win, vec(ab_gm_ln_g[0]), vec(ab_gm_ln_b[0]), vec(ab_mla_q_norm[0]),
                              vec(ab_mla_kv_norm[0]), wq, wkv, fc, sg)
    yb = _mla_attn(q.reshape(BATCH, SEQ, -1), k.reshape(BATCH, SEQ, -1), vt)
    wr, br = _router_weights(moe_w_rg[0], moe_b_rg[0], moe_w_re[0], moe_b_re[0])
    x1, x1b, wf, cnt = _mix_out(x2, yb.reshape(TOKENS, -1), ab_w_o[0].astype(BF16), vec(ln1_g[0]),
                                vec(ln1_b[0]), wr, br, gm=(gu, vn, ab_gm_ws[0], jnp.transpose(ab_gm_bs[0])))
    x2 = moe_layer(0, x1b, x1, wf, cnt)

    w_nsa, gb = _nsa_in_weights(c_w_in[0], c_gate_b[0])
    q, kc, vc, ks, vst, kw, vwt, gates = _nsa_in(x2, w_nsa, gb)
    pk, wk1, wk2 = _compress_weights(c_cmp_pos[0, 0], c_w_ck1[0], c_w_ck2[0])
    pv, wv1, wv2 = _compress_weights(c_cmp_pos[0, 1], c_w_cv1[0], c_w_cv2[0])
    blocks = lambda a: a.reshape(BATCH, NSA_NCMP, NSA_CMP_STRIDE * LANES)
    kc2, vct = _compress(blocks(kc), blocks(vc), pk, pv, wk1, wv1, wk2, wv2)
    b3 = lambda a: a.reshape(BATCH, SEQ, -1)
    o = _nsa_attn(b3(q), kc2, vct, b3(ks), vst, b3(kw), vwt,
                  gates.reshape(NSA_GROUPS, BATCH, SEQ, LANES), _selection_tables())
    wr, br = _router_weights(moe_w_rg[1], moe_b_rg[1], moe_w_re[1], moe_b_re[1])
    x1, x1b, wf, cnt = _mix_out(x2, o.reshape(TOKENS, -1), c_w_o[0].astype(BF16), vec(ln1_g[1]), vec(ln1_b[1]),
                                wr, br)
    x2 = moe_layer(1, x1b, x1, wf, cnt)
    return x2.reshape(BATCH, SEQ, D_MODEL)
```

```python
import functools
import math

import jax
import jax.numpy as jnp
from jax import lax
from jax.experimental import pallas as pl
from jax.experimental.pallas import tpu as pltpu

F32 = jnp.float32
BF16 = jnp.bfloat16

D_MODEL = 1024
BATCH = 16
SEQ = 2048
TOKENS = BATCH * SEQ
DEPTH = 2
DN_ALPHA = (2.0 * DEPTH) ** 0.25
LN_EPS = 1e-5
NEG = -1e30
LOG2E = math.log2(math.e)
LANES = 128
HALF = LANES // 2

GM_WIDTH = 512
GM_GROUPS = 4
GM_CHUNK = 128

MLA_HEADS = 8
MLA_NOPE = 64
MLA_ROPE = 32
MLA_V = 64
MLA_Q_RANK = 256
MLA_KV_RANK = 128
ROPE_BASE = 10000.0
MLA_SCALE = (MLA_NOPE + MLA_ROPE) ** -0.5

NSA_HEADS = 16
NSA_GROUPS = 2
NSA_HPG = 8
NSA_DH = 64
NSA_CMP_LEN = 32
NSA_CMP_STRIDE = 16
NSA_CMP_HIDDEN = 256
NSA_SEL_LEN = 64
NSA_TOPK = 8
NSA_WINDOW = 512
NSA_NSEL = SEQ // NSA_SEL_LEN
NSA_NCMP = SEQ // NSA_CMP_STRIDE
NSA_FORCE = 1e4
NSA_SCALE = NSA_DH ** -0.5
C_MIX = NSA_HEADS * NSA_DH

MOE_GROUPS = 4
MOE_EPG = 8
MOE_EXPERTS = 32
MOE_HIDDEN = 256

TM_PROJ = 512
TQ_MLA = 256
TQ_NSA = 256
TK_ATT = 256
WIN_CHUNKS = (NSA_WINDOW + TQ_NSA) // TK_ATT
MLA_VT_ROWS = LANES + 16
TM_MOE = 512
EXPERTS_VMEM_BYTES = (2 * 3 * MOE_EPG * D_MODEL * MOE_HIDDEN * 2 + 2 * 2 * TM_MOE * D_MODEL * 2
                      + 2 * MOE_EPG * TM_MOE * MOE_HIDDEN * 4 + TM_MOE * MOE_EPG * MOE_HIDDEN * 2
                      + 2 * TM_MOE * D_MODEL * 4)

GSEL_LANE = MOE_EXPERTS
GRAN = 16
TBL_W = 3 * MOE_GROUPS
ROWS_LOCAL = 640
ROWS_SORTED = TM_MOE * (TOKENS // TM_MOE + MOE_GROUPS
                        + -(-(TOKENS // TM_PROJ) * MOE_GROUPS * (GRAN - 1) // TM_MOE))


def _dot(a, b):
    return jnp.dot(a, b, preferred_element_type=F32)


def _dot_nt(a, b):
    return lax.dot_general(a, b, (((1,), (1,)), ((), ())), preferred_element_type=F32)


def _gelu(x):
    return 0.5 * x * (1.0 + jnp.tanh(math.sqrt(2.0 / math.pi) * (x + 0.044715 * (x * x * x))))


def _layer_norm(x, g, b):
    mu = jnp.mean(x, axis=-1, keepdims=True)
    xc = x - mu
    var = jnp.mean(xc * xc, axis=-1, keepdims=True)
    return xc * lax.rsqrt(var + LN_EPS) * g + b


def _rms_norm(x, g):
    return x * lax.rsqrt(jnp.mean(x * x, axis=-1, keepdims=True) + LN_EPS) * g


def _store_transposed(ref, blocks):
    extra = ref.shape[2] - LANES
    if extra:
        ones_rows = jnp.where(lax.broadcasted_iota(jnp.int32, (extra, TK_ATT), 0) == 0, 1.0, 0.0)
    for n, blk in enumerate(blocks):
        t = blk.T
        for c in range(t.shape[1] // TK_ATT):
            chunk = t[:, c * TK_ATT:(c + 1) * TK_ATT]
            if extra:
                chunk = jnp.concatenate([chunk, ones_rows], axis=0)
            ref[n, c] = chunk.astype(ref.dtype)


def _ab_in_kernel(x_ref, pos_ref, win_ref, lng_ref, lnb_ref, qg_ref, kvg_ref, wq_ref, wkv_ref,
                  fc_ref, sg_ref, gu_ref, vn_ref, q_ref, k_ref, vt_ref):
    h = _dot(x_ref[...].astype(BF16), win_ref[...])
    gu_ref[...] = _gelu(h[:, 0:512]).astype(BF16)
    vn_ref[...] = _layer_norm(_gelu(h[:, 512:1024]), lng_ref[...], lnb_ref[...]).astype(BF16)

    ang = pos_ref[...].astype(F32) * fc_ref[...]
    cc = jnp.cos(ang)
    ss = jnp.sin(ang) * sg_ref[...]

    cqn = _rms_norm(h[:, 1024:1280], qg_ref[...]).astype(BF16)
    qq = _dot(cqn, wq_ref[...])
    for hd in range(MLA_HEADS):
        lo, hi = hd * LANES, (hd + 1) * LANES
        q_ref[:, lo:hi] = ((qq[:, lo:hi] * cc + qq[:, 1024 + lo:1024 + hi] * ss) * (MLA_SCALE * LOG2E)).astype(BF16)

    ckvn = _rms_norm(h[:, 1280:1408], kvg_ref[...]).astype(BF16)
    kv = _dot(ckvn, wkv_ref[...])
    k_rope = h[:, 1408:1536] * cc + h[:, 1536:1664] * ss
    for hd in range(MLA_HEADS):
        lo, hi = hd * LANES, (hd + 1) * LANES
        k_ref[:, lo:hi] = (kv[:, lo:hi] + k_rope).astype(BF16)
    _store_transposed(vt_ref, [kv[:, 1024 + p * LANES:1024 + (p + 1) * LANES] for p in range(MLA_HEADS // 2)])


def _ab_in(x2, pos2, win, lng, lnb, qg, kvg, wq, wkv, fc, sg):
    tm = TM_PROJ
    row = lambda n: pl.BlockSpec((tm, n), lambda i: (i, 0))
    full = lambda a: pl.BlockSpec(a.shape, lambda i: (0,) * a.ndim)
    return pl.pallas_call(
        _ab_in_kernel,
        grid=(TOKENS // tm,),
        in_specs=[row(D_MODEL), row(1), full(win), full(lng), full(lnb), full(qg), full(kvg),
                  full(wq), full(wkv), full(fc), full(sg)],
        out_specs=[row(512), row(512), row(1024), row(1024),
                   pl.BlockSpec((MLA_HEADS // 2, tm // TK_ATT, MLA_VT_ROWS, TK_ATT), lambda i: (0, i, 0, 0))],
        out_shape=[jax.ShapeDtypeStruct((TOKENS, 512), BF16), jax.ShapeDtypeStruct((TOKENS, 512), BF16),
                   jax.ShapeDtypeStruct((TOKENS, 1024), BF16), jax.ShapeDtypeStruct((TOKENS, 1024), BF16),
                   jax.ShapeDtypeStruct((MLA_HEADS // 2, TOKENS // TK_ATT, MLA_VT_ROWS, TK_ATT), BF16)],
        compiler_params=pltpu.CompilerParams(dimension_semantics=("arbitrary",)),
        name="ab_in",
    )(x2, pos2, win, lng, lnb, qg, kvg, wq, wkv, fc, sg)


def _mla_attn_kernel(q_ref, k_ref, vt_ref, o_ref, m_ref, l_ref, acc_ref):
    tq, tk = TQ_MLA, TK_ATT
    qi = pl.program_id(1)
    krow = lax.broadcasted_iota(jnp.int32, (tk, tq), 0)
    qcol = lax.broadcasted_iota(jnp.int32, (tk, tq), 1)
    top = lax.broadcasted_iota(jnp.int32, (LANES, tq), 0) < HALF
    m_ref[...] = jnp.full(m_ref.shape, NEG, F32)
    l_ref[...] = jnp.zeros(l_ref.shape, F32)
    acc_ref[...] = jnp.zeros(acc_ref.shape, F32)

    def tile(j, masked):
        r0 = pl.multiple_of(j * tk, tk)
        scores = []
        for h in range(MLA_HEADS):
            s = _dot_nt(k_ref[0, pl.ds(r0, tk), h * LANES:(h + 1) * LANES], q_ref[0, :, h * LANES:(h + 1) * LANES])
            scores.append(jnp.where(krow <= qcol, s, NEG) if masked else s)
        probs, alphas = [], []
        for h in range(MLA_HEADS):
            m_old = m_ref[h]
            m_new = jnp.maximum(m_old, jnp.max(scores[h], axis=0, keepdims=True))
            alphas.append(jnp.exp2(m_old - m_new))
            probs.append(jnp.exp2(scores[h] - m_new).astype(BF16))
            m_ref[h] = m_new
        for pr in range(MLA_HEADS // 2):
            pv = _dot(vt_ref[pr, j], jnp.concatenate([probs[2 * pr], probs[2 * pr + 1]], axis=1))
            for hh, cols in ((2 * pr, slice(0, tq)), (2 * pr + 1, slice(tq, 2 * tq))):
                l_ref[hh] = alphas[hh] * l_ref[hh] + pv[LANES:LANES + 1, cols]
            a = jnp.where(top, alphas[2 * pr], alphas[2 * pr + 1])
            acc_ref[pr] = a * acc_ref[pr] + jnp.where(top, pv[:LANES, :tq], pv[:LANES, tq:])

    def body(j, c):
        tile(j, False)
        return c

    lax.fori_loop(0, qi, body, 0)
    tile(qi, True)
    for pr in range(MLA_HEADS // 2):
        l = jnp.where(top, l_ref[2 * pr], l_ref[2 * pr + 1])
        o_ref[0, :, pr * LANES:(pr + 1) * LANES] = (acc_ref[pr] / l).T.astype(BF16)


def _mla_attn(q3, k3, vt):
    tq = TQ_MLA
    n_chunks = SEQ // TK_ATT
    return pl.pallas_call(
        _mla_attn_kernel,
        grid=(BATCH, SEQ // tq),
        in_specs=[pl.BlockSpec((1, tq, MLA_HEADS * LANES), lambda b, i: (b, i, 0)),
                  pl.BlockSpec((1, SEQ, MLA_HEADS * LANES), lambda b, i: (b, 0, 0)),
                  pl.BlockSpec((MLA_HEADS // 2, n_chunks, MLA_VT_ROWS, TK_ATT), lambda b, i: (0, b, 0, 0))],
        out_specs=pl.BlockSpec((1, tq, MLA_HEADS * MLA_V), lambda b, i: (b, i, 0)),
        out_shape=jax.ShapeDtypeStruct((BATCH, SEQ, MLA_HEADS * MLA_V), BF16),
        scratch_shapes=[pltpu.VMEM((MLA_HEADS, 1, tq), F32), pltpu.VMEM((MLA_HEADS, 1, tq), F32),
                        pltpu.VMEM((MLA_HEADS // 2, LANES, tq), F32)],
        compiler_params=pltpu.CompilerParams(dimension_semantics=("arbitrary",) * 2),
        name="mla_attn",
    )(q3, k3, vt)


def _router(x1, wr, br):
    tm = x1.shape[0]
    x_hi = x1.astype(BF16)
    x_lo = (x1 - x_hi.astype(F32)).astype(BF16)
    logits = _dot(x_hi, wr[0]) + (_dot(x_lo, wr[0]) + _dot(x_hi, wr[1])) + br
    lane = lax.broadcasted_iota(jnp.int32, (tm, LANES), 1).astype(F32)
    big = 1e6
    is_g = (lane >= MOE_EXPERTS) & (lane < MOE_EXPERTS + MOE_GROUPS)
    gl = jnp.where(is_g, logits, NEG)
    gmax = jnp.max(gl, axis=-1, keepdims=True)
    g_sel = jnp.min(jnp.where(is_g & (gl == gmax), lane, big), axis=-1, keepdims=True) - MOE_EXPERTS
    g_w = 1.0 / jnp.sum(jnp.where(is_g, jnp.exp(gl - gmax), 0.0), axis=-1, keepdims=True)
    in_grp = (lane >= g_sel * MOE_EPG) & (lane < (g_sel + 1) * MOE_EPG)
    el = jnp.where(in_grp, logits, NEG)
    emax = jnp.max(el, axis=-1, keepdims=True)
    ee = jnp.where(in_grp, jnp.exp(el - emax), 0.0)
    pe = ee / jnp.sum(ee, axis=-1, keepdims=True)
    p1 = jnp.max(pe, axis=-1, keepdims=True)
    i1 = jnp.min(jnp.where(in_grp & (pe == p1), lane, big), axis=-1, keepdims=True)
    rest = in_grp & (lane != i1)
    pr = jnp.where(rest, pe, -1.0)
    p2 = jnp.max(pr, axis=-1, keepdims=True)
    i2 = jnp.min(jnp.where(rest & (pr == p2), lane, big), axis=-1, keepdims=True)
    tot = p1 + p2
    wf = jnp.where(lane == i1, p1 / tot * g_w, jnp.where(lane == i2, p2 / tot * g_w, 0.0))
    wf = jnp.where(lane == GSEL_LANE, g_sel, wf)
    cnt = jnp.sum(jnp.where(lane == g_sel, 1.0, 0.0), axis=0, keepdims=True)
    return wf, cnt


def _mix_out_kernel(*refs, gmlp):
    if gmlp:
        (x_ref, gu_ref, vn_ref, ws_ref, bs_ref, yb_ref, wo_ref, g_ref, b_ref, wr_ref, br_ref,
         x1_ref, x1b_ref, wf_ref, cnt_ref, ya_ref) = refs
        tm = x_ref.shape[0]
        r = lax.broadcasted_iota(jnp.int32, (GM_CHUNK, GM_CHUNK), 0)
        c = lax.broadcasted_iota(jnp.int32, (GM_CHUNK, GM_CHUNK), 1)
        for g in range(GM_GROUPS):
            ws = jnp.where(r >= c, ws_ref[g], 0.0).astype(BF16)
            bias = bs_ref[:, g:g + 1]
            for ch in range(tm // GM_CHUNK):
                rows = slice(ch * GM_CHUNK, (ch + 1) * GM_CHUNK)
                cols = slice(g * LANES, (g + 1) * LANES)
                s = _dot(ws, vn_ref[rows, cols]) + bias
                ya_ref[rows, cols] = (gu_ref[rows, cols].astype(F32) * s).astype(BF16)
        mix = _dot(ya_ref[...], wo_ref[0:GM_WIDTH, :]) + _dot(yb_ref[...], wo_ref[GM_WIDTH:, :])
    else:
        x_ref, y_ref, wo_ref, g_ref, b_ref, wr_ref, br_ref, x1_ref, x1b_ref, wf_ref, cnt_ref = refs
        mix = _dot(y_ref[...], wo_ref[...])
    x1 = _layer_norm(DN_ALPHA * x_ref[...] + mix, g_ref[...], b_ref[...])
    x1_ref[...] = x1
    x1b_ref[...] = x1.astype(BF16)
    wf, cnt = _router(x1, wr_ref[...], br_ref[...])
    wf_ref[...] = wf
    cnt_ref[0] = jnp.broadcast_to(cnt, cnt_ref.shape[1:])


def _mix_out(x2, ys, wo, g, b, wr, br, gm=None):
    tm = TM_PROJ
    row = lambda n: pl.BlockSpec((tm, n), lambda i: (i, 0))
    full = lambda a: pl.BlockSpec(a.shape, lambda i: (0,) * a.ndim)
    if gm is not None:
        gu, vn, ws, bs = gm
        args = (x2, gu, vn, ws, bs, ys, wo, g, b, wr, br)
        in_specs = [row(D_MODEL), row(512), row(512), full(ws), full(bs), row(512), full(wo),
                    full(g), full(b), full(wr), full(br)]
        scratch = [pltpu.VMEM((tm, GM_WIDTH), BF16)]
    else:
        args = (x2, ys, wo, g, b, wr, br)
        in_specs = [row(D_MODEL), row(C_MIX), full(wo), full(g), full(b), full(wr), full(br)]
        scratch = []
    return pl.pallas_call(
        functools.partial(_mix_out_kernel, gmlp=gm is not None),
        grid=(TOKENS // tm,),
        in_specs=in_specs,
        out_specs=[row(D_MODEL), row(D_MODEL), row(LANES), pl.BlockSpec((1, 8, LANES), lambda i: (i, 0, 0))],
        out_shape=[jax.ShapeDtypeStruct((TOKENS, D_MODEL), F32), jax.ShapeDtypeStruct((TOKENS, D_MODEL), BF16),
                   jax.ShapeDtypeStruct((TOKENS, LANES), F32),
                   jax.ShapeDtypeStruct((TOKENS // tm, 8, LANES), F32)],
        scratch_shapes=scratch,
        compiler_params=pltpu.CompilerParams(dimension_semantics=("arbitrary",)),
        name="mix_out_gmlp" if gm is not None else "mix_out",
    )(*args)


def _group_dest(wf, ltri_ref, ustr_ref):
    tm = wf.shape[0]
    lane = lax.broadcasted_iota(jnp.int32, (tm, LANES), 1).astype(F32)
    onehot = jnp.where(lane == wf[:, GSEL_LANE:GSEL_LANE + 1], 1.0, 0.0)
    before = _dot(ltri_ref[...], onehot.astype(BF16))
    cnt = jnp.sum(onehot, axis=0, keepdims=True)
    gran = jnp.floor((cnt + (GRAN - 1)) * (1.0 / GRAN))
    start = _dot(jnp.broadcast_to(gran, (8, LANES)).astype(BF16), ustr_ref[...])[0:1]
    return jnp.sum(onehot * (GRAN * start + before), axis=-1, keepdims=True)


def _granule_copies(tbl_ref, tile, vmem_bufs, hbm_refs, sems, to_hbm, act):
    for g in range(MOE_GROUPS):
        n = tbl_ref[tile * TBL_W + g]
        loc = tbl_ref[tile * TBL_W + MOE_GROUPS + g]
        glb = tbl_ref[tile * TBL_W + 2 * MOE_GROUPS + g]

        def body(k, c, loc=loc, glb=glb):
            lo = pl.multiple_of((loc + k) * GRAN, GRAN)
            hi = pl.multiple_of((glb + k) * GRAN, GRAN)
            for idx, (vb, hb) in enumerate(zip(vmem_bufs, hbm_refs)):
                v_sl, h_sl = vb.at[pl.ds(lo, GRAN)], hb.at[pl.ds(hi, GRAN)]
                src, dst = (v_sl, h_sl) if to_hbm else (h_sl, v_sl)
                act(pltpu.make_async_copy(src, dst, sems[idx]))
            return c

        lax.fori_loop(0, n, body, 0)


def _dispatch_kernel(tbl_ref, xb_ref, wf_ref, ltri_ref, ustr_ref, xs_in, ws_in, xs_out, ws_out, xbuf, wbuf, sems):
    del xs_in, ws_in
    tile = pl.program_id(0)
    slot = tile % 2
    tm = xb_ref.shape[0]
    wf = wf_ref[...]
    dest = _group_dest(wf, ltri_ref, ustr_ref)
    dest_row = jnp.broadcast_to(dest, (tm, LANES)).T[0:1]
    r = lax.broadcasted_iota(jnp.int32, (ROWS_LOCAL, tm), 0).astype(F32)
    perm = jnp.where(r == dest_row, 1.0, 0.0).astype(BF16)
    xbuf[slot] = _dot(perm, xb_ref[...]).astype(BF16)
    hi = wf.astype(BF16)
    r1 = wf - hi.astype(F32)
    mid = r1.astype(BF16)
    lo = (r1 - mid.astype(F32)).astype(BF16)
    wbuf[slot] = _dot(perm, hi) + _dot(perm, mid) + _dot(perm, lo)

    def copies(t, s, act):
        _granule_copies(tbl_ref, t, (xbuf.at[s], wbuf.at[s]), (xs_out, ws_out), (sems.at[s, 0], sems.at[s, 1]),
                        True, act)

    copies(tile, slot, lambda c: c.start())

    @pl.when(tile > 0)
    def _():
        copies(tile - 1, 1 - slot, lambda c: c.wait())

    @pl.when(tile == pl.num_programs(0) - 1)
    def _():
        copies(tile, slot, lambda c: c.wait())


def _dispatch(tbl, x1b, wf, ltri, ustr):
    tm = TM_PROJ
    row = lambda n: pl.BlockSpec((tm, n), lambda i, t: (i, 0))
    full = lambda a: pl.BlockSpec(a.shape, lambda i, t: (0,) * a.ndim)
    anyspace = pl.BlockSpec(memory_space=pl.ANY)
    xs0 = jnp.zeros((ROWS_SORTED, D_MODEL), BF16)
    ws0 = jnp.zeros((ROWS_SORTED, LANES), F32)
    return pl.pallas_call(
        _dispatch_kernel,
        grid_spec=pltpu.PrefetchScalarGridSpec(
            num_scalar_prefetch=1, grid=(TOKENS // tm,),
            in_specs=[row(D_MODEL), row(LANES), full(ltri), full(ustr), anyspace, anyspace],
            out_specs=[anyspace, anyspace],
            scratch_shapes=[pltpu.VMEM((2, ROWS_LOCAL, D_MODEL), BF16), pltpu.VMEM((2, ROWS_LOCAL, LANES), F32),
                            pltpu.SemaphoreType.DMA((2, 2))]),
        out_shape=[jax.ShapeDtypeStruct((ROWS_SORTED, D_MODEL), BF16),
                   jax.ShapeDtypeStruct((ROWS_SORTED, LANES), F32)],
        input_output_aliases={5: 0, 6: 1},
        compiler_params=pltpu.CompilerParams(dimension_semantics=("arbitrary",)),
        name="moe_dispatch",
    )(tbl, x1b, wf, ltri, ustr, xs0, ws0)


def _experts_kernel(gid_ref, valid_ref, xs_ref, ws_ref, wg_ref, wu_ref, wd_ref, y_ref):
    i = pl.program_id(0)

    @pl.when(valid_ref[i] == 1)
    def _():
        x = xs_ref[...]
        gates = [_dot(x, wg_ref[e]) for e in range(MOE_EPG)]
        ups = [_dot(x, wu_ref[e]) for e in range(MOE_EPG)]
        ws = ws_ref[...]
        lane = lax.broadcasted_iota(jnp.int32, ws.shape, 1)
        hidden = []
        for e in range(MOE_EPG):
            w_tok = jnp.sum(jnp.where(lane == gid_ref[i] * MOE_EPG + e, ws, 0.0), axis=-1, keepdims=True)
            hidden.append((gates[e] * jax.nn.sigmoid(gates[e]) * ups[e] * w_tok).astype(BF16))
        wd = wd_ref[...].reshape(MOE_EPG * MOE_HIDDEN, D_MODEL)
        y_ref[...] = _dot(jnp.concatenate(hidden, axis=1), wd).astype(BF16)

    @pl.when(valid_ref[i] == 0)
    def _():
        y_ref[...] = jnp.zeros_like(y_ref)


def _experts(gid, valid, xs, ws, wg, wu, wd):
    tm = TM_MOE
    row = lambda n: pl.BlockSpec((tm, n), lambda i, gid, valid: (i, 0))
    wspec = lambda a, b: pl.BlockSpec((MOE_EPG, a, b), lambda i, gid, valid: (gid[i], 0, 0))
    return pl.pallas_call(
        _experts_kernel,
        grid_spec=pltpu.PrefetchScalarGridSpec(
            num_scalar_prefetch=2, grid=(ROWS_SORTED // tm,),
            in_specs=[row(D_MODEL), row(LANES), wspec(D_MODEL, MOE_HIDDEN), wspec(D_MODEL, MOE_HIDDEN),
                      wspec(MOE_HIDDEN, D_MODEL)],
            out_specs=row(D_MODEL)),
        out_shape=jax.ShapeDtypeStruct((ROWS_SORTED, D_MODEL), BF16),
        compiler_params=pltpu.CompilerParams(dimension_semantics=("arbitrary",),
                                             vmem_limit_bytes=EXPERTS_VMEM_BYTES),
        name="moe_experts",
    )(gid, valid, xs, ws, wg, wu, wd)


def _combine_kernel(tbl_ref, wf_ref, x1_ref, ltri_ref, ustr_ref, g_ref, b_ref, y_hbm, o_ref, ybuf, sems):
    tile = pl.program_id(0)
    slot = tile % 2
    tm = x1_ref.shape[0]

    def copies(t, s, act):
        _granule_copies(tbl_ref, t, (ybuf.at[s],), (y_hbm,), (sems.at[s],), False, act)

    def fetch(t, s):
        ybuf[s] = jnp.zeros(ybuf.shape[1:], ybuf.dtype)
        copies(t, s, lambda c: c.start())

    @pl.when(tile == 0)
    def _():
        fetch(tile, slot)

    @pl.when(tile + 1 < pl.num_programs(0))
    def _():
        fetch(tile + 1, 1 - slot)

    dest = _group_dest(wf_ref[...], ltri_ref, ustr_ref)
    c = lax.broadcasted_iota(jnp.int32, (tm, ROWS_LOCAL), 1).astype(F32)
    unperm = jnp.where(c == dest, 1.0, 0.0).astype(BF16)
    copies(tile, slot, lambda c: c.wait())
    ffn = _dot(unperm, ybuf[slot])
    o_ref[...] = _layer_norm(DN_ALPHA * x1_ref[...] + ffn, g_ref[...], b_ref[...])


def _combine(tbl, wf, x1, ltri, ustr, g, b, y):
    tm = TM_PROJ
    row = lambda n: pl.BlockSpec((tm, n), lambda i, t: (i, 0))
    full = lambda a: pl.BlockSpec(a.shape, lambda i, t: (0,) * a.ndim)
    return pl.pallas_call(
        _combine_kernel,
        grid_spec=pltpu.PrefetchScalarGridSpec(
            num_scalar_prefetch=1, grid=(TOKENS // tm,),
            in_specs=[row(LANES), row(D_MODEL), full(ltri), full(ustr), full(g), full(b),
                      pl.BlockSpec(memory_space=pl.ANY)],
            out_specs=row(D_MODEL),
            scratch_shapes=[pltpu.VMEM((2, ROWS_LOCAL, D_MODEL), BF16), pltpu.SemaphoreType.DMA((2,))]),
        out_shape=jax.ShapeDtypeStruct((TOKENS, D_MODEL), F32),
        compiler_params=pltpu.CompilerParams(dimension_semantics=("arbitrary",)),
        name="moe_combine",
    )(tbl, wf, x1, ltri, ustr, g, b, y)


def _routing_tables(cnt):
    n_tiles = cnt.shape[0]
    c = cnt[:, 0, :MOE_GROUPS].astype(jnp.int32)
    gran = (c + GRAN - 1) // GRAN
    local = jnp.cumsum(gran, axis=1) - gran
    per_tile = TM_MOE // GRAN
    tiles_g = (jnp.sum(gran, axis=0) + per_tile - 1) // per_tile
    ends = jnp.cumsum(tiles_g)
    base = (ends - tiles_g) * per_tile
    glob = base[None, :] + jnp.cumsum(gran, axis=0) - gran
    tbl = jnp.concatenate([gran, local, glob], axis=1).reshape(n_tiles * TBL_W)
    idx = jnp.arange(ROWS_SORTED // TM_MOE)
    gid = jnp.minimum(jnp.sum(idx[:, None] >= ends[None, :], axis=1), MOE_GROUPS - 1).astype(jnp.int32)
    valid = (idx < ends[-1]).astype(jnp.int32)
    return tbl, gid, valid


def _sort_tables():
    t = jnp.arange(TM_PROJ)
    ltri = (t[None, :] < t[:, None]).astype(BF16)
    l = jnp.arange(LANES)
    ustr = (l[:, None] < l[None, :]).astype(BF16)
    return ltri, ustr


def _dup_halves(t):
    lane = lax.broadcasted_iota(jnp.int32, t.shape, 1)
    r = pltpu.roll(t, HALF, 1)
    return jnp.where(lane < HALF, t, r), jnp.where(lane < HALF, r, t)


def _nsa_in_kernel(x_ref, w_ref, gb_ref, q_ref, kc_ref, vc_ref, ks_ref, vst_ref, kw_ref, vwt_ref, gate_ref):
    h = _dot(x_ref[...].astype(BF16), w_ref[...])
    q_ref[...] = (h[:, 0:C_MIX] * (NSA_SCALE * LOG2E)).astype(BF16)
    kc_ref[...] = h[:, 1024:1152].astype(BF16)
    vc_ref[...] = h[:, 1152:1280].astype(BF16)
    for idx, ref in ((0, ks_ref), (2, kw_ref)):
        d0, d1 = _dup_halves(h[:, 1280 + idx * LANES:1280 + (idx + 1) * LANES])
        ref[:, 0:LANES] = d0.astype(BF16)
        ref[:, LANES:2 * LANES] = d1.astype(BF16)
    lane = lax.broadcasted_iota(jnp.int32, (h.shape[0], LANES), 1)
    for idx, ref in ((1, vst_ref), (3, vwt_ref)):
        tail = jnp.where(lane == HALF, 1.0, 0.0)
        _store_transposed(ref, [jnp.where(lane < HALF, d, tail)
                                for d in _dup_halves(h[:, 1280 + idx * LANES:1280 + (idx + 1) * LANES])])
    for g in range(NSA_GROUPS):
        gate_ref[g] = jax.nn.sigmoid(h[:, 1792 + g * LANES:1792 + (g + 1) * LANES] + gb_ref[g])


def _nsa_in(x2, w, gb):
    tm = TM_PROJ
    row = lambda n: pl.BlockSpec((tm, n), lambda i: (i, 0))
    full = lambda a: pl.BlockSpec(a.shape, lambda i: (0,) * a.ndim)
    sd = jax.ShapeDtypeStruct
    vt_spec = pl.BlockSpec((NSA_GROUPS, tm // TK_ATT, LANES, TK_ATT), lambda i: (0, i, 0, 0))
    vt_shape = sd((NSA_GROUPS, TOKENS // TK_ATT, LANES, TK_ATT), BF16)
    return pl.pallas_call(
        _nsa_in_kernel,
        grid=(TOKENS // tm,),
        in_specs=[row(D_MODEL), full(w), full(gb)],
        out_specs=[row(C_MIX), row(LANES), row(LANES), row(2 * LANES), vt_spec, row(2 * LANES), vt_spec,
                   pl.BlockSpec((NSA_GROUPS, tm, LANES), lambda i: (0, i, 0))],
        out_shape=[sd((TOKENS, C_MIX), BF16), sd((TOKENS, LANES), BF16), sd((TOKENS, LANES), BF16),
                   sd((TOKENS, 2 * LANES), BF16), vt_shape, sd((TOKENS, 2 * LANES), BF16), vt_shape,
                   sd((NSA_GROUPS, TOKENS, LANES), F32)],
        compiler_params=pltpu.CompilerParams(dimension_semantics=("arbitrary",)),
        name="nsa_in",
    )(x2, w, gb)


def _compress_kernel(kc_ref, vc_ref, pk_ref, pv_ref, wk1_ref, wv1_ref, wk2_ref, wv2_ref, ko_ref, vo_ref):
    for a_ref, p_ref, w1_ref, w2_ref, o_ref in ((kc_ref, pk_ref, wk1_ref, wk2_ref, ko_ref),
                                                (vc_ref, pv_ref, wv1_ref, wv2_ref, vo_ref)):
        a = a_ref[0].astype(F32)
        a0 = (a + p_ref[0]).astype(BF16)
        a1 = (a + p_ref[1]).astype(BF16)
        outs = []
        for g in range(NSA_GROUPS):
            first = _dot(a0, w1_ref[g])
            second = _dot(a1, w1_ref[NSA_GROUPS + g])
            hid = first + pltpu.roll(second, NSA_NCMP - 1, 0)
            outs.append(_dot(_gelu(hid).astype(BF16), w2_ref[...]))
        if o_ref is ko_ref:
            o_ref[0] = jnp.concatenate(outs, axis=1).astype(BF16)
        else:
            for g in range(NSA_GROUPS):
                o_ref[0, g] = outs[g].T.astype(BF16)


def _compress(kc_r, vc_r, pk, pv, wk1, wv1, wk2, wv2):
    blk = pl.BlockSpec((1, NSA_NCMP, NSA_CMP_STRIDE * LANES), lambda b: (b, 0, 0))
    full = lambda a: pl.BlockSpec(a.shape, lambda b: (0,) * a.ndim)
    sd = jax.ShapeDtypeStruct
    return pl.pallas_call(
        _compress_kernel,
        grid=(BATCH,),
        in_specs=[blk, blk, full(pk), full(pv), full(wk1), full(wv1), full(wk2), full(wv2)],
        out_specs=[pl.BlockSpec((1, NSA_NCMP, 2 * LANES), lambda b: (b, 0, 0)),
                   pl.BlockSpec((1, NSA_GROUPS, LANES, NSA_NCMP), lambda b: (b, 0, 0, 0))],
        out_shape=[sd((BATCH, NSA_NCMP, 2 * LANES), BF16), sd((BATCH, NSA_GROUPS, LANES, NSA_NCMP), BF16)],
        compiler_params=pltpu.CompilerParams(dimension_semantics=("arbitrary",)),
        name="nsa_compress",
    )(kc_r, vc_r, pk, pv, wk1, wv1, wk2, wv2)


def _nsa_attn_kernel(q_ref, kc_ref, vct_ref, ks_ref, vst_ref, kw_ref, vwt_ref, gate_ref, cover_ref,
                     o_ref, sel_ref, m_ref, acc_ref):
    tq, tk, hpg = TQ_NSA, TK_ATT, NSA_HPG
    qi = pl.program_id(2)
    q0 = qi * tq
    lane = lax.broadcasted_iota(jnp.int32, (tq, LANES), 1)
    t_tok = q0 + lax.broadcasted_iota(jnp.int32, (1, tq), 1)
    head = lambda x, i: x[:, i * tq:(i + 1) * tq]
    heads = range(hpg)

    parts = []
    for p in range(hpg // 2):
        qp = q_ref[0, :, p * LANES:(p + 1) * LANES]
        zero = jnp.zeros_like(qp)
        parts.append(jnp.where(lane < HALF, qp, zero))
        parts.append(jnp.where(lane < HALF, zero, qp))
    qs = jnp.concatenate(parts, axis=0)

    s_c = _dot_nt(kc_ref[0], qs)
    n_sub = lax.broadcasted_iota(jnp.int32, (NSA_NCMP, 1), 0)
    vis = t_tok >= n_sub * NSA_CMP_STRIDE + (NSA_CMP_LEN - 1)
    p_sum = jnp.zeros((NSA_NCMP, tq), F32)
    p_c = []
    for i in heads:
        sm = jnp.where(vis, head(s_c, i), NEG)
        e = jnp.exp2(sm - jnp.max(sm, axis=0, keepdims=True))
        p = jnp.where(vis, e / jnp.sum(e, axis=0, keepdims=True), 0.0)
        p_sum = p_sum + p
        p_c.append(p.astype(BF16))
    o_c = _dot(vct_ref[0, 0], jnp.concatenate(p_c, axis=1))

    imp = jnp.dot(cover_ref[...], p_sum, preferred_element_type=F32,
                  precision=lax.Precision.HIGHEST)[0:NSA_NSEL]
    jj = lax.broadcasted_iota(jnp.int32, (NSA_NSEL, 1), 0)
    tb = t_tok // NSA_SEL_LEN
    forced = (jj == 0) | (jj == tb) | (jj == tb - 1)
    score = jnp.where(forced, NSA_FORCE, jnp.where(jj <= tb, imp, -NSA_FORCE))
    rank = jnp.zeros((NSA_NSEL, tq), jnp.int32)
    for i in range(NSA_NSEL):
        si = score[i:i + 1, :]
        beats = (si > score) | ((si == score) & (jj > i))
        rank = rank + beats.astype(jnp.int32)
    sel_ref[...] = jnp.where(rank < NSA_TOPK, 1.0, 0.0)

    m_ref[...] = jnp.full(m_ref.shape, NEG, F32)
    acc_ref[...] = jnp.zeros(acc_ref.shape, F32)
    k_sub = lax.broadcasted_iota(jnp.int32, (tk, 1), 0)
    blocks_per_tile = tk // NSA_SEL_LEN

    def sel_step(kt, c):
        r0 = pl.multiple_of(kt * tk, tk)
        s = _dot_nt(ks_ref[0, pl.ds(r0, tk), :], qs)
        member = jnp.concatenate(
            [jnp.broadcast_to(sel_ref[pl.ds(kt * blocks_per_tile + a, 1), :], (NSA_SEL_LEN, tq))
             for a in range(blocks_per_tile)], axis=0)
        mask = (member > 0.5) & (r0 + k_sub <= t_tok)
        probs, alphas = [], []
        for i in heads:
            sm = jnp.where(mask, head(s, i), NEG)
            m_old = m_ref[i]
            m_new = jnp.maximum(m_old, jnp.max(sm, axis=0, keepdims=True))
            alphas.append(jnp.exp2(m_old - m_new))
            probs.append(jnp.exp2(sm - m_new).astype(BF16))
            m_ref[i] = m_new
        pv = _dot(vst_ref[0, kt], jnp.concatenate(probs, axis=1))
        acc_ref[...] = jnp.concatenate(alphas, axis=1) * acc_ref[...] + pv
        return c

    lax.fori_loop(0, (q0 + tq - 1) // tk + 1, sel_step, 0)

    c0 = jnp.maximum(q0 + tq - WIN_CHUNKS * tk, 0) // tk
    s_w, in_win = [], []
    for c in range(WIN_CHUNKS):
        r0 = pl.multiple_of((c0 + c) * tk, tk)
        s_w.append(_dot_nt(kw_ref[0, pl.ds(r0, tk), :], qs))
        kpos = r0 + k_sub
        in_win.append((kpos <= t_tok) & (kpos > t_tok - NSA_WINDOW))
    e_w = [[] for _ in range(WIN_CHUNKS)]
    for i in heads:
        sm = [jnp.where(in_win[c], head(s_w[c], i), NEG) for c in range(WIN_CHUNKS)]
        m = functools.reduce(jnp.maximum, [jnp.max(x, axis=0, keepdims=True) for x in sm])
        for c in range(WIN_CHUNKS):
            e_w[c].append(jnp.exp2(sm[c] - m).astype(BF16))
    o_w = functools.reduce(jnp.add, [_dot(vwt_ref[0, c0 + c], jnp.concatenate(e_w[c], axis=1))
                                     for c in range(WIN_CHUNKS)])

    gt = gate_ref[0, 0].T
    o_s = acc_ref[...]
    outs = []
    for i in heads:
        c_i, s_i, w_i = head(o_c, i), head(o_s, i), head(o_w, i)
        outs.append(gt[i:i + 1] * c_i[:HALF]
                    + gt[hpg + i:hpg + i + 1] / s_i[HALF:HALF + 1] * s_i[:HALF]
                    + gt[2 * hpg + i:2 * hpg + i + 1] / w_i[HALF:HALF + 1] * w_i[:HALF])
    for p in range(hpg // 2):
        pair = jnp.concatenate([outs[2 * p], outs[2 * p + 1]], axis=0)
        o_ref[0, :, p * LANES:(p + 1) * LANES] = pair.T.astype(BF16)


def _nsa_attn(q3, kc2, vct, ks3, vst, kw3, vwt, gates4, cover_t):
    tq = TQ_NSA
    half_w = NSA_HPG * NSA_DH
    n_chunks = SEQ // TK_ATT
    kv = pl.BlockSpec((1, SEQ, LANES), lambda b, g, i: (b, 0, g))
    vt = pl.BlockSpec((1, n_chunks, LANES, TK_ATT), lambda b, g, i: (g, b, 0, 0))
    full = lambda a: pl.BlockSpec(a.shape, lambda b, g, i: (0,) * a.ndim)
    return pl.pallas_call(
        _nsa_attn_kernel,
        grid=(BATCH, NSA_GROUPS, SEQ // tq),
        in_specs=[pl.BlockSpec((1, tq, half_w), lambda b, g, i: (b, i, g)),
                  pl.BlockSpec((1, NSA_NCMP, LANES), lambda b, g, i: (b, 0, g)),
                  pl.BlockSpec((1, 1, LANES, NSA_NCMP), lambda b, g, i: (b, g, 0, 0)),
                  kv, vt, kv, vt,
                  pl.BlockSpec((1, 1, tq, LANES), lambda b, g, i: (g, b, i, 0)),
                  full(cover_t)],
        out_specs=pl.BlockSpec((1, tq, half_w), lambda b, g, i: (b, i, g)),
        out_shape=jax.ShapeDtypeStruct((BATCH, SEQ, C_MIX), BF16),
        scratch_shapes=[pltpu.VMEM((NSA_NSEL, tq), F32), pltpu.VMEM((NSA_HPG, 1, tq), F32),
                        pltpu.VMEM((LANES, NSA_HPG * tq), F32)],
        compiler_params=pltpu.CompilerParams(dimension_semantics=("arbitrary",) * 3),
        name="nsa_attn",
    )(q3, kc2, vct, ks3, vst, kw3, vwt, gates4, cover_t)


def _rope_tables():
    half = MLA_ROPE // 2
    freq = jnp.exp(-math.log(ROPE_BASE) * jnp.arange(half, dtype=F32) / half)
    zeros64 = jnp.zeros((MLA_NOPE,), F32)
    zeros32 = jnp.zeros((LANES - MLA_NOPE - MLA_ROPE,), F32)
    fc = jnp.concatenate([zeros64, freq, freq, zeros32])[None, :]
    sg = jnp.concatenate([zeros64, -jnp.ones((half,), F32), jnp.ones((half,), F32), zeros32])[None, :]
    return fc, sg


def _swap_halves(w):
    half = w.shape[-1] // 2
    return jnp.concatenate([w[..., half:], w[..., :half]], axis=-1)


def _pad_last(w, n):
    return jnp.pad(w, [(0, 0)] * (w.ndim - 1) + [(0, n - w.shape[-1])])


def _ab_weights(w_in, w_uq, w_uk, w_uv):
    w_kr = w_in[:, 1408:1440]
    place = lambda w: jnp.pad(w, ((0, 0), (MLA_NOPE, LANES - MLA_NOPE - MLA_ROPE)))
    win = jnp.concatenate([w_in[:, :1408], place(w_kr), place(_swap_halves(w_kr))], axis=1).astype(BF16)
    uq = w_uq.reshape(MLA_Q_RANK, MLA_HEADS, MLA_NOPE + MLA_ROPE)
    nope, rp = uq[..., :MLA_NOPE], uq[..., MLA_NOPE:]
    q_pad = _pad_last(jnp.concatenate([nope, rp], -1), LANES).reshape(MLA_Q_RANK, MLA_HEADS * LANES)
    q_sw = _pad_last(jnp.concatenate([jnp.zeros_like(nope), _swap_halves(rp)], -1), LANES)
    wq = jnp.concatenate([q_pad, q_sw.reshape(MLA_Q_RANK, MLA_HEADS * LANES)], axis=1).astype(BF16)
    k_pad = _pad_last(w_uk.reshape(MLA_KV_RANK, MLA_HEADS, MLA_NOPE), LANES).reshape(MLA_KV_RANK, -1)
    wkv = jnp.concatenate([k_pad, w_uv], axis=1).astype(BF16)
    return win, wq, wkv


def _router_weights(w_rg, b_rg, w_re, b_re):
    wr = _pad_last(jnp.concatenate([w_re, w_rg], axis=1), LANES)
    wr_hi = wr.astype(BF16)
    wr_lo = (wr - wr_hi.astype(F32)).astype(BF16)
    br = _pad_last(jnp.concatenate([b_re, b_rg])[None, :], LANES)
    return jnp.stack([wr_hi, wr_lo]), br


def _nsa_in_weights(w_in, gate_b):
    g_cols = w_in[:, C_MIX + 768:].reshape(D_MODEL, 3, NSA_GROUPS, NSA_HPG)
    g_blocks = [_pad_last(g_cols[:, :, g, :].reshape(D_MODEL, 3 * NSA_HPG), LANES) for g in range(NSA_GROUPS)]
    w = jnp.concatenate([w_in[:, :C_MIX + 768]] + g_blocks, axis=1).astype(BF16)
    gb = gate_b.reshape(3, NSA_GROUPS, NSA_HPG)
    gb = jnp.stack([_pad_last(gb[:, g, :].reshape(1, 3 * NSA_HPG), LANES) for g in range(NSA_GROUPS)])
    return w, gb


def _compress_weights(pos, w1, w2):
    w1r = w1.reshape(2, NSA_CMP_STRIDE, NSA_DH, NSA_CMP_HIDDEN)
    zero = jnp.zeros_like(w1r)
    per_g = []
    for g in range(NSA_GROUPS):
        parts = [w1r if gg == g else zero for gg in range(NSA_GROUPS)]
        per_g.append(jnp.stack(parts, axis=2).reshape(2, NSA_CMP_STRIDE * LANES, NSA_CMP_HIDDEN))
    w1x = jnp.stack(per_g, axis=1).reshape(2 * NSA_GROUPS, NSA_CMP_STRIDE * LANES, NSA_CMP_HIDDEN)
    posr = pos.reshape(2, NSA_CMP_STRIDE, 1, NSA_DH)
    posx = jnp.broadcast_to(posr, (2, NSA_CMP_STRIDE, NSA_GROUPS, NSA_DH)).reshape(2, 1, NSA_CMP_STRIDE * LANES)
    w2x = jnp.concatenate([w2, w2], axis=1)
    return posx, w1x.astype(BF16), w2x.astype(BF16)


def _selection_tables():
    n = jnp.arange(LANES)[:, None]
    j = jnp.arange(LANES)[None, :]
    c0 = n * NSA_CMP_STRIDE
    s0 = j * NSA_SEL_LEN
    cover = ((c0 < s0 + NSA_SEL_LEN) & (c0 + NSA_CMP_LEN > s0) & (n < NSA_NCMP - 1) & (j < NSA_NSEL))
    return jnp.transpose(cover).astype(F32)


def kernel(x, positions, ab_w_in, ab_gm_ln_g, ab_gm_ln_b, ab_gm_ws, ab_gm_bs, ab_mla_q_norm,
           ab_mla_kv_norm, ab_mla_w_uq, ab_mla_w_uk, ab_mla_w_uv, ab_w_o, c_w_in, c_cmp_pos, c_w_ck1,
           c_w_ck2, c_w_cv1, c_w_cv2, c_gate_b, c_w_o, moe_w_rg, moe_b_rg, moe_w_re, moe_b_re,
           moe_w_gate, moe_w_up, moe_w_down, ln1_g, ln1_b, ln2_g, ln2_b):
    x2 = x.reshape(TOKENS, D_MODEL)
    pos2 = positions.reshape(TOKENS, 1)
    vec = lambda a: a[None, :]

    ltri, ustr = _sort_tables()

    def moe_layer(layer, x1b, x1, wf, cnt):
        tbl, gid, valid = _routing_tables(cnt)
        xs, ws = _dispatch(tbl, x1b, wf, ltri, ustr)
        y = _experts(gid, valid, xs, ws, moe_w_gate[layer].astype(BF16), moe_w_up[layer].astype(BF16),
                     moe_w_down[layer].astype(BF16))
        return _combine(tbl, wf, x1, ltri, ustr, vec(ln2_g[layer]), vec(ln2_b[layer]), y)

    win, wq, wkv = _ab_weights(ab_w_in[0], ab_mla_w_uq[0], ab_mla_w_uk[0], ab_mla_w_uv[0])
    fc, sg = _rope_tables()
    gu, vn, q, k, vt = _ab_in(x2, pos2, win, vec(ab_gm_ln_g[0]), vec(ab_gm_ln_b[0]), vec(ab_mla_q_norm[0]),
                              vec(ab_mla_kv_norm[0]), wq, wkv, fc, sg)
    yb = _mla_attn(q.reshape(BATCH, SEQ, -1), k.reshape(BATCH, SEQ, -1), vt)
    wr, br = _router_weights(moe_w_rg[0], moe_b_rg[0], moe_w_re[0], moe_b_re[0])
    x1, x1b, wf, cnt = _mix_out(x2, yb.reshape(TOKENS, -1), ab_w_o[0].astype(BF16), vec(ln1_g[0]),
                                vec(ln1_b[0]), wr, br, gm=(gu, vn, ab_gm_ws[0], jnp.transpose(ab_gm_bs[0])))
    x2 = moe_layer(0, x1b, x1, wf, cnt)

    w_nsa, gb = _nsa_in_weights(c_w_in[0], c_gate_b[0])
    q, kc, vc, ks, vst, kw, vwt, gates = _nsa_in(x2, w_nsa, gb)
    pk, wk1, wk2 = _compress_weights(c_cmp_pos[0, 0], c_w_ck1[0], c_w_ck2[0])
    pv, wv1, wv2 = _compress_weights(c_cmp_pos[0, 1], c_w_cv1[0], c_w_cv2[0])
    blocks = lambda a: a.reshape(BATCH, NSA_NCMP, NSA_CMP_STRIDE * LANES)
    kc2, vct = _compress(blocks(kc), blocks(vc), pk, pv, wk1, wv1, wk2, wv2)
    b3 = lambda a: a.reshape(BATCH, SEQ, -1)
    o = _nsa_attn(b3(q), kc2, vct, b3(ks), vst, b3(kw), vwt,
                  gates.reshape(NSA_GROUPS, BATCH, SEQ, LANES), _selection_tables())
    wr, br = _router_weights(moe_w_rg[1], moe_b_rg[1], moe_w_re[1], moe_b_re[1])
    x1, x1b, wf, cnt = _mix_out(x2, o.reshape(TOKENS, -1), c_w_o[0].astype(BF16), vec(ln1_g[1]), vec(ln1_b[1]),
                                wr, br)
    x2 = moe_layer(1, x1b, x1, wf, cnt)
    return x2.reshape(BATCH, SEQ, D_MODEL)
```

```python
import functools
import math

import jax
import jax.numpy as jnp
from jax import lax
from jax.experimental import pallas as pl
from jax.experimental.pallas import tpu as pltpu

F32 = jnp.float32
BF16 = jnp.bfloat16

D_MODEL = 1024
BATCH = 16
SEQ = 2048
TOKENS = BATCH * SEQ
DEPTH = 2
DN_ALPHA = (2.0 * DEPTH) ** 0.25
LN_EPS = 1e-5
NEG = -1e30
LOG2E = math.log2(math.e)
LANES = 128
HALF = LANES // 2

GM_WIDTH = 512
GM_GROUPS = 4
GM_CHUNK = 128

MLA_HEADS = 8
MLA_NOPE = 64
MLA_ROPE = 32
MLA_V = 64
MLA_Q_RANK = 256
MLA_KV_RANK = 128
ROPE_BASE = 10000.0
MLA_SCALE = (MLA_NOPE + MLA_ROPE) ** -0.5

NSA_HEADS = 16
NSA_GROUPS = 2
NSA_HPG = 8
NSA_DH = 64
NSA_CMP_LEN = 32
NSA_CMP_STRIDE = 16
NSA_CMP_HIDDEN = 256
NSA_SEL_LEN = 64
NSA_TOPK = 8
NSA_WINDOW = 512
NSA_NSEL = SEQ // NSA_SEL_LEN
NSA_NCMP = SEQ // NSA_CMP_STRIDE
NSA_FORCE = 1e4
NSA_SCALE = NSA_DH ** -0.5
C_MIX = NSA_HEADS * NSA_DH

MOE_GROUPS = 4
MOE_EPG = 8
MOE_EXPERTS = 32
MOE_HIDDEN = 256

TM_PROJ = 512
TQ_MLA = 256
TQ_NSA = 256
TK_ATT = 256
WIN_CHUNKS = (NSA_WINDOW + TQ_NSA) // TK_ATT
MLA_VT_ROWS = LANES + 16
TM_MOE = 512
EXPERTS_VMEM_BYTES = (2 * 3 * MOE_EPG * D_MODEL * MOE_HIDDEN * 2 + 2 * 2 * TM_MOE * D_MODEL * 2
                      + 2 * MOE_EPG * TM_MOE * MOE_HIDDEN * 4 + TM_MOE * MOE_EPG * MOE_HIDDEN * 2
                      + 2 * TM_MOE * D_MODEL * 4)

GSEL_LANE = MOE_EXPERTS
GRAN = 16
TBL_W = 3 * MOE_GROUPS
ROWS_LOCAL = 640
ROWS_SORTED = TM_MOE * (TOKENS // TM_MOE + MOE_GROUPS
                        + -(-(TOKENS // TM_PROJ) * MOE_GROUPS * (GRAN - 1) // TM_MOE))


def _dot(a, b):
    return jnp.dot(a, b, preferred_element_type=F32)


def _dot_nt(a, b):
    return lax.dot_general(a, b, (((1,), (1,)), ((), ())), preferred_element_type=F32)


def _gelu(x):
    return 0.5 * x * (1.0 + jnp.tanh(math.sqrt(2.0 / math.pi) * (x + 0.044715 * (x * x * x))))


def _layer_norm(x, g, b):
    mu = jnp.mean(x, axis=-1, keepdims=True)
    xc = x - mu
    var = jnp.mean(xc * xc, axis=-1, keepdims=True)
    return xc * lax.rsqrt(var + LN_EPS) * g + b


def _rms_norm(x, g):
    return x * lax.rsqrt(jnp.mean(x * x, axis=-1, keepdims=True) + LN_EPS) * g


def _store_transposed(ref, blocks):
    extra = ref.shape[2] - LANES
    if extra:
        ones_rows = jnp.where(lax.broadcasted_iota(jnp.int32, (extra, TK_ATT), 0) == 0, 1.0, 0.0)
    for n, blk in enumerate(blocks):
        t = blk.T
        for c in range(t.shape[1] // TK_ATT):
            chunk = t[:, c * TK_ATT:(c + 1) * TK_ATT]
            if extra:
                chunk = jnp.concatenate([chunk, ones_rows], axis=0)
            ref[n, c] = chunk.astype(ref.dtype)


def _ab_in_kernel(x_ref, pos_ref, win_ref, lng_ref, lnb_ref, qg_ref, kvg_ref, wq_ref, wkv_ref,
                  fc_ref, sg_ref, gu_ref, vn_ref, q_ref, k_ref, vt_ref):
    h = _dot(x_ref[...].astype(BF16), win_ref[...])
    gu_ref[...] = _gelu(h[:, 0:512]).astype(BF16)
    vn_ref[...] = _layer_norm(_gelu(h[:, 512:1024]), lng_ref[...], lnb_ref[...]).astype(BF16)

    ang = pos_ref[...].astype(F32) * fc_ref[...]
    cc = jnp.cos(ang)
    ss = jnp.sin(ang) * sg_ref[...]

    cqn = _rms_norm(h[:, 1024:1280], qg_ref[...]).astype(BF16)
    qq = _dot(cqn, wq_ref[...])
    for hd in range(MLA_HEADS):
        lo, hi = hd * LANES, (hd + 1) * LANES
        q_ref[:, lo:hi] = ((qq[:, lo:hi] * cc + qq[:, 1024 + lo:1024 + hi] * ss) * (MLA_SCALE * LOG2E)).astype(BF16)

    ckvn = _rms_norm(h[:, 1280:1408], kvg_ref[...]).astype(BF16)
    kv = _dot(ckvn, wkv_ref[...])
    k_rope = h[:, 1408:1536] * cc + h[:, 1536:1664] * ss
    for hd in range(MLA_HEADS):
        lo, hi = hd * LANES, (hd + 1) * LANES
        k_ref[:, lo:hi] = (kv[:, lo:hi] + k_rope).astype(BF16)
    _store_transposed(vt_ref, [kv[:, 1024 + p * LANES:1024 + (p + 1) * LANES] for p in range(MLA_HEADS // 2)])


def _ab_in(x2, pos2, win, lng, lnb, qg, kvg, wq, wkv, fc, sg):
    tm = TM_PROJ
    row = lambda n: pl.BlockSpec((tm, n), lambda i: (i, 0))
    full = lambda a: pl.BlockSpec(a.shape, lambda i: (0,) * a.ndim)
    return pl.pallas_call(
        _ab_in_kernel,
        grid=(TOKENS // tm,),
        in_specs=[row(D_MODEL), row(1), full(win), full(lng), full(lnb), full(qg), full(kvg),
                  full(wq), full(wkv), full(fc), full(sg)],
        out_specs=[row(512), row(512), row(1024), row(1024),
                   pl.BlockSpec((MLA_HEADS // 2, tm // TK_ATT, MLA_VT_ROWS, TK_ATT), lambda i: (0, i, 0, 0))],
        out_shape=[jax.ShapeDtypeStruct((TOKENS, 512), BF16), jax.ShapeDtypeStruct((TOKENS, 512), BF16),
                   jax.ShapeDtypeStruct((TOKENS, 1024), BF16), jax.ShapeDtypeStruct((TOKENS, 1024), BF16),
                   jax.ShapeDtypeStruct((MLA_HEADS // 2, TOKENS // TK_ATT, MLA_VT_ROWS, TK_ATT), BF16)],
        compiler_params=pltpu.CompilerParams(dimension_semantics=("arbitrary",)),
        name="ab_in",
    )(x2, pos2, win, lng, lnb, qg, kvg, wq, wkv, fc, sg)


def _mla_attn_kernel(q_ref, k_ref, vt_ref, o_ref, m_ref, l_ref, acc_ref):
    tq, tk = TQ_MLA, TK_ATT
    qi = pl.program_id(1)
    krow = lax.broadcasted_iota(jnp.int32, (tk, tq), 0)
    qcol = lax.broadcasted_iota(jnp.int32, (tk, tq), 1)
    top = lax.broadcasted_iota(jnp.int32, (LANES, tq), 0) < HALF
    m_ref[...] = jnp.full(m_ref.shape, NEG, F32)
    l_ref[...] = jnp.zeros(l_ref.shape, F32)
    acc_ref[...] = jnp.zeros(acc_ref.shape, F32)

    def tile(j, masked):
        r0 = pl.multiple_of(j * tk, tk)
        scores = [_dot_nt(k_ref[0, pl.ds(r0, tk), h * LANES:(h + 1) * LANES], q_ref[0, :, h * LANES:(h + 1) * LANES])
                  for h in range(MLA_HEADS)]
        for pr in range(MLA_HEADS // 2):
            probs, alphas = [], []
            for h in (2 * pr, 2 * pr + 1):
                s = jnp.where(krow <= qcol, scores[h], NEG) if masked else scores[h]
                m_old = m_ref[h]
                m_new = jnp.maximum(m_old, jnp.max(s, axis=0, keepdims=True))
                alphas.append(jnp.exp2(m_old - m_new))
                probs.append(jnp.exp2(s - m_new).astype(BF16))
                m_ref[h] = m_new
            pv = _dot(vt_ref[pr, j], jnp.concatenate(probs, axis=1))
            for n, hh in enumerate((2 * pr, 2 * pr + 1)):
                l_ref[hh] = alphas[n] * l_ref[hh] + pv[LANES:LANES + 1, n * tq:(n + 1) * tq]
            a = jnp.where(top, alphas[0], alphas[1])
            acc_ref[pr] = a * acc_ref[pr] + jnp.where(top, pv[:LANES, :tq], pv[:LANES, tq:])

    def body(j, c):
        tile(j, False)
        return c

    lax.fori_loop(0, qi, body, 0)
    tile(qi, True)
    for pr in range(MLA_HEADS // 2):
        l = jnp.where(top, l_ref[2 * pr], l_ref[2 * pr + 1])
        o_ref[0, :, pr * LANES:(pr + 1) * LANES] = (acc_ref[pr] / l).T.astype(BF16)


def _mla_attn(q3, k3, vt):
    tq = TQ_MLA
    n_chunks = SEQ // TK_ATT
    return pl.pallas_call(
        _mla_attn_kernel,
        grid=(BATCH, SEQ // tq),
        in_specs=[pl.BlockSpec((1, tq, MLA_HEADS * LANES), lambda b, i: (b, i, 0)),
                  pl.BlockSpec((1, SEQ, MLA_HEADS * LANES), lambda b, i: (b, 0, 0)),
                  pl.BlockSpec((MLA_HEADS // 2, n_chunks, MLA_VT_ROWS, TK_ATT), lambda b, i: (0, b, 0, 0))],
        out_specs=pl.BlockSpec((1, tq, MLA_HEADS * MLA_V), lambda b, i: (b, i, 0)),
        out_shape=jax.ShapeDtypeStruct((BATCH, SEQ, MLA_HEADS * MLA_V), BF16),
        scratch_shapes=[pltpu.VMEM((MLA_HEADS, 1, tq), F32), pltpu.VMEM((MLA_HEADS, 1, tq), F32),
                        pltpu.VMEM((MLA_HEADS // 2, LANES, tq), F32)],
        compiler_params=pltpu.CompilerParams(dimension_semantics=("arbitrary",) * 2),
        name="mla_attn",
    )(q3, k3, vt)


def _router(x1, wr, br):
    tm = x1.shape[0]
    x_hi = x1.astype(BF16)
    x_lo = (x1 - x_hi.astype(F32)).astype(BF16)
    logits = _dot(x_hi, wr[0]) + (_dot(x_lo, wr[0]) + _dot(x_hi, wr[1])) + br
    lane = lax.broadcasted_iota(jnp.int32, (tm, LANES), 1).astype(F32)
    big = 1e6
    is_g = (lane >= MOE_EXPERTS) & (lane < MOE_EXPERTS + MOE_GROUPS)
    gl = jnp.where(is_g, logits, NEG)
    gmax = jnp.max(gl, axis=-1, keepdims=True)
    g_sel = jnp.min(jnp.where(is_g & (gl == gmax), lane, big), axis=-1, keepdims=True) - MOE_EXPERTS
    g_w = 1.0 / jnp.sum(jnp.where(is_g, jnp.exp(gl - gmax), 0.0), axis=-1, keepdims=True)
    in_grp = (lane >= g_sel * MOE_EPG) & (lane < (g_sel + 1) * MOE_EPG)
    el = jnp.where(in_grp, logits, NEG)
    emax = jnp.max(el, axis=-1, keepdims=True)
    ee = jnp.where(in_grp, jnp.exp(el - emax), 0.0)
    pe = ee / jnp.sum(ee, axis=-1, keepdims=True)
    p1 = jnp.max(pe, axis=-1, keepdims=True)
    i1 = jnp.min(jnp.where(in_grp & (pe == p1), lane, big), axis=-1, keepdims=True)
    rest = in_grp & (lane != i1)
    pr = jnp.where(rest, pe, -1.0)
    p2 = jnp.max(pr, axis=-1, keepdims=True)
    i2 = jnp.min(jnp.where(rest & (pr == p2), lane, big), axis=-1, keepdims=True)
    tot = p1 + p2
    wf = jnp.where(lane == i1, p1 / tot * g_w, jnp.where(lane == i2, p2 / tot * g_w, 0.0))
    wf = jnp.where(lane == GSEL_LANE, g_sel, wf)
    cnt = jnp.sum(jnp.where(lane == g_sel, 1.0, 0.0), axis=0, keepdims=True)
    return wf, cnt


def _mix_out_kernel(*refs, gmlp):
    if gmlp:
        (x_ref, gu_ref, vn_ref, ws_ref, bs_ref, yb_ref, wo_ref, g_ref, b_ref, wr_ref, br_ref,
         x1_ref, x1b_ref, wf_ref, cnt_ref, ya_ref) = refs
        tm = x_ref.shape[0]
        r = lax.broadcasted_iota(jnp.int32, (GM_CHUNK, GM_CHUNK), 0)
        c = lax.broadcasted_iota(jnp.int32, (GM_CHUNK, GM_CHUNK), 1)
        for g in range(GM_GROUPS):
            ws = jnp.where(r >= c, ws_ref[g], 0.0).astype(BF16)
            bias = bs_ref[:, g:g + 1]
            for ch in range(tm // GM_CHUNK):
                rows = slice(ch * GM_CHUNK, (ch + 1) * GM_CHUNK)
                cols = slice(g * LANES, (g + 1) * LANES)
                s = _dot(ws, vn_ref[rows, cols]) + bias
                ya_ref[rows, cols] = (gu_ref[rows, cols].astype(F32) * s).astype(BF16)
        mix = _dot(ya_ref[...], wo_ref[0:GM_WIDTH, :]) + _dot(yb_ref[...], wo_ref[GM_WIDTH:, :])
    else:
        x_ref, y_ref, wo_ref, g_ref, b_ref, wr_ref, br_ref, x1_ref, x1b_ref, wf_ref, cnt_ref = refs
        mix = _dot(y_ref[...], wo_ref[...])
    x1 = _layer_norm(DN_ALPHA * x_ref[...] + mix, g_ref[...], b_ref[...])
    x1_ref[...] = x1
    x1b_ref[...] = x1.astype(BF16)
    wf, cnt = _router(x1, wr_ref[...], br_ref[...])
    wf_ref[...] = wf
    cnt_ref[0] = jnp.broadcast_to(cnt, cnt_ref.shape[1:])


def _mix_out(x2, ys, wo, g, b, wr, br, gm=None):
    tm = TM_PROJ
    row = lambda n: pl.BlockSpec((tm, n), lambda i: (i, 0))
    full = lambda a: pl.BlockSpec(a.shape, lambda i: (0,) * a.ndim)
    if gm is not None:
        gu, vn, ws, bs = gm
        args = (x2, gu, vn, ws, bs, ys, wo, g, b, wr, br)
        in_specs = [row(D_MODEL), row(512), row(512), full(ws), full(bs), row(512), full(wo),
                    full(g), full(b), full(wr), full(br)]
        scratch = [pltpu.VMEM((tm, GM_WIDTH), BF16)]
    else:
        args = (x2, ys, wo, g, b, wr, br)
        in_specs = [row(D_MODEL), row(C_MIX), full(wo), full(g), full(b), full(wr), full(br)]
        scratch = []
    return pl.pallas_call(
        functools.partial(_mix_out_kernel, gmlp=gm is not None),
        grid=(TOKENS // tm,),
        in_specs=in_specs,
        out_specs=[row(D_MODEL), row(D_MODEL), row(LANES), pl.BlockSpec((1, 8, LANES), lambda i: (i, 0, 0))],
        out_shape=[jax.ShapeDtypeStruct((TOKENS, D_MODEL), F32), jax.ShapeDtypeStruct((TOKENS, D_MODEL), BF16),
                   jax.ShapeDtypeStruct((TOKENS, LANES), F32),
                   jax.ShapeDtypeStruct((TOKENS // tm, 8, LANES), F32)],
        scratch_shapes=scratch,
        compiler_params=pltpu.CompilerParams(dimension_semantics=("arbitrary",)),
        name="mix_out_gmlp" if gm is not None else "mix_out",
    )(*args)


def _group_dest(wf, ltri_ref, ustr_ref):
    tm = wf.shape[0]
    lane = lax.broadcasted_iota(jnp.int32, (tm, LANES), 1).astype(F32)
    onehot = jnp.where(lane == wf[:, GSEL_LANE:GSEL_LANE + 1], 1.0, 0.0)
    before = _dot(ltri_ref[...], onehot.astype(BF16))
    cnt = jnp.sum(onehot, axis=0, keepdims=True)
    gran = jnp.floor((cnt + (GRAN - 1)) * (1.0 / GRAN))
    start = _dot(jnp.broadcast_to(gran, (8, LANES)).astype(BF16), ustr_ref[...])[0:1]
    return jnp.sum(onehot * (GRAN * start + before), axis=-1, keepdims=True)


def _granule_copies(tbl_ref, tile, vmem_bufs, hbm_refs, sems, to_hbm, act):
    for g in range(MOE_GROUPS):
        n = tbl_ref[tile * TBL_W + g]
        loc = tbl_ref[tile * TBL_W + MOE_GROUPS + g]
        glb = tbl_ref[tile * TBL_W + 2 * MOE_GROUPS + g]

        def body(k, c, loc=loc, glb=glb):
            lo = pl.multiple_of((loc + k) * GRAN, GRAN)
            hi = pl.multiple_of((glb + k) * GRAN, GRAN)
            for idx, (vb, hb) in enumerate(zip(vmem_bufs, hbm_refs)):
                v_sl, h_sl = vb.at[pl.ds(lo, GRAN)], hb.at[pl.ds(hi, GRAN)]
                src, dst = (v_sl, h_sl) if to_hbm else (h_sl, v_sl)
                act(pltpu.make_async_copy(src, dst, sems[idx]))
            return c

        lax.fori_loop(0, n, body, 0)


def _dispatch_kernel(tbl_ref, xb_ref, wf_ref, ltri_ref, ustr_ref, xs_in, ws_in, xs_out, ws_out, xbuf, wbuf, sems):
    del xs_in, ws_in
    tile = pl.program_id(0)
    slot = tile % 2
    tm = xb_ref.shape[0]
    wf = wf_ref[...]
    dest = _group_dest(wf, ltri_ref, ustr_ref)
    dest_row = jnp.broadcast_to(dest, (tm, LANES)).T[0:1]
    r = lax.broadcasted_iota(jnp.int32, (ROWS_LOCAL, tm), 0).astype(F32)
    perm = jnp.where(r == dest_row, 1.0, 0.0).astype(BF16)
    xbuf[slot] = _dot(perm, xb_ref[...]).astype(BF16)
    hi = wf.astype(BF16)
    r1 = wf - hi.astype(F32)
    mid = r1.astype(BF16)
    lo = (r1 - mid.astype(F32)).astype(BF16)
    wbuf[slot] = _dot(perm, hi) + _dot(perm, mid) + _dot(perm, lo)

    def copies(t, s, act):
        _granule_copies(tbl_ref, t, (xbuf.at[s], wbuf.at[s]), (xs_out, ws_out), (sems.at[s, 0], sems.at[s, 1]),
                        True, act)

    copies(tile, slot, lambda c: c.start())

    @pl.when(tile > 0)
    def _():
        copies(tile - 1, 1 - slot, lambda c: c.wait())

    @pl.when(tile == pl.num_programs(0) - 1)
    def _():
        copies(tile, slot, lambda c: c.wait())


def _dispatch(tbl, x1b, wf, ltri, ustr):
    tm = TM_PROJ
    row = lambda n: pl.BlockSpec((tm, n), lambda i, t: (i, 0))
    full = lambda a: pl.BlockSpec(a.shape, lambda i, t: (0,) * a.ndim)
    anyspace = pl.BlockSpec(memory_space=pl.ANY)
    xs0 = jnp.zeros((ROWS_SORTED, D_MODEL), BF16)
    ws0 = jnp.zeros((ROWS_SORTED, LANES), F32)
    return pl.pallas_call(
        _dispatch_kernel,
        grid_spec=pltpu.PrefetchScalarGridSpec(
            num_scalar_prefetch=1, grid=(TOKENS // tm,),
            in_specs=[row(D_MODEL), row(LANES), full(ltri), full(ustr), anyspace, anyspace],
            out_specs=[anyspace, anyspace],
            scratch_shapes=[pltpu.VMEM((2, ROWS_LOCAL, D_MODEL), BF16), pltpu.VMEM((2, ROWS_LOCAL, LANES), F32),
                            pltpu.SemaphoreType.DMA((2, 2))]),
        out_shape=[jax.ShapeDtypeStruct((ROWS_SORTED, D_MODEL), BF16),
                   jax.ShapeDtypeStruct((ROWS_SORTED, LANES), F32)],
        input_output_aliases={5: 0, 6: 1},
        compiler_params=pltpu.CompilerParams(dimension_semantics=("arbitrary",)),
        name="moe_dispatch",
    )(tbl, x1b, wf, ltri, ustr, xs0, ws0)


def _experts_kernel(gid_ref, valid_ref, xs_ref, ws_ref, wg_ref, wu_ref, wd_ref, y_ref):
    i = pl.program_id(0)

    @pl.when(valid_ref[i] == 1)
    def _():
        x = xs_ref[...]
        gates = [_dot(x, wg_ref[e]) for e in range(MOE_EPG)]
        ups = [_dot(x, wu_ref[e]) for e in range(MOE_EPG)]
        ws = ws_ref[...]
        lane = lax.broadcasted_iota(jnp.int32, ws.shape, 1)
        hidden = []
        for e in range(MOE_EPG):
            w_tok = jnp.sum(jnp.where(lane == gid_ref[i] * MOE_EPG + e, ws, 0.0), axis=-1, keepdims=True)
            hidden.append((gates[e] * jax.nn.sigmoid(gates[e]) * ups[e] * w_tok).astype(BF16))
        wd = wd_ref[...].reshape(MOE_EPG * MOE_HIDDEN, D_MODEL)
        y_ref[...] = _dot(jnp.concatenate(hidden, axis=1), wd).astype(BF16)

    @pl.when(valid_ref[i] == 0)
    def _():
        y_ref[...] = jnp.zeros_like(y_ref)


def _experts(gid, valid, xs, ws, wg, wu, wd):
    tm = TM_MOE
    row = lambda n: pl.BlockSpec((tm, n), lambda i, gid, valid: (i, 0))
    wspec = lambda a, b: pl.BlockSpec((MOE_EPG, a, b), lambda i, gid, valid: (gid[i], 0, 0))
    return pl.pallas_call(
        _experts_kernel,
        grid_spec=pltpu.PrefetchScalarGridSpec(
            num_scalar_prefetch=2, grid=(ROWS_SORTED // tm,),
            in_specs=[row(D_MODEL), row(LANES), wspec(D_MODEL, MOE_HIDDEN), wspec(D_MODEL, MOE_HIDDEN),
                      wspec(MOE_HIDDEN, D_MODEL)],
            out_specs=row(D_MODEL)),
        out_shape=jax.ShapeDtypeStruct((ROWS_SORTED, D_MODEL), BF16),
        compiler_params=pltpu.CompilerParams(dimension_semantics=("arbitrary",),
                                             vmem_limit_bytes=EXPERTS_VMEM_BYTES),
        name="moe_experts",
    )(gid, valid, xs, ws, wg, wu, wd)


def _combine_kernel(tbl_ref, wf_ref, x1_ref, ltri_ref, ustr_ref, g_ref, b_ref, y_hbm, o_ref, ybuf, sems):
    tile = pl.program_id(0)
    slot = tile % 2
    tm = x1_ref.shape[0]

    def copies(t, s, act):
        _granule_copies(tbl_ref, t, (ybuf.at[s],), (y_hbm,), (sems.at[s],), False, act)

    def fetch(t, s):
        ybuf[s] = jnp.zeros(ybuf.shape[1:], ybuf.dtype)
        copies(t, s, lambda c: c.start())

    @pl.when(tile == 0)
    def _():
        fetch(tile, slot)

    @pl.when(tile + 1 < pl.num_programs(0))
    def _():
        fetch(tile + 1, 1 - slot)

    dest = _group_dest(wf_ref[...], ltri_ref, ustr_ref)
    c = lax.broadcasted_iota(jnp.int32, (tm, ROWS_LOCAL), 1).astype(F32)
    unperm = jnp.where(c == dest, 1.0, 0.0).astype(BF16)
    copies(tile, slot, lambda c: c.wait())
    ffn = _dot(unperm, ybuf[slot])
    o_ref[...] = _layer_norm(DN_ALPHA * x1_ref[...] + ffn, g_ref[...], b_ref[...])


def _combine(tbl, wf, x1, ltri, ustr, g, b, y):
    tm = TM_PROJ
    row = lambda n: pl.BlockSpec((tm, n), lambda i, t: (i, 0))
    full = lambda a: pl.BlockSpec(a.shape, lambda i, t: (0,) * a.ndim)
    return pl.pallas_call(
        _combine_kernel,
        grid_spec=pltpu.PrefetchScalarGridSpec(
            num_scalar_prefetch=1, grid=(TOKENS // tm,),
            in_specs=[row(LANES), row(D_MODEL), full(ltri), full(ustr), full(g), full(b),
                      pl.BlockSpec(memory_space=pl.ANY)],
            out_specs=row(D_MODEL),
            scratch_shapes=[pltpu.VMEM((2, ROWS_LOCAL, D_MODEL), BF16), pltpu.SemaphoreType.DMA((2,))]),
        out_shape=jax.ShapeDtypeStruct((TOKENS, D_MODEL), F32),
        compiler_params=pltpu.CompilerParams(dimension_semantics=("arbitrary",)),
        name="moe_combine",
    )(tbl, wf, x1, ltri, ustr, g, b, y)


def _routing_tables(cnt):
    n_tiles = cnt.shape[0]
    c = cnt[:, 0, :MOE_GROUPS].astype(jnp.int32)
    gran = (c + GRAN - 1) // GRAN
    local = jnp.cumsum(gran, axis=1) - gran
    per_tile = TM_MOE // GRAN
    tiles_g = (jnp.sum(gran, axis=0) + per_tile - 1) // per_tile
    ends = jnp.cumsum(tiles_g)
    base = (ends - tiles_g) * per_tile
    glob = base[None, :] + jnp.cumsum(gran, axis=0) - gran
    tbl = jnp.concatenate([gran, local, glob], axis=1).reshape(n_tiles * TBL_W)
    idx = jnp.arange(ROWS_SORTED // TM_MOE)
    gid = jnp.minimum(jnp.sum(idx[:, None] >= ends[None, :], axis=1), MOE_GROUPS - 1).astype(jnp.int32)
    valid = (idx < ends[-1]).astype(jnp.int32)
    return tbl, gid, valid


def _sort_tables():
    t = jnp.arange(TM_PROJ)
    ltri = (t[None, :] < t[:, None]).astype(BF16)
    l = jnp.arange(LANES)
    ustr = (l[:, None] < l[None, :]).astype(BF16)
    return ltri, ustr


def _dup_halves(t):
    lane = lax.broadcasted_iota(jnp.int32, t.shape, 1)
    r = pltpu.roll(t, HALF, 1)
    return jnp.where(lane < HALF, t, r), jnp.where(lane < HALF, r, t)


def _nsa_in_kernel(x_ref, w_ref, gb_ref, q_ref, kc_ref, vc_ref, ks_ref, vst_ref, kw_ref, vwt_ref, gate_ref):
    h = _dot(x_ref[...].astype(BF16), w_ref[...])
    q_ref[...] = (h[:, 0:C_MIX] * (NSA_SCALE * LOG2E)).astype(BF16)
    kc_ref[...] = h[:, 1024:1152].astype(BF16)
    vc_ref[...] = h[:, 1152:1280].astype(BF16)
    for idx, ref in ((0, ks_ref), (2, kw_ref)):
        d0, d1 = _dup_halves(h[:, 1280 + idx * LANES:1280 + (idx + 1) * LANES])
        ref[:, 0:LANES] = d0.astype(BF16)
        ref[:, LANES:2 * LANES] = d1.astype(BF16)
    lane = lax.broadcasted_iota(jnp.int32, (h.shape[0], LANES), 1)
    for idx, ref in ((1, vst_ref), (3, vwt_ref)):
        tail = jnp.where(lane == HALF, 1.0, 0.0)
        _store_transposed(ref, [jnp.where(lane < HALF, d, tail)
                                for d in _dup_halves(h[:, 1280 + idx * LANES:1280 + (idx + 1) * LANES])])
    for g in range(NSA_GROUPS):
        gate_ref[g] = jax.nn.sigmoid(h[:, 1792 + g * LANES:1792 + (g + 1) * LANES] + gb_ref[g])


def _nsa_in(x2, w, gb):
    tm = TM_PROJ
    row = lambda n: pl.BlockSpec((tm, n), lambda i: (i, 0))
    full = lambda a: pl.BlockSpec(a.shape, lambda i: (0,) * a.ndim)
    sd = jax.ShapeDtypeStruct
    vt_spec = pl.BlockSpec((NSA_GROUPS, tm // TK_ATT, LANES, TK_ATT), lambda i: (0, i, 0, 0))
    vt_shape = sd((NSA_GROUPS, TOKENS // TK_ATT, LANES, TK_ATT), BF16)
    return pl.pallas_call(
        _nsa_in_kernel,
        grid=(TOKENS // tm,),
        in_specs=[row(D_MODEL), full(w), full(gb)],
        out_specs=[row(C_MIX), row(LANES), row(LANES), row(2 * LANES), vt_spec, row(2 * LANES), vt_spec,
                   pl.BlockSpec((NSA_GROUPS, tm, LANES), lambda i: (0, i, 0))],
        out_shape=[sd((TOKENS, C_MIX), BF16), sd((TOKENS, LANES), BF16), sd((TOKENS, LANES), BF16),
                   sd((TOKENS, 2 * LANES), BF16), vt_shape, sd((TOKENS, 2 * LANES), BF16), vt_shape,
                   sd((NSA_GROUPS, TOKENS, LANES), F32)],
        compiler_params=pltpu.CompilerParams(dimension_semantics=("arbitrary",)),
        name="nsa_in",
    )(x2, w, gb)


def _compress_kernel(kc_ref, vc_ref, pk_ref, pv_ref, wk1_ref, wv1_ref, wk2_ref, wv2_ref, ko_ref, vo_ref):
    for a_ref, p_ref, w1_ref, w2_ref, o_ref in ((kc_ref, pk_ref, wk1_ref, wk2_ref, ko_ref),
                                                (vc_ref, pv_ref, wv1_ref, wv2_ref, vo_ref)):
        a = a_ref[0].astype(F32)
        a0 = (a + p_ref[0]).astype(BF16)
        a1 = (a + p_ref[1]).astype(BF16)
        outs = []
        for g in range(NSA_GROUPS):
            first = _dot(a0, w1_ref[g])
            second = _dot(a1, w1_ref[NSA_GROUPS + g])
            hid = first + pltpu.roll(second, NSA_NCMP - 1, 0)
            outs.append(_dot(_gelu(hid).astype(BF16), w2_ref[...]))
        if o_ref is ko_ref:
            o_ref[0] = jnp.concatenate(outs, axis=1).astype(BF16)
        else:
            for g in range(NSA_GROUPS):
                o_ref[0, g] = outs[g].T.astype(BF16)


def _compress(kc_r, vc_r, pk, pv, wk1, wv1, wk2, wv2):
    blk = pl.BlockSpec((1, NSA_NCMP, NSA_CMP_STRIDE * LANES), lambda b: (b, 0, 0))
    full = lambda a: pl.BlockSpec(a.shape, lambda b: (0,) * a.ndim)
    sd = jax.ShapeDtypeStruct
    return pl.pallas_call(
        _compress_kernel,
        grid=(BATCH,),
        in_specs=[blk, blk, full(pk), full(pv), full(wk1), full(wv1), full(wk2), full(wv2)],
        out_specs=[pl.BlockSpec((1, NSA_NCMP, 2 * LANES), lambda b: (b, 0, 0)),
                   pl.BlockSpec((1, NSA_GROUPS, LANES, NSA_NCMP), lambda b: (b, 0, 0, 0))],
        out_shape=[sd((BATCH, NSA_NCMP, 2 * LANES), BF16), sd((BATCH, NSA_GROUPS, LANES, NSA_NCMP), BF16)],
        compiler_params=pltpu.CompilerParams(dimension_semantics=("arbitrary",)),
        name="nsa_compress",
    )(kc_r, vc_r, pk, pv, wk1, wv1, wk2, wv2)


def _nsa_attn_kernel(q_ref, kc_ref, vct_ref, ks_ref, vst_ref, kw_ref, vwt_ref, gate_ref, cover_ref,
                     o_ref, sel_ref, m_ref, acc_ref, sa_ref, sb_ref):
    tq, tk, hpg = TQ_NSA, TK_ATT, NSA_HPG
    qi = pl.program_id(2)
    q0 = qi * tq
    lane = lax.broadcasted_iota(jnp.int32, (tq, LANES), 1)
    t_tok = q0 + lax.broadcasted_iota(jnp.int32, (1, tq), 1)
    head = lambda x, i: x[:, i * tq:(i + 1) * tq]
    heads = range(hpg)

    parts = []
    for p in range(hpg // 2):
        qp = q_ref[0, :, p * LANES:(p + 1) * LANES]
        zero = jnp.zeros_like(qp)
        parts.append(jnp.where(lane < HALF, qp, zero))
        parts.append(jnp.where(lane < HALF, zero, qp))
    qs = jnp.concatenate(parts, axis=0)

    s_c = _dot_nt(kc_ref[0], qs)
    n_sub = lax.broadcasted_iota(jnp.int32, (NSA_NCMP, 1), 0)
    vis = t_tok >= n_sub * NSA_CMP_STRIDE + (NSA_CMP_LEN - 1)
    p_sum = jnp.zeros((NSA_NCMP, tq), F32)
    p_c = []
    for i in heads:
        sm = jnp.where(vis, head(s_c, i), NEG)
        e = jnp.exp2(sm - jnp.max(sm, axis=0, keepdims=True))
        p = jnp.where(vis, e / jnp.sum(e, axis=0, keepdims=True), 0.0)
        p_sum = p_sum + p
        p_c.append(p.astype(BF16))
    o_c = _dot(vct_ref[0, 0], jnp.concatenate(p_c, axis=1))

    imp = jnp.dot(cover_ref[...], p_sum, preferred_element_type=F32,
                  precision=lax.Precision.HIGHEST)[0:NSA_NSEL]
    jj = lax.broadcasted_iota(jnp.int32, (NSA_NSEL, 1), 0)
    tb = t_tok // NSA_SEL_LEN
    forced = (jj == 0) | (jj == tb) | (jj == tb - 1)
    score = jnp.where(forced, NSA_FORCE, jnp.where(jj <= tb, imp, -NSA_FORCE))
    rank = jnp.zeros((NSA_NSEL, tq), jnp.int32)
    for i in range(NSA_NSEL):
        si = score[i:i + 1, :]
        beats = (si > score) | ((si == score) & (jj > i))
        rank = rank + beats.astype(jnp.int32)
    sel_ref[...] = jnp.where(rank < NSA_TOPK, 1.0, 0.0)

    m_ref[...] = jnp.full(m_ref.shape, NEG, F32)
    acc_ref[...] = jnp.zeros(acc_ref.shape, F32)
    k_sub = lax.broadcasted_iota(jnp.int32, (tk, 1), 0)
    blocks_per_tile = tk // NSA_SEL_LEN

    def scores_into(ref, kt):
        ref[...] = _dot_nt(ks_ref[0, pl.ds(pl.multiple_of(kt * tk, tk), tk), :], qs)

    def process(ref, kt):
        r0 = kt * tk
        s = ref[...]
        member = jnp.concatenate(
            [jnp.broadcast_to(sel_ref[pl.ds(kt * blocks_per_tile + a, 1), :], (NSA_SEL_LEN, tq))
             for a in range(blocks_per_tile)], axis=0)
        mask = (member > 0.5) & (r0 + k_sub <= t_tok)
        vt = vst_ref[0, kt]
        for i in heads:
            cols = slice(i * tq, (i + 1) * tq)
            probs, alphas = [], []
            for part in range(tq // LANES):
                sub = slice(part * LANES, (part + 1) * LANES)
                sm = jnp.where(mask[:, sub], s[:, i * tq + part * LANES:i * tq + (part + 1) * LANES], NEG)
                m_old = m_ref[i, :, sub]
                m_new = jnp.maximum(m_old, jnp.max(sm, axis=0, keepdims=True))
                m_ref[i, :, sub] = m_new
                probs.append(jnp.exp2(sm - m_new).astype(BF16))
                alphas.append(jnp.exp2(m_old - m_new))
            pv = _dot(vt, jnp.concatenate(probs, axis=1))
            acc_ref[:, cols] = jnp.concatenate(alphas, axis=1) * acc_ref[:, cols] + pv

    n_tiles = (q0 + tq - 1) // tk + 1
    odd = n_tiles % 2
    last = n_tiles - 1
    scores_into(sa_ref, 0)

    @pl.when(odd == 1)
    def _():
        process(sa_ref, 0)
        scores_into(sa_ref, jnp.minimum(1, last))

    def pair(k, c):
        t0 = odd + 2 * k
        scores_into(sb_ref, t0 + 1)
        process(sa_ref, t0)
        scores_into(sa_ref, jnp.minimum(t0 + 2, last))
        process(sb_ref, t0 + 1)
        return c

    lax.fori_loop(0, n_tiles // 2, pair, 0)

    c0 = jnp.maximum(q0 + tq - WIN_CHUNKS * tk, 0) // tk
    s_w, in_win = [], []
    for c in range(WIN_CHUNKS):
        r0 = pl.multiple_of((c0 + c) * tk, tk)
        s_w.append(_dot_nt(kw_ref[0, pl.ds(r0, tk), :], qs))
        kpos = r0 + k_sub
        in_win.append((kpos <= t_tok) & (kpos > t_tok - NSA_WINDOW))
    vwt = [vwt_ref[0, c0 + c] for c in range(WIN_CHUNKS)]
    o_w = []
    for i in heads:
        sm = [jnp.where(in_win[c], head(s_w[c], i), NEG) for c in range(WIN_CHUNKS)]
        m = functools.reduce(jnp.maximum, [jnp.max(x, axis=0, keepdims=True) for x in sm])
        o_w.append(functools.reduce(jnp.add, [_dot(vwt[c], jnp.exp2(sm[c] - m).astype(BF16))
                                              for c in range(WIN_CHUNKS)]))

    gt = gate_ref[0, 0].T
    o_s = acc_ref[...]
    outs = []
    for i in heads:
        c_i, s_i, w_i = head(o_c, i), head(o_s, i), o_w[i]
        outs.append(gt[i:i + 1] * c_i[:HALF]
                    + gt[hpg + i:hpg + i + 1] / s_i[HALF:HALF + 1] * s_i[:HALF]
                    + gt[2 * hpg + i:2 * hpg + i + 1] / w_i[HALF:HALF + 1] * w_i[:HALF])
    for p in range(hpg // 2):
        pair = jnp.concatenate([outs[2 * p], outs[2 * p + 1]], axis=0)
        o_ref[0, :, p * LANES:(p + 1) * LANES] = pair.T.astype(BF16)


def _nsa_attn(q3, kc2, vct, ks3, vst, kw3, vwt, gates4, cover_t):
    tq = TQ_NSA
    half_w = NSA_HPG * NSA_DH
    n_chunks = SEQ // TK_ATT
    kv = pl.BlockSpec((1, SEQ, LANES), lambda b, g, i: (b, 0, g))
    vt = pl.BlockSpec((1, n_chunks, LANES, TK_ATT), lambda b, g, i: (g, b, 0, 0))
    full = lambda a: pl.BlockSpec(a.shape, lambda b, g, i: (0,) * a.ndim)
    return pl.pallas_call(
        _nsa_attn_kernel,
        grid=(BATCH, NSA_GROUPS, SEQ // tq),
        in_specs=[pl.BlockSpec((1, tq, half_w), lambda b, g, i: (b, i, g)),
                  pl.BlockSpec((1, NSA_NCMP, LANES), lambda b, g, i: (b, 0, g)),
                  pl.BlockSpec((1, 1, LANES, NSA_NCMP), lambda b, g, i: (b, g, 0, 0)),
                  kv, vt, kv, vt,
                  pl.BlockSpec((1, 1, tq, LANES), lambda b, g, i: (g, b, i, 0)),
                  full(cover_t)],
        out_specs=pl.BlockSpec((1, tq, half_w), lambda b, g, i: (b, i, g)),
        out_shape=jax.ShapeDtypeStruct((BATCH, SEQ, C_MIX), BF16),
        scratch_shapes=[pltpu.VMEM((NSA_NSEL, tq), F32), pltpu.VMEM((NSA_HPG, 1, tq), F32),
                        pltpu.VMEM((LANES, NSA_HPG * tq), F32),
                        pltpu.VMEM((TK_ATT, NSA_HPG * tq), F32), pltpu.VMEM((TK_ATT, NSA_HPG * tq), F32)],
        compiler_params=pltpu.CompilerParams(dimension_semantics=("arbitrary",) * 3),
        name="nsa_attn",
    )(q3, kc2, vct, ks3, vst, kw3, vwt, gates4, cover_t)


def _rope_tables():
    half = MLA_ROPE // 2
    freq = jnp.exp(-math.log(ROPE_BASE) * jnp.arange(half, dtype=F32) / half)
    zeros64 = jnp.zeros((MLA_NOPE,), F32)
    zeros32 = jnp.zeros((LANES - MLA_NOPE - MLA_ROPE,), F32)
    fc = jnp.concatenate([zeros64, freq, freq, zeros32])[None, :]
    sg = jnp.concatenate([zeros64, -jnp.ones((half,), F32), jnp.ones((half,), F32), zeros32])[None, :]
    return fc, sg


def _swap_halves(w):
    half = w.shape[-1] // 2
    return jnp.concatenate([w[..., half:], w[..., :half]], axis=-1)


def _pad_last(w, n):
    return jnp.pad(w, [(0, 0)] * (w.ndim - 1) + [(0, n - w.shape[-1])])


def _ab_weights(w_in, w_uq, w_uk, w_uv):
    w_kr = w_in[:, 1408:1440]
    place = lambda w: jnp.pad(w, ((0, 0), (MLA_NOPE, LANES - MLA_NOPE - MLA_ROPE)))
    win = jnp.concatenate([w_in[:, :1408], place(w_kr), place(_swap_halves(w_kr))], axis=1).astype(BF16)
    uq = w_uq.reshape(MLA_Q_RANK, MLA_HEADS, MLA_NOPE + MLA_ROPE)
    nope, rp = uq[..., :MLA_NOPE], uq[..., MLA_NOPE:]
    q_pad = _pad_last(jnp.concatenate([nope, rp], -1), LANES).reshape(MLA_Q_RANK, MLA_HEADS * LANES)
    q_sw = _pad_last(jnp.concatenate([jnp.zeros_like(nope), _swap_halves(rp)], -1), LANES)
    wq = jnp.concatenate([q_pad, q_sw.reshape(MLA_Q_RANK, MLA_HEADS * LANES)], axis=1).astype(BF16)
    k_pad = _pad_last(w_uk.reshape(MLA_KV_RANK, MLA_HEADS, MLA_NOPE), LANES).reshape(MLA_KV_RANK, -1)
    wkv = jnp.concatenate([k_pad, w_uv], axis=1).astype(BF16)
    return win, wq, wkv


def _router_weights(w_rg, b_rg, w_re, b_re):
    wr = _pad_last(jnp.concatenate([w_re, w_rg], axis=1), LANES)
    wr_hi = wr.astype(BF16)
    wr_lo = (wr - wr_hi.astype(F32)).astype(BF16)
    br = _pad_last(jnp.concatenate([b_re, b_rg])[None, :], LANES)
    return jnp.stack([wr_hi, wr_lo]), br


def _nsa_in_weights(w_in, gate_b):
    g_cols = w_in[:, C_MIX + 768:].reshape(D_MODEL, 3, NSA_GROUPS, NSA_HPG)
    g_blocks = [_pad_last(g_cols[:, :, g, :].reshape(D_MODEL, 3 * NSA_HPG), LANES) for g in range(NSA_GROUPS)]
    w = jnp.concatenate([w_in[:, :C_MIX + 768]] + g_blocks, axis=1).astype(BF16)
    gb = gate_b.reshape(3, NSA_GROUPS, NSA_HPG)
    gb = jnp.stack([_pad_last(gb[:, g, :].reshape(1, 3 * NSA_HPG), LANES) for g in range(NSA_GROUPS)])
    return w, gb


def _compress_weights(pos, w1, w2):
    w1r = w1.reshape(2, NSA_CMP_STRIDE, NSA_DH, NSA_CMP_HIDDEN)
    zero = jnp.zeros_like(w1r)
    per_g = []
    for g in range(NSA_GROUPS):
        parts = [w1r if gg == g else zero for gg in range(NSA_GROUPS)]
        per_g.append(jnp.stack(parts, axis=2).reshape(2, NSA_CMP_STRIDE * LANES, NSA_CMP_HIDDEN))
    w1x = jnp.stack(per_g, axis=1).reshape(2 * NSA_GROUPS, NSA_CMP_STRIDE * LANES, NSA_CMP_HIDDEN)
    posr = pos.reshape(2, NSA_CMP_STRIDE, 1, NSA_DH)
    posx = jnp.broadcast_to(posr, (2, NSA_CMP_STRIDE, NSA_GROUPS, NSA_DH)).reshape(2, 1, NSA_CMP_STRIDE * LANES)
    w2x = jnp.concatenate([w2, w2], axis=1)
    return posx, w1x.astype(BF16), w2x.astype(BF16)


def _selection_tables():
    n = jnp.arange(LANES)[:, None]
    j = jnp.arange(LANES)[None, :]
    c0 = n * NSA_CMP_STRIDE
    s0 = j * NSA_SEL_LEN
    cover = ((c0 < s0 + NSA_SEL_LEN) & (c0 + NSA_CMP_LEN > s0) & (n < NSA_NCMP - 1) & (j < NSA_NSEL))
    return jnp.transpose(cover).astype(F32)


def kernel(x, positions, ab_w_in, ab_gm_ln_g, ab_gm_ln_b, ab_gm_ws, ab_gm_bs, ab_mla_q_norm,
           ab_mla_kv_norm, ab_mla_w_uq, ab_mla_w_uk, ab_mla_w_uv, ab_w_o, c_w_in, c_cmp_pos, c_w_ck1,
           c_w_ck2, c_w_cv1, c_w_cv2, c_gate_b, c_w_o, moe_w_rg, moe_b_rg, moe_w_re, moe_b_re,
           moe_w_gate, moe_w_up, moe_w_down, ln1_g, ln1_b, ln2_g, ln2_b):
    x2 = x.reshape(TOKENS, D_MODEL)
    pos2 = positions.reshape(TOKENS, 1)
    vec = lambda a: a[None, :]

    ltri, ustr = _sort_tables()

    def moe_layer(layer, x1b, x1, wf, cnt):
        tbl, gid, valid = _routing_tables(cnt)
        xs, ws = _dispatch(tbl, x1b, wf, ltri, ustr)
        y = _experts(gid, valid, xs, ws, moe_w_gate[layer].astype(BF16), moe_w_up[layer].astype(BF16),
                     moe_w_down[layer].astype(BF16))
        return _combine(tbl, wf, x1, ltri, ustr, vec(ln2_g[layer]), vec(ln2_b[layer]), y)

    win, wq, wkv = _ab_weights(ab_w_in[0], ab_mla_w_uq[0], ab_mla_w_uk[0], ab_mla_w_uv[0])
    fc, sg = _rope_tables()
    gu, vn, q, k, vt = _ab_in(x2, pos2, win, vec(ab_gm_ln_g[0]), vec(ab_gm_ln_b[0]), vec(ab_mla_q_norm[0]),
                              vec(ab_mla_kv_norm[0]), wq, wkv, fc, sg)
    yb = _mla_attn(q.reshape(BATCH, SEQ, -1), k.reshape(BATCH, SEQ, -1), vt)
    wr, br = _router_weights(moe_w_rg[0], moe_b_rg[0], moe_w_re[0], moe_b_re[0])
    x1, x1b, wf, cnt = _mix_out(x2, yb.reshape(TOKENS, -1), ab_w_o[0].astype(BF16), vec(ln1_g[0]),
                                vec(ln1_b[0]), wr, br, gm=(gu, vn, ab_gm_ws[0], jnp.transpose(ab_gm_bs[0])))
    x2 = moe_layer(0, x1b, x1, wf, cnt)

    w_nsa, gb = _nsa_in_weights(c_w_in[0], c_gate_b[0])
    q, kc, vc, ks, vst, kw, vwt, gates = _nsa_in(x2, w_nsa, gb)
    pk, wk1, wk2 = _compress_weights(c_cmp_pos[0, 0], c_w_ck1[0], c_w_ck2[0])
    pv, wv1, wv2 = _compress_weights(c_cmp_pos[0, 1], c_w_cv1[0], c_w_cv2[0])
    blocks = lambda a: a.reshape(BATCH, NSA_NCMP, NSA_CMP_STRIDE * LANES)
    kc2, vct = _compress(blocks(kc), blocks(vc), pk, pv, wk1, wv1, wk2, wv2)
    b3 = lambda a: a.reshape(BATCH, SEQ, -1)
    o = _nsa_attn(b3(q), kc2, vct, b3(ks), vst, b3(kw), vwt,
                  gates.reshape(NSA_GROUPS, BATCH, SEQ, LANES), _selection_tables())
    wr, br = _router_weights(moe_w_rg[1], moe_b_rg[1], moe_w_re[1], moe_b_re[1])
    x1, x1b, wf, cnt = _mix_out(x2, o.reshape(TOKENS, -1), c_w_o[0].astype(BF16), vec(ln1_g[1]), vec(ln1_b[1]),
                                wr, br)
    x2 = moe_layer(1, x1b, x1, wf, cnt)
    return x2.reshape(BATCH, SEQ, D_MODEL)
```

```python
import functools
import math

import jax
import jax.numpy as jnp
from jax import lax
from jax.experimental import pallas as pl
from jax.experimental.pallas import tpu as pltpu

F32 = jnp.float32
BF16 = jnp.bfloat16

D_MODEL = 1024
BATCH = 16
SEQ = 2048
TOKENS = BATCH * SEQ
DEPTH = 2
DN_ALPHA = (2.0 * DEPTH) ** 0.25
LN_EPS = 1e-5
NEG = -1e30
LOG2E = math.log2(math.e)
LANES = 128
HALF = LANES // 2

GM_WIDTH = 512
GM_GROUPS = 4
GM_CHUNK = 128

MLA_HEADS = 8
MLA_NOPE = 64
MLA_ROPE = 32
MLA_V = 64
MLA_Q_RANK = 256
MLA_KV_RANK = 128
ROPE_BASE = 10000.0
MLA_SCALE = (MLA_NOPE + MLA_ROPE) ** -0.5

NSA_HEADS = 16
NSA_GROUPS = 2
NSA_HPG = 8
NSA_DH = 64
NSA_CMP_LEN = 32
NSA_CMP_STRIDE = 16
NSA_CMP_HIDDEN = 256
NSA_SEL_LEN = 64
NSA_TOPK = 8
NSA_WINDOW = 512
NSA_NSEL = SEQ // NSA_SEL_LEN
NSA_NCMP = SEQ // NSA_CMP_STRIDE
NSA_FORCE = 1e4
NSA_SCALE = NSA_DH ** -0.5
C_MIX = NSA_HEADS * NSA_DH

MOE_GROUPS = 4
MOE_EPG = 8
MOE_EXPERTS = 32
MOE_HIDDEN = 256

TM_PROJ = 512
TQ_MLA = 256
TQ_NSA = 256
TK_ATT = 256
WIN_CHUNKS = -(-(NSA_WINDOW + TQ_NSA) // TK_ATT)
MLA_VT_ROWS = LANES + 16
TM_MOE = 512
EXPERTS_VMEM_BYTES = (2 * 3 * MOE_EPG * D_MODEL * MOE_HIDDEN * 2 + 2 * 2 * TM_MOE * D_MODEL * 2
                      + 2 * MOE_EPG * TM_MOE * MOE_HIDDEN * 4 + TM_MOE * MOE_EPG * MOE_HIDDEN * 2
                      + 2 * TM_MOE * D_MODEL * 4)

GSEL_LANE = MOE_EXPERTS
GRAN = 16
TBL_W = 3 * MOE_GROUPS
ROWS_LOCAL = 640
ROWS_SORTED = TM_MOE * (TOKENS // TM_MOE + MOE_GROUPS
                        + -(-(TOKENS // TM_PROJ) * MOE_GROUPS * (GRAN - 1) // TM_MOE))


def _dot(a, b):
    return jnp.dot(a, b, preferred_element_type=F32)


def _dot_nt(a, b):
    return lax.dot_general(a, b, (((1,), (1,)), ((), ())), preferred_element_type=F32)


def _gelu(x):
    return 0.5 * x * (1.0 + jnp.tanh(math.sqrt(2.0 / math.pi) * (x + 0.044715 * (x * x * x))))


def _layer_norm(x, g, b):
    mu = jnp.mean(x, axis=-1, keepdims=True)
    xc = x - mu
    var = jnp.mean(xc * xc, axis=-1, keepdims=True)
    return xc * lax.rsqrt(var + LN_EPS) * g + b


def _rms_norm(x, g):
    return x * lax.rsqrt(jnp.mean(x * x, axis=-1, keepdims=True) + LN_EPS) * g


def _store_transposed(ref, blocks):
    extra = ref.shape[2] - LANES
    if extra:
        ones_rows = jnp.where(lax.broadcasted_iota(jnp.int32, (extra, TK_ATT), 0) == 0, 1.0, 0.0)
    for n, blk in enumerate(blocks):
        t = blk.T
        for c in range(t.shape[1] // TK_ATT):
            chunk = t[:, c * TK_ATT:(c + 1) * TK_ATT]
            if extra:
                chunk = jnp.concatenate([chunk, ones_rows], axis=0)
            ref[n, c] = chunk.astype(ref.dtype)


def _ab_in_kernel(x_ref, pos_ref, win_ref, lng_ref, lnb_ref, qg_ref, kvg_ref, wq_ref, wkv_ref,
                  fc_ref, gu_ref, vn_ref, q_ref, k_ref, vt_ref):
    h = _dot(x_ref[...].astype(BF16), win_ref[...])
    gu_ref[...] = _gelu(h[:, 0:512]).astype(BF16)
    vn_ref[...] = _layer_norm(_gelu(h[:, 512:1024]), lng_ref[...], lnb_ref[...]).astype(BF16)

    tm = x_ref.shape[0]
    ang = fc_ref[...] * pos_ref[0].astype(F32)
    cos_t, sin_t = jnp.cos(ang), jnp.sin(ang)
    ones_t, zeros_t = jnp.ones((MLA_NOPE, tm), F32), jnp.zeros((MLA_NOPE, tm), F32)
    pad = LANES - MLA_NOPE - MLA_ROPE
    cc = jnp.concatenate([ones_t, cos_t, cos_t, ones_t[:pad]], axis=0).T
    ss = jnp.concatenate([zeros_t, -sin_t, sin_t, zeros_t[:pad]], axis=0).T

    cqn = _rms_norm(h[:, 1024:1280], qg_ref[...]).astype(BF16)
    qq = _dot(cqn, wq_ref[...])
    for hd in range(MLA_HEADS):
        lo, hi = hd * LANES, (hd + 1) * LANES
        q_ref[:, lo:hi] = ((qq[:, lo:hi] * cc + qq[:, 1024 + lo:1024 + hi] * ss) * (MLA_SCALE * LOG2E)).astype(BF16)

    ckvn = _rms_norm(h[:, 1280:1408], kvg_ref[...]).astype(BF16)
    kv = _dot(ckvn, wkv_ref[...])
    k_rope = h[:, 1408:1536] * cc + h[:, 1536:1664] * ss
    for hd in range(MLA_HEADS):
        lo, hi = hd * LANES, (hd + 1) * LANES
        k_ref[:, lo:hi] = (kv[:, lo:hi] + k_rope).astype(BF16)
    _store_transposed(vt_ref, [kv[:, 1024 + p * LANES:1024 + (p + 1) * LANES] for p in range(MLA_HEADS // 2)])


def _ab_in(x2, pos3, win, lng, lnb, qg, kvg, wq, wkv, fc):
    tm = TM_PROJ
    row = lambda n: pl.BlockSpec((tm, n), lambda i: (i, 0))
    full = lambda a: pl.BlockSpec(a.shape, lambda i: (0,) * a.ndim)
    return pl.pallas_call(
        _ab_in_kernel,
        grid=(TOKENS // tm,),
        in_specs=[row(D_MODEL), pl.BlockSpec((1, 1, tm), lambda i: (i, 0, 0)), full(win), full(lng), full(lnb),
                  full(qg), full(kvg), full(wq), full(wkv), full(fc)],
        out_specs=[row(512), row(512), row(1024), row(1024),
                   pl.BlockSpec((MLA_HEADS // 2, tm // TK_ATT, MLA_VT_ROWS, TK_ATT), lambda i: (0, i, 0, 0))],
        out_shape=[jax.ShapeDtypeStruct((TOKENS, 512), BF16), jax.ShapeDtypeStruct((TOKENS, 512), BF16),
                   jax.ShapeDtypeStruct((TOKENS, 1024), BF16), jax.ShapeDtypeStruct((TOKENS, 1024), BF16),
                   jax.ShapeDtypeStruct((MLA_HEADS // 2, TOKENS // TK_ATT, MLA_VT_ROWS, TK_ATT), BF16)],
        compiler_params=pltpu.CompilerParams(dimension_semantics=("arbitrary",)),
        name="ab_in",
    )(x2, pos3, win, lng, lnb, qg, kvg, wq, wkv, fc)


def _mla_attn_kernel(q_ref, k_ref, vt_ref, o_ref, m_ref, l_ref, acc_ref):
    tq, tk = TQ_MLA, TK_ATT
    qi = pl.program_id(1)
    krow = lax.broadcasted_iota(jnp.int32, (tk, tq), 0)
    qcol = lax.broadcasted_iota(jnp.int32, (tk, tq), 1)
    top = lax.broadcasted_iota(jnp.int32, (LANES, tq), 0) < HALF
    m_ref[...] = jnp.full(m_ref.shape, NEG, F32)
    l_ref[...] = jnp.zeros(l_ref.shape, F32)
    acc_ref[...] = jnp.zeros(acc_ref.shape, F32)

    def tile(j, masked):
        r0 = pl.multiple_of(j * tk, tk)
        scores = [_dot_nt(k_ref[0, pl.ds(r0, tk), h * LANES:(h + 1) * LANES], q_ref[0, :, h * LANES:(h + 1) * LANES])
                  for h in range(MLA_HEADS)]
        for pr in range(MLA_HEADS // 2):
            probs, alphas = [], []
            for h in (2 * pr, 2 * pr + 1):
                s = jnp.where(krow <= qcol, scores[h], NEG) if masked else scores[h]
                m_old = m_ref[h]
                m_new = jnp.maximum(m_old, jnp.max(s, axis=0, keepdims=True))
                alphas.append(jnp.exp2(m_old - m_new))
                probs.append(jnp.exp2(s - m_new).astype(BF16))
                m_ref[h] = m_new
            pv = _dot(vt_ref[pr, j], jnp.concatenate(probs, axis=1))
            for n, hh in enumerate((2 * pr, 2 * pr + 1)):
                l_ref[hh] = alphas[n] * l_ref[hh] + pv[LANES:LANES + 1, n * tq:(n + 1) * tq]
            a = jnp.where(top, alphas[0], alphas[1])
            acc_ref[pr] = a * acc_ref[pr] + jnp.where(top, pv[:LANES, :tq], pv[:LANES, tq:])

    def body(j, c):
        tile(j, False)
        return c

    lax.fori_loop(0, qi, body, 0)
    tile(qi, True)
    for pr in range(MLA_HEADS // 2):
        l = jnp.where(top, l_ref[2 * pr], l_ref[2 * pr + 1])
        o_ref[0, :, pr * LANES:(pr + 1) * LANES] = (acc_ref[pr] / l).T.astype(BF16)


def _mla_attn(q3, k3, vt):
    tq = TQ_MLA
    n_chunks = SEQ // TK_ATT
    return pl.pallas_call(
        _mla_attn_kernel,
        grid=(BATCH, SEQ // tq),
        in_specs=[pl.BlockSpec((1, tq, MLA_HEADS * LANES), lambda b, i: (b, i, 0)),
                  pl.BlockSpec((1, SEQ, MLA_HEADS * LANES), lambda b, i: (b, 0, 0)),
                  pl.BlockSpec((MLA_HEADS // 2, n_chunks, MLA_VT_ROWS, TK_ATT), lambda b, i: (0, b, 0, 0))],
        out_specs=pl.BlockSpec((1, tq, MLA_HEADS * MLA_V), lambda b, i: (b, i, 0)),
        out_shape=jax.ShapeDtypeStruct((BATCH, SEQ, MLA_HEADS * MLA_V), BF16),
        scratch_shapes=[pltpu.VMEM((MLA_HEADS, 1, tq), F32), pltpu.VMEM((MLA_HEADS, 1, tq), F32),
                        pltpu.VMEM((MLA_HEADS // 2, LANES, tq), F32)],
        compiler_params=pltpu.CompilerParams(dimension_semantics=("arbitrary",) * 2),
        name="mla_attn",
    )(q3, k3, vt)


def _router(x1, wr, br):
    tm = x1.shape[0]
    x_hi = x1.astype(BF16)
    x_lo = (x1 - x_hi.astype(F32)).astype(BF16)
    parts = _dot(jnp.concatenate([x_hi, x_lo], axis=0), wr)
    logits = (parts[:tm, :LANES] + (parts[:tm, LANES:] + parts[tm:, :LANES]) + parts[tm:, LANES:]) + br
    lane = lax.broadcasted_iota(jnp.int32, (tm, LANES), 1).astype(F32)
    big = 1e6
    is_g = (lane >= MOE_EXPERTS) & (lane < MOE_EXPERTS + MOE_GROUPS)
    gl = jnp.where(is_g, logits, NEG)
    gmax = jnp.max(gl, axis=-1, keepdims=True)
    g_sel = jnp.min(jnp.where(is_g & (gl == gmax), lane, big), axis=-1, keepdims=True) - MOE_EXPERTS
    g_w = 1.0 / jnp.sum(jnp.where(is_g, jnp.exp(gl - gmax), 0.0), axis=-1, keepdims=True)
    in_grp = (lane >= g_sel * MOE_EPG) & (lane < (g_sel + 1) * MOE_EPG)
    el = jnp.where(in_grp, logits, NEG)
    emax = jnp.max(el, axis=-1, keepdims=True)
    ee = jnp.where(in_grp, jnp.exp(el - emax), 0.0)
    pe = ee / jnp.sum(ee, axis=-1, keepdims=True)
    p1 = jnp.max(pe, axis=-1, keepdims=True)
    i1 = jnp.min(jnp.where(in_grp & (pe == p1), lane, big), axis=-1, keepdims=True)
    rest = in_grp & (lane != i1)
    pr = jnp.where(rest, pe, -1.0)
    p2 = jnp.max(pr, axis=-1, keepdims=True)
    i2 = jnp.min(jnp.where(rest & (pr == p2), lane, big), axis=-1, keepdims=True)
    tot = p1 + p2
    wf = jnp.where(lane == i1, p1 / tot * g_w, jnp.where(lane == i2, p2 / tot * g_w, 0.0))
    wf = jnp.where(lane == GSEL_LANE, g_sel, wf)
    cnt = jnp.sum(jnp.where(lane == g_sel, 1.0, 0.0), axis=0, keepdims=True)
    return wf, cnt


def _mix_out_kernel(*refs, gmlp):
    if gmlp:
        (x_ref, gu_ref, vn_ref, ws_ref, bs_ref, yb_ref, wo_ref, g_ref, b_ref, wr_ref, br_ref,
         x1_ref, x1b_ref, wf_ref, cnt_ref, ya_ref) = refs
        tm = x_ref.shape[0]
        r = lax.broadcasted_iota(jnp.int32, (GM_CHUNK, GM_CHUNK), 0)
        c = lax.broadcasted_iota(jnp.int32, (GM_CHUNK, GM_CHUNK), 1)
        for g in range(GM_GROUPS):
            ws = jnp.where(r >= c, ws_ref[g], 0.0).astype(BF16)
            bias = bs_ref[:, g:g + 1]
            for ch in range(tm // GM_CHUNK):
                rows = slice(ch * GM_CHUNK, (ch + 1) * GM_CHUNK)
                cols = slice(g * LANES, (g + 1) * LANES)
                s = _dot(ws, vn_ref[rows, cols]) + bias
                ya_ref[rows, cols] = (gu_ref[rows, cols].astype(F32) * s).astype(BF16)
        mix = _dot(ya_ref[...], wo_ref[0:GM_WIDTH, :]) + _dot(yb_ref[...], wo_ref[GM_WIDTH:, :])
    else:
        x_ref, y_ref, wo_ref, g_ref, b_ref, wr_ref, br_ref, x1_ref, x1b_ref, wf_ref, cnt_ref = refs
        mix = _dot(y_ref[...], wo_ref[...])
    x1 = _layer_norm(DN_ALPHA * x_ref[...] + mix, g_ref[...], b_ref[...])
    x1_ref[...] = x1
    x1b_ref[...] = x1.astype(BF16)
    wf, cnt = _router(x1, wr_ref[...], br_ref[...])
    wf_ref[...] = wf
    cnt_ref[0] = jnp.broadcast_to(cnt, cnt_ref.shape[1:])


def _mix_out(x2, ys, wo, g, b, wr, br, gm=None):
    tm = TM_PROJ
    row = lambda n: pl.BlockSpec((tm, n), lambda i: (i, 0))
    full = lambda a: pl.BlockSpec(a.shape, lambda i: (0,) * a.ndim)
    if gm is not None:
        gu, vn, ws, bs = gm
        args = (x2, gu, vn, ws, bs, ys, wo, g, b, wr, br)
        in_specs = [row(D_MODEL), row(512), row(512), full(ws), full(bs), row(512), full(wo),
                    full(g), full(b), full(wr), full(br)]
        scratch = [pltpu.VMEM((tm, GM_WIDTH), BF16)]
    else:
        args = (x2, ys, wo, g, b, wr, br)
        in_specs = [row(D_MODEL), row(C_MIX), full(wo), full(g), full(b), full(wr), full(br)]
        scratch = []
    return pl.pallas_call(
        functools.partial(_mix_out_kernel, gmlp=gm is not None),
        grid=(TOKENS // tm,),
        in_specs=in_specs,
        out_specs=[row(D_MODEL), row(D_MODEL), row(LANES), pl.BlockSpec((1, 8, LANES), lambda i: (i, 0, 0))],
        out_shape=[jax.ShapeDtypeStruct((TOKENS, D_MODEL), F32), jax.ShapeDtypeStruct((TOKENS, D_MODEL), BF16),
                   jax.ShapeDtypeStruct((TOKENS, LANES), F32),
                   jax.ShapeDtypeStruct((TOKENS // tm, 8, LANES), F32)],
        scratch_shapes=scratch,
        compiler_params=pltpu.CompilerParams(dimension_semantics=("arbitrary",)),
        name="mix_out_gmlp" if gm is not None else "mix_out",
    )(*args)


def _group_dest(wf, ltri_ref, ustr_ref):
    tm = wf.shape[0]
    lane = lax.broadcasted_iota(jnp.int32, (tm, LANES), 1).astype(F32)
    onehot = jnp.where(lane == wf[:, GSEL_LANE:GSEL_LANE + 1], 1.0, 0.0)
    before = _dot(ltri_ref[...], onehot.astype(BF16))
    cnt = jnp.sum(onehot, axis=0, keepdims=True)
    gran = jnp.floor((cnt + (GRAN - 1)) * (1.0 / GRAN))
    start = _dot(jnp.broadcast_to(gran, (8, LANES)).astype(BF16), ustr_ref[...])[0:1]
    return jnp.sum(onehot * (GRAN * start + before), axis=-1, keepdims=True)


def _granule_copies(tbl_ref, tile, vmem_bufs, hbm_refs, sems, to_hbm, act):
    for g in range(MOE_GROUPS):
        n = tbl_ref[tile * TBL_W + g]
        loc = tbl_ref[tile * TBL_W + MOE_GROUPS + g]
        glb = tbl_ref[tile * TBL_W + 2 * MOE_GROUPS + g]

        def body(k, c, loc=loc, glb=glb):
            lo = pl.multiple_of((loc + k) * GRAN, GRAN)
            hi = pl.multiple_of((glb + k) * GRAN, GRAN)
            for idx, (vb, hb) in enumerate(zip(vmem_bufs, hbm_refs)):
                v_sl, h_sl = vb.at[pl.ds(lo, GRAN)], hb.at[pl.ds(hi, GRAN)]
                src, dst = (v_sl, h_sl) if to_hbm else (h_sl, v_sl)
                act(pltpu.make_async_copy(src, dst, sems[idx]))
            return c

        lax.fori_loop(0, n, body, 0)


def _dispatch_kernel(tbl_ref, xb_ref, wf_ref, ltri_ref, ustr_ref, xs_in, ws_in, xs_out, ws_out, xbuf, wbuf, sems):
    del xs_in, ws_in
    tile = pl.program_id(0)
    slot = tile % 2
    tm = xb_ref.shape[0]
    wf = wf_ref[...]
    dest = _group_dest(wf, ltri_ref, ustr_ref)
    dest_row = jnp.broadcast_to(dest, (tm, LANES)).T[0:1]
    r = lax.broadcasted_iota(jnp.int32, (ROWS_LOCAL, tm), 0).astype(F32)
    perm = jnp.where(r == dest_row, 1.0, 0.0).astype(BF16)
    xbuf[slot] = _dot(perm, xb_ref[...]).astype(BF16)
    hi = wf.astype(BF16)
    r1 = wf - hi.astype(F32)
    mid = r1.astype(BF16)
    lo = (r1 - mid.astype(F32)).astype(BF16)
    pieces = _dot(perm, jnp.concatenate([hi, mid, lo], axis=1))
    wbuf[slot] = pieces[:, :LANES] + pieces[:, LANES:2 * LANES] + pieces[:, 2 * LANES:]

    def copies(t, s, act):
        _granule_copies(tbl_ref, t, (xbuf.at[s], wbuf.at[s]), (xs_out, ws_out), (sems.at[s, 0], sems.at[s, 1]),
                        True, act)

    copies(tile, slot, lambda c: c.start())

    @pl.when(tile > 0)
    def _():
        copies(tile - 1, 1 - slot, lambda c: c.wait())

    @pl.when(tile == pl.num_programs(0) - 1)
    def _():
        copies(tile, slot, lambda c: c.wait())


def _dispatch(tbl, x1b, wf, ltri, ustr):
    tm = TM_PROJ
    row = lambda n: pl.BlockSpec((tm, n), lambda i, t: (i, 0))
    full = lambda a: pl.BlockSpec(a.shape, lambda i, t: (0,) * a.ndim)
    anyspace = pl.BlockSpec(memory_space=pl.ANY)
    xs0 = jnp.zeros((ROWS_SORTED, D_MODEL), BF16)
    ws0 = jnp.zeros((ROWS_SORTED, LANES), F32)
    return pl.pallas_call(
        _dispatch_kernel,
        grid_spec=pltpu.PrefetchScalarGridSpec(
            num_scalar_prefetch=1, grid=(TOKENS // tm,),
            in_specs=[row(D_MODEL), row(LANES), full(ltri), full(ustr), anyspace, anyspace],
            out_specs=[anyspace, anyspace],
            scratch_shapes=[pltpu.VMEM((2, ROWS_LOCAL, D_MODEL), BF16), pltpu.VMEM((2, ROWS_LOCAL, LANES), F32),
                            pltpu.SemaphoreType.DMA((2, 2))]),
        out_shape=[jax.ShapeDtypeStruct((ROWS_SORTED, D_MODEL), BF16),
                   jax.ShapeDtypeStruct((ROWS_SORTED, LANES), F32)],
        input_output_aliases={5: 0, 6: 1},
        compiler_params=pltpu.CompilerParams(dimension_semantics=("arbitrary",)),
        name="moe_dispatch",
    )(tbl, x1b, wf, ltri, ustr, xs0, ws0)


def _experts_kernel(gid_ref, valid_ref, xs_ref, ws_ref, wg_ref, wu_ref, wd_ref, y_ref):
    i = pl.program_id(0)

    @pl.when(valid_ref[i] == 1)
    def _():
        x = xs_ref[...]
        gates = [_dot(x, wg_ref[e]) for e in range(MOE_EPG)]
        ups = [_dot(x, wu_ref[e]) for e in range(MOE_EPG)]
        ws = ws_ref[...]
        lane = lax.broadcasted_iota(jnp.int32, ws.shape, 1)
        hidden = []
        for e in range(MOE_EPG):
            w_tok = jnp.sum(jnp.where(lane == gid_ref[i] * MOE_EPG + e, ws, 0.0), axis=-1, keepdims=True)
            hidden.append((gates[e] * jax.nn.sigmoid(gates[e]) * ups[e] * w_tok).astype(BF16))
        wd = wd_ref[...].reshape(MOE_EPG * MOE_HIDDEN, D_MODEL)
        y_ref[...] = _dot(jnp.concatenate(hidden, axis=1), wd).astype(BF16)

    @pl.when(valid_ref[i] == 0)
    def _():
        y_ref[...] = jnp.zeros_like(y_ref)


def _experts(gid, valid, xs, ws, wg, wu, wd):
    tm = TM_MOE
    row = lambda n: pl.BlockSpec((tm, n), lambda i, gid, valid: (i, 0))
    wspec = lambda a, b: pl.BlockSpec((MOE_EPG, a, b), lambda i, gid, valid: (gid[i], 0, 0))
    return pl.pallas_call(
        _experts_kernel,
        grid_spec=pltpu.PrefetchScalarGridSpec(
            num_scalar_prefetch=2, grid=(ROWS_SORTED // tm,),
            in_specs=[row(D_MODEL), row(LANES), wspec(D_MODEL, MOE_HIDDEN), wspec(D_MODEL, MOE_HIDDEN),
                      wspec(MOE_HIDDEN, D_MODEL)],
            out_specs=row(D_MODEL)),
        out_shape=jax.ShapeDtypeStruct((ROWS_SORTED, D_MODEL), BF16),
        compiler_params=pltpu.CompilerParams(dimension_semantics=("arbitrary",),
                                             vmem_limit_bytes=EXPERTS_VMEM_BYTES),
        name="moe_experts",
    )(gid, valid, xs, ws, wg, wu, wd)


def _combine_kernel(tbl_ref, wf_ref, x1_ref, ltri_ref, ustr_ref, g_ref, b_ref, y_hbm, o_ref, ybuf, sems):
    tile = pl.program_id(0)
    slot = tile % 2
    tm = x1_ref.shape[0]

    def copies(t, s, act):
        _granule_copies(tbl_ref, t, (ybuf.at[s],), (y_hbm,), (sems.at[s],), False, act)

    def fetch(t, s):
        ybuf[s] = jnp.zeros(ybuf.shape[1:], ybuf.dtype)
        copies(t, s, lambda c: c.start())

    @pl.when(tile == 0)
    def _():
        fetch(tile, slot)

    @pl.when(tile + 1 < pl.num_programs(0))
    def _():
        fetch(tile + 1, 1 - slot)

    dest = _group_dest(wf_ref[...], ltri_ref, ustr_ref)
    c = lax.broadcasted_iota(jnp.int32, (tm, ROWS_LOCAL), 1).astype(F32)
    unperm = jnp.where(c == dest, 1.0, 0.0).astype(BF16)
    copies(tile, slot, lambda c: c.wait())
    ffn = _dot(unperm, ybuf[slot])
    o_ref[...] = _layer_norm(DN_ALPHA * x1_ref[...] + ffn, g_ref[...], b_ref[...])


def _combine(tbl, wf, x1, ltri, ustr, g, b, y):
    tm = TM_PROJ
    row = lambda n: pl.BlockSpec((tm, n), lambda i, t: (i, 0))
    full = lambda a: pl.BlockSpec(a.shape, lambda i, t: (0,) * a.ndim)
    return pl.pallas_call(
        _combine_kernel,
        grid_spec=pltpu.PrefetchScalarGridSpec(
            num_scalar_prefetch=1, grid=(TOKENS // tm,),
            in_specs=[row(LANES), row(D_MODEL), full(ltri), full(ustr), full(g), full(b),
                      pl.BlockSpec(memory_space=pl.ANY)],
            out_specs=row(D_MODEL),
            scratch_shapes=[pltpu.VMEM((2, ROWS_LOCAL, D_MODEL), BF16), pltpu.SemaphoreType.DMA((2,))]),
        out_shape=jax.ShapeDtypeStruct((TOKENS, D_MODEL), F32),
        compiler_params=pltpu.CompilerParams(dimension_semantics=("arbitrary",)),
        name="moe_combine",
    )(tbl, wf, x1, ltri, ustr, g, b, y)


def _routing_tables(cnt):
    n_tiles = cnt.shape[0]
    c = cnt[:, 0, :MOE_GROUPS].astype(jnp.int32)
    gran = (c + GRAN - 1) // GRAN
    local = jnp.cumsum(gran, axis=1) - gran
    per_tile = TM_MOE // GRAN
    tiles_g = (jnp.sum(gran, axis=0) + per_tile - 1) // per_tile
    ends = jnp.cumsum(tiles_g)
    base = (ends - tiles_g) * per_tile
    glob = base[None, :] + jnp.cumsum(gran, axis=0) - gran
    tbl = jnp.concatenate([gran, local, glob], axis=1).reshape(n_tiles * TBL_W)
    idx = jnp.arange(ROWS_SORTED // TM_MOE)
    gid = jnp.minimum(jnp.sum(idx[:, None] >= ends[None, :], axis=1), MOE_GROUPS - 1).astype(jnp.int32)
    valid = (idx < ends[-1]).astype(jnp.int32)
    return tbl, gid, valid


def _sort_tables():
    t = jnp.arange(TM_PROJ)
    ltri = (t[None, :] < t[:, None]).astype(BF16)
    l = jnp.arange(LANES)
    ustr = (l[:, None] < l[None, :]).astype(BF16)
    return ltri, ustr


def _dup_halves(t):
    lane = lax.broadcasted_iota(jnp.int32, t.shape, 1)
    r = pltpu.roll(t, HALF, 1)
    return jnp.where(lane < HALF, t, r), jnp.where(lane < HALF, r, t)


def _nsa_in_kernel(x_ref, w_ref, gb_ref, q_ref, kc_ref, vc_ref, ks_ref, vst_ref, kw_ref, vwt_ref, gate_ref):
    h = _dot(x_ref[...].astype(BF16), w_ref[...])
    q_ref[...] = (h[:, 0:C_MIX] * (NSA_SCALE * LOG2E)).astype(BF16)
    kc_ref[...] = h[:, 1024:1152].astype(BF16)
    vc_ref[...] = h[:, 1152:1280].astype(BF16)
    tm = h.shape[0]
    lane = lax.broadcasted_iota(jnp.int32, (tm, LANES), 1)
    for g, d in enumerate(_dup_halves(h[:, 1536:1664])):
        kw_ref[:, g * LANES:(g + 1) * LANES] = d.astype(BF16)
    pos = (pl.program_id(0) * tm) % SEQ + lax.broadcasted_iota(jnp.int32, (tm, LANES), 0)
    block_onehot = jnp.where(lane == pos // NSA_SEL_LEN, 1.0, 0.0).astype(BF16)
    for g, d in enumerate(_dup_halves(h[:, 1280:1408])):
        ks_ref[:, 2 * g * LANES:(2 * g + 1) * LANES] = d.astype(BF16)
        ks_ref[:, (2 * g + 1) * LANES:(2 * g + 2) * LANES] = block_onehot
    for idx, ref in ((1, vst_ref), (3, vwt_ref)):
        tail = jnp.where(lane == HALF, 1.0, 0.0)
        _store_transposed(ref, [jnp.where(lane < HALF, d, tail)
                                for d in _dup_halves(h[:, 1280 + idx * LANES:1280 + (idx + 1) * LANES])])
    for g in range(NSA_GROUPS):
        gate_ref[g] = jax.nn.sigmoid(h[:, 1792 + g * LANES:1792 + (g + 1) * LANES] + gb_ref[g])


def _nsa_in(x2, w, gb):
    tm = TM_PROJ
    row = lambda n: pl.BlockSpec((tm, n), lambda i: (i, 0))
    full = lambda a: pl.BlockSpec(a.shape, lambda i: (0,) * a.ndim)
    sd = jax.ShapeDtypeStruct
    vt_spec = pl.BlockSpec((NSA_GROUPS, tm // TK_ATT, LANES, TK_ATT), lambda i: (0, i, 0, 0))
    vt_shape = sd((NSA_GROUPS, TOKENS // TK_ATT, LANES, TK_ATT), BF16)
    return pl.pallas_call(
        _nsa_in_kernel,
        grid=(TOKENS // tm,),
        in_specs=[row(D_MODEL), full(w), full(gb)],
        out_specs=[row(C_MIX), row(LANES), row(LANES), row(4 * LANES), vt_spec, row(2 * LANES), vt_spec,
                   pl.BlockSpec((NSA_GROUPS, tm, LANES), lambda i: (0, i, 0))],
        out_shape=[sd((TOKENS, C_MIX), BF16), sd((TOKENS, LANES), BF16), sd((TOKENS, LANES), BF16),
                   sd((TOKENS, 4 * LANES), BF16), vt_shape, sd((TOKENS, 2 * LANES), BF16), vt_shape,
                   sd((NSA_GROUPS, TOKENS, LANES), F32)],
        compiler_params=pltpu.CompilerParams(dimension_semantics=("arbitrary",)),
        name="nsa_in",
    )(x2, w, gb)


def _compress_kernel(kc_ref, vc_ref, pk_ref, pv_ref, wk1_ref, wv1_ref, wk2_ref, wv2_ref, ko_ref, vo_ref):
    for a_ref, p_ref, w1_ref, w2_ref, o_ref in ((kc_ref, pk_ref, wk1_ref, wk2_ref, ko_ref),
                                                (vc_ref, pv_ref, wv1_ref, wv2_ref, vo_ref)):
        a = a_ref[0].astype(F32)
        a0 = (a + p_ref[0]).astype(BF16)
        a1 = (a + p_ref[1]).astype(BF16)
        outs = []
        for g in range(NSA_GROUPS):
            first = _dot(a0, w1_ref[g])
            second = _dot(a1, w1_ref[NSA_GROUPS + g])
            hid = first + pltpu.roll(second, NSA_NCMP - 1, 0)
            outs.append(_dot(_gelu(hid).astype(BF16), w2_ref[...]))
        if o_ref is ko_ref:
            o_ref[0] = jnp.concatenate(outs, axis=1).astype(BF16)
        else:
            for g in range(NSA_GROUPS):
                o_ref[0, g] = outs[g].T.astype(BF16)


def _compress(kc_r, vc_r, pk, pv, wk1, wv1, wk2, wv2):
    blk = pl.BlockSpec((1, NSA_NCMP, NSA_CMP_STRIDE * LANES), lambda b: (b, 0, 0))
    full = lambda a: pl.BlockSpec(a.shape, lambda b: (0,) * a.ndim)
    sd = jax.ShapeDtypeStruct
    return pl.pallas_call(
        _compress_kernel,
        grid=(BATCH,),
        in_specs=[blk, blk, full(pk), full(pv), full(wk1), full(wv1), full(wk2), full(wv2)],
        out_specs=[pl.BlockSpec((1, NSA_NCMP, 2 * LANES), lambda b: (b, 0, 0)),
                   pl.BlockSpec((1, NSA_GROUPS, LANES, NSA_NCMP), lambda b: (b, 0, 0, 0))],
        out_shape=[sd((BATCH, NSA_NCMP, 2 * LANES), BF16), sd((BATCH, NSA_GROUPS, LANES, NSA_NCMP), BF16)],
        compiler_params=pltpu.CompilerParams(dimension_semantics=("arbitrary",)),
        name="nsa_compress",
    )(kc_r, vc_r, pk, pv, wk1, wv1, wk2, wv2)


def _nsa_attn_kernel(q_ref, kc_ref, vct_ref, ks_ref, vst_ref, kw_ref, vwt_ref, gate_ref, cover_ref,
                     o_ref, m_ref, acc_ref, sa_ref, sb_ref):
    tq, tk, hpg = TQ_NSA, TK_ATT, NSA_HPG
    qi = pl.program_id(2)
    q0 = qi * tq
    lane = lax.broadcasted_iota(jnp.int32, (tq, LANES), 1)
    t_tok = q0 + lax.broadcasted_iota(jnp.int32, (1, tq), 1)
    head = lambda x, i: x[:, i * tq:(i + 1) * tq]
    heads = range(hpg)

    parts = []
    for p in range(hpg // 2):
        qp = q_ref[0, :, p * LANES:(p + 1) * LANES]
        zero = jnp.zeros_like(qp)
        parts.append(jnp.where(lane < HALF, qp, zero))
        parts.append(jnp.where(lane < HALF, zero, qp))
    qs = jnp.concatenate(parts, axis=0)

    s_c = _dot_nt(kc_ref[0], qs)
    n_sub = lax.broadcasted_iota(jnp.int32, (NSA_NCMP, 1), 0)
    vis = t_tok >= n_sub * NSA_CMP_STRIDE + (NSA_CMP_LEN - 1)
    p_sum = jnp.zeros((NSA_NCMP, tq), F32)
    p_c = []
    for i in heads:
        sm = jnp.where(vis, head(s_c, i), NEG)
        e = jnp.exp2(sm - jnp.max(sm, axis=0, keepdims=True))
        p = jnp.where(vis, e / jnp.sum(e, axis=0, keepdims=True), 0.0)
        p_sum = p_sum + p
        p_c.append(p.astype(BF16))
    o_c = _dot(vct_ref[0, 0], jnp.concatenate(p_c, axis=1))

    imp = jnp.dot(cover_ref[...], p_sum, preferred_element_type=F32,
                  precision=lax.Precision.HIGHEST)[0:NSA_NSEL]
    jj = lax.broadcasted_iota(jnp.int32, (NSA_NSEL, 1), 0)
    tb = t_tok // NSA_SEL_LEN
    forced = (jj == 0) | (jj == tb) | (jj == tb - 1)
    score = jnp.where(forced, NSA_FORCE, jnp.where(jj <= tb, imp, -NSA_FORCE))
    rank = jnp.zeros((NSA_NSEL, tq), jnp.int32)
    for i in range(NSA_NSEL):
        si = score[i:i + 1, :]
        beats = (si > score) | ((si == score) & (jj > i))
        rank = rank + beats.astype(jnp.int32)
    sel_bias = jnp.where(rank < NSA_TOPK, 0.0, NEG)
    bias_rows = jnp.concatenate([sel_bias, jnp.zeros((LANES - NSA_NSEL, tq), F32)], axis=0).T.astype(BF16)
    qs_sel = jnp.concatenate([qs, jnp.concatenate([bias_rows] * hpg, axis=0)], axis=1)

    m_ref[...] = jnp.full(m_ref.shape, NEG, F32)
    acc_ref[...] = jnp.zeros(acc_ref.shape, F32)
    k_sub = lax.broadcasted_iota(jnp.int32, (tk, 1), 0)
    diag = q0 // tk
    causal = diag * tk + k_sub <= t_tok
    per_dot = max(1, 2 * LANES // tq)

    def scores_into(ref, kt):
        ref[...] = _dot_nt(ks_ref[0, pl.ds(pl.multiple_of(kt * tk, tk), tk), :], qs_sel)

    def process(ref, kt, diagonal):
        s = ref[...]
        vt = vst_ref[0, kt]
        for i0 in range(0, hpg, per_dot):
            cols = slice(i0 * tq, (i0 + per_dot) * tq)
            probs, alphas = [], []
            for i in range(i0, i0 + per_dot):
                for part in range(tq // LANES):
                    sub = slice(part * LANES, (part + 1) * LANES)
                    sm = s[:, i * tq + part * LANES:i * tq + (part + 1) * LANES]
                    if diagonal:
                        sm = jnp.where(causal[:, sub], sm, NEG)
                    m_old = m_ref[i, :, sub]
                    m_new = jnp.maximum(m_old, jnp.max(sm, axis=0, keepdims=True))
                    m_ref[i, :, sub] = m_new
                    probs.append(jnp.exp2((sm - m_new).astype(BF16)))
                    alphas.append(jnp.exp2(m_old - m_new))
            pv = _dot(vt, jnp.concatenate(probs, axis=1))
            acc_ref[:, cols] = jnp.concatenate(alphas, axis=1) * acc_ref[:, cols] + pv

    odd = diag % 2
    scores_into(sa_ref, 0)

    @pl.when(odd == 1)
    def _():
        process(sa_ref, 0, False)
        scores_into(sa_ref, 1)

    def pair(k, c):
        t0 = odd + 2 * k
        scores_into(sb_ref, t0 + 1)
        process(sa_ref, t0, False)
        scores_into(sa_ref, t0 + 2)
        process(sb_ref, t0 + 1, False)
        return c

    lax.fori_loop(0, diag // 2, pair, 0)
    process(sa_ref, diag, True)

    c0 = jnp.maximum(q0 - (NSA_WINDOW - 1), 0) // tk
    s_w, in_win = [], []
    for c in range(WIN_CHUNKS):
        r0 = pl.multiple_of((c0 + c) * tk, tk)
        s_w.append(_dot_nt(kw_ref[0, pl.ds(r0, tk), :], qs))
        kpos = r0 + k_sub
        in_win.append((kpos <= t_tok) & (kpos > t_tok - NSA_WINDOW))
    vwt = [vwt_ref[0, c0 + c] for c in range(WIN_CHUNKS)]
    o_w = []
    for i0 in range(0, hpg, per_dot):
        e_w = [[] for _ in range(WIN_CHUNKS)]
        for i in range(i0, i0 + per_dot):
            sm = [jnp.where(in_win[c], head(s_w[c], i), NEG) for c in range(WIN_CHUNKS)]
            m = functools.reduce(jnp.maximum, [jnp.max(x, axis=0, keepdims=True) for x in sm])
            for c in range(WIN_CHUNKS):
                e_w[c].append(jnp.exp2((sm[c] - m).astype(BF16)))
        o_grp = functools.reduce(jnp.add, [_dot(vwt[c], jnp.concatenate(e_w[c], axis=1))
                                           for c in range(WIN_CHUNKS)])
        o_w.extend(o_grp[:, n * tq:(n + 1) * tq] for n in range(per_dot))

    gt = gate_ref[0, 0].T
    o_s = acc_ref[...]
    outs = []
    for i in heads:
        c_i, s_i, w_i = head(o_c, i), head(o_s, i), o_w[i]
        outs.append(gt[i:i + 1] * c_i[:HALF]
                    + gt[hpg + i:hpg + i + 1] / s_i[HALF:HALF + 1] * s_i[:HALF]
                    + gt[2 * hpg + i:2 * hpg + i + 1] / w_i[HALF:HALF + 1] * w_i[:HALF])
    for p in range(hpg // 2):
        pair = jnp.concatenate([outs[2 * p], outs[2 * p + 1]], axis=0)
        o_ref[0, :, p * LANES:(p + 1) * LANES] = pair.T.astype(BF16)


def _nsa_attn(q3, kc2, vct, ks3, vst, kw3, vwt, gates4, cover_t):
    tq = TQ_NSA
    half_w = NSA_HPG * NSA_DH
    n_chunks = SEQ // TK_ATT
    kv = pl.BlockSpec((1, SEQ, LANES), lambda b, g, i: (b, 0, g))
    vt = pl.BlockSpec((1, n_chunks, LANES, TK_ATT), lambda b, g, i: (g, b, 0, 0))
    full = lambda a: pl.BlockSpec(a.shape, lambda b, g, i: (0,) * a.ndim)
    return pl.pallas_call(
        _nsa_attn_kernel,
        grid=(BATCH, NSA_GROUPS, SEQ // tq),
        in_specs=[pl.BlockSpec((1, tq, half_w), lambda b, g, i: (b, i, g)),
                  pl.BlockSpec((1, NSA_NCMP, LANES), lambda b, g, i: (b, 0, g)),
                  pl.BlockSpec((1, 1, LANES, NSA_NCMP), lambda b, g, i: (b, g, 0, 0)),
                  pl.BlockSpec((1, SEQ, 2 * LANES), lambda b, g, i: (b, 0, g)), vt, kv, vt,
                  pl.BlockSpec((1, 1, tq, LANES), lambda b, g, i: (g, b, i, 0)),
                  full(cover_t)],
        out_specs=pl.BlockSpec((1, tq, half_w), lambda b, g, i: (b, i, g)),
        out_shape=jax.ShapeDtypeStruct((BATCH, SEQ, C_MIX), BF16),
        scratch_shapes=[pltpu.VMEM((NSA_HPG, 1, tq), F32),
                        pltpu.VMEM((LANES, NSA_HPG * tq), F32),
                        pltpu.VMEM((TK_ATT, NSA_HPG * tq), F32), pltpu.VMEM((TK_ATT, NSA_HPG * tq), F32)],
        compiler_params=pltpu.CompilerParams(dimension_semantics=("arbitrary",) * 3),
        name="nsa_attn",
    )(q3, kc2, vct, ks3, vst, kw3, vwt, gates4, cover_t)


def _rope_tables():
    half = MLA_ROPE // 2
    freq = jnp.exp(-math.log(ROPE_BASE) * jnp.arange(half, dtype=F32) / half)
    return freq[:, None]


def _swap_halves(w):
    half = w.shape[-1] // 2
    return jnp.concatenate([w[..., half:], w[..., :half]], axis=-1)


def _pad_last(w, n):
    return jnp.pad(w, [(0, 0)] * (w.ndim - 1) + [(0, n - w.shape[-1])])


def _ab_weights(w_in, w_uq, w_uk, w_uv):
    w_kr = w_in[:, 1408:1440]
    place = lambda w: jnp.pad(w, ((0, 0), (MLA_NOPE, LANES - MLA_NOPE - MLA_ROPE)))
    win = jnp.concatenate([w_in[:, :1408], place(w_kr), place(_swap_halves(w_kr))], axis=1).astype(BF16)
    uq = w_uq.reshape(MLA_Q_RANK, MLA_HEADS, MLA_NOPE + MLA_ROPE)
    nope, rp = uq[..., :MLA_NOPE], uq[..., MLA_NOPE:]
    q_pad = _pad_last(jnp.concatenate([nope, rp], -1), LANES).reshape(MLA_Q_RANK, MLA_HEADS * LANES)
    q_sw = _pad_last(jnp.concatenate([jnp.zeros_like(nope), _swap_halves(rp)], -1), LANES)
    wq = jnp.concatenate([q_pad, q_sw.reshape(MLA_Q_RANK, MLA_HEADS * LANES)], axis=1).astype(BF16)
    k_pad = _pad_last(w_uk.reshape(MLA_KV_RANK, MLA_HEADS, MLA_NOPE), LANES).reshape(MLA_KV_RANK, -1)
    wkv = jnp.concatenate([k_pad, w_uv], axis=1).astype(BF16)
    return win, wq, wkv


def _router_weights(w_rg, b_rg, w_re, b_re):
    wr = _pad_last(jnp.concatenate([w_re, w_rg], axis=1), LANES)
    wr_hi = wr.astype(BF16)
    wr_lo = (wr - wr_hi.astype(F32)).astype(BF16)
    br = _pad_last(jnp.concatenate([b_re, b_rg])[None, :], LANES)
    return jnp.concatenate([wr_hi, wr_lo], axis=1), br


def _nsa_in_weights(w_in, gate_b):
    g_cols = w_in[:, C_MIX + 768:].reshape(D_MODEL, 3, NSA_GROUPS, NSA_HPG)
    g_blocks = [_pad_last(g_cols[:, :, g, :].reshape(D_MODEL, 3 * NSA_HPG), LANES) for g in range(NSA_GROUPS)]
    w = jnp.concatenate([w_in[:, :C_MIX + 768]] + g_blocks, axis=1).astype(BF16)
    gb = gate_b.reshape(3, NSA_GROUPS, NSA_HPG)
    gb = jnp.stack([_pad_last(gb[:, g, :].reshape(1, 3 * NSA_HPG), LANES) for g in range(NSA_GROUPS)])
    return w, gb


def _compress_weights(pos, w1, w2):
    w1r = w1.reshape(2, NSA_CMP_STRIDE, NSA_DH, NSA_CMP_HIDDEN)
    zero = jnp.zeros_like(w1r)
    per_g = []
    for g in range(NSA_GROUPS):
        parts = [w1r if gg == g else zero for gg in range(NSA_GROUPS)]
        per_g.append(jnp.stack(parts, axis=2).reshape(2, NSA_CMP_STRIDE * LANES, NSA_CMP_HIDDEN))
    w1x = jnp.stack(per_g, axis=1).reshape(2 * NSA_GROUPS, NSA_CMP_STRIDE * LANES, NSA_CMP_HIDDEN)
    posr = pos.reshape(2, NSA_CMP_STRIDE, 1, NSA_DH)
    posx = jnp.broadcast_to(posr, (2, NSA_CMP_STRIDE, NSA_GROUPS, NSA_DH)).reshape(2, 1, NSA_CMP_STRIDE * LANES)
    w2x = jnp.concatenate([w2, w2], axis=1)
    return posx, w1x.astype(BF16), w2x.astype(BF16)


def _selection_tables():
    n = jnp.arange(LANES)[:, None]
    j = jnp.arange(LANES)[None, :]
    c0 = n * NSA_CMP_STRIDE
    s0 = j * NSA_SEL_LEN
    cover = ((c0 < s0 + NSA_SEL_LEN) & (c0 + NSA_CMP_LEN > s0) & (n < NSA_NCMP - 1) & (j < NSA_NSEL))
    return jnp.transpose(cover).astype(F32)


def kernel(x, positions, ab_w_in, ab_gm_ln_g, ab_gm_ln_b, ab_gm_ws, ab_gm_bs, ab_mla_q_norm,
           ab_mla_kv_norm, ab_mla_w_uq, ab_mla_w_uk, ab_mla_w_uv, ab_w_o, c_w_in, c_cmp_pos, c_w_ck1,
           c_w_ck2, c_w_cv1, c_w_cv2, c_gate_b, c_w_o, moe_w_rg, moe_b_rg, moe_w_re, moe_b_re,
           moe_w_gate, moe_w_up, moe_w_down, ln1_g, ln1_b, ln2_g, ln2_b):
    x2 = x.reshape(TOKENS, D_MODEL)
    pos3 = positions.reshape(TOKENS // TM_PROJ, 1, TM_PROJ)
    vec = lambda a: a[None, :]

    ltri, ustr = _sort_tables()

    def moe_layer(layer, x1b, x1, wf, cnt):
        tbl, gid, valid = _routing_tables(cnt)
        xs, ws = _dispatch(tbl, x1b, wf, ltri, ustr)
        y = _experts(gid, valid, xs, ws, moe_w_gate[layer].astype(BF16), moe_w_up[layer].astype(BF16),
                     moe_w_down[layer].astype(BF16))
        return _combine(tbl, wf, x1, ltri, ustr, vec(ln2_g[layer]), vec(ln2_b[layer]), y)

    win, wq, wkv = _ab_weights(ab_w_in[0], ab_mla_w_uq[0], ab_mla_w_uk[0], ab_mla_w_uv[0])
    gu, vn, q, k, vt = _ab_in(x2, pos3, win, vec(ab_gm_ln_g[0]), vec(ab_gm_ln_b[0]), vec(ab_mla_q_norm[0]),
                              vec(ab_mla_kv_norm[0]), wq, wkv, _rope_tables())
    yb = _mla_attn(q.reshape(BATCH, SEQ, -1), k.reshape(BATCH, SEQ, -1), vt)
    wr, br = _router_weights(moe_w_rg[0], moe_b_rg[0], moe_w_re[0], moe_b_re[0])
    x1, x1b, wf, cnt = _mix_out(x2, yb.reshape(TOKENS, -1), ab_w_o[0].astype(BF16), vec(ln1_g[0]),
                                vec(ln1_b[0]), wr, br, gm=(gu, vn, ab_gm_ws[0], jnp.transpose(ab_gm_bs[0])))
    x2 = moe_layer(0, x1b, x1, wf, cnt)

    w_nsa, gb = _nsa_in_weights(c_w_in[0], c_gate_b[0])
    q, kc, vc, ks, vst, kw, vwt, gates = _nsa_in(x2, w_nsa, gb)
    pk, wk1, wk2 = _compress_weights(c_cmp_pos[0, 0], c_w_ck1[0], c_w_ck2[0])
    pv, wv1, wv2 = _compress_weights(c_cmp_pos[0, 1], c_w_cv1[0], c_w_cv2[0])
    blocks = lambda a: a.reshape(BATCH, NSA_NCMP, NSA_CMP_STRIDE * LANES)
    kc2, vct = _compress(blocks(kc), blocks(vc), pk, pv, wk1, wv1, wk2, wv2)
    b3 = lambda a: a.reshape(BATCH, SEQ, -1)
    o = _nsa_attn(b3(q), kc2, vct, b3(ks), vst, b3(kw), vwt,
                  gates.reshape(NSA_GROUPS, BATCH, SEQ, LANES), _selection_tables())
    wr, br = _router_weights(moe_w_rg[1], moe_b_rg[1], moe_w_re[1], moe_b_re[1])
    x1, x1b, wf, cnt = _mix_out(x2, o.reshape(TOKENS, -1), c_w_o[0].astype(BF16), vec(ln1_g[1]), vec(ln1_b[1]),
                                wr, br)
    x2 = moe_layer(1, x1b, x1, wf, cnt)
    return x2.reshape(BATCH, SEQ, D_MODEL)
```

```python
import functools
import math

import jax
import jax.numpy as jnp
from jax import lax
from jax.experimental import pallas as pl
from jax.experimental.pallas import tpu as pltpu

F32 = jnp.float32
BF16 = jnp.bfloat16

D_MODEL = 1024
BATCH = 16
SEQ = 2048
TOKENS = BATCH * SEQ
DEPTH = 2
DN_ALPHA = (2.0 * DEPTH) ** 0.25
LN_EPS = 1e-5
NEG = -1e30
LOG2E = math.log2(math.e)
LANES = 128
HALF = LANES // 2

GM_WIDTH = 512
GM_GROUPS = 4
GM_CHUNK = 128

MLA_HEADS = 8
MLA_NOPE = 64
MLA_ROPE = 32
MLA_V = 64
MLA_Q_RANK = 256
MLA_KV_RANK = 128
ROPE_BASE = 10000.0
MLA_SCALE = (MLA_NOPE + MLA_ROPE) ** -0.5

NSA_HEADS = 16
NSA_GROUPS = 2
NSA_HPG = 8
NSA_DH = 64
NSA_CMP_LEN = 32
NSA_CMP_STRIDE = 16
NSA_CMP_HIDDEN = 256
NSA_SEL_LEN = 64
NSA_TOPK = 8
NSA_WINDOW = 512
NSA_NSEL = SEQ // NSA_SEL_LEN
NSA_NCMP = SEQ // NSA_CMP_STRIDE
NSA_FORCE = 1e4
NSA_SCALE = NSA_DH ** -0.5
C_MIX = NSA_HEADS * NSA_DH

MOE_GROUPS = 4
MOE_EPG = 8
MOE_EXPERTS = 32
MOE_HIDDEN = 256

TM_PROJ = 512
TQ_MLA = 256
TQ_NSA = 256
TK_ATT = 256
WIN_CHUNKS = -(-(NSA_WINDOW + TQ_NSA) // TK_ATT)
MLA_VT_ROWS = LANES + 16
TM_MOE = 512
EXPERTS_VMEM_BYTES = (2 * 3 * MOE_EPG * D_MODEL * MOE_HIDDEN * 2 + 2 * 2 * TM_MOE * D_MODEL * 2
                      + 2 * MOE_EPG * TM_MOE * MOE_HIDDEN * 4 + TM_MOE * MOE_EPG * MOE_HIDDEN * 2
                      + 2 * TM_MOE * D_MODEL * 4)

GSEL_LANE = MOE_EXPERTS
GRAN = 16
TBL_W = 3 * MOE_GROUPS
ROWS_LOCAL = 640
ROWS_SORTED = TM_MOE * (TOKENS // TM_MOE + MOE_GROUPS
                        + -(-(TOKENS // TM_PROJ) * MOE_GROUPS * (GRAN - 1) // TM_MOE))


def _dot(a, b):
    return jnp.dot(a, b, preferred_element_type=F32)


def _dot_nt(a, b):
    return lax.dot_general(a, b, (((1,), (1,)), ((), ())), preferred_element_type=F32)


def _gelu(x):
    return 0.5 * x * (1.0 + jnp.tanh(math.sqrt(2.0 / math.pi) * (x + 0.044715 * (x * x * x))))


def _layer_norm(x, g, b):
    mu = jnp.mean(x, axis=-1, keepdims=True)
    xc = x - mu
    var = jnp.mean(xc * xc, axis=-1, keepdims=True)
    return xc * lax.rsqrt(var + LN_EPS) * g + b


def _rms_norm(x, g):
    return x * lax.rsqrt(jnp.mean(x * x, axis=-1, keepdims=True) + LN_EPS) * g


def _store_transposed(ref, blocks):
    extra = ref.shape[2] - LANES
    if extra:
        ones_rows = jnp.where(lax.broadcasted_iota(jnp.int32, (extra, TK_ATT), 0) == 0, 1.0, 0.0)
    for n, blk in enumerate(blocks):
        t = blk.T
        for c in range(t.shape[1] // TK_ATT):
            chunk = t[:, c * TK_ATT:(c + 1) * TK_ATT]
            if extra:
                chunk = jnp.concatenate([chunk, ones_rows], axis=0)
            ref[n, c] = chunk.astype(ref.dtype)


def _ab_in_kernel(x_ref, pos_ref, win_ref, lng_ref, lnb_ref, qg_ref, kvg_ref, wq_ref, wkv_ref,
                  fc_ref, gu_ref, vn_ref, q_ref, k_ref, vt_ref):
    h = _dot(x_ref[...].astype(BF16), win_ref[...])
    gu_ref[...] = _gelu(h[:, 0:512]).astype(BF16)
    vn_ref[...] = _layer_norm(_gelu(h[:, 512:1024]), lng_ref[...], lnb_ref[...]).astype(BF16)

    tm = x_ref.shape[0]
    ang = fc_ref[...] * pos_ref[0].astype(F32)
    cos_t, sin_t = jnp.cos(ang), jnp.sin(ang)
    ones_t, zeros_t = jnp.ones((MLA_NOPE, tm), F32), jnp.zeros((MLA_NOPE, tm), F32)
    pad = LANES - MLA_NOPE - MLA_ROPE
    cc = jnp.concatenate([ones_t, cos_t, cos_t, ones_t[:pad]], axis=0).T
    ss = jnp.concatenate([zeros_t, -sin_t, sin_t, zeros_t[:pad]], axis=0).T

    cqn = _rms_norm(h[:, 1024:1280], qg_ref[...]).astype(BF16)
    qq = _dot(cqn, wq_ref[...])
    for hd in range(MLA_HEADS):
        lo, hi = hd * LANES, (hd + 1) * LANES
        q_ref[:, lo:hi] = ((qq[:, lo:hi] * cc + qq[:, 1024 + lo:1024 + hi] * ss) * (MLA_SCALE * LOG2E)).astype(BF16)

    ckvn = _rms_norm(h[:, 1280:1408], kvg_ref[...]).astype(BF16)
    kv = _dot(ckvn, wkv_ref[...])
    k_rope = h[:, 1408:1536] * cc + h[:, 1536:1664] * ss
    for hd in range(MLA_HEADS):
        lo, hi = hd * LANES, (hd + 1) * LANES
        k_ref[:, lo:hi] = (kv[:, lo:hi] + k_rope).astype(BF16)
    _store_transposed(vt_ref, [kv[:, 1024 + p * LANES:1024 + (p + 1) * LANES] for p in range(MLA_HEADS // 2)])


def _ab_in(x2, pos3, win, lng, lnb, qg, kvg, wq, wkv, fc):
    tm = TM_PROJ
    row = lambda n: pl.BlockSpec((tm, n), lambda i: (i, 0))
    full = lambda a: pl.BlockSpec(a.shape, lambda i: (0,) * a.ndim)
    return pl.pallas_call(
        _ab_in_kernel,
        grid=(TOKENS // tm,),
        in_specs=[row(D_MODEL), pl.BlockSpec((1, 1, tm), lambda i: (i, 0, 0)), full(win), full(lng), full(lnb),
                  full(qg), full(kvg), full(wq), full(wkv), full(fc)],
        out_specs=[row(512), row(512), row(1024), row(1024),
                   pl.BlockSpec((MLA_HEADS // 2, tm // TK_ATT, MLA_VT_ROWS, TK_ATT), lambda i: (0, i, 0, 0))],
        out_shape=[jax.ShapeDtypeStruct((TOKENS, 512), BF16), jax.ShapeDtypeStruct((TOKENS, 512), BF16),
                   jax.ShapeDtypeStruct((TOKENS, 1024), BF16), jax.ShapeDtypeStruct((TOKENS, 1024), BF16),
                   jax.ShapeDtypeStruct((MLA_HEADS // 2, TOKENS // TK_ATT, MLA_VT_ROWS, TK_ATT), BF16)],
        compiler_params=pltpu.CompilerParams(dimension_semantics=("arbitrary",)),
        name="ab_in",
    )(x2, pos3, win, lng, lnb, qg, kvg, wq, wkv, fc)


def _mla_attn_kernel(q_ref, k_ref, vt_ref, o_ref, m_ref, l_ref, acc_ref):
    tq, tk = TQ_MLA, TK_ATT
    qi = pl.program_id(1)
    krow = lax.broadcasted_iota(jnp.int32, (tk, tq), 0)
    qcol = lax.broadcasted_iota(jnp.int32, (tk, tq), 1)
    top = lax.broadcasted_iota(jnp.int32, (LANES, tq), 0) < HALF
    m_ref[...] = jnp.full(m_ref.shape, NEG, F32)
    l_ref[...] = jnp.zeros(l_ref.shape, F32)
    acc_ref[...] = jnp.zeros(acc_ref.shape, F32)

    def tile(j, masked):
        r0 = pl.multiple_of(j * tk, tk)
        scores = [_dot_nt(k_ref[0, pl.ds(r0, tk), h * LANES:(h + 1) * LANES], q_ref[0, :, h * LANES:(h + 1) * LANES])
                  for h in range(MLA_HEADS)]
        for pr in range(MLA_HEADS // 2):
            probs, alphas = [], []
            for h in (2 * pr, 2 * pr + 1):
                s = jnp.where(krow <= qcol, scores[h], NEG) if masked else scores[h]
                m_old = m_ref[h]
                m_new = jnp.maximum(m_old, jnp.max(s, axis=0, keepdims=True))
                alphas.append(jnp.exp2(m_old - m_new))
                probs.append(jnp.exp2(s - m_new).astype(BF16))
                m_ref[h] = m_new
            pv = _dot(vt_ref[pr, j], jnp.concatenate(probs, axis=1))
            for n, hh in enumerate((2 * pr, 2 * pr + 1)):
                l_ref[hh] = alphas[n] * l_ref[hh] + pv[LANES:LANES + 1, n * tq:(n + 1) * tq]
            a = jnp.where(top, alphas[0], alphas[1])
            acc_ref[pr] = a * acc_ref[pr] + jnp.where(top, pv[:LANES, :tq], pv[:LANES, tq:])

    def body(j, c):
        tile(j, False)
        return c

    lax.fori_loop(0, qi, body, 0)
    tile(qi, True)
    for pr in range(MLA_HEADS // 2):
        l = jnp.where(top, l_ref[2 * pr], l_ref[2 * pr + 1])
        o_ref[0, :, pr * LANES:(pr + 1) * LANES] = (acc_ref[pr] / l).T.astype(BF16)


def _mla_attn(q3, k3, vt):
    tq = TQ_MLA
    n_chunks = SEQ // TK_ATT
    return pl.pallas_call(
        _mla_attn_kernel,
        grid=(BATCH, SEQ // tq),
        in_specs=[pl.BlockSpec((1, tq, MLA_HEADS * LANES), lambda b, i: (b, i, 0)),
                  pl.BlockSpec((1, SEQ, MLA_HEADS * LANES), lambda b, i: (b, 0, 0)),
                  pl.BlockSpec((MLA_HEADS // 2, n_chunks, MLA_VT_ROWS, TK_ATT), lambda b, i: (0, b, 0, 0))],
        out_specs=pl.BlockSpec((1, tq, MLA_HEADS * MLA_V), lambda b, i: (b, i, 0)),
        out_shape=jax.ShapeDtypeStruct((BATCH, SEQ, MLA_HEADS * MLA_V), BF16),
        scratch_shapes=[pltpu.VMEM((MLA_HEADS, 1, tq), F32), pltpu.VMEM((MLA_HEADS, 1, tq), F32),
                        pltpu.VMEM((MLA_HEADS // 2, LANES, tq), F32)],
        compiler_params=pltpu.CompilerParams(dimension_semantics=("arbitrary",) * 2),
        name="mla_attn",
    )(q3, k3, vt)


def _router(x1, wr, br):
    tm = x1.shape[0]
    x_hi = x1.astype(BF16)
    x_lo = (x1 - x_hi.astype(F32)).astype(BF16)
    parts = _dot(jnp.concatenate([x_hi, x_lo], axis=0), wr)
    logits = (parts[:tm, :LANES] + (parts[:tm, LANES:] + parts[tm:, :LANES]) + parts[tm:, LANES:]) + br
    lane = lax.broadcasted_iota(jnp.int32, (tm, LANES), 1).astype(F32)
    big = 1e6
    is_g = (lane >= MOE_EXPERTS) & (lane < MOE_EXPERTS + MOE_GROUPS)
    gl = jnp.where(is_g, logits, NEG)
    gmax = jnp.max(gl, axis=-1, keepdims=True)
    g_sel = jnp.min(jnp.where(is_g & (gl == gmax), lane, big), axis=-1, keepdims=True) - MOE_EXPERTS
    g_w = 1.0 / jnp.sum(jnp.where(is_g, jnp.exp(gl - gmax), 0.0), axis=-1, keepdims=True)
    in_grp = (lane >= g_sel * MOE_EPG) & (lane < (g_sel + 1) * MOE_EPG)
    el = jnp.where(in_grp, logits, NEG)
    emax = jnp.max(el, axis=-1, keepdims=True)
    ee = jnp.where(in_grp, jnp.exp(el - emax), 0.0)
    pe = ee / jnp.sum(ee, axis=-1, keepdims=True)
    p1 = jnp.max(pe, axis=-1, keepdims=True)
    i1 = jnp.min(jnp.where(in_grp & (pe == p1), lane, big), axis=-1, keepdims=True)
    rest = in_grp & (lane != i1)
    pr = jnp.where(rest, pe, -1.0)
    p2 = jnp.max(pr, axis=-1, keepdims=True)
    i2 = jnp.min(jnp.where(rest & (pr == p2), lane, big), axis=-1, keepdims=True)
    tot = p1 + p2
    wf = jnp.where(lane == i1, p1 / tot * g_w, jnp.where(lane == i2, p2 / tot * g_w, 0.0))
    wf = jnp.where(lane == GSEL_LANE, g_sel, wf)
    cnt = jnp.sum(jnp.where(lane == g_sel, 1.0, 0.0), axis=0, keepdims=True)
    return wf, cnt


def _mix_out_kernel(*refs, gmlp):
    if gmlp:
        (x_ref, gu_ref, vn_ref, ws_ref, bs_ref, yb_ref, wo_ref, g_ref, b_ref, wr_ref, br_ref,
         x1_ref, x1b_ref, wf_ref, cnt_ref, ya_ref) = refs
        tm = x_ref.shape[0]
        r = lax.broadcasted_iota(jnp.int32, (GM_CHUNK, GM_CHUNK), 0)
        c = lax.broadcasted_iota(jnp.int32, (GM_CHUNK, GM_CHUNK), 1)
        for g in range(GM_GROUPS):
            ws = jnp.where(r >= c, ws_ref[g], 0.0).astype(BF16)
            bias = bs_ref[:, g:g + 1]
            for ch in range(tm // GM_CHUNK):
                rows = slice(ch * GM_CHUNK, (ch + 1) * GM_CHUNK)
                cols = slice(g * LANES, (g + 1) * LANES)
                s = _dot(ws, vn_ref[rows, cols]) + bias
                ya_ref[rows, cols] = (gu_ref[rows, cols].astype(F32) * s).astype(BF16)
        mix = _dot(ya_ref[...], wo_ref[0:GM_WIDTH, :]) + _dot(yb_ref[...], wo_ref[GM_WIDTH:, :])
    else:
        x_ref, y_ref, wo_ref, g_ref, b_ref, wr_ref, br_ref, x1_ref, x1b_ref, wf_ref, cnt_ref = refs
        mix = _dot(y_ref[...], wo_ref[...])
    x1 = _layer_norm(DN_ALPHA * x_ref[...] + mix, g_ref[...], b_ref[...])
    x1_ref[...] = x1
    x1b_ref[...] = x1.astype(BF16)
    wf, cnt = _router(x1, wr_ref[...], br_ref[...])
    wf_ref[...] = wf
    cnt_ref[0] = jnp.broadcast_to(cnt, cnt_ref.shape[1:])


def _mix_out(x2, ys, wo, g, b, wr, br, gm=None):
    tm = TM_PROJ
    row = lambda n: pl.BlockSpec((tm, n), lambda i: (i, 0))
    full = lambda a: pl.BlockSpec(a.shape, lambda i: (0,) * a.ndim)
    if gm is not None:
        gu, vn, ws, bs = gm
        args = (x2, gu, vn, ws, bs, ys, wo, g, b, wr, br)
        in_specs = [row(D_MODEL), row(512), row(512), full(ws), full(bs), row(512), full(wo),
                    full(g), full(b), full(wr), full(br)]
        scratch = [pltpu.VMEM((tm, GM_WIDTH), BF16)]
    else:
        args = (x2, ys, wo, g, b, wr, br)
        in_specs = [row(D_MODEL), row(C_MIX), full(wo), full(g), full(b), full(wr), full(br)]
        scratch = []
    return pl.pallas_call(
        functools.partial(_mix_out_kernel, gmlp=gm is not None),
        grid=(TOKENS // tm,),
        in_specs=in_specs,
        out_specs=[row(D_MODEL), row(D_MODEL), row(LANES), pl.BlockSpec((1, 8, LANES), lambda i: (i, 0, 0))],
        out_shape=[jax.ShapeDtypeStruct((TOKENS, D_MODEL), F32), jax.ShapeDtypeStruct((TOKENS, D_MODEL), BF16),
                   jax.ShapeDtypeStruct((TOKENS, LANES), F32),
                   jax.ShapeDtypeStruct((TOKENS // tm, 8, LANES), F32)],
        scratch_shapes=scratch,
        compiler_params=pltpu.CompilerParams(dimension_semantics=("arbitrary",)),
        name="mix_out_gmlp" if gm is not None else "mix_out",
    )(*args)


def _group_dest(wf, ltri_ref, ustr_ref):
    tm = wf.shape[0]
    lane = lax.broadcasted_iota(jnp.int32, (tm, LANES), 1).astype(F32)
    onehot = jnp.where(lane == wf[:, GSEL_LANE:GSEL_LANE + 1], 1.0, 0.0)
    before = _dot(ltri_ref[...], onehot.astype(BF16))
    cnt = jnp.sum(onehot, axis=0, keepdims=True)
    gran = jnp.floor((cnt + (GRAN - 1)) * (1.0 / GRAN))
    start = _dot(jnp.broadcast_to(gran, (8, LANES)).astype(BF16), ustr_ref[...])[0:1]
    return jnp.sum(onehot * (GRAN * start + before), axis=-1, keepdims=True)


def _granule_copies(tbl_ref, tile, vmem_bufs, hbm_refs, sems, to_hbm, act):
    for g in range(MOE_GROUPS):
        n = tbl_ref[tile * TBL_W + g]
        loc = tbl_ref[tile * TBL_W + MOE_GROUPS + g]
        glb = tbl_ref[tile * TBL_W + 2 * MOE_GROUPS + g]

        def body(k, c, loc=loc, glb=glb):
            lo = pl.multiple_of((loc + k) * GRAN, GRAN)
            hi = pl.multiple_of((glb + k) * GRAN, GRAN)
            for idx, (vb, hb) in enumerate(zip(vmem_bufs, hbm_refs)):
                v_sl, h_sl = vb.at[pl.ds(lo, GRAN)], hb.at[pl.ds(hi, GRAN)]
                src, dst = (v_sl, h_sl) if to_hbm else (h_sl, v_sl)
                act(pltpu.make_async_copy(src, dst, sems[idx]))
            return c

        lax.fori_loop(0, n, body, 0)


def _zero_fill_copies(tbl_ref, zero_bufs, hbm_refs, sems, act):
    tail = (TOKENS // TM_PROJ) * TBL_W
    for n in range(MOE_GROUPS + 1):
        first = tbl_ref[tail + 2 * n]

        def body(k, c, first=first):
            hi = pl.multiple_of((first + k) * GRAN, GRAN)
            for idx, (zb, hb) in enumerate(zip(zero_bufs, hbm_refs)):
                act(pltpu.make_async_copy(zb, hb.at[pl.ds(hi, GRAN)], sems[idx]))
            return c

        lax.fori_loop(0, tbl_ref[tail + 2 * n + 1], body, 0)


def _dispatch_kernel(tbl_ref, xb_ref, wf_ref, ltri_ref, ustr_ref, xs_out, ws_out, xbuf, wbuf, zx, zw, sems):
    tile = pl.program_id(0)
    slot = tile % 2
    tm = xb_ref.shape[0]
    wf = wf_ref[...]
    dest = _group_dest(wf, ltri_ref, ustr_ref)
    dest_row = jnp.broadcast_to(dest, (tm, LANES)).T[0:1]
    r = lax.broadcasted_iota(jnp.int32, (ROWS_LOCAL, tm), 0).astype(F32)
    perm = jnp.where(r == dest_row, 1.0, 0.0).astype(BF16)
    xbuf[slot] = _dot(perm, xb_ref[...]).astype(BF16)
    hi = wf.astype(BF16)
    r1 = wf - hi.astype(F32)
    mid = r1.astype(BF16)
    lo = (r1 - mid.astype(F32)).astype(BF16)
    pieces = _dot(perm, jnp.concatenate([hi, mid, lo], axis=1))
    wbuf[slot] = pieces[:, :LANES] + pieces[:, LANES:2 * LANES] + pieces[:, 2 * LANES:]

    def copies(t, s, act):
        _granule_copies(tbl_ref, t, (xbuf.at[s], wbuf.at[s]), (xs_out, ws_out), (sems.at[s, 0], sems.at[s, 1]),
                        True, act)

    copies(tile, slot, lambda c: c.start())

    @pl.when(tile > 0)
    def _():
        copies(tile - 1, 1 - slot, lambda c: c.wait())

    @pl.when(tile == pl.num_programs(0) - 1)
    def _():
        zx[...] = jnp.zeros_like(zx)
        zw[...] = jnp.zeros_like(zw)
        fill = functools.partial(_zero_fill_copies, tbl_ref, (zx, zw), (xs_out, ws_out), (sems.at[2, 0], sems.at[2, 1]))
        fill(lambda c: c.start())
        copies(tile, slot, lambda c: c.wait())
        fill(lambda c: c.wait())


def _dispatch(tbl, x1b, wf, ltri, ustr):
    tm = TM_PROJ
    row = lambda n: pl.BlockSpec((tm, n), lambda i, t: (i, 0))
    full = lambda a: pl.BlockSpec(a.shape, lambda i, t: (0,) * a.ndim)
    anyspace = pl.BlockSpec(memory_space=pl.ANY)
    return pl.pallas_call(
        _dispatch_kernel,
        grid_spec=pltpu.PrefetchScalarGridSpec(
            num_scalar_prefetch=1, grid=(TOKENS // tm,),
            in_specs=[row(D_MODEL), row(LANES), full(ltri), full(ustr)],
            out_specs=[anyspace, anyspace],
            scratch_shapes=[pltpu.VMEM((2, ROWS_LOCAL, D_MODEL), BF16), pltpu.VMEM((2, ROWS_LOCAL, LANES), F32),
                            pltpu.VMEM((GRAN, D_MODEL), BF16), pltpu.VMEM((GRAN, LANES), F32),
                            pltpu.SemaphoreType.DMA((3, 2))]),
        out_shape=[jax.ShapeDtypeStruct((ROWS_SORTED, D_MODEL), BF16),
                   jax.ShapeDtypeStruct((ROWS_SORTED, LANES), F32)],
        compiler_params=pltpu.CompilerParams(dimension_semantics=("arbitrary",)),
        name="moe_dispatch",
    )(tbl, x1b, wf, ltri, ustr)


def _experts_kernel(gid_ref, valid_ref, xs_ref, ws_ref, wg_ref, wu_ref, wd_ref, y_ref):
    i = pl.program_id(0)

    @pl.when(valid_ref[i] == 1)
    def _():
        x = xs_ref[...]
        gates = [_dot(x, wg_ref[e]) for e in range(MOE_EPG)]
        ups = [_dot(x, wu_ref[e]) for e in range(MOE_EPG)]
        ws = ws_ref[...]
        lane = lax.broadcasted_iota(jnp.int32, ws.shape, 1)
        hidden = []
        for e in range(MOE_EPG):
            w_tok = jnp.sum(jnp.where(lane == gid_ref[i] * MOE_EPG + e, ws, 0.0), axis=-1, keepdims=True)
            hidden.append((gates[e] * jax.nn.sigmoid(gates[e]) * ups[e] * w_tok).astype(BF16))
        wd = wd_ref[...].reshape(MOE_EPG * MOE_HIDDEN, D_MODEL)
        y_ref[...] = _dot(jnp.concatenate(hidden, axis=1), wd).astype(BF16)

    @pl.when(valid_ref[i] == 0)
    def _():
        y_ref[...] = jnp.zeros_like(y_ref)


def _experts(gid, valid, xs, ws, wg, wu, wd):
    tm = TM_MOE
    row = lambda n: pl.BlockSpec((tm, n), lambda i, gid, valid: (i, 0))
    wspec = lambda a, b: pl.BlockSpec((MOE_EPG, a, b), lambda i, gid, valid: (gid[i], 0, 0))
    return pl.pallas_call(
        _experts_kernel,
        grid_spec=pltpu.PrefetchScalarGridSpec(
            num_scalar_prefetch=2, grid=(ROWS_SORTED // tm,),
            in_specs=[row(D_MODEL), row(LANES), wspec(D_MODEL, MOE_HIDDEN), wspec(D_MODEL, MOE_HIDDEN),
                      wspec(MOE_HIDDEN, D_MODEL)],
            out_specs=row(D_MODEL)),
        out_shape=jax.ShapeDtypeStruct((ROWS_SORTED, D_MODEL), BF16),
        compiler_params=pltpu.CompilerParams(dimension_semantics=("arbitrary",),
                                             vmem_limit_bytes=EXPERTS_VMEM_BYTES),
        name="moe_experts",
    )(gid, valid, xs, ws, wg, wu, wd)


def _combine_kernel(tbl_ref, wf_ref, x1_ref, ltri_ref, ustr_ref, g_ref, b_ref, y_hbm, o_ref, ybuf, sems):
    tile = pl.program_id(0)
    slot = tile % 2
    tm = x1_ref.shape[0]

    def copies(t, s, act):
        _granule_copies(tbl_ref, t, (ybuf.at[s],), (y_hbm,), (sems.at[s],), False, act)

    def fetch(t, s):
        ybuf[s] = jnp.zeros(ybuf.shape[1:], ybuf.dtype)
        copies(t, s, lambda c: c.start())

    @pl.when(tile == 0)
    def _():
        fetch(tile, slot)

    @pl.when(tile + 1 < pl.num_programs(0))
    def _():
        fetch(tile + 1, 1 - slot)

    dest = _group_dest(wf_ref[...], ltri_ref, ustr_ref)
    c = lax.broadcasted_iota(jnp.int32, (tm, ROWS_LOCAL), 1).astype(F32)
    unperm = jnp.where(c == dest, 1.0, 0.0).astype(BF16)
    copies(tile, slot, lambda c: c.wait())
    ffn = _dot(unperm, ybuf[slot])
    o_ref[...] = _layer_norm(DN_ALPHA * x1_ref[...] + ffn, g_ref[...], b_ref[...])


def _combine(tbl, wf, x1, ltri, ustr, g, b, y):
    tm = TM_PROJ
    row = lambda n: pl.BlockSpec((tm, n), lambda i, t: (i, 0))
    full = lambda a: pl.BlockSpec(a.shape, lambda i, t: (0,) * a.ndim)
    return pl.pallas_call(
        _combine_kernel,
        grid_spec=pltpu.PrefetchScalarGridSpec(
            num_scalar_prefetch=1, grid=(TOKENS // tm,),
            in_specs=[row(LANES), row(D_MODEL), full(ltri), full(ustr), full(g), full(b),
                      pl.BlockSpec(memory_space=pl.ANY)],
            out_specs=row(D_MODEL),
            scratch_shapes=[pltpu.VMEM((2, ROWS_LOCAL, D_MODEL), BF16), pltpu.SemaphoreType.DMA((2,))]),
        out_shape=jax.ShapeDtypeStruct((TOKENS, D_MODEL), F32),
        compiler_params=pltpu.CompilerParams(dimension_semantics=("arbitrary",)),
        name="moe_combine",
    )(tbl, wf, x1, ltri, ustr, g, b, y)


def _routing_tables(cnt):
    n_tiles = cnt.shape[0]
    c = cnt[:, 0, :MOE_GROUPS].astype(jnp.int32)
    gran = (c + GRAN - 1) // GRAN
    local = jnp.cumsum(gran, axis=1) - gran
    per_tile = TM_MOE // GRAN
    tiles_g = (jnp.sum(gran, axis=0) + per_tile - 1) // per_tile
    ends = jnp.cumsum(tiles_g)
    base = (ends - tiles_g) * per_tile
    glob = base[None, :] + jnp.cumsum(gran, axis=0) - gran
    used = jnp.sum(gran, axis=0)
    pad_first = jnp.concatenate([base + used, ends[-1:] * per_tile])
    pad_count = jnp.concatenate([tiles_g * per_tile - used, ROWS_SORTED // GRAN - ends[-1:] * per_tile])
    tail = jnp.stack([pad_first, pad_count], axis=1).reshape(-1)
    tbl = jnp.concatenate([jnp.concatenate([gran, local, glob], axis=1).reshape(n_tiles * TBL_W), tail])
    idx = jnp.arange(ROWS_SORTED // TM_MOE)
    gid = jnp.minimum(jnp.sum(idx[:, None] >= ends[None, :], axis=1), MOE_GROUPS - 1).astype(jnp.int32)
    valid = (idx < ends[-1]).astype(jnp.int32)
    return tbl, gid, valid


def _sort_tables():
    t = jnp.arange(TM_PROJ)
    ltri = (t[None, :] < t[:, None]).astype(BF16)
    l = jnp.arange(LANES)
    ustr = (l[:, None] < l[None, :]).astype(BF16)
    return ltri, ustr


def _dup_halves(t):
    lane = lax.broadcasted_iota(jnp.int32, t.shape, 1)
    r = pltpu.roll(t, HALF, 1)
    return jnp.where(lane < HALF, t, r), jnp.where(lane < HALF, r, t)


def _nsa_in_kernel(x_ref, w_ref, gb_ref, q_ref, kc_ref, vc_ref, ks_ref, vst_ref, kw_ref, vwt_ref, gate_ref):
    h = _dot(x_ref[...].astype(BF16), w_ref[...])
    q_ref[...] = (h[:, 0:C_MIX] * (NSA_SCALE * LOG2E)).astype(BF16)
    kc_ref[...] = h[:, 1024:1152].astype(BF16)
    vc_ref[...] = h[:, 1152:1280].astype(BF16)
    tm = h.shape[0]
    lane = lax.broadcasted_iota(jnp.int32, (tm, LANES), 1)
    for g, d in enumerate(_dup_halves(h[:, 1536:1664])):
        kw_ref[:, g * LANES:(g + 1) * LANES] = d.astype(BF16)
    pos = (pl.program_id(0) * tm) % SEQ + lax.broadcasted_iota(jnp.int32, (tm, LANES), 0)
    block_onehot = jnp.where(lane == pos // NSA_SEL_LEN, 1.0, 0.0).astype(BF16)
    for g, d in enumerate(_dup_halves(h[:, 1280:1408])):
        ks_ref[:, 2 * g * LANES:(2 * g + 1) * LANES] = d.astype(BF16)
        ks_ref[:, (2 * g + 1) * LANES:(2 * g + 2) * LANES] = block_onehot
    for idx, ref in ((1, vst_ref), (3, vwt_ref)):
        tail = jnp.where(lane == HALF, 1.0, 0.0)
        _store_transposed(ref, [jnp.where(lane < HALF, d, tail)
                                for d in _dup_halves(h[:, 1280 + idx * LANES:1280 + (idx + 1) * LANES])])
    for g in range(NSA_GROUPS):
        gate_ref[g] = jax.nn.sigmoid(h[:, 1792 + g * LANES:1792 + (g + 1) * LANES] + gb_ref[g])


def _nsa_in(x2, w, gb):
    tm = TM_PROJ
    row = lambda n: pl.BlockSpec((tm, n), lambda i: (i, 0))
    full = lambda a: pl.BlockSpec(a.shape, lambda i: (0,) * a.ndim)
    sd = jax.ShapeDtypeStruct
    vt_spec = pl.BlockSpec((NSA_GROUPS, tm // TK_ATT, LANES, TK_ATT), lambda i: (0, i, 0, 0))
    vt_shape = sd((NSA_GROUPS, TOKENS // TK_ATT, LANES, TK_ATT), BF16)
    return pl.pallas_call(
        _nsa_in_kernel,
        grid=(TOKENS // tm,),
        in_specs=[row(D_MODEL), full(w), full(gb)],
        out_specs=[row(C_MIX), row(LANES), row(LANES), row(4 * LANES), vt_spec, row(2 * LANES), vt_spec,
                   pl.BlockSpec((NSA_GROUPS, tm, LANES), lambda i: (0, i, 0))],
        out_shape=[sd((TOKENS, C_MIX), BF16), sd((TOKENS, LANES), BF16), sd((TOKENS, LANES), BF16),
                   sd((TOKENS, 4 * LANES), BF16), vt_shape, sd((TOKENS, 2 * LANES), BF16), vt_shape,
                   sd((NSA_GROUPS, TOKENS, LANES), F32)],
        compiler_params=pltpu.CompilerParams(dimension_semantics=("arbitrary",)),
        name="nsa_in",
    )(x2, w, gb)


def _compress_kernel(kc_ref, vc_ref, pk_ref, pv_ref, wk1_ref, wv1_ref, wk2_ref, wv2_ref, ko_ref, vo_ref):
    for a_ref, p_ref, w1_ref, w2_ref, o_ref in ((kc_ref, pk_ref, wk1_ref, wk2_ref, ko_ref),
                                                (vc_ref, pv_ref, wv1_ref, wv2_ref, vo_ref)):
        a = a_ref[0].astype(F32)
        a0 = (a + p_ref[0]).astype(BF16)
        a1 = (a + p_ref[1]).astype(BF16)
        outs = []
        for g in range(NSA_GROUPS):
            first = _dot(a0, w1_ref[g])
            second = _dot(a1, w1_ref[NSA_GROUPS + g])
            hid = first + pltpu.roll(second, NSA_NCMP - 1, 0)
            outs.append(_dot(_gelu(hid).astype(BF16), w2_ref[...]))
        if o_ref is ko_ref:
            o_ref[0] = jnp.concatenate(outs, axis=1).astype(BF16)
        else:
            for g in range(NSA_GROUPS):
                o_ref[0, g] = outs[g].T.astype(BF16)


def _compress(kc_r, vc_r, pk, pv, wk1, wv1, wk2, wv2):
    blk = pl.BlockSpec((1, NSA_NCMP, NSA_CMP_STRIDE * LANES), lambda b: (b, 0, 0))
    full = lambda a: pl.BlockSpec(a.shape, lambda b: (0,) * a.ndim)
    sd = jax.ShapeDtypeStruct
    return pl.pallas_call(
        _compress_kernel,
        grid=(BATCH,),
        in_specs=[blk, blk, full(pk), full(pv), full(wk1), full(wv1), full(wk2), full(wv2)],
        out_specs=[pl.BlockSpec((1, NSA_NCMP, 2 * LANES), lambda b: (b, 0, 0)),
                   pl.BlockSpec((1, NSA_GROUPS, LANES, NSA_NCMP), lambda b: (b, 0, 0, 0))],
        out_shape=[sd((BATCH, NSA_NCMP, 2 * LANES), BF16), sd((BATCH, NSA_GROUPS, LANES, NSA_NCMP), BF16)],
        compiler_params=pltpu.CompilerParams(dimension_semantics=("arbitrary",)),
        name="nsa_compress",
    )(kc_r, vc_r, pk, pv, wk1, wv1, wk2, wv2)


def _nsa_attn_kernel(q_ref, kc_ref, vct_ref, ks_ref, vst_ref, kw_ref, vwt_ref, gate_ref, cover_ref,
                     o_ref, m_ref, acc_ref, sa_ref, sb_ref):
    tq, tk, hpg = TQ_NSA, TK_ATT, NSA_HPG
    qi = pl.program_id(2)
    q0 = qi * tq
    lane = lax.broadcasted_iota(jnp.int32, (tq, LANES), 1)
    t_tok = q0 + lax.broadcasted_iota(jnp.int32, (1, tq), 1)
    head = lambda x, i: x[:, i * tq:(i + 1) * tq]
    heads = range(hpg)

    parts = []
    for p in range(hpg // 2):
        qp = q_ref[0, :, p * LANES:(p + 1) * LANES]
        zero = jnp.zeros_like(qp)
        parts.append(jnp.where(lane < HALF, qp, zero))
        parts.append(jnp.where(lane < HALF, zero, qp))
    qs = jnp.concatenate(parts, axis=0)

    s_c = _dot_nt(kc_ref[0], qs)
    n_sub = lax.broadcasted_iota(jnp.int32, (NSA_NCMP, 1), 0)
    vis = t_tok >= n_sub * NSA_CMP_STRIDE + (NSA_CMP_LEN - 1)
    sees_any = t_tok >= NSA_CMP_LEN - 1
    p_sum = jnp.zeros((NSA_NCMP, tq), F32)
    p_c = []
    for i in heads:
        sm = jnp.where(vis, head(s_c, i), NEG)
        e = jnp.exp2(sm - jnp.max(sm, axis=0, keepdims=True))
        p = e * jnp.where(sees_any, 1.0 / jnp.sum(e, axis=0, keepdims=True), 0.0)
        p_sum = p_sum + p
        p_c.append(p.astype(BF16))
    o_c = _dot(vct_ref[0, 0], jnp.concatenate(p_c, axis=1))

    imp = jnp.dot(cover_ref[...], p_sum, preferred_element_type=F32,
                  precision=lax.Precision.HIGHEST)[0:NSA_NSEL]
    jj = lax.broadcasted_iota(jnp.int32, (NSA_NSEL, 1), 0)
    tb = t_tok // NSA_SEL_LEN
    forced = (jj == 0) | (jj == tb) | (jj == tb - 1)
    score = jnp.where(forced, NSA_FORCE, jnp.where(jj <= tb, imp, -NSA_FORCE))
    sub = 8
    rows = [score[b * sub:(b + 1) * sub] for b in range(NSA_NSEL // sub)]
    ranks = [jnp.zeros((sub, tq), jnp.int32) for _ in rows]
    j_in = lax.broadcasted_iota(jnp.int32, (sub, 1), 0)
    for i in range(NSA_NSEL):
        si = score[i:i + 1, :]
        for b, blk in enumerate(rows):
            if b < i // sub:
                beats = si > blk
            elif b > i // sub:
                beats = si >= blk
            else:
                beats = (si > blk) | ((si == blk) & (j_in > i % sub))
            ranks[b] = ranks[b] + beats.astype(jnp.int32)
    rank = jnp.concatenate(ranks, axis=0)
    sel_bias = jnp.where(rank < NSA_TOPK, 0.0, NEG)
    bias_rows = jnp.concatenate([sel_bias, jnp.zeros((LANES - NSA_NSEL, tq), F32)], axis=0).T.astype(BF16)
    qs_sel = jnp.concatenate([qs, jnp.concatenate([bias_rows] * hpg, axis=0)], axis=1)

    m_ref[...] = jnp.full(m_ref.shape, NEG, F32)
    acc_ref[...] = jnp.zeros(acc_ref.shape, F32)
    k_sub = lax.broadcasted_iota(jnp.int32, (tk, 1), 0)
    diag = q0 // tk
    causal = diag * tk + k_sub <= t_tok
    per_dot = max(1, 2 * LANES // tq)

    def scores_into(ref, kt):
        ref[...] = _dot_nt(ks_ref[0, pl.ds(pl.multiple_of(kt * tk, tk), tk), :], qs_sel)

    def process(ref, kt, diagonal):
        s = ref[...]
        vt = vst_ref[0, kt]
        for i0 in range(0, hpg, per_dot):
            cols = slice(i0 * tq, (i0 + per_dot) * tq)
            probs, alphas = [], []
            for i in range(i0, i0 + per_dot):
                for part in range(tq // LANES):
                    sub = slice(part * LANES, (part + 1) * LANES)
                    sm = s[:, i * tq + part * LANES:i * tq + (part + 1) * LANES]
                    if diagonal:
                        sm = jnp.where(causal[:, sub], sm, NEG)
                    m_old = m_ref[i, :, sub]
                    m_new = jnp.maximum(m_old, jnp.max(sm, axis=0, keepdims=True))
                    m_ref[i, :, sub] = m_new
                    probs.append(jnp.exp2((sm - m_new).astype(BF16)))
                    alphas.append(jnp.exp2(m_old - m_new))
            pv = _dot(vt, jnp.concatenate(probs, axis=1))
            acc_ref[:, cols] = jnp.concatenate(alphas, axis=1) * acc_ref[:, cols] + pv

    odd = diag % 2
    scores_into(sa_ref, 0)

    @pl.when(odd == 1)
    def _():
        process(sa_ref, 0, False)
        scores_into(sa_ref, 1)

    def pair(k, c):
        t0 = odd + 2 * k
        scores_into(sb_ref, t0 + 1)
        process(sa_ref, t0, False)
        scores_into(sa_ref, t0 + 2)
        process(sb_ref, t0 + 1, False)
        return c

    lax.fori_loop(0, diag // 2, pair, 0)
    process(sa_ref, diag, True)

    c0 = jnp.maximum(q0 - (NSA_WINDOW - 1), 0) // tk
    s_w, in_win = [], []
    for c in range(WIN_CHUNKS):
        r0 = pl.multiple_of((c0 + c) * tk, tk)
        s_w.append(_dot_nt(kw_ref[0, pl.ds(r0, tk), :], qs))
        kpos = r0 + k_sub
        in_win.append((kpos <= t_tok) & (kpos > t_tok - NSA_WINDOW))
    vwt = [vwt_ref[0, c0 + c] for c in range(WIN_CHUNKS)]
    o_w = []
    for i0 in range(0, hpg, per_dot):
        e_w = [[] for _ in range(WIN_CHUNKS)]
        for i in range(i0, i0 + per_dot):
            sm = [jnp.where(in_win[c], head(s_w[c], i), NEG) for c in range(WIN_CHUNKS)]
            m = functools.reduce(jnp.maximum, [jnp.max(x, axis=0, keepdims=True) for x in sm])
            for c in range(WIN_CHUNKS):
                e_w[c].append(jnp.exp2((sm[c] - m).astype(BF16)))
        o_grp = functools.reduce(jnp.add, [_dot(vwt[c], jnp.concatenate(e_w[c], axis=1))
                                           for c in range(WIN_CHUNKS)])
        o_w.extend(o_grp[:, n * tq:(n + 1) * tq] for n in range(per_dot))

    gt = gate_ref[0, 0].T
    o_s = acc_ref[...]
    outs = []
    for i in heads:
        c_i, s_i, w_i = head(o_c, i), head(o_s, i), o_w[i]
        outs.append(gt[i:i + 1] * c_i[:HALF]
                    + gt[hpg + i:hpg + i + 1] / s_i[HALF:HALF + 1] * s_i[:HALF]
                    + gt[2 * hpg + i:2 * hpg + i + 1] / w_i[HALF:HALF + 1] * w_i[:HALF])
    for p in range(hpg // 2):
        pair = jnp.concatenate([outs[2 * p], outs[2 * p + 1]], axis=0)
        o_ref[0, :, p * LANES:(p + 1) * LANES] = pair.T.astype(BF16)


def _nsa_attn(q3, kc2, vct, ks3, vst, kw3, vwt, gates4, cover_t):
    tq = TQ_NSA
    half_w = NSA_HPG * NSA_DH
    n_chunks = SEQ // TK_ATT
    kv = pl.BlockSpec((1, SEQ, LANES), lambda b, g, i: (b, 0, g))
    vt = pl.BlockSpec((1, n_chunks, LANES, TK_ATT), lambda b, g, i: (g, b, 0, 0))
    full = lambda a: pl.BlockSpec(a.shape, lambda b, g, i: (0,) * a.ndim)
    return pl.pallas_call(
        _nsa_attn_kernel,
        grid=(BATCH, NSA_GROUPS, SEQ // tq),
        in_specs=[pl.BlockSpec((1, tq, half_w), lambda b, g, i: (b, i, g)),
                  pl.BlockSpec((1, NSA_NCMP, LANES), lambda b, g, i: (b, 0, g)),
                  pl.BlockSpec((1, 1, LANES, NSA_NCMP), lambda b, g, i: (b, g, 0, 0)),
                  pl.BlockSpec((1, SEQ, 2 * LANES), lambda b, g, i: (b, 0, g)), vt, kv, vt,
                  pl.BlockSpec((1, 1, tq, LANES), lambda b, g, i: (g, b, i, 0)),
                  full(cover_t)],
        out_specs=pl.BlockSpec((1, tq, half_w), lambda b, g, i: (b, i, g)),
        out_shape=jax.ShapeDtypeStruct((BATCH, SEQ, C_MIX), BF16),
        scratch_shapes=[pltpu.VMEM((NSA_HPG, 1, tq), F32),
                        pltpu.VMEM((LANES, NSA_HPG * tq), F32),
                        pltpu.VMEM((TK_ATT, NSA_HPG * tq), F32), pltpu.VMEM((TK_ATT, NSA_HPG * tq), F32)],
        compiler_params=pltpu.CompilerParams(dimension_semantics=("arbitrary",) * 3),
        name="nsa_attn",
    )(q3, kc2, vct, ks3, vst, kw3, vwt, gates4, cover_t)


def _rope_tables():
    half = MLA_ROPE // 2
    freq = jnp.exp(-math.log(ROPE_BASE) * jnp.arange(half, dtype=F32) / half)
    return freq[:, None]


def _swap_halves(w):
    half = w.shape[-1] // 2
    return jnp.concatenate([w[..., half:], w[..., :half]], axis=-1)


def _pad_last(w, n):
    return jnp.pad(w, [(0, 0)] * (w.ndim - 1) + [(0, n - w.shape[-1])])


def _ab_weights(w_in, w_uq, w_uk, w_uv):
    w_kr = w_in[:, 1408:1440]
    place = lambda w: jnp.pad(w, ((0, 0), (MLA_NOPE, LANES - MLA_NOPE - MLA_ROPE)))
    win = jnp.concatenate([w_in[:, :1408], place(w_kr), place(_swap_halves(w_kr))], axis=1).astype(BF16)
    uq = w_uq.reshape(MLA_Q_RANK, MLA_HEADS, MLA_NOPE + MLA_ROPE)
    nope, rp = uq[..., :MLA_NOPE], uq[..., MLA_NOPE:]
    q_pad = _pad_last(jnp.concatenate([nope, rp], -1), LANES).reshape(MLA_Q_RANK, MLA_HEADS * LANES)
    q_sw = _pad_last(jnp.concatenate([jnp.zeros_like(nope), _swap_halves(rp)], -1), LANES)
    wq = jnp.concatenate([q_pad, q_sw.reshape(MLA_Q_RANK, MLA_HEADS * LANES)], axis=1).astype(BF16)
    k_pad = _pad_last(w_uk.reshape(MLA_KV_RANK, MLA_HEADS, MLA_NOPE), LANES).reshape(MLA_KV_RANK, -1)
    wkv = jnp.concatenate([k_pad, w_uv], axis=1).astype(BF16)
    return win, wq, wkv


def _router_weights(w_rg, b_rg, w_re, b_re):
    wr = _pad_last(jnp.concatenate([w_re, w_rg], axis=1), LANES)
    wr_hi = wr.astype(BF16)
    wr_lo = (wr - wr_hi.astype(F32)).astype(BF16)
    br = _pad_last(jnp.concatenate([b_re, b_rg])[None, :], LANES)
    return jnp.concatenate([wr_hi, wr_lo], axis=1), br


def _nsa_in_weights(w_in, gate_b):
    g_cols = w_in[:, C_MIX + 768:].reshape(D_MODEL, 3, NSA_GROUPS, NSA_HPG)
    g_blocks = [_pad_last(g_cols[:, :, g, :].reshape(D_MODEL, 3 * NSA_HPG), LANES) for g in range(NSA_GROUPS)]
    w = jnp.concatenate([w_in[:, :C_MIX + 768]] + g_blocks, axis=1).astype(BF16)
    gb = gate_b.reshape(3, NSA_GROUPS, NSA_HPG)
    gb = jnp.stack([_pad_last(gb[:, g, :].reshape(1, 3 * NSA_HPG), LANES) for g in range(NSA_GROUPS)])
    return w, gb


def _compress_weights(pos, w1, w2):
    w1r = w1.reshape(2, NSA_CMP_STRIDE, NSA_DH, NSA_CMP_HIDDEN)
    zero = jnp.zeros_like(w1r)
    per_g = []
    for g in range(NSA_GROUPS):
        parts = [w1r if gg == g else zero for gg in range(NSA_GROUPS)]
        per_g.append(jnp.stack(parts, axis=2).reshape(2, NSA_CMP_STRIDE * LANES, NSA_CMP_HIDDEN))
    w1x = jnp.stack(per_g, axis=1).reshape(2 * NSA_GROUPS, NSA_CMP_STRIDE * LANES, NSA_CMP_HIDDEN)
    posr = pos.reshape(2, NSA_CMP_STRIDE, 1, NSA_DH)
    posx = jnp.broadcast_to(posr, (2, NSA_CMP_STRIDE, NSA_GROUPS, NSA_DH)).reshape(2, 1, NSA_CMP_STRIDE * LANES)
    w2x = jnp.concatenate([w2, w2], axis=1)
    return posx, w1x.astype(BF16), w2x.astype(BF16)


def _selection_tables():
    n = jnp.arange(LANES)[:, None]
    j = jnp.arange(LANES)[None, :]
    c0 = n * NSA_CMP_STRIDE
    s0 = j * NSA_SEL_LEN
    cover = ((c0 < s0 + NSA_SEL_LEN) & (c0 + NSA_CMP_LEN > s0) & (n < NSA_NCMP - 1) & (j < NSA_NSEL))
    return jnp.transpose(cover).astype(F32)


def kernel(x, positions, ab_w_in, ab_gm_ln_g, ab_gm_ln_b, ab_gm_ws, ab_gm_bs, ab_mla_q_norm,
           ab_mla_kv_norm, ab_mla_w_uq, ab_mla_w_uk, ab_mla_w_uv, ab_w_o, c_w_in, c_cmp_pos, c_w_ck1,
           c_w_ck2, c_w_cv1, c_w_cv2, c_gate_b, c_w_o, moe_w_rg, moe_b_rg, moe_w_re, moe_b_re,
           moe_w_gate, moe_w_up, moe_w_down, ln1_g, ln1_b, ln2_g, ln2_b):
    x2 = x.reshape(TOKENS, D_MODEL)
    pos3 = positions.reshape(TOKENS // TM_PROJ, 1, TM_PROJ)
    vec = lambda a: a[None, :]

    ltri, ustr = _sort_tables()

    def moe_layer(layer, x1b, x1, wf, cnt):
        tbl, gid, valid = _routing_tables(cnt)
        xs, ws = _dispatch(tbl, x1b, wf, ltri, ustr)
        y = _experts(gid, valid, xs, ws, moe_w_gate[layer].astype(BF16), moe_w_up[layer].astype(BF16),
                     moe_w_down[layer].astype(BF16))
        return _combine(tbl, wf, x1, ltri, ustr, vec(ln2_g[layer]), vec(ln2_b[layer]), y)

    win, wq, wkv = _ab_weights(ab_w_in[0], ab_mla_w_uq[0], ab_mla_w_uk[0], ab_mla_w_uv[0])
    gu, vn, q, k, vt = _ab_in(x2, pos3, win, vec(ab_gm_ln_g[0]), vec(ab_gm_ln_b[0]), vec(ab_mla_q_norm[0]),
                              vec(ab_mla_kv_norm[0]), wq, wkv, _rope_tables())
    yb = _mla_attn(q.reshape(BATCH, SEQ, -1), k.reshape(BATCH, SEQ, -1), vt)
    wr, br = _router_weights(moe_w_rg[0], moe_b_rg[0], moe_w_re[0], moe_b_re[0])
    x1, x1b, wf, cnt = _mix_out(x2, yb.reshape(TOKENS, -1), ab_w_o[0].astype(BF16), vec(ln1_g[0]),
                                vec(ln1_b[0]), wr, br, gm=(gu, vn, ab_gm_ws[0], jnp.transpose(ab_gm_bs[0])))
    x2 = moe_layer(0, x1b, x1, wf, cnt)

    w_nsa, gb = _nsa_in_weights(c_w_in[0], c_gate_b[0])
    q, kc, vc, ks, vst, kw, vwt, gates = _nsa_in(x2, w_nsa, gb)
    pk, wk1, wk2 = _compress_weights(c_cmp_pos[0, 0], c_w_ck1[0], c_w_ck2[0])
    pv, wv1, wv2 = _compress_weights(c_cmp_pos[0, 1], c_w_cv1[0], c_w_cv2[0])
    blocks = lambda a: a.reshape(BATCH, NSA_NCMP, NSA_CMP_STRIDE * LANES)
    kc2, vct = _compress(blocks(kc), blocks(vc), pk, pv, wk1, wv1, wk2, wv2)
    b3 = lambda a: a.reshape(BATCH, SEQ, -1)
    o = _nsa_attn(b3(q), kc2, vct, b3(ks), vst, b3(kw), vwt,
                  gates.reshape(NSA_GROUPS, BATCH, SEQ, LANES), _selection_tables())
    wr, br = _router_weights(moe_w_rg[1], moe_b_rg[1], moe_w_re[1], moe_b_re[1])
    x1, x1b, wf, cnt = _mix_out(x2, o.reshape(TOKENS, -1), c_w_o[0].astype(BF16), vec(ln1_g[1]), vec(ln1_b[1]),
                                wr, br)
    x2 = moe_layer(1, x1b, x1, wf, cnt)
    return x2.reshape(BATCH, SEQ, D_MODEL)
```

```python
import functools
import math

import jax
import jax.numpy as jnp
from jax import lax
from jax.experimental import pallas as pl
from jax.experimental.pallas import tpu as pltpu

F32 = jnp.float32
BF16 = jnp.bfloat16

D_MODEL = 1024
BATCH = 16
SEQ = 2048
TOKENS = BATCH * SEQ
DEPTH = 2
DN_ALPHA = (2.0 * DEPTH) ** 0.25
LN_EPS = 1e-5
NEG = -1e30
LOG2E = math.log2(math.e)
LANES = 128
HALF = LANES // 2

GM_WIDTH = 512
GM_GROUPS = 4
GM_CHUNK = 128

MLA_HEADS = 8
MLA_NOPE = 64
MLA_ROPE = 32
MLA_V = 64
MLA_Q_RANK = 256
MLA_KV_RANK = 128
ROPE_BASE = 10000.0
MLA_SCALE = (MLA_NOPE + MLA_ROPE) ** -0.5

NSA_HEADS = 16
NSA_GROUPS = 2
NSA_HPG = 8
NSA_DH = 64
NSA_CMP_LEN = 32
NSA_CMP_STRIDE = 16
NSA_CMP_HIDDEN = 256
NSA_SEL_LEN = 64
NSA_TOPK = 8
NSA_WINDOW = 512
NSA_NSEL = SEQ // NSA_SEL_LEN
NSA_NCMP = SEQ // NSA_CMP_STRIDE
NSA_FORCE = 1e4
NSA_SCALE = NSA_DH ** -0.5
C_MIX = NSA_HEADS * NSA_DH

MOE_GROUPS = 4
MOE_EPG = 8
MOE_EXPERTS = 32
MOE_HIDDEN = 256

TM_PROJ = 512
TQ_MLA = 256
TQ_NSA = 256
TK_ATT = 256
WIN_SUB = LANES
WIN_BAND = NSA_WINDOW + WIN_SUB
MLA_VT_ROWS = LANES + 16
TM_MOE = 512
EXPERTS_VMEM_BYTES = (2 * 3 * MOE_EPG * D_MODEL * MOE_HIDDEN * 2 + 2 * 2 * TM_MOE * D_MODEL * 2
                      + 2 * MOE_EPG * TM_MOE * MOE_HIDDEN * 4 + TM_MOE * MOE_EPG * MOE_HIDDEN * 2
                      + 2 * TM_MOE * D_MODEL * 4)

GSEL_LANE = MOE_EXPERTS
GRAN = 16
TBL_W = 3 * MOE_GROUPS
ROWS_LOCAL = 640
ROWS_SORTED = TM_MOE * (TOKENS // TM_MOE + MOE_GROUPS
                        + -(-(TOKENS // TM_PROJ) * MOE_GROUPS * (GRAN - 1) // TM_MOE))


def _dot(a, b):
    return jnp.dot(a, b, preferred_element_type=F32)


def _dot_nt(a, b):
    return lax.dot_general(a, b, (((1,), (1,)), ((), ())), preferred_element_type=F32)


def _gelu(x):
    return 0.5 * x * (1.0 + jnp.tanh(math.sqrt(2.0 / math.pi) * (x + 0.044715 * (x * x * x))))


def _layer_norm(x, g, b):
    mu = jnp.mean(x, axis=-1, keepdims=True)
    xc = x - mu
    var = jnp.mean(xc * xc, axis=-1, keepdims=True)
    return xc * lax.rsqrt(var + LN_EPS) * g + b


def _rms_norm(x, g):
    return x * lax.rsqrt(jnp.mean(x * x, axis=-1, keepdims=True) + LN_EPS) * g


def _store_transposed(ref, blocks):
    extra, width = ref.shape[2] - LANES, ref.shape[3]
    if extra:
        ones_rows = jnp.where(lax.broadcasted_iota(jnp.int32, (extra, width), 0) == 0, 1.0, 0.0)
    for n, blk in enumerate(blocks):
        t = blk.T
        for c in range(t.shape[1] // width):
            chunk = t[:, c * width:(c + 1) * width]
            if extra:
                chunk = jnp.concatenate([chunk, ones_rows], axis=0)
            ref[n, c] = chunk.astype(ref.dtype)


def _ab_in_kernel(x_ref, pos_ref, win_ref, lng_ref, lnb_ref, qg_ref, kvg_ref, wq_ref, wkv_ref,
                  fc_ref, gu_ref, vn_ref, q_ref, k_ref, vt_ref):
    h = _dot(x_ref[...].astype(BF16), win_ref[...])
    gu_ref[...] = _gelu(h[:, 0:512]).astype(BF16)
    vn_ref[...] = _layer_norm(_gelu(h[:, 512:1024]), lng_ref[...], lnb_ref[...]).astype(BF16)

    tm = x_ref.shape[0]
    ang = fc_ref[...] * pos_ref[0].astype(F32)
    cos_t, sin_t = jnp.cos(ang), jnp.sin(ang)
    ones_t, zeros_t = jnp.ones((MLA_NOPE, tm), F32), jnp.zeros((MLA_NOPE, tm), F32)
    pad = LANES - MLA_NOPE - MLA_ROPE
    cc = jnp.concatenate([ones_t, cos_t, cos_t, ones_t[:pad]], axis=0).T
    ss = jnp.concatenate([zeros_t, -sin_t, sin_t, zeros_t[:pad]], axis=0).T

    cqn = _rms_norm(h[:, 1024:1280], qg_ref[...]).astype(BF16)
    qq = _dot(cqn, wq_ref[...])
    for hd in range(MLA_HEADS):
        lo, hi = hd * LANES, (hd + 1) * LANES
        q_ref[:, lo:hi] = ((qq[:, lo:hi] * cc + qq[:, 1024 + lo:1024 + hi] * ss) * (MLA_SCALE * LOG2E)).astype(BF16)

    ckvn = _rms_norm(h[:, 1280:1408], kvg_ref[...]).astype(BF16)
    kv = _dot(ckvn, wkv_ref[...])
    k_rope = h[:, 1408:1536] * cc + h[:, 1536:1664] * ss
    for hd in range(MLA_HEADS):
        lo, hi = hd * LANES, (hd + 1) * LANES
        k_ref[:, lo:hi] = (kv[:, lo:hi] + k_rope).astype(BF16)
    _store_transposed(vt_ref, [kv[:, 1024 + p * LANES:1024 + (p + 1) * LANES] for p in range(MLA_HEADS // 2)])


def _ab_in(x2, pos3, win, lng, lnb, qg, kvg, wq, wkv, fc):
    tm = TM_PROJ
    row = lambda n: pl.BlockSpec((tm, n), lambda i: (i, 0))
    full = lambda a: pl.BlockSpec(a.shape, lambda i: (0,) * a.ndim)
    return pl.pallas_call(
        _ab_in_kernel,
        grid=(TOKENS // tm,),
        in_specs=[row(D_MODEL), pl.BlockSpec((1, 1, tm), lambda i: (i, 0, 0)), full(win), full(lng), full(lnb),
                  full(qg), full(kvg), full(wq), full(wkv), full(fc)],
        out_specs=[row(512), row(512), row(1024), row(1024),
                   pl.BlockSpec((MLA_HEADS // 2, tm // TK_ATT, MLA_VT_ROWS, TK_ATT), lambda i: (0, i, 0, 0))],
        out_shape=[jax.ShapeDtypeStruct((TOKENS, 512), BF16), jax.ShapeDtypeStruct((TOKENS, 512), BF16),
                   jax.ShapeDtypeStruct((TOKENS, 1024), BF16), jax.ShapeDtypeStruct((TOKENS, 1024), BF16),
                   jax.ShapeDtypeStruct((MLA_HEADS // 2, TOKENS // TK_ATT, MLA_VT_ROWS, TK_ATT), BF16)],
        compiler_params=pltpu.CompilerParams(dimension_semantics=("arbitrary",)),
        name="ab_in",
    )(x2, pos3, win, lng, lnb, qg, kvg, wq, wkv, fc)


def _mla_attn_kernel(q_ref, k_ref, vt_ref, o_ref, m_ref, l_ref, acc_ref):
    tq, tk = TQ_MLA, TK_ATT
    qi = pl.program_id(1)
    krow = lax.broadcasted_iota(jnp.int32, (tk, tq), 0)
    qcol = lax.broadcasted_iota(jnp.int32, (tk, tq), 1)
    top = lax.broadcasted_iota(jnp.int32, (LANES, tq), 0) < HALF
    m_ref[...] = jnp.full(m_ref.shape, NEG, F32)
    l_ref[...] = jnp.zeros(l_ref.shape, F32)
    acc_ref[...] = jnp.zeros(acc_ref.shape, F32)

    def tile(j, masked):
        r0 = pl.multiple_of(j * tk, tk)
        scores = [_dot_nt(k_ref[0, pl.ds(r0, tk), h * LANES:(h + 1) * LANES], q_ref[0, :, h * LANES:(h + 1) * LANES])
                  for h in range(MLA_HEADS)]
        for pr in range(MLA_HEADS // 2):
            probs, alphas = [], []
            for h in (2 * pr, 2 * pr + 1):
                s = jnp.where(krow <= qcol, scores[h], NEG) if masked else scores[h]
                m_old = m_ref[h]
                m_new = jnp.maximum(m_old, jnp.max(s, axis=0, keepdims=True))
                alphas.append(jnp.exp2(m_old - m_new))
                probs.append(jnp.exp2(s - m_new).astype(BF16))
                m_ref[h] = m_new
            pv = _dot(vt_ref[pr, j], jnp.concatenate(probs, axis=1))
            for n, hh in enumerate((2 * pr, 2 * pr + 1)):
                l_ref[hh] = alphas[n] * l_ref[hh] + pv[LANES:LANES + 1, n * tq:(n + 1) * tq]
            a = jnp.where(top, alphas[0], alphas[1])
            acc_ref[pr] = a * acc_ref[pr] + jnp.where(top, pv[:LANES, :tq], pv[:LANES, tq:])

    def body(j, c):
        tile(j, False)
        return c

    lax.fori_loop(0, qi, body, 0)
    tile(qi, True)
    for pr in range(MLA_HEADS // 2):
        l = jnp.where(top, l_ref[2 * pr], l_ref[2 * pr + 1])
        o_ref[0, :, pr * LANES:(pr + 1) * LANES] = (acc_ref[pr] / l).T.astype(BF16)


def _mla_attn(q3, k3, vt):
    tq = TQ_MLA
    n_chunks = SEQ // TK_ATT
    return pl.pallas_call(
        _mla_attn_kernel,
        grid=(BATCH, SEQ // tq),
        in_specs=[pl.BlockSpec((1, tq, MLA_HEADS * LANES), lambda b, i: (b, i, 0)),
                  pl.BlockSpec((1, SEQ, MLA_HEADS * LANES), lambda b, i: (b, 0, 0)),
                  pl.BlockSpec((MLA_HEADS // 2, n_chunks, MLA_VT_ROWS, TK_ATT), lambda b, i: (0, b, 0, 0))],
        out_specs=pl.BlockSpec((1, tq, MLA_HEADS * MLA_V), lambda b, i: (b, i, 0)),
        out_shape=jax.ShapeDtypeStruct((BATCH, SEQ, MLA_HEADS * MLA_V), BF16),
        scratch_shapes=[pltpu.VMEM((MLA_HEADS, 1, tq), F32), pltpu.VMEM((MLA_HEADS, 1, tq), F32),
                        pltpu.VMEM((MLA_HEADS // 2, LANES, tq), F32)],
        compiler_params=pltpu.CompilerParams(dimension_semantics=("arbitrary",) * 2),
        name="mla_attn",
    )(q3, k3, vt)


def _router(x1, wr, br):
    tm = x1.shape[0]
    x_hi = x1.astype(BF16)
    x_lo = (x1 - x_hi.astype(F32)).astype(BF16)
    parts = _dot(jnp.concatenate([x_hi, x_lo], axis=0), wr)
    logits = (parts[:tm, :LANES] + (parts[:tm, LANES:] + parts[tm:, :LANES]) + parts[tm:, LANES:]) + br
    lane = lax.broadcasted_iota(jnp.int32, (tm, LANES), 1).astype(F32)
    big = 1e6
    is_g = (lane >= MOE_EXPERTS) & (lane < MOE_EXPERTS + MOE_GROUPS)
    gl = jnp.where(is_g, logits, NEG)
    gmax = jnp.max(gl, axis=-1, keepdims=True)
    g_sel = jnp.min(jnp.where(is_g & (gl == gmax), lane, big), axis=-1, keepdims=True) - MOE_EXPERTS
    g_w = 1.0 / jnp.sum(jnp.where(is_g, jnp.exp(gl - gmax), 0.0), axis=-1, keepdims=True)
    in_grp = (lane >= g_sel * MOE_EPG) & (lane < (g_sel + 1) * MOE_EPG)
    el = jnp.where(in_grp, logits, NEG)
    emax = jnp.max(el, axis=-1, keepdims=True)
    ee = jnp.where(in_grp, jnp.exp(el - emax), 0.0)
    pe = ee / jnp.sum(ee, axis=-1, keepdims=True)
    p1 = jnp.max(pe, axis=-1, keepdims=True)
    i1 = jnp.min(jnp.where(in_grp & (pe == p1), lane, big), axis=-1, keepdims=True)
    rest = in_grp & (lane != i1)
    pr = jnp.where(rest, pe, -1.0)
    p2 = jnp.max(pr, axis=-1, keepdims=True)
    i2 = jnp.min(jnp.where(rest & (pr == p2), lane, big), axis=-1, keepdims=True)
    tot = p1 + p2
    wf = jnp.where(lane == i1, p1 / tot * g_w, jnp.where(lane == i2, p2 / tot * g_w, 0.0))
    wf = jnp.where(lane == GSEL_LANE, g_sel, wf)
    cnt = jnp.sum(jnp.where(lane == g_sel, 1.0, 0.0), axis=0, keepdims=True)
    return wf, cnt


def _mix_out_kernel(*refs, gmlp):
    if gmlp:
        (x_ref, gu_ref, vn_ref, ws_ref, bs_ref, yb_ref, wo_ref, g_ref, b_ref, wr_ref, br_ref,
         x1_ref, x1b_ref, wf_ref, cnt_ref, ya_ref) = refs
        tm = x_ref.shape[0]
        r = lax.broadcasted_iota(jnp.int32, (GM_CHUNK, GM_CHUNK), 0)
        c = lax.broadcasted_iota(jnp.int32, (GM_CHUNK, GM_CHUNK), 1)
        for g in range(GM_GROUPS):
            ws = jnp.where(r >= c, ws_ref[g], 0.0).astype(BF16)
            bias = bs_ref[:, g:g + 1]
            for ch in range(tm // GM_CHUNK):
                rows = slice(ch * GM_CHUNK, (ch + 1) * GM_CHUNK)
                cols = slice(g * LANES, (g + 1) * LANES)
                s = _dot(ws, vn_ref[rows, cols]) + bias
                ya_ref[rows, cols] = (gu_ref[rows, cols].astype(F32) * s).astype(BF16)
        mix = _dot(ya_ref[...], wo_ref[0:GM_WIDTH, :]) + _dot(yb_ref[...], wo_ref[GM_WIDTH:, :])
    else:
        x_ref, y_ref, wo_ref, g_ref, b_ref, wr_ref, br_ref, x1_ref, x1b_ref, wf_ref, cnt_ref = refs
        mix = _dot(y_ref[...], wo_ref[...])
    x1 = _layer_norm(DN_ALPHA * x_ref[...] + mix, g_ref[...], b_ref[...])
    x1_ref[...] = x1
    x1b_ref[...] = x1.astype(BF16)
    wf, cnt = _router(x1, wr_ref[...], br_ref[...])
    wf_ref[...] = wf
    cnt_ref[0] = jnp.broadcast_to(cnt, cnt_ref.shape[1:])


def _mix_out(x2, ys, wo, g, b, wr, br, gm=None):
    tm = TM_PROJ
    row = lambda n: pl.BlockSpec((tm, n), lambda i: (i, 0))
    full = lambda a: pl.BlockSpec(a.shape, lambda i: (0,) * a.ndim)
    if gm is not None:
        gu, vn, ws, bs = gm
        args = (x2, gu, vn, ws, bs, ys, wo, g, b, wr, br)
        in_specs = [row(D_MODEL), row(512), row(512), full(ws), full(bs), row(512), full(wo),
                    full(g), full(b), full(wr), full(br)]
        scratch = [pltpu.VMEM((tm, GM_WIDTH), BF16)]
    else:
        args = (x2, ys, wo, g, b, wr, br)
        in_specs = [row(D_MODEL), row(C_MIX), full(wo), full(g), full(b), full(wr), full(br)]
        scratch = []
    return pl.pallas_call(
        functools.partial(_mix_out_kernel, gmlp=gm is not None),
        grid=(TOKENS // tm,),
        in_specs=in_specs,
        out_specs=[row(D_MODEL), row(D_MODEL), row(LANES), pl.BlockSpec((1, 8, LANES), lambda i: (i, 0, 0))],
        out_shape=[jax.ShapeDtypeStruct((TOKENS, D_MODEL), F32), jax.ShapeDtypeStruct((TOKENS, D_MODEL), BF16),
                   jax.ShapeDtypeStruct((TOKENS, LANES), F32),
                   jax.ShapeDtypeStruct((TOKENS // tm, 8, LANES), F32)],
        scratch_shapes=scratch,
        compiler_params=pltpu.CompilerParams(dimension_semantics=("arbitrary",)),
        name="mix_out_gmlp" if gm is not None else "mix_out",
    )(*args)


def _group_dest(wf, ltri_ref, ustr_ref):
    tm = wf.shape[0]
    lane = lax.broadcasted_iota(jnp.int32, (tm, LANES), 1).astype(F32)
    onehot = jnp.where(lane == wf[:, GSEL_LANE:GSEL_LANE + 1], 1.0, 0.0)
    before = _dot(ltri_ref[...], onehot.astype(BF16))
    cnt = jnp.sum(onehot, axis=0, keepdims=True)
    gran = jnp.floor((cnt + (GRAN - 1)) * (1.0 / GRAN))
    start = _dot(jnp.broadcast_to(gran, (8, LANES)).astype(BF16), ustr_ref[...])[0:1]
    return jnp.sum(onehot * (GRAN * start + before), axis=-1, keepdims=True)


def _granule_copies(tbl_ref, tile, vmem_bufs, hbm_refs, sems, to_hbm, act):
    for g in range(MOE_GROUPS):
        n = tbl_ref[tile * TBL_W + g]
        loc = tbl_ref[tile * TBL_W + MOE_GROUPS + g]
        glb = tbl_ref[tile * TBL_W + 2 * MOE_GROUPS + g]

        def body(k, c, loc=loc, glb=glb):
            lo = pl.multiple_of((loc + k) * GRAN, GRAN)
            hi = pl.multiple_of((glb + k) * GRAN, GRAN)
            for idx, (vb, hb) in enumerate(zip(vmem_bufs, hbm_refs)):
                v_sl, h_sl = vb.at[pl.ds(lo, GRAN)], hb.at[pl.ds(hi, GRAN)]
                src, dst = (v_sl, h_sl) if to_hbm else (h_sl, v_sl)
                act(pltpu.make_async_copy(src, dst, sems[idx]))
            return c

        lax.fori_loop(0, n, body, 0)


def _zero_fill_copies(tbl_ref, zero_bufs, hbm_refs, sems, act):
    tail = (TOKENS // TM_PROJ) * TBL_W
    for n in range(MOE_GROUPS + 1):
        first = tbl_ref[tail + 2 * n]

        def body(k, c, first=first):
            hi = pl.multiple_of((first + k) * GRAN, GRAN)
            for idx, (zb, hb) in enumerate(zip(zero_bufs, hbm_refs)):
                act(pltpu.make_async_copy(zb, hb.at[pl.ds(hi, GRAN)], sems[idx]))
            return c

        lax.fori_loop(0, tbl_ref[tail + 2 * n + 1], body, 0)


def _dispatch_kernel(tbl_ref, xb_ref, wf_ref, ltri_ref, ustr_ref, xs_out, ws_out, xbuf, wbuf, zx, zw, sems):
    tile = pl.program_id(0)
    slot = tile % 2
    tm = xb_ref.shape[0]
    wf = wf_ref[...]
    dest = _group_dest(wf, ltri_ref, ustr_ref)
    dest_row = jnp.broadcast_to(dest, (tm, LANES)).T[0:1]
    r = lax.broadcasted_iota(jnp.int32, (ROWS_LOCAL, tm), 0).astype(F32)
    perm = jnp.where(r == dest_row, 1.0, 0.0).astype(BF16)
    xbuf[slot] = _dot(perm, xb_ref[...]).astype(BF16)
    hi = wf.astype(BF16)
    r1 = wf - hi.astype(F32)
    mid = r1.astype(BF16)
    lo = (r1 - mid.astype(F32)).astype(BF16)
    pieces = _dot(perm, jnp.concatenate([hi, mid, lo], axis=1))
    wbuf[slot] = pieces[:, :LANES] + pieces[:, LANES:2 * LANES] + pieces[:, 2 * LANES:]

    def copies(t, s, act):
        _granule_copies(tbl_ref, t, (xbuf.at[s], wbuf.at[s]), (xs_out, ws_out), (sems.at[s, 0], sems.at[s, 1]),
                        True, act)

    copies(tile, slot, lambda c: c.start())

    @pl.when(tile > 0)
    def _():
        copies(tile - 1, 1 - slot, lambda c: c.wait())

    @pl.when(tile == pl.num_programs(0) - 1)
    def _():
        zx[...] = jnp.zeros_like(zx)
        zw[...] = jnp.zeros_like(zw)
        fill = functools.partial(_zero_fill_copies, tbl_ref, (zx, zw), (xs_out, ws_out), (sems.at[2, 0], sems.at[2, 1]))
        fill(lambda c: c.start())
        copies(tile, slot, lambda c: c.wait())
        fill(lambda c: c.wait())


def _dispatch(tbl, x1b, wf, ltri, ustr):
    tm = TM_PROJ
    row = lambda n: pl.BlockSpec((tm, n), lambda i, t: (i, 0))
    full = lambda a: pl.BlockSpec(a.shape, lambda i, t: (0,) * a.ndim)
    anyspace = pl.BlockSpec(memory_space=pl.ANY)
    return pl.pallas_call(
        _dispatch_kernel,
        grid_spec=pltpu.PrefetchScalarGridSpec(
            num_scalar_prefetch=1, grid=(TOKENS // tm,),
            in_specs=[row(D_MODEL), row(LANES), full(ltri), full(ustr)],
            out_specs=[anyspace, anyspace],
            scratch_shapes=[pltpu.VMEM((2, ROWS_LOCAL, D_MODEL), BF16), pltpu.VMEM((2, ROWS_LOCAL, LANES), F32),
                            pltpu.VMEM((GRAN, D_MODEL), BF16), pltpu.VMEM((GRAN, LANES), F32),
                            pltpu.SemaphoreType.DMA((3, 2))]),
        out_shape=[jax.ShapeDtypeStruct((ROWS_SORTED, D_MODEL), BF16),
                   jax.ShapeDtypeStruct((ROWS_SORTED, LANES), F32)],
        compiler_params=pltpu.CompilerParams(dimension_semantics=("arbitrary",)),
        name="moe_dispatch",
    )(tbl, x1b, wf, ltri, ustr)


def _experts_kernel(gid_ref, valid_ref, xs_ref, ws_ref, wg_ref, wu_ref, wd_ref, y_ref):
    i = pl.program_id(0)

    @pl.when(valid_ref[i] == 1)
    def _():
        x = xs_ref[...]
        gates = [_dot(x, wg_ref[e]) for e in range(MOE_EPG)]
        ups = [_dot(x, wu_ref[e]) for e in range(MOE_EPG)]
        ws = ws_ref[...]
        lane = lax.broadcasted_iota(jnp.int32, ws.shape, 1)
        hidden = []
        for e in range(MOE_EPG):
            w_tok = jnp.sum(jnp.where(lane == gid_ref[i] * MOE_EPG + e, ws, 0.0), axis=-1, keepdims=True)
            hidden.append((gates[e] * jax.nn.sigmoid(gates[e]) * ups[e] * w_tok).astype(BF16))
        wd = wd_ref[...].reshape(MOE_EPG * MOE_HIDDEN, D_MODEL)
        y_ref[...] = _dot(jnp.concatenate(hidden, axis=1), wd).astype(BF16)

    @pl.when(valid_ref[i] == 0)
    def _():
        y_ref[...] = jnp.zeros_like(y_ref)


def _experts(gid, valid, xs, ws, wg, wu, wd):
    tm = TM_MOE
    row = lambda n: pl.BlockSpec((tm, n), lambda i, gid, valid: (i, 0))
    wspec = lambda a, b: pl.BlockSpec((MOE_EPG, a, b), lambda i, gid, valid: (gid[i], 0, 0))
    return pl.pallas_call(
        _experts_kernel,
        grid_spec=pltpu.PrefetchScalarGridSpec(
            num_scalar_prefetch=2, grid=(ROWS_SORTED // tm,),
            in_specs=[row(D_MODEL), row(LANES), wspec(D_MODEL, MOE_HIDDEN), wspec(D_MODEL, MOE_HIDDEN),
                      wspec(MOE_HIDDEN, D_MODEL)],
            out_specs=row(D_MODEL)),
        out_shape=jax.ShapeDtypeStruct((ROWS_SORTED, D_MODEL), BF16),
        compiler_params=pltpu.CompilerParams(dimension_semantics=("arbitrary",),
                                             vmem_limit_bytes=EXPERTS_VMEM_BYTES),
        name="moe_experts",
    )(gid, valid, xs, ws, wg, wu, wd)


def _combine_kernel(tbl_ref, wf_ref, x1_ref, ltri_ref, ustr_ref, g_ref, b_ref, y_hbm, o_ref, ybuf, sems):
    tile = pl.program_id(0)
    slot = tile % 2
    tm = x1_ref.shape[0]

    def copies(t, s, act):
        _granule_copies(tbl_ref, t, (ybuf.at[s],), (y_hbm,), (sems.at[s],), False, act)

    def fetch(t, s):
        ybuf[s] = jnp.zeros(ybuf.shape[1:], ybuf.dtype)
        copies(t, s, lambda c: c.start())

    @pl.when(tile == 0)
    def _():
        fetch(tile, slot)

    @pl.when(tile + 1 < pl.num_programs(0))
    def _():
        fetch(tile + 1, 1 - slot)

    dest = _group_dest(wf_ref[...], ltri_ref, ustr_ref)
    c = lax.broadcasted_iota(jnp.int32, (tm, ROWS_LOCAL), 1).astype(F32)
    unperm = jnp.where(c == dest, 1.0, 0.0).astype(BF16)
    copies(tile, slot, lambda c: c.wait())
    ffn = _dot(unperm, ybuf[slot])
    o_ref[...] = _layer_norm(DN_ALPHA * x1_ref[...] + ffn, g_ref[...], b_ref[...])


def _combine(tbl, wf, x1, ltri, ustr, g, b, y):
    tm = TM_PROJ
    row = lambda n: pl.BlockSpec((tm, n), lambda i, t: (i, 0))
    full = lambda a: pl.BlockSpec(a.shape, lambda i, t: (0,) * a.ndim)
    return pl.pallas_call(
        _combine_kernel,
        grid_spec=pltpu.PrefetchScalarGridSpec(
            num_scalar_prefetch=1, grid=(TOKENS // tm,),
            in_specs=[row(LANES), row(D_MODEL), full(ltri), full(ustr), full(g), full(b),
                      pl.BlockSpec(memory_space=pl.ANY)],
            out_specs=row(D_MODEL),
            scratch_shapes=[pltpu.VMEM((2, ROWS_LOCAL, D_MODEL), BF16), pltpu.SemaphoreType.DMA((2,))]),
        out_shape=jax.ShapeDtypeStruct((TOKENS, D_MODEL), F32),
        compiler_params=pltpu.CompilerParams(dimension_semantics=("arbitrary",)),
        name="moe_combine",
    )(tbl, wf, x1, ltri, ustr, g, b, y)


def _routing_tables(cnt):
    n_tiles = cnt.shape[0]
    c = cnt[:, 0, :MOE_GROUPS].astype(jnp.int32)
    gran = (c + GRAN - 1) // GRAN
    local = jnp.cumsum(gran, axis=1) - gran
    per_tile = TM_MOE // GRAN
    tiles_g = (jnp.sum(gran, axis=0) + per_tile - 1) // per_tile
    ends = jnp.cumsum(tiles_g)
    base = (ends - tiles_g) * per_tile
    glob = base[None, :] + jnp.cumsum(gran, axis=0) - gran
    used = jnp.sum(gran, axis=0)
    pad_first = jnp.concatenate([base + used, ends[-1:] * per_tile])
    pad_count = jnp.concatenate([tiles_g * per_tile - used, ROWS_SORTED // GRAN - ends[-1:] * per_tile])
    tail = jnp.stack([pad_first, pad_count], axis=1).reshape(-1)
    tbl = jnp.concatenate([jnp.concatenate([gran, local, glob], axis=1).reshape(n_tiles * TBL_W), tail])
    idx = jnp.arange(ROWS_SORTED // TM_MOE)
    gid = jnp.minimum(jnp.sum(idx[:, None] >= ends[None, :], axis=1), MOE_GROUPS - 1).astype(jnp.int32)
    valid = (idx < ends[-1]).astype(jnp.int32)
    return tbl, gid, valid


def _sort_tables():
    t = jnp.arange(TM_PROJ)
    ltri = (t[None, :] < t[:, None]).astype(BF16)
    l = jnp.arange(LANES)
    ustr = (l[:, None] < l[None, :]).astype(BF16)
    return ltri, ustr


def _dup_halves(t):
    lane = lax.broadcasted_iota(jnp.int32, t.shape, 1)
    r = pltpu.roll(t, HALF, 1)
    return jnp.where(lane < HALF, t, r), jnp.where(lane < HALF, r, t)


def _nsa_in_kernel(x_ref, w_ref, gb_ref, q_ref, kc_ref, vc_ref, ks_ref, vst_ref, kw_ref, vwt_ref, gate_ref):
    h = _dot(x_ref[...].astype(BF16), w_ref[...])
    q_ref[...] = (h[:, 0:C_MIX] * (NSA_SCALE * LOG2E)).astype(BF16)
    kc_ref[...] = h[:, 1024:1152].astype(BF16)
    vc_ref[...] = h[:, 1152:1280].astype(BF16)
    tm = h.shape[0]
    lane = lax.broadcasted_iota(jnp.int32, (tm, LANES), 1)
    for g, d in enumerate(_dup_halves(h[:, 1536:1664])):
        kw_ref[:, g * LANES:(g + 1) * LANES] = d.astype(BF16)
    pos = (pl.program_id(0) * tm) % SEQ + lax.broadcasted_iota(jnp.int32, (tm, LANES), 0)
    block_onehot = jnp.where(lane == pos // NSA_SEL_LEN, 1.0, 0.0).astype(BF16)
    for g, d in enumerate(_dup_halves(h[:, 1280:1408])):
        ks_ref[:, 2 * g * LANES:(2 * g + 1) * LANES] = d.astype(BF16)
        ks_ref[:, (2 * g + 1) * LANES:(2 * g + 2) * LANES] = block_onehot
    for idx, ref in ((1, vst_ref), (3, vwt_ref)):
        tail = jnp.where(lane == HALF, 1.0, 0.0)
        _store_transposed(ref, [jnp.where(lane < HALF, d, tail)
                                for d in _dup_halves(h[:, 1280 + idx * LANES:1280 + (idx + 1) * LANES])])
    for g in range(NSA_GROUPS):
        gate_ref[g] = jax.nn.sigmoid(h[:, 1792 + g * LANES:1792 + (g + 1) * LANES] + gb_ref[g])


def _nsa_in(x2, w, gb):
    tm = TM_PROJ
    row = lambda n: pl.BlockSpec((tm, n), lambda i: (i, 0))
    full = lambda a: pl.BlockSpec(a.shape, lambda i: (0,) * a.ndim)
    sd = jax.ShapeDtypeStruct
    vt_spec = lambda w: pl.BlockSpec((NSA_GROUPS, tm // w, LANES, w), lambda i: (0, i, 0, 0))
    vt_shape = lambda w: sd((NSA_GROUPS, TOKENS // w, LANES, w), BF16)
    return pl.pallas_call(
        _nsa_in_kernel,
        grid=(TOKENS // tm,),
        in_specs=[row(D_MODEL), full(w), full(gb)],
        out_specs=[row(C_MIX), row(LANES), row(LANES), row(4 * LANES), vt_spec(TK_ATT), row(2 * LANES),
                   vt_spec(WIN_SUB), pl.BlockSpec((NSA_GROUPS, tm, LANES), lambda i: (0, i, 0))],
        out_shape=[sd((TOKENS, C_MIX), BF16), sd((TOKENS, LANES), BF16), sd((TOKENS, LANES), BF16),
                   sd((TOKENS, 4 * LANES), BF16), vt_shape(TK_ATT), sd((TOKENS, 2 * LANES), BF16), vt_shape(WIN_SUB),
                   sd((NSA_GROUPS, TOKENS, LANES), F32)],
        compiler_params=pltpu.CompilerParams(dimension_semantics=("arbitrary",)),
        name="nsa_in",
    )(x2, w, gb)


def _compress_kernel(kc_ref, vc_ref, pk_ref, pv_ref, wk1_ref, wv1_ref, wk2_ref, wv2_ref, ko_ref, vo_ref):
    for a_ref, p_ref, w1_ref, w2_ref, o_ref in ((kc_ref, pk_ref, wk1_ref, wk2_ref, ko_ref),
                                                (vc_ref, pv_ref, wv1_ref, wv2_ref, vo_ref)):
        a = a_ref[0].astype(F32)
        a0 = (a + p_ref[0]).astype(BF16)
        a1 = (a + p_ref[1]).astype(BF16)
        outs = []
        for g in range(NSA_GROUPS):
            first = _dot(a0, w1_ref[g])
            second = _dot(a1, w1_ref[NSA_GROUPS + g])
            hid = first + pltpu.roll(second, NSA_NCMP - 1, 0)
            outs.append(_dot(_gelu(hid).astype(BF16), w2_ref[...]))
        if o_ref is ko_ref:
            o_ref[0] = jnp.concatenate(outs, axis=1).astype(BF16)
        else:
            for g in range(NSA_GROUPS):
                o_ref[0, g] = outs[g].T.astype(BF16)


def _compress(kc_r, vc_r, pk, pv, wk1, wv1, wk2, wv2):
    blk = pl.BlockSpec((1, NSA_NCMP, NSA_CMP_STRIDE * LANES), lambda b: (b, 0, 0))
    full = lambda a: pl.BlockSpec(a.shape, lambda b: (0,) * a.ndim)
    sd = jax.ShapeDtypeStruct
    return pl.pallas_call(
        _compress_kernel,
        grid=(BATCH,),
        in_specs=[blk, blk, full(pk), full(pv), full(wk1), full(wv1), full(wk2), full(wv2)],
        out_specs=[pl.BlockSpec((1, NSA_NCMP, 2 * LANES), lambda b: (b, 0, 0)),
                   pl.BlockSpec((1, NSA_GROUPS, LANES, NSA_NCMP), lambda b: (b, 0, 0, 0))],
        out_shape=[sd((BATCH, NSA_NCMP, 2 * LANES), BF16), sd((BATCH, NSA_GROUPS, LANES, NSA_NCMP), BF16)],
        compiler_params=pltpu.CompilerParams(dimension_semantics=("arbitrary",)),
        name="nsa_compress",
    )(kc_r, vc_r, pk, pv, wk1, wv1, wk2, wv2)


def _nsa_attn_kernel(q_ref, kc_ref, vct_ref, ks_ref, vst_ref, kw_ref, vwt_ref, gate_ref, cover_ref,
                     o_ref, m_ref, acc_ref, sa_ref, sb_ref):
    tq, tk, hpg = TQ_NSA, TK_ATT, NSA_HPG
    qi = pl.program_id(2)
    q0 = qi * tq
    lane = lax.broadcasted_iota(jnp.int32, (tq, LANES), 1)
    t_tok = q0 + lax.broadcasted_iota(jnp.int32, (1, tq), 1)
    head = lambda x, i: x[:, i * tq:(i + 1) * tq]
    heads = range(hpg)

    parts = []
    for p in range(hpg // 2):
        qp = q_ref[0, :, p * LANES:(p + 1) * LANES]
        zero = jnp.zeros_like(qp)
        parts.append(jnp.where(lane < HALF, qp, zero))
        parts.append(jnp.where(lane < HALF, zero, qp))
    qs = jnp.concatenate(parts, axis=0)

    s_c = _dot_nt(kc_ref[0], qs)
    n_sub = lax.broadcasted_iota(jnp.int32, (NSA_NCMP, 1), 0)
    vis = t_tok >= n_sub * NSA_CMP_STRIDE + (NSA_CMP_LEN - 1)
    sees_any = t_tok >= NSA_CMP_LEN - 1
    p_sum = jnp.zeros((NSA_NCMP, tq), F32)
    p_c = []
    for i in heads:
        sm = jnp.where(vis, head(s_c, i), NEG)
        e = jnp.exp2(sm - jnp.max(sm, axis=0, keepdims=True))
        p = e * jnp.where(sees_any, 1.0 / jnp.sum(e, axis=0, keepdims=True), 0.0)
        p_sum = p_sum + p
        p_c.append(p.astype(BF16))
    o_c = _dot(vct_ref[0, 0], jnp.concatenate(p_c, axis=1))

    imp = jnp.dot(cover_ref[...], p_sum, preferred_element_type=F32,
                  precision=lax.Precision.HIGHEST)[0:NSA_NSEL]
    jj = lax.broadcasted_iota(jnp.int32, (NSA_NSEL, 1), 0)
    tb = t_tok // NSA_SEL_LEN
    forced = (jj == 0) | (jj == tb) | (jj == tb - 1)
    score = jnp.where(forced, NSA_FORCE, jnp.where(jj <= tb, imp, -NSA_FORCE))
    sub = 8
    rows = [score[b * sub:(b + 1) * sub] for b in range(NSA_NSEL // sub)]
    ranks = [jnp.zeros((sub, tq), jnp.int32) for _ in rows]
    j_in = lax.broadcasted_iota(jnp.int32, (sub, 1), 0)
    for i in range(NSA_NSEL):
        si = score[i:i + 1, :]
        for b, blk in enumerate(rows):
            if b < i // sub:
                beats = si > blk
            elif b > i // sub:
                beats = si >= blk
            else:
                beats = (si > blk) | ((si == blk) & (j_in > i % sub))
            ranks[b] = ranks[b] + beats.astype(jnp.int32)
    rank = jnp.concatenate(ranks, axis=0)
    sel_bias = jnp.where(rank < NSA_TOPK, 0.0, NEG)
    bias_rows = jnp.concatenate([sel_bias, jnp.zeros((LANES - NSA_NSEL, tq), F32)], axis=0).T.astype(BF16)
    qs_sel = jnp.concatenate([qs, jnp.concatenate([bias_rows] * hpg, axis=0)], axis=1)

    m_ref[...] = jnp.full(m_ref.shape, NEG, F32)
    acc_ref[...] = jnp.zeros(acc_ref.shape, F32)
    k_sub = lax.broadcasted_iota(jnp.int32, (tk, 1), 0)
    diag = q0 // tk
    causal = diag * tk + k_sub <= t_tok
    per_dot = max(1, 2 * LANES // tq)

    def scores_into(ref, kt):
        ref[...] = _dot_nt(ks_ref[0, pl.ds(pl.multiple_of(kt * tk, tk), tk), :], qs_sel)

    def process(ref, kt, diagonal):
        s = ref[...]
        vt = vst_ref[0, kt]
        for i0 in range(0, hpg, per_dot):
            cols = slice(i0 * tq, (i0 + per_dot) * tq)
            probs, alphas = [], []
            for i in range(i0, i0 + per_dot):
                for part in range(tq // LANES):
                    sub = slice(part * LANES, (part + 1) * LANES)
                    sm = s[:, i * tq + part * LANES:i * tq + (part + 1) * LANES]
                    if diagonal:
                        sm = jnp.where(causal[:, sub], sm, NEG)
                    m_old = m_ref[i, :, sub]
                    m_new = jnp.maximum(m_old, jnp.max(sm, axis=0, keepdims=True))
                    m_ref[i, :, sub] = m_new
                    probs.append(jnp.exp2((sm - m_new).astype(BF16)))
                    alphas.append(jnp.exp2(m_old - m_new))
            pv = _dot(vt, jnp.concatenate(probs, axis=1))
            acc_ref[:, cols] = jnp.concatenate(alphas, axis=1) * acc_ref[:, cols] + pv

    odd = diag % 2
    scores_into(sa_ref, 0)

    @pl.when(odd == 1)
    def _():
        process(sa_ref, 0, False)
        scores_into(sa_ref, 1)

    def pair(k, c):
        t0 = odd + 2 * k
        scores_into(sb_ref, t0 + 1)
        process(sa_ref, t0, False)
        scores_into(sa_ref, t0 + 2)
        process(sb_ref, t0 + 1, False)
        return c

    lax.fori_loop(0, diag // 2, pair, 0)
    process(sa_ref, diag, True)

    kb_sub = lax.broadcasted_iota(jnp.int32, (WIN_BAND, 1), 0)
    o_w_parts = [[] for _ in heads]
    for part in range(tq // WIN_SUB):
        q_lo = q0 + part * WIN_SUB
        start = pl.multiple_of(jnp.maximum(q_lo - NSA_WINDOW, 0), WIN_SUB)
        qs_part = jnp.concatenate([qs[i * tq + part * WIN_SUB:i * tq + (part + 1) * WIN_SUB] for i in heads], axis=0)
        s_w = _dot_nt(kw_ref[0, pl.ds(start, WIN_BAND), :], qs_part)
        kpos = start + kb_sub
        t_part = q_lo + lax.broadcasted_iota(jnp.int32, (1, WIN_SUB), 1)
        in_win = (kpos <= t_part) & (kpos > t_part - NSA_WINDOW)
        vw_band = jnp.concatenate([vwt_ref[0, start // WIN_SUB + c] for c in range(WIN_BAND // WIN_SUB)], axis=1)
        for i0 in range(0, hpg, 2):
            e_w = []
            for i in (i0, i0 + 1):
                sm = jnp.where(in_win, s_w[:, i * WIN_SUB:(i + 1) * WIN_SUB], NEG)
                e_w.append(jnp.exp2((sm - jnp.max(sm, axis=0, keepdims=True)).astype(BF16)))
            o_pair = _dot(vw_band, jnp.concatenate(e_w, axis=1))
            o_w_parts[i0].append(o_pair[:, :WIN_SUB])
            o_w_parts[i0 + 1].append(o_pair[:, WIN_SUB:])
    o_w = [jnp.concatenate(parts, axis=1) for parts in o_w_parts]

    gt = gate_ref[0, 0].T
    o_s = acc_ref[...]
    outs = []
    for i in heads:
        c_i, s_i, w_i = head(o_c, i), head(o_s, i), o_w[i]
        outs.append(gt[i:i + 1] * c_i[:HALF]
                    + gt[hpg + i:hpg + i + 1] / s_i[HALF:HALF + 1] * s_i[:HALF]
                    + gt[2 * hpg + i:2 * hpg + i + 1] / w_i[HALF:HALF + 1] * w_i[:HALF])
    for p in range(hpg // 2):
        pair = jnp.concatenate([outs[2 * p], outs[2 * p + 1]], axis=0)
        o_ref[0, :, p * LANES:(p + 1) * LANES] = pair.T.astype(BF16)


def _nsa_attn(q3, kc2, vct, ks3, vst, kw3, vwt, gates4, cover_t):
    tq = TQ_NSA
    half_w = NSA_HPG * NSA_DH
    kv = pl.BlockSpec((1, SEQ, LANES), lambda b, g, i: (b, 0, g))
    vt = lambda w: pl.BlockSpec((1, SEQ // w, LANES, w), lambda b, g, i: (g, b, 0, 0))
    full = lambda a: pl.BlockSpec(a.shape, lambda b, g, i: (0,) * a.ndim)
    return pl.pallas_call(
        _nsa_attn_kernel,
        grid=(BATCH, NSA_GROUPS, SEQ // tq),
        in_specs=[pl.BlockSpec((1, tq, half_w), lambda b, g, i: (b, i, g)),
                  pl.BlockSpec((1, NSA_NCMP, LANES), lambda b, g, i: (b, 0, g)),
                  pl.BlockSpec((1, 1, LANES, NSA_NCMP), lambda b, g, i: (b, g, 0, 0)),
                  pl.BlockSpec((1, SEQ, 2 * LANES), lambda b, g, i: (b, 0, g)), vt(TK_ATT), kv, vt(WIN_SUB),
                  pl.BlockSpec((1, 1, tq, LANES), lambda b, g, i: (g, b, i, 0)),
                  full(cover_t)],
        out_specs=pl.BlockSpec((1, tq, half_w), lambda b, g, i: (b, i, g)),
        out_shape=jax.ShapeDtypeStruct((BATCH, SEQ, C_MIX), BF16),
        scratch_shapes=[pltpu.VMEM((NSA_HPG, 1, tq), F32),
                        pltpu.VMEM((LANES, NSA_HPG * tq), F32),
                        pltpu.VMEM((TK_ATT, NSA_HPG * tq), F32), pltpu.VMEM((TK_ATT, NSA_HPG * tq), F32)],
        compiler_params=pltpu.CompilerParams(dimension_semantics=("arbitrary",) * 3),
        name="nsa_attn",
    )(q3, kc2, vct, ks3, vst, kw3, vwt, gates4, cover_t)


def _rope_tables():
    half = MLA_ROPE // 2
    freq = jnp.exp(-math.log(ROPE_BASE) * jnp.arange(half, dtype=F32) / half)
    return freq[:, None]


def _swap_halves(w):
    half = w.shape[-1] // 2
    return jnp.concatenate([w[..., half:], w[..., :half]], axis=-1)


def _pad_last(w, n):
    return jnp.pad(w, [(0, 0)] * (w.ndim - 1) + [(0, n - w.shape[-1])])


def _ab_weights(w_in, w_uq, w_uk, w_uv):
    w_kr = w_in[:, 1408:1440]
    place = lambda w: jnp.pad(w, ((0, 0), (MLA_NOPE, LANES - MLA_NOPE - MLA_ROPE)))
    win = jnp.concatenate([w_in[:, :1408], place(w_kr), place(_swap_halves(w_kr))], axis=1).astype(BF16)
    uq = w_uq.reshape(MLA_Q_RANK, MLA_HEADS, MLA_NOPE + MLA_ROPE)
    nope, rp = uq[..., :MLA_NOPE], uq[..., MLA_NOPE:]
    q_pad = _pad_last(jnp.concatenate([nope, rp], -1), LANES).reshape(MLA_Q_RANK, MLA_HEADS * LANES)
    q_sw = _pad_last(jnp.concatenate([jnp.zeros_like(nope), _swap_halves(rp)], -1), LANES)
    wq = jnp.concatenate([q_pad, q_sw.reshape(MLA_Q_RANK, MLA_HEADS * LANES)], axis=1).astype(BF16)
    k_pad = _pad_last(w_uk.reshape(MLA_KV_RANK, MLA_HEADS, MLA_NOPE), LANES).reshape(MLA_KV_RANK, -1)
    wkv = jnp.concatenate([k_pad, w_uv], axis=1).astype(BF16)
    return win, wq, wkv


def _router_weights(w_rg, b_rg, w_re, b_re):
    wr = _pad_last(jnp.concatenate([w_re, w_rg], axis=1), LANES)
    wr_hi = wr.astype(BF16)
    wr_lo = (wr - wr_hi.astype(F32)).astype(BF16)
    br = _pad_last(jnp.concatenate([b_re, b_rg])[None, :], LANES)
    return jnp.concatenate([wr_hi, wr_lo], axis=1), br


def _nsa_in_weights(w_in, gate_b):
    g_cols = w_in[:, C_MIX + 768:].reshape(D_MODEL, 3, NSA_GROUPS, NSA_HPG)
    g_blocks = [_pad_last(g_cols[:, :, g, :].reshape(D_MODEL, 3 * NSA_HPG), LANES) for g in range(NSA_GROUPS)]
    w = jnp.concatenate([w_in[:, :C_MIX + 768]] + g_blocks, axis=1).astype(BF16)
    gb = gate_b.reshape(3, NSA_GROUPS, NSA_HPG)
    gb = jnp.stack([_pad_last(gb[:, g, :].reshape(1, 3 * NSA_HPG), LANES) for g in range(NSA_GROUPS)])
    return w, gb


def _compress_weights(pos, w1, w2):
    w1r = w1.reshape(2, NSA_CMP_STRIDE, NSA_DH, NSA_CMP_HIDDEN)
    zero = jnp.zeros_like(w1r)
    per_g = []
    for g in range(NSA_GROUPS):
        parts = [w1r if gg == g else zero for gg in range(NSA_GROUPS)]
        per_g.append(jnp.stack(parts, axis=2).reshape(2, NSA_CMP_STRIDE * LANES, NSA_CMP_HIDDEN))
    w1x = jnp.stack(per_g, axis=1).reshape(2 * NSA_GROUPS, NSA_CMP_STRIDE * LANES, NSA_CMP_HIDDEN)
    posr = pos.reshape(2, NSA_CMP_STRIDE, 1, NSA_DH)
    posx = jnp.broadcast_to(posr, (2, NSA_CMP_STRIDE, NSA_GROUPS, NSA_DH)).reshape(2, 1, NSA_CMP_STRIDE * LANES)
    w2x = jnp.concatenate([w2, w2], axis=1)
    return posx, w1x.astype(BF16), w2x.astype(BF16)


def _selection_tables():
    n = jnp.arange(LANES)[:, None]
    j = jnp.arange(LANES)[None, :]
    c0 = n * NSA_CMP_STRIDE
    s0 = j * NSA_SEL_LEN
    cover = ((c0 < s0 + NSA_SEL_LEN) & (c0 + NSA_CMP_LEN > s0) & (n < NSA_NCMP - 1) & (j < NSA_NSEL))
    return jnp.transpose(cover).astype(F32)


def kernel(x, positions, ab_w_in, ab_gm_ln_g, ab_gm_ln_b, ab_gm_ws, ab_gm_bs, ab_mla_q_norm,
           ab_mla_kv_norm, ab_mla_w_uq, ab_mla_w_uk, ab_mla_w_uv, ab_w_o, c_w_in, c_cmp_pos, c_w_ck1,
           c_w_ck2, c_w_cv1, c_w_cv2, c_gate_b, c_w_o, moe_w_rg, moe_b_rg, moe_w_re, moe_b_re,
           moe_w_gate, moe_w_up, moe_w_down, ln1_g, ln1_b, ln2_g, ln2_b):
    x2 = x.reshape(TOKENS, D_MODEL)
    pos3 = positions.reshape(TOKENS // TM_PROJ, 1, TM_PROJ)
    vec = lambda a: a[None, :]

    ltri, ustr = _sort_tables()

    def moe_layer(layer, x1b, x1, wf, cnt):
        tbl, gid, valid = _routing_tables(cnt)
        xs, ws = _dispatch(tbl, x1b, wf, ltri, ustr)
        y = _experts(gid, valid, xs, ws, moe_w_gate[layer].astype(BF16), moe_w_up[layer].astype(BF16),
                     moe_w_down[layer].astype(BF16))
        return _combine(tbl, wf, x1, ltri, ustr, vec(ln2_g[layer]), vec(ln2_b[layer]), y)

    win, wq, wkv = _ab_weights(ab_w_in[0], ab_mla_w_uq[0], ab_mla_w_uk[0], ab_mla_w_uv[0])
    gu, vn, q, k, vt = _ab_in(x2, pos3, win, vec(ab_gm_ln_g[0]), vec(ab_gm_ln_b[0]), vec(ab_mla_q_norm[0]),
                              vec(ab_mla_kv_norm[0]), wq, wkv, _rope_tables())
    yb = _mla_attn(q.reshape(BATCH, SEQ, -1), k.reshape(BATCH, SEQ, -1), vt)
    wr, br = _router_weights(moe_w_rg[0], moe_b_rg[0], moe_w_re[0], moe_b_re[0])
    x1, x1b, wf, cnt = _mix_out(x2, yb.reshape(TOKENS, -1), ab_w_o[0].astype(BF16), vec(ln1_g[0]),
                                vec(ln1_b[0]), wr, br, gm=(gu, vn, ab_gm_ws[0], jnp.transpose(ab_gm_bs[0])))
    x2 = moe_layer(0, x1b, x1, wf, cnt)

    w_nsa, gb = _nsa_in_weights(c_w_in[0], c_gate_b[0])
    q, kc, vc, ks, vst, kw, vwt, gates = _nsa_in(x2, w_nsa, gb)
    pk, wk1, wk2 = _compress_weights(c_cmp_pos[0, 0], c_w_ck1[0], c_w_ck2[0])
    pv, wv1, wv2 = _compress_weights(c_cmp_pos[0, 1], c_w_cv1[0], c_w_cv2[0])
    blocks = lambda a: a.reshape(BATCH, NSA_NCMP, NSA_CMP_STRIDE * LANES)
    kc2, vct = _compress(blocks(kc), blocks(vc), pk, pv, wk1, wv1, wk2, wv2)
    b3 = lambda a: a.reshape(BATCH, SEQ, -1)
    o = _nsa_attn(b3(q), kc2, vct, b3(ks), vst, b3(kw), vwt,
                  gates.reshape(NSA_GROUPS, BATCH, SEQ, LANES), _selection_tables())
    wr, br = _router_weights(moe_w_rg[1], moe_b_rg[1], moe_w_re[1], moe_b_re[1])
    x1, x1b, wf, cnt = _mix_out(x2, o.reshape(TOKENS, -1), c_w_o[0].astype(BF16), vec(ln1_g[1]), vec(ln1_b[1]),
                                wr, br)
    x2 = moe_layer(1, x1b, x1, wf, cnt)
    return x2.reshape(BATCH, SEQ, D_MODEL)
```

```python
import functools
import math

import jax
import jax.numpy as jnp
from jax import lax
from jax.experimental import pallas as pl
from jax.experimental.pallas import tpu as pltpu

F32 = jnp.float32
BF16 = jnp.bfloat16

D_MODEL = 1024
BATCH = 16
SEQ = 2048
TOKENS = BATCH * SEQ
DEPTH = 2
DN_ALPHA = (2.0 * DEPTH) ** 0.25
LN_EPS = 1e-5
NEG = -1e30
LOG2E = math.log2(math.e)
LANES = 128
HALF = LANES // 2

GM_WIDTH = 512
GM_GROUPS = 4
GM_CHUNK = 128

MLA_HEADS = 8
MLA_NOPE = 64
MLA_ROPE = 32
MLA_V = 64
MLA_Q_RANK = 256
MLA_KV_RANK = 128
ROPE_BASE = 10000.0
MLA_SCALE = (MLA_NOPE + MLA_ROPE) ** -0.5

NSA_HEADS = 16
NSA_GROUPS = 2
NSA_HPG = 8
NSA_DH = 64
NSA_CMP_LEN = 32
NSA_CMP_STRIDE = 16
NSA_CMP_HIDDEN = 256
NSA_SEL_LEN = 64
NSA_TOPK = 8
NSA_WINDOW = 512
NSA_NSEL = SEQ // NSA_SEL_LEN
NSA_NCMP = SEQ // NSA_CMP_STRIDE
NSA_FORCE = 1e4
NSA_SCALE = NSA_DH ** -0.5
C_MIX = NSA_HEADS * NSA_DH

MOE_GROUPS = 4
MOE_EPG = 8
MOE_EXPERTS = 32
MOE_HIDDEN = 256

TM_PROJ = 512
TQ_MLA = 256
TQ_NSA = 256
TK_ATT = 256
WIN_SUB = LANES
WIN_BAND = NSA_WINDOW + WIN_SUB
MLA_VT_ROWS = LANES + 16
TM_MOE = 512
EXPERTS_VMEM_BYTES = (2 * 3 * MOE_EPG * D_MODEL * MOE_HIDDEN * 2 + 2 * 2 * TM_MOE * D_MODEL * 2
                      + 2 * MOE_EPG * TM_MOE * MOE_HIDDEN * 4 + TM_MOE * MOE_EPG * MOE_HIDDEN * 2
                      + 2 * TM_MOE * D_MODEL * 4)

GSEL_LANE = MOE_EXPERTS
GRAN = 16
TBL_W = 3 * MOE_GROUPS
ROWS_LOCAL = 640
ROWS_SORTED = TM_MOE * (TOKENS // TM_MOE + MOE_GROUPS
                        + -(-(TOKENS // TM_PROJ) * MOE_GROUPS * (GRAN - 1) // TM_MOE))


def _dot(a, b):
    return jnp.dot(a, b, preferred_element_type=F32)


def _dot_nt(a, b):
    return lax.dot_general(a, b, (((1,), (1,)), ((), ())), preferred_element_type=F32)


def _gelu(x):
    return 0.5 * x * (1.0 + jnp.tanh(math.sqrt(2.0 / math.pi) * (x + 0.044715 * (x * x * x))))


def _layer_norm(x, g, b):
    mu = jnp.mean(x, axis=-1, keepdims=True)
    xc = x - mu
    var = jnp.mean(xc * xc, axis=-1, keepdims=True)
    return xc * lax.rsqrt(var + LN_EPS) * g + b


def _rms_norm(x, g):
    return x * lax.rsqrt(jnp.mean(x * x, axis=-1, keepdims=True) + LN_EPS) * g


def _store_transposed(ref, blocks):
    extra, width = ref.shape[2] - LANES, ref.shape[3]
    if extra:
        ones_rows = jnp.where(lax.broadcasted_iota(jnp.int32, (extra, width), 0) == 0, 1.0, 0.0)
    for n, blk in enumerate(blocks):
        t = blk.T
        for c in range(t.shape[1] // width):
            chunk = t[:, c * width:(c + 1) * width]
            if extra:
                chunk = jnp.concatenate([chunk, ones_rows], axis=0)
            ref[n, c] = chunk.astype(ref.dtype)


def _ab_in_kernel(x_ref, pos_ref, win_ref, lng_ref, lnb_ref, qg_ref, kvg_ref, wq_ref, wkv_ref,
                  fc_ref, gu_ref, vn_ref, q_ref, k_ref, vt_ref):
    h = _dot(x_ref[...].astype(BF16), win_ref[...])
    gu_ref[...] = _gelu(h[:, 0:512]).astype(BF16)
    vn_ref[...] = _layer_norm(_gelu(h[:, 512:1024]), lng_ref[...], lnb_ref[...]).astype(BF16)

    tm = x_ref.shape[0]
    ang = fc_ref[...] * pos_ref[0].astype(F32)
    cos_t, sin_t = jnp.cos(ang), jnp.sin(ang)
    ones_t, zeros_t = jnp.ones((MLA_NOPE, tm), F32), jnp.zeros((MLA_NOPE, tm), F32)
    pad = LANES - MLA_NOPE - MLA_ROPE
    cc = jnp.concatenate([ones_t, cos_t, cos_t, ones_t[:pad]], axis=0).T
    ss = jnp.concatenate([zeros_t, -sin_t, sin_t, zeros_t[:pad]], axis=0).T

    cqn = _rms_norm(h[:, 1024:1280], qg_ref[...]).astype(BF16)
    qq = _dot(cqn, wq_ref[...])
    for hd in range(MLA_HEADS):
        lo, hi = hd * LANES, (hd + 1) * LANES
        q_ref[:, lo:hi] = ((qq[:, lo:hi] * cc + qq[:, 1024 + lo:1024 + hi] * ss) * (MLA_SCALE * LOG2E)).astype(BF16)

    ckvn = _rms_norm(h[:, 1280:1408], kvg_ref[...]).astype(BF16)
    kv = _dot(ckvn, wkv_ref[...])
    k_rope = h[:, 1408:1536] * cc + h[:, 1536:1664] * ss
    for hd in range(MLA_HEADS):
        lo, hi = hd * LANES, (hd + 1) * LANES
        k_ref[:, lo:hi] = (kv[:, lo:hi] + k_rope).astype(BF16)
    _store_transposed(vt_ref, [kv[:, 1024 + p * LANES:1024 + (p + 1) * LANES] for p in range(MLA_HEADS // 2)])


def _ab_in(x2, pos3, win, lng, lnb, qg, kvg, wq, wkv, fc):
    tm = TM_PROJ
    row = lambda n: pl.BlockSpec((tm, n), lambda i: (i, 0))
    full = lambda a: pl.BlockSpec(a.shape, lambda i: (0,) * a.ndim)
    return pl.pallas_call(
        _ab_in_kernel,
        grid=(TOKENS // tm,),
        in_specs=[row(D_MODEL), pl.BlockSpec((1, 1, tm), lambda i: (i, 0, 0)), full(win), full(lng), full(lnb),
                  full(qg), full(kvg), full(wq), full(wkv), full(fc)],
        out_specs=[row(512), row(512), row(1024), row(1024),
                   pl.BlockSpec((MLA_HEADS // 2, tm // TK_ATT, MLA_VT_ROWS, TK_ATT), lambda i: (0, i, 0, 0))],
        out_shape=[jax.ShapeDtypeStruct((TOKENS, 512), BF16), jax.ShapeDtypeStruct((TOKENS, 512), BF16),
                   jax.ShapeDtypeStruct((TOKENS, 1024), BF16), jax.ShapeDtypeStruct((TOKENS, 1024), BF16),
                   jax.ShapeDtypeStruct((MLA_HEADS // 2, TOKENS // TK_ATT, MLA_VT_ROWS, TK_ATT), BF16)],
        compiler_params=pltpu.CompilerParams(dimension_semantics=("arbitrary",)),
        name="ab_in",
    )(x2, pos3, win, lng, lnb, qg, kvg, wq, wkv, fc)


def _mla_attn_kernel(q_ref, k_ref, vt_ref, o_ref, m_ref, l_ref, acc_ref):
    tq, tk = TQ_MLA, TK_ATT
    qi = pl.program_id(1)
    krow = lax.broadcasted_iota(jnp.int32, (tk, tq), 0)
    qcol = lax.broadcasted_iota(jnp.int32, (tk, tq), 1)
    top = lax.broadcasted_iota(jnp.int32, (LANES, tq), 0) < HALF
    m_ref[...] = jnp.full(m_ref.shape, NEG, F32)
    l_ref[...] = jnp.zeros(l_ref.shape, F32)
    acc_ref[...] = jnp.zeros(acc_ref.shape, F32)

    def tile(j, masked):
        r0 = pl.multiple_of(j * tk, tk)
        scores = [_dot_nt(k_ref[0, pl.ds(r0, tk), h * LANES:(h + 1) * LANES], q_ref[0, :, h * LANES:(h + 1) * LANES])
                  for h in range(MLA_HEADS)]
        for pr in range(MLA_HEADS // 2):
            probs, alphas = [], []
            for h in (2 * pr, 2 * pr + 1):
                s = jnp.where(krow <= qcol, scores[h], NEG) if masked else scores[h]
                m_old = m_ref[h]
                m_new = jnp.maximum(m_old, jnp.max(s, axis=0, keepdims=True))
                alphas.append(jnp.exp2(m_old - m_new))
                probs.append(jnp.exp2(s - m_new).astype(BF16))
                m_ref[h] = m_new
            pv = _dot(vt_ref[pr, j], jnp.concatenate(probs, axis=1))
            for n, hh in enumerate((2 * pr, 2 * pr + 1)):
                l_ref[hh] = alphas[n] * l_ref[hh] + pv[LANES:LANES + 1, n * tq:(n + 1) * tq]
            a = jnp.where(top, alphas[0], alphas[1])
            acc_ref[pr] = a * acc_ref[pr] + jnp.where(top, pv[:LANES, :tq], pv[:LANES, tq:])

    def body(j, c):
        tile(j, False)
        return c

    lax.fori_loop(0, qi, body, 0)
    tile(qi, True)
    for pr in range(MLA_HEADS // 2):
        l = jnp.where(top, l_ref[2 * pr], l_ref[2 * pr + 1])
        o_ref[0, :, pr * LANES:(pr + 1) * LANES] = (acc_ref[pr] / l).T.astype(BF16)


def _mla_attn(q3, k3, vt):
    tq = TQ_MLA
    n_chunks = SEQ // TK_ATT
    return pl.pallas_call(
        _mla_attn_kernel,
        grid=(BATCH, SEQ // tq),
        in_specs=[pl.BlockSpec((1, tq, MLA_HEADS * LANES), lambda b, i: (b, i, 0)),
                  pl.BlockSpec((1, SEQ, MLA_HEADS * LANES), lambda b, i: (b, 0, 0)),
                  pl.BlockSpec((MLA_HEADS // 2, n_chunks, MLA_VT_ROWS, TK_ATT), lambda b, i: (0, b, 0, 0))],
        out_specs=pl.BlockSpec((1, tq, MLA_HEADS * MLA_V), lambda b, i: (b, i, 0)),
        out_shape=jax.ShapeDtypeStruct((BATCH, SEQ, MLA_HEADS * MLA_V), BF16),
        scratch_shapes=[pltpu.VMEM((MLA_HEADS, 1, tq), F32), pltpu.VMEM((MLA_HEADS, 1, tq), F32),
                        pltpu.VMEM((MLA_HEADS // 2, LANES, tq), F32)],
        compiler_params=pltpu.CompilerParams(dimension_semantics=("arbitrary",) * 2),
        name="mla_attn",
    )(q3, k3, vt)


def _router(x1, wr, br):
    tm = x1.shape[0]
    x_hi = x1.astype(BF16)
    x_lo = (x1 - x_hi.astype(F32)).astype(BF16)
    parts = _dot(jnp.concatenate([x_hi, x_lo], axis=0), wr)
    logits = (parts[:tm, :LANES] + (parts[:tm, LANES:] + parts[tm:, :LANES]) + parts[tm:, LANES:]) + br
    lane = lax.broadcasted_iota(jnp.int32, (tm, LANES), 1).astype(F32)
    big = 1e6
    is_g = (lane >= MOE_EXPERTS) & (lane < MOE_EXPERTS + MOE_GROUPS)
    gl = jnp.where(is_g, logits, NEG)
    gmax = jnp.max(gl, axis=-1, keepdims=True)
    g_sel = jnp.min(jnp.where(is_g & (gl == gmax), lane, big), axis=-1, keepdims=True) - MOE_EXPERTS
    g_w = 1.0 / jnp.sum(jnp.where(is_g, jnp.exp(gl - gmax), 0.0), axis=-1, keepdims=True)
    in_grp = (lane >= g_sel * MOE_EPG) & (lane < (g_sel + 1) * MOE_EPG)
    el = jnp.where(in_grp, logits, NEG)
    emax = jnp.max(el, axis=-1, keepdims=True)
    ee = jnp.where(in_grp, jnp.exp(el - emax), 0.0)
    pe = ee / jnp.sum(ee, axis=-1, keepdims=True)
    p1 = jnp.max(pe, axis=-1, keepdims=True)
    i1 = jnp.min(jnp.where(in_grp & (pe == p1), lane, big), axis=-1, keepdims=True)
    rest = in_grp & (lane != i1)
    pr = jnp.where(rest, pe, -1.0)
    p2 = jnp.max(pr, axis=-1, keepdims=True)
    i2 = jnp.min(jnp.where(rest & (pr == p2), lane, big), axis=-1, keepdims=True)
    tot = p1 + p2
    wf = jnp.where(lane == i1, p1 / tot * g_w, jnp.where(lane == i2, p2 / tot * g_w, 0.0))
    wf = jnp.where(lane == GSEL_LANE, g_sel, wf)
    cnt = jnp.sum(jnp.where(lane == g_sel, 1.0, 0.0), axis=0, keepdims=True)
    return wf, cnt


def _mix_out_kernel(*refs, gmlp):
    if gmlp:
        (x_ref, gu_ref, vn_ref, ws_ref, bs_ref, yb_ref, wo_ref, g_ref, b_ref, wr_ref, br_ref,
         x1_ref, x1b_ref, wf_ref, cnt_ref, ya_ref) = refs
        tm = x_ref.shape[0]
        r = lax.broadcasted_iota(jnp.int32, (GM_CHUNK, GM_CHUNK), 0)
        c = lax.broadcasted_iota(jnp.int32, (GM_CHUNK, GM_CHUNK), 1)
        for g in range(GM_GROUPS):
            ws = jnp.where(r >= c, ws_ref[g], 0.0).astype(BF16)
            bias = bs_ref[:, g:g + 1]
            for ch in range(tm // GM_CHUNK):
                rows = slice(ch * GM_CHUNK, (ch + 1) * GM_CHUNK)
                cols = slice(g * LANES, (g + 1) * LANES)
                s = _dot(ws, vn_ref[rows, cols]) + bias
                ya_ref[rows, cols] = (gu_ref[rows, cols].astype(F32) * s).astype(BF16)
        mix = _dot(ya_ref[...], wo_ref[0:GM_WIDTH, :]) + _dot(yb_ref[...], wo_ref[GM_WIDTH:, :])
    else:
        x_ref, y_ref, wo_ref, g_ref, b_ref, wr_ref, br_ref, x1_ref, x1b_ref, wf_ref, cnt_ref = refs
        mix = _dot(y_ref[...], wo_ref[...])
    x1 = _layer_norm(DN_ALPHA * x_ref[...] + mix, g_ref[...], b_ref[...])
    x1_ref[...] = x1
    x1b_ref[...] = x1.astype(BF16)
    wf, cnt = _router(x1, wr_ref[...], br_ref[...])
    wf_ref[...] = wf
    cnt_ref[0] = jnp.broadcast_to(cnt, cnt_ref.shape[1:])


def _mix_out(x2, ys, wo, g, b, wr, br, gm=None):
    tm = TM_PROJ
    row = lambda n: pl.BlockSpec((tm, n), lambda i: (i, 0))
    full = lambda a: pl.BlockSpec(a.shape, lambda i: (0,) * a.ndim)
    if gm is not None:
        gu, vn, ws, bs = gm
        args = (x2, gu, vn, ws, bs, ys, wo, g, b, wr, br)
        in_specs = [row(D_MODEL), row(512), row(512), full(ws), full(bs), row(512), full(wo),
                    full(g), full(b), full(wr), full(br)]
        scratch = [pltpu.VMEM((tm, GM_WIDTH), BF16)]
    else:
        args = (x2, ys, wo, g, b, wr, br)
        in_specs = [row(D_MODEL), row(C_MIX), full(wo), full(g), full(b), full(wr), full(br)]
        scratch = []
    return pl.pallas_call(
        functools.partial(_mix_out_kernel, gmlp=gm is not None),
        grid=(TOKENS // tm,),
        in_specs=in_specs,
        out_specs=[row(D_MODEL), row(D_MODEL), row(LANES), pl.BlockSpec((1, 8, LANES), lambda i: (i, 0, 0))],
        out_shape=[jax.ShapeDtypeStruct((TOKENS, D_MODEL), F32), jax.ShapeDtypeStruct((TOKENS, D_MODEL), BF16),
                   jax.ShapeDtypeStruct((TOKENS, LANES), F32),
                   jax.ShapeDtypeStruct((TOKENS // tm, 8, LANES), F32)],
        scratch_shapes=scratch,
        compiler_params=pltpu.CompilerParams(dimension_semantics=("arbitrary",)),
        name="mix_out_gmlp" if gm is not None else "mix_out",
    )(*args)


def _group_dest(wf, ltri_ref, ustr_ref):
    tm = wf.shape[0]
    lane = lax.broadcasted_iota(jnp.int32, (tm, LANES), 1).astype(F32)
    onehot = jnp.where(lane == wf[:, GSEL_LANE:GSEL_LANE + 1], 1.0, 0.0)
    before = _dot(ltri_ref[...], onehot.astype(BF16))
    cnt = jnp.sum(onehot, axis=0, keepdims=True)
    gran = jnp.floor((cnt + (GRAN - 1)) * (1.0 / GRAN))
    start = _dot(jnp.broadcast_to(gran, (8, LANES)).astype(BF16), ustr_ref[...])[0:1]
    return jnp.sum(onehot * (GRAN * start + before), axis=-1, keepdims=True)


def _granule_copies(tbl_ref, tile, vmem_bufs, hbm_refs, sems, to_hbm, act):
    for g in range(MOE_GROUPS):
        n = tbl_ref[tile * TBL_W + g]
        loc = tbl_ref[tile * TBL_W + MOE_GROUPS + g]
        glb = tbl_ref[tile * TBL_W + 2 * MOE_GROUPS + g]

        def body(k, c, loc=loc, glb=glb):
            lo = pl.multiple_of((loc + k) * GRAN, GRAN)
            hi = pl.multiple_of((glb + k) * GRAN, GRAN)
            for idx, (vb, hb) in enumerate(zip(vmem_bufs, hbm_refs)):
                v_sl, h_sl = vb.at[pl.ds(lo, GRAN)], hb.at[pl.ds(hi, GRAN)]
                src, dst = (v_sl, h_sl) if to_hbm else (h_sl, v_sl)
                act(pltpu.make_async_copy(src, dst, sems[idx]))
            return c

        lax.fori_loop(0, n, body, 0)


def _zero_fill_copies(tbl_ref, zero_bufs, hbm_refs, sems, act):
    tail = (TOKENS // TM_PROJ) * TBL_W
    for n in range(MOE_GROUPS + 1):
        first = tbl_ref[tail + 2 * n]

        def body(k, c, first=first):
            hi = pl.multiple_of((first + k) * GRAN, GRAN)
            for idx, (zb, hb) in enumerate(zip(zero_bufs, hbm_refs)):
                act(pltpu.make_async_copy(zb, hb.at[pl.ds(hi, GRAN)], sems[idx]))
            return c

        lax.fori_loop(0, tbl_ref[tail + 2 * n + 1], body, 0)


def _dispatch_kernel(tbl_ref, xb_ref, wf_ref, ltri_ref, ustr_ref, xs_out, ws_out, xbuf, wbuf, zx, zw, sems):
    tile = pl.program_id(0)
    slot = tile % 2
    tm = xb_ref.shape[0]
    wf = wf_ref[...]
    dest = _group_dest(wf, ltri_ref, ustr_ref)
    dest_row = jnp.broadcast_to(dest, (tm, LANES)).T[0:1]
    r = lax.broadcasted_iota(jnp.int32, (ROWS_LOCAL, tm), 0).astype(F32)
    perm = jnp.where(r == dest_row, 1.0, 0.0).astype(BF16)
    xbuf[slot] = _dot(perm, xb_ref[...]).astype(BF16)
    hi = wf.astype(BF16)
    r1 = wf - hi.astype(F32)
    mid = r1.astype(BF16)
    lo = (r1 - mid.astype(F32)).astype(BF16)
    pieces = _dot(perm, jnp.concatenate([hi, mid, lo], axis=1))
    wbuf[slot] = pieces[:, :LANES] + pieces[:, LANES:2 * LANES] + pieces[:, 2 * LANES:]

    def copies(t, s, act):
        _granule_copies(tbl_ref, t, (xbuf.at[s], wbuf.at[s]), (xs_out, ws_out), (sems.at[s, 0], sems.at[s, 1]),
                        True, act)

    copies(tile, slot, lambda c: c.start())

    @pl.when(tile > 0)
    def _():
        copies(tile - 1, 1 - slot, lambda c: c.wait())

    @pl.when(tile == pl.num_programs(0) - 1)
    def _():
        zx[...] = jnp.zeros_like(zx)
        zw[...] = jnp.zeros_like(zw)
        fill = functools.partial(_zero_fill_copies, tbl_ref, (zx, zw), (xs_out, ws_out), (sems.at[2, 0], sems.at[2, 1]))
        fill(lambda c: c.start())
        copies(tile, slot, lambda c: c.wait())
        fill(lambda c: c.wait())


def _dispatch(tbl, x1b, wf, ltri, ustr):
    tm = TM_PROJ
    row = lambda n: pl.BlockSpec((tm, n), lambda i, t: (i, 0))
    full = lambda a: pl.BlockSpec(a.shape, lambda i, t: (0,) * a.ndim)
    anyspace = pl.BlockSpec(memory_space=pl.ANY)
    return pl.pallas_call(
        _dispatch_kernel,
        grid_spec=pltpu.PrefetchScalarGridSpec(
            num_scalar_prefetch=1, grid=(TOKENS // tm,),
            in_specs=[row(D_MODEL), row(LANES), full(ltri), full(ustr)],
            out_specs=[anyspace, anyspace],
            scratch_shapes=[pltpu.VMEM((2, ROWS_LOCAL, D_MODEL), BF16), pltpu.VMEM((2, ROWS_LOCAL, LANES), F32),
                            pltpu.VMEM((GRAN, D_MODEL), BF16), pltpu.VMEM((GRAN, LANES), F32),
                            pltpu.SemaphoreType.DMA((3, 2))]),
        out_shape=[jax.ShapeDtypeStruct((ROWS_SORTED, D_MODEL), BF16),
                   jax.ShapeDtypeStruct((ROWS_SORTED, LANES), F32)],
        compiler_params=pltpu.CompilerParams(dimension_semantics=("arbitrary",)),
        name="moe_dispatch",
    )(tbl, x1b, wf, ltri, ustr)


def _experts_kernel(gid_ref, valid_ref, xs_ref, ws_ref, wg_ref, wu_ref, wd_ref, y_ref):
    i = pl.program_id(0)

    @pl.when(valid_ref[i] == 1)
    def _():
        x = xs_ref[...]
        gates = [_dot(x, wg_ref[e]) for e in range(MOE_EPG)]
        ups = [_dot(x, wu_ref[e]) for e in range(MOE_EPG)]
        ws = ws_ref[...]
        lane = lax.broadcasted_iota(jnp.int32, ws.shape, 1)
        hidden = []
        for e in range(MOE_EPG):
            w_tok = jnp.sum(jnp.where(lane == gid_ref[i] * MOE_EPG + e, ws, 0.0), axis=-1, keepdims=True)
            hidden.append((gates[e] * jax.nn.sigmoid(gates[e]) * ups[e] * w_tok).astype(BF16))
        wd = wd_ref[...].reshape(MOE_EPG * MOE_HIDDEN, D_MODEL)
        y_ref[...] = _dot(jnp.concatenate(hidden, axis=1), wd).astype(BF16)

    @pl.when(valid_ref[i] == 0)
    def _():
        y_ref[...] = jnp.zeros_like(y_ref)


def _experts(gid, valid, xs, ws, wg, wu, wd):
    tm = TM_MOE
    row = lambda n: pl.BlockSpec((tm, n), lambda i, gid, valid: (i, 0))
    wspec = lambda a, b: pl.BlockSpec((MOE_EPG, a, b), lambda i, gid, valid: (gid[i], 0, 0))
    return pl.pallas_call(
        _experts_kernel,
        grid_spec=pltpu.PrefetchScalarGridSpec(
            num_scalar_prefetch=2, grid=(ROWS_SORTED // tm,),
            in_specs=[row(D_MODEL), row(LANES), wspec(D_MODEL, MOE_HIDDEN), wspec(D_MODEL, MOE_HIDDEN),
                      wspec(MOE_HIDDEN, D_MODEL)],
            out_specs=row(D_MODEL)),
        out_shape=jax.ShapeDtypeStruct((ROWS_SORTED, D_MODEL), BF16),
        compiler_params=pltpu.CompilerParams(dimension_semantics=("arbitrary",),
                                             vmem_limit_bytes=EXPERTS_VMEM_BYTES),
        name="moe_experts",
    )(gid, valid, xs, ws, wg, wu, wd)


def _combine_kernel(tbl_ref, wf_ref, x1_ref, ltri_ref, ustr_ref, g_ref, b_ref, y_hbm, o_ref, ybuf, sems):
    tile = pl.program_id(0)
    slot = tile % 2
    tm = x1_ref.shape[0]

    def copies(t, s, act):
        _granule_copies(tbl_ref, t, (ybuf.at[s],), (y_hbm,), (sems.at[s],), False, act)

    def fetch(t, s):
        ybuf[s] = jnp.zeros(ybuf.shape[1:], ybuf.dtype)
        copies(t, s, lambda c: c.start())

    @pl.when(tile == 0)
    def _():
        fetch(tile, slot)

    @pl.when(tile + 1 < pl.num_programs(0))
    def _():
        fetch(tile + 1, 1 - slot)

    dest = _group_dest(wf_ref[...], ltri_ref, ustr_ref)
    c = lax.broadcasted_iota(jnp.int32, (tm, ROWS_LOCAL), 1).astype(F32)
    unperm = jnp.where(c == dest, 1.0, 0.0).astype(BF16)
    copies(tile, slot, lambda c: c.wait())
    ffn = _dot(unperm, ybuf[slot])
    o_ref[...] = _layer_norm(DN_ALPHA * x1_ref[...] + ffn, g_ref[...], b_ref[...])


def _combine(tbl, wf, x1, ltri, ustr, g, b, y):
    tm = TM_PROJ
    row = lambda n: pl.BlockSpec((tm, n), lambda i, t: (i, 0))
    full = lambda a: pl.BlockSpec(a.shape, lambda i, t: (0,) * a.ndim)
    return pl.pallas_call(
        _combine_kernel,
        grid_spec=pltpu.PrefetchScalarGridSpec(
            num_scalar_prefetch=1, grid=(TOKENS // tm,),
            in_specs=[row(LANES), row(D_MODEL), full(ltri), full(ustr), full(g), full(b),
                      pl.BlockSpec(memory_space=pl.ANY)],
            out_specs=row(D_MODEL),
            scratch_shapes=[pltpu.VMEM((2, ROWS_LOCAL, D_MODEL), BF16), pltpu.SemaphoreType.DMA((2,))]),
        out_shape=jax.ShapeDtypeStruct((TOKENS, D_MODEL), F32),
        compiler_params=pltpu.CompilerParams(dimension_semantics=("arbitrary",)),
        name="moe_combine",
    )(tbl, wf, x1, ltri, ustr, g, b, y)


def _routing_tables(cnt):
    n_tiles = cnt.shape[0]
    c = cnt[:, 0, :MOE_GROUPS].astype(jnp.int32)
    gran = (c + GRAN - 1) // GRAN
    local = jnp.cumsum(gran, axis=1) - gran
    per_tile = TM_MOE // GRAN
    tiles_g = (jnp.sum(gran, axis=0) + per_tile - 1) // per_tile
    ends = jnp.cumsum(tiles_g)
    base = (ends - tiles_g) * per_tile
    glob = base[None, :] + jnp.cumsum(gran, axis=0) - gran
    used = jnp.sum(gran, axis=0)
    pad_first = jnp.concatenate([base + used, ends[-1:] * per_tile])
    pad_count = jnp.concatenate([tiles_g * per_tile - used, ROWS_SORTED // GRAN - ends[-1:] * per_tile])
    tail = jnp.stack([pad_first, pad_count], axis=1).reshape(-1)
    tbl = jnp.concatenate([jnp.concatenate([gran, local, glob], axis=1).reshape(n_tiles * TBL_W), tail])
    idx = jnp.arange(ROWS_SORTED // TM_MOE)
    gid = jnp.minimum(jnp.sum(idx[:, None] >= ends[None, :], axis=1), MOE_GROUPS - 1).astype(jnp.int32)
    valid = (idx < ends[-1]).astype(jnp.int32)
    return tbl, gid, valid


def _sort_tables():
    t = jnp.arange(TM_PROJ)
    ltri = (t[None, :] < t[:, None]).astype(BF16)
    l = jnp.arange(LANES)
    ustr = (l[:, None] < l[None, :]).astype(BF16)
    return ltri, ustr


def _dup_halves(t):
    lane = lax.broadcasted_iota(jnp.int32, t.shape, 1)
    r = pltpu.roll(t, HALF, 1)
    return jnp.where(lane < HALF, t, r), jnp.where(lane < HALF, r, t)


def _nsa_in_kernel(x_ref, w_ref, gb_ref, q_ref, kc_ref, vc_ref, ks_ref, vst_ref, kw_ref, vwt_ref, gate_ref):
    h = _dot(x_ref[...].astype(BF16), w_ref[...])
    q_ref[...] = (h[:, 0:C_MIX] * (NSA_SCALE * LOG2E)).astype(BF16)
    kc_ref[...] = h[:, 1024:1152].astype(BF16)
    vc_ref[...] = h[:, 1152:1280].astype(BF16)
    tm = h.shape[0]
    lane = lax.broadcasted_iota(jnp.int32, (tm, LANES), 1)
    for g, d in enumerate(_dup_halves(h[:, 1536:1664])):
        kw_ref[:, g * LANES:(g + 1) * LANES] = d.astype(BF16)
    pos = (pl.program_id(0) * tm) % SEQ + lax.broadcasted_iota(jnp.int32, (tm, LANES), 0)
    block_onehot = jnp.where(lane == pos // NSA_SEL_LEN, 1.0, 0.0).astype(BF16)
    for g, d in enumerate(_dup_halves(h[:, 1280:1408])):
        ks_ref[:, 2 * g * LANES:(2 * g + 1) * LANES] = d.astype(BF16)
        ks_ref[:, (2 * g + 1) * LANES:(2 * g + 2) * LANES] = block_onehot
    for idx, ref in ((1, vst_ref), (3, vwt_ref)):
        tail = jnp.where(lane == HALF, 1.0, 0.0)
        _store_transposed(ref, [jnp.where(lane < HALF, d, tail)
                                for d in _dup_halves(h[:, 1280 + idx * LANES:1280 + (idx + 1) * LANES])])
    for g in range(NSA_GROUPS):
        gate_ref[g] = jax.nn.sigmoid(h[:, 1792 + g * LANES:1792 + (g + 1) * LANES] + gb_ref[g])


def _nsa_in(x2, w, gb):
    tm = TM_PROJ
    row = lambda n: pl.BlockSpec((tm, n), lambda i: (i, 0))
    full = lambda a: pl.BlockSpec(a.shape, lambda i: (0,) * a.ndim)
    sd = jax.ShapeDtypeStruct
    vt_spec = lambda w: pl.BlockSpec((NSA_GROUPS, tm // w, LANES, w), lambda i: (0, i, 0, 0))
    vt_shape = lambda w: sd((NSA_GROUPS, TOKENS // w, LANES, w), BF16)
    return pl.pallas_call(
        _nsa_in_kernel,
        grid=(TOKENS // tm,),
        in_specs=[row(D_MODEL), full(w), full(gb)],
        out_specs=[row(C_MIX), row(LANES), row(LANES), row(4 * LANES), vt_spec(TK_ATT), row(2 * LANES),
                   vt_spec(WIN_SUB), pl.BlockSpec((NSA_GROUPS, tm, LANES), lambda i: (0, i, 0))],
        out_shape=[sd((TOKENS, C_MIX), BF16), sd((TOKENS, LANES), BF16), sd((TOKENS, LANES), BF16),
                   sd((TOKENS, 4 * LANES), BF16), vt_shape(TK_ATT), sd((TOKENS, 2 * LANES), BF16), vt_shape(WIN_SUB),
                   sd((NSA_GROUPS, TOKENS, LANES), F32)],
        compiler_params=pltpu.CompilerParams(dimension_semantics=("arbitrary",)),
        name="nsa_in",
    )(x2, w, gb)


def _compress_kernel(kc_ref, vc_ref, pk_ref, pv_ref, wk1_ref, wv1_ref, wk2_ref, wv2_ref, ko_ref, vo_ref):
    for a_ref, p_ref, w1_ref, w2_ref, o_ref in ((kc_ref, pk_ref, wk1_ref, wk2_ref, ko_ref),
                                                (vc_ref, pv_ref, wv1_ref, wv2_ref, vo_ref)):
        a = a_ref[0].astype(F32)
        a0 = (a + p_ref[0]).astype(BF16)
        a1 = (a + p_ref[1]).astype(BF16)
        outs = []
        for g in range(NSA_GROUPS):
            first = _dot(a0, w1_ref[g])
            second = _dot(a1, w1_ref[NSA_GROUPS + g])
            hid = first + pltpu.roll(second, NSA_NCMP - 1, 0)
            outs.append(_dot(_gelu(hid).astype(BF16), w2_ref[...]))
        if o_ref is ko_ref:
            o_ref[0] = jnp.concatenate(outs, axis=1).astype(BF16)
        else:
            for g in range(NSA_GROUPS):
                o_ref[0, g] = outs[g].T.astype(BF16)


def _compress(kc_r, vc_r, pk, pv, wk1, wv1, wk2, wv2):
    blk = pl.BlockSpec((1, NSA_NCMP, NSA_CMP_STRIDE * LANES), lambda b: (b, 0, 0))
    full = lambda a: pl.BlockSpec(a.shape, lambda b: (0,) * a.ndim)
    sd = jax.ShapeDtypeStruct
    return pl.pallas_call(
        _compress_kernel,
        grid=(BATCH,),
        in_specs=[blk, blk, full(pk), full(pv), full(wk1), full(wv1), full(wk2), full(wv2)],
        out_specs=[pl.BlockSpec((1, NSA_NCMP, 2 * LANES), lambda b: (b, 0, 0)),
                   pl.BlockSpec((1, NSA_GROUPS, LANES, NSA_NCMP), lambda b: (b, 0, 0, 0))],
        out_shape=[sd((BATCH, NSA_NCMP, 2 * LANES), BF16), sd((BATCH, NSA_GROUPS, LANES, NSA_NCMP), BF16)],
        compiler_params=pltpu.CompilerParams(dimension_semantics=("arbitrary",)),
        name="nsa_compress",
    )(kc_r, vc_r, pk, pv, wk1, wv1, wk2, wv2)


def _nsa_attn_kernel(q_ref, kc_ref, vct_ref, ks_ref, vst_ref, kw_ref, vwt_ref, gate_ref, cover_ref,
                     o_ref, m_ref, acc_ref, sa_ref, sb_ref):
    tq, tk, hpg = TQ_NSA, TK_ATT, NSA_HPG
    qi = pl.program_id(2)
    q0 = qi * tq
    lane = lax.broadcasted_iota(jnp.int32, (tq, LANES), 1)
    t_tok = q0 + lax.broadcasted_iota(jnp.int32, (1, tq), 1)
    head = lambda x, i: x[:, i * tq:(i + 1) * tq]
    heads = range(hpg)

    parts = []
    for p in range(hpg // 2):
        qp = q_ref[0, :, p * LANES:(p + 1) * LANES]
        zero = jnp.zeros_like(qp)
        parts.append(jnp.where(lane < HALF, qp, zero))
        parts.append(jnp.where(lane < HALF, zero, qp))
    qs = jnp.concatenate(parts, axis=0)

    kb_sub = lax.broadcasted_iota(jnp.int32, (WIN_BAND, 1), 0)

    def window_scores(part):
        q_lo = q0 + part * WIN_SUB
        start = pl.multiple_of(jnp.maximum(q_lo - NSA_WINDOW, 0), WIN_SUB)
        qs_part = jnp.concatenate([qs[i * tq + part * WIN_SUB:i * tq + (part + 1) * WIN_SUB] for i in heads], axis=0)
        return q_lo, start, _dot_nt(kw_ref[0, pl.ds(start, WIN_BAND), :], qs_part)

    def window_finish(q_lo, start, s_w):
        kpos = start + kb_sub
        t_part = q_lo + lax.broadcasted_iota(jnp.int32, (1, WIN_SUB), 1)
        in_win = (kpos <= t_part) & (kpos > t_part - NSA_WINDOW)
        vw_band = jnp.concatenate([vwt_ref[0, start // WIN_SUB + c] for c in range(WIN_BAND // WIN_SUB)], axis=1)
        out = []
        for i0 in range(0, hpg, 2):
            e_w = []
            for i in (i0, i0 + 1):
                sm = jnp.where(in_win, s_w[:, i * WIN_SUB:(i + 1) * WIN_SUB], NEG)
                e_w.append(jnp.exp2((sm - jnp.max(sm, axis=0, keepdims=True)).astype(BF16)))
            o_pair = _dot(vw_band, jnp.concatenate(e_w, axis=1))
            out += [o_pair[:, :WIN_SUB], o_pair[:, WIN_SUB:]]
        return out

    s_c = _dot_nt(kc_ref[0], qs)
    win0 = window_scores(0)
    n_sub = lax.broadcasted_iota(jnp.int32, (NSA_NCMP, 1), 0)
    vis = t_tok >= n_sub * NSA_CMP_STRIDE + (NSA_CMP_LEN - 1)
    sees_any = t_tok >= NSA_CMP_LEN - 1
    p_sum = jnp.zeros((NSA_NCMP, tq), F32)
    p_c = []
    for i in heads:
        sm = jnp.where(vis, head(s_c, i), NEG)
        e = jnp.exp2(sm - jnp.max(sm, axis=0, keepdims=True))
        p = e * jnp.where(sees_any, 1.0 / jnp.sum(e, axis=0, keepdims=True), 0.0)
        p_sum = p_sum + p
        p_c.append(p.astype(BF16))
    o_c = _dot(vct_ref[0, 0], jnp.concatenate(p_c, axis=1))

    imp = jnp.dot(cover_ref[...], p_sum, preferred_element_type=F32,
                  precision=lax.Precision.HIGHEST)[0:NSA_NSEL]
    o_w_parts = [window_finish(*win0)]
    o_w_parts += [window_finish(*window_scores(part)) for part in range(1, tq // WIN_SUB)]
    jj = lax.broadcasted_iota(jnp.int32, (NSA_NSEL, 1), 0)
    tb = t_tok // NSA_SEL_LEN
    forced = (jj == 0) | (jj == tb) | (jj == tb - 1)
    score = jnp.where(forced, NSA_FORCE, jnp.where(jj <= tb, imp, -NSA_FORCE))
    sub = 8
    rows = [score[b * sub:(b + 1) * sub] for b in range(NSA_NSEL // sub)]
    ranks = [jnp.zeros((sub, tq), jnp.int32) for _ in rows]
    j_in = lax.broadcasted_iota(jnp.int32, (sub, 1), 0)
    for i in range(NSA_NSEL):
        si = score[i:i + 1, :]
        for b, blk in enumerate(rows):
            if b < i // sub:
                beats = si > blk
            elif b > i // sub:
                beats = si >= blk
            else:
                beats = (si > blk) | ((si == blk) & (j_in > i % sub))
            ranks[b] = ranks[b] + beats.astype(jnp.int32)
    rank = jnp.concatenate(ranks, axis=0)
    sel_bias = jnp.where(rank < NSA_TOPK, 0.0, NEG)
    bias_rows = jnp.concatenate([sel_bias, jnp.zeros((LANES - NSA_NSEL, tq), F32)], axis=0).T.astype(BF16)
    qs_sel = jnp.concatenate([qs, jnp.concatenate([bias_rows] * hpg, axis=0)], axis=1)

    m_ref[...] = jnp.full(m_ref.shape, NEG, F32)
    acc_ref[...] = jnp.zeros(acc_ref.shape, F32)
    k_sub = lax.broadcasted_iota(jnp.int32, (tk, 1), 0)
    diag = q0 // tk
    causal = diag * tk + k_sub <= t_tok
    per_dot = max(1, 2 * LANES // tq)

    def scores_into(ref, kt):
        ref[...] = _dot_nt(ks_ref[0, pl.ds(pl.multiple_of(kt * tk, tk), tk), :], qs_sel)

    def process(ref, kt, diagonal):
        s = ref[...]
        vt = vst_ref[0, kt]
        for i0 in range(0, hpg, per_dot):
            cols = slice(i0 * tq, (i0 + per_dot) * tq)
            probs, alphas = [], []
            for i in range(i0, i0 + per_dot):
                for part in range(tq // LANES):
                    sub = slice(part * LANES, (part + 1) * LANES)
                    sm = s[:, i * tq + part * LANES:i * tq + (part + 1) * LANES]
                    if diagonal:
                        sm = jnp.where(causal[:, sub], sm, NEG)
                    m_old = m_ref[i, :, sub]
                    m_new = jnp.maximum(m_old, jnp.max(sm, axis=0, keepdims=True))
                    m_ref[i, :, sub] = m_new
                    probs.append(jnp.exp2((sm - m_new).astype(BF16)))
                    alphas.append(jnp.exp2(m_old - m_new))
            pv = _dot(vt, jnp.concatenate(probs, axis=1))
            acc_ref[:, cols] = jnp.concatenate(alphas, axis=1) * acc_ref[:, cols] + pv

    odd = diag % 2
    scores_into(sa_ref, 0)

    @pl.when(odd == 1)
    def _():
        process(sa_ref, 0, False)
        scores_into(sa_ref, 1)

    def pair(k, c):
        t0 = odd + 2 * k
        scores_into(sb_ref, t0 + 1)
        process(sa_ref, t0, False)
        scores_into(sa_ref, t0 + 2)
        process(sb_ref, t0 + 1, False)
        return c

    lax.fori_loop(0, diag // 2, pair, 0)
    process(sa_ref, diag, True)

    o_w =[jnp.concatenate([part[i] for part in o_w_parts], axis=1) for i in heads]

    gt = gate_ref[0, 0].T
    o_s = acc_ref[...]
    outs = []
    for i in heads:
        c_i, s_i, w_i = head(o_c, i), head(o_s, i), o_w[i]
        outs.append(gt[i:i + 1] * c_i[:HALF]
                    + gt[hpg + i:hpg + i + 1] / s_i[HALF:HALF + 1] * s_i[:HALF]
                    + gt[2 * hpg + i:2 * hpg + i + 1] / w_i[HALF:HALF + 1] * w_i[:HALF])
    for p in range(hpg // 2):
        pair = jnp.concatenate([outs[2 * p], outs[2 * p + 1]], axis=0)
        o_ref[0, :, p * LANES:(p + 1) * LANES] = pair.T.astype(BF16)


def _nsa_attn(q3, kc2, vct, ks3, vst, kw3, vwt, gates4, cover_t):
    tq = TQ_NSA
    half_w = NSA_HPG * NSA_DH
    kv = pl.BlockSpec((1, SEQ, LANES), lambda b, g, i: (b, 0, g))
    vt = lambda w: pl.BlockSpec((1, SEQ // w, LANES, w), lambda b, g, i: (g, b, 0, 0))
    full = lambda a: pl.BlockSpec(a.shape, lambda b, g, i: (0,) * a.ndim)
    return pl.pallas_call(
        _nsa_attn_kernel,
        grid=(BATCH, NSA_GROUPS, SEQ // tq),
        in_specs=[pl.BlockSpec((1, tq, half_w), lambda b, g, i: (b, i, g)),
                  pl.BlockSpec((1, NSA_NCMP, LANES), lambda b, g, i: (b, 0, g)),
                  pl.BlockSpec((1, 1, LANES, NSA_NCMP), lambda b, g, i: (b, g, 0, 0)),
                  pl.BlockSpec((1, SEQ, 2 * LANES), lambda b, g, i: (b, 0, g)), vt(TK_ATT), kv, vt(WIN_SUB),
                  pl.BlockSpec((1, 1, tq, LANES), lambda b, g, i: (g, b, i, 0)),
                  full(cover_t)],
        out_specs=pl.BlockSpec((1, tq, half_w), lambda b, g, i: (b, i, g)),
        out_shape=jax.ShapeDtypeStruct((BATCH, SEQ, C_MIX), BF16),
        scratch_shapes=[pltpu.VMEM((NSA_HPG, 1, tq), F32),
                        pltpu.VMEM((LANES, NSA_HPG * tq), F32),
                        pltpu.VMEM((TK_ATT, NSA_HPG * tq), F32), pltpu.VMEM((TK_ATT, NSA_HPG * tq), F32)],
        compiler_params=pltpu.CompilerParams(dimension_semantics=("arbitrary",) * 3),
        name="nsa_attn",
    )(q3, kc2, vct, ks3, vst, kw3, vwt, gates4, cover_t)


def _rope_tables():
    half = MLA_ROPE // 2
    freq = jnp.exp(-math.log(ROPE_BASE) * jnp.arange(half, dtype=F32) / half)
    return freq[:, None]


def _swap_halves(w):
    half = w.shape[-1] // 2
    return jnp.concatenate([w[..., half:], w[..., :half]], axis=-1)


def _pad_last(w, n):
    return jnp.pad(w, [(0, 0)] * (w.ndim - 1) + [(0, n - w.shape[-1])])


def _ab_weights(w_in, w_uq, w_uk, w_uv):
    w_kr = w_in[:, 1408:1440]
    place = lambda w: jnp.pad(w, ((0, 0), (MLA_NOPE, LANES - MLA_NOPE - MLA_ROPE)))
    win = jnp.concatenate([w_in[:, :1408], place(w_kr), place(_swap_halves(w_kr))], axis=1).astype(BF16)
    uq = w_uq.reshape(MLA_Q_RANK, MLA_HEADS, MLA_NOPE + MLA_ROPE)
    nope, rp = uq[..., :MLA_NOPE], uq[..., MLA_NOPE:]
    q_pad = _pad_last(jnp.concatenate([nope, rp], -1), LANES).reshape(MLA_Q_RANK, MLA_HEADS * LANES)
    q_sw = _pad_last(jnp.concatenate([jnp.zeros_like(nope), _swap_halves(rp)], -1), LANES)
    wq = jnp.concatenate([q_pad, q_sw.reshape(MLA_Q_RANK, MLA_HEADS * LANES)], axis=1).astype(BF16)
    k_pad = _pad_last(w_uk.reshape(MLA_KV_RANK, MLA_HEADS, MLA_NOPE), LANES).reshape(MLA_KV_RANK, -1)
    wkv = jnp.concatenate([k_pad, w_uv], axis=1).astype(BF16)
    return win, wq, wkv


def _router_weights(w_rg, b_rg, w_re, b_re):
    wr = _pad_last(jnp.concatenate([w_re, w_rg], axis=1), LANES)
    wr_hi = wr.astype(BF16)
    wr_lo = (wr - wr_hi.astype(F32)).astype(BF16)
    br = _pad_last(jnp.concatenate([b_re, b_rg])[None, :], LANES)
    return jnp.concatenate([wr_hi, wr_lo], axis=1), br


def _nsa_in_weights(w_in, gate_b):
    g_cols = w_in[:, C_MIX + 768:].reshape(D_MODEL, 3, NSA_GROUPS, NSA_HPG)
    g_blocks = [_pad_last(g_cols[:, :, g, :].reshape(D_MODEL, 3 * NSA_HPG), LANES) for g in range(NSA_GROUPS)]
    w = jnp.concatenate([w_in[:, :C_MIX + 768]] + g_blocks, axis=1).astype(BF16)
    gb = gate_b.reshape(3, NSA_GROUPS, NSA_HPG)
    gb = jnp.stack([_pad_last(gb[:, g, :].reshape(1, 3 * NSA_HPG), LANES) for g in range(NSA_GROUPS)])
    return w, gb


def _compress_weights(pos, w1, w2):
    w1r = w1.reshape(2, NSA_CMP_STRIDE, NSA_DH, NSA_CMP_HIDDEN)
    zero = jnp.zeros_like(w1r)
    per_g = []
    for g in range(NSA_GROUPS):
        parts = [w1r if gg == g else zero for gg in range(NSA_GROUPS)]
        per_g.append(jnp.stack(parts, axis=2).reshape(2, NSA_CMP_STRIDE * LANES, NSA_CMP_HIDDEN))
    w1x = jnp.stack(per_g, axis=1).reshape(2 * NSA_GROUPS, NSA_CMP_STRIDE * LANES, NSA_CMP_HIDDEN)
    posr = pos.reshape(2, NSA_CMP_STRIDE, 1, NSA_DH)
    posx = jnp.broadcast_to(posr, (2, NSA_CMP_STRIDE, NSA_GROUPS, NSA_DH)).reshape(2, 1, NSA_CMP_STRIDE * LANES)
    w2x = jnp.concatenate([w2, w2], axis=1)
    return posx, w1x.astype(BF16), w2x.astype(BF16)


def _selection_tables():
    n = jnp.arange(LANES)[:, None]
    j = jnp.arange(LANES)[None, :]
    c0 = n * NSA_CMP_STRIDE
    s0 = j * NSA_SEL_LEN
    cover = ((c0 < s0 + NSA_SEL_LEN) & (c0 + NSA_CMP_LEN > s0) & (n < NSA_NCMP - 1) & (j < NSA_NSEL))
    return jnp.transpose(cover).astype(F32)


def kernel(x, positions, ab_w_in, ab_gm_ln_g, ab_gm_ln_b, ab_gm_ws, ab_gm_bs, ab_mla_q_norm,
           ab_mla_kv_norm, ab_mla_w_uq, ab_mla_w_uk, ab_mla_w_uv, ab_w_o, c_w_in, c_cmp_pos, c_w_ck1,
           c_w_ck2, c_w_cv1, c_w_cv2, c_gate_b, c_w_o, moe_w_rg, moe_b_rg, moe_w_re, moe_b_re,
           moe_w_gate, moe_w_up, moe_w_down, ln1_g, ln1_b, ln2_g, ln2_b):
    x2 = x.reshape(TOKENS, D_MODEL)
    pos3 = positions.reshape(TOKENS // TM_PROJ, 1, TM_PROJ)
    vec = lambda a: a[None, :]

    ltri, ustr = _sort_tables()

    def moe_layer(layer, x1b, x1, wf, cnt):
        tbl, gid, valid = _routing_tables(cnt)
        xs, ws = _dispatch(tbl, x1b, wf, ltri, ustr)
        y = _experts(gid, valid, xs, ws, moe_w_gate[layer].astype(BF16), moe_w_up[layer].astype(BF16),
                     moe_w_down[layer].astype(BF16))
        return _combine(tbl, wf, x1, ltri, ustr, vec(ln2_g[layer]), vec(ln2_b[layer]), y)

    win, wq, wkv = _ab_weights(ab_w_in[0], ab_mla_w_uq[0], ab_mla_w_uk[0], ab_mla_w_uv[0])
    gu, vn, q, k, vt = _ab_in(x2, pos3, win, vec(ab_gm_ln_g[0]), vec(ab_gm_ln_b[0]), vec(ab_mla_q_norm[0]),
                              vec(ab_mla_kv_norm[0]), wq, wkv, _rope_tables())
    yb = _mla_attn(q.reshape(BATCH, SEQ, -1), k.reshape(BATCH, SEQ, -1), vt)
    wr, br = _router_weights(moe_w_rg[0], moe_b_rg[0], moe_w_re[0], moe_b_re[0])
    x1, x1b, wf, cnt = _mix_out(x2, yb.reshape(TOKENS, -1), ab_w_o[0].astype(BF16), vec(ln1_g[0]),
                                vec(ln1_b[0]), wr, br, gm=(gu, vn, ab_gm_ws[0], jnp.transpose(ab_gm_bs[0])))
    x2 = moe_layer(0, x1b, x1, wf, cnt)

    w_nsa, gb = _nsa_in_weights(c_w_in[0], c_gate_b[0])
    q, kc, vc, ks, vst, kw, vwt, gates = _nsa_in(x2, w_nsa, gb)
    pk, wk1, wk2 = _compress_weights(c_cmp_pos[0, 0], c_w_ck1[0], c_w_ck2[0])
    pv, wv1, wv2 = _compress_weights(c_cmp_pos[0, 1], c_w_cv1[0], c_w_cv2[0])
    blocks = lambda a: a.reshape(BATCH, NSA_NCMP, NSA_CMP_STRIDE * LANES)
    kc2, vct = _compress(blocks(kc), blocks(vc), pk, pv, wk1, wv1, wk2, wv2)
    b3 = lambda a: a.reshape(BATCH, SEQ, -1)
    o = _nsa_attn(b3(q), kc2, vct, b3(ks), vst, b3(kw), vwt,
                  gates.reshape(NSA_GROUPS, BATCH, SEQ, LANES), _selection_tables())
    wr, br = _router_weights(moe_w_rg[1], moe_b_rg[1], moe_w_re[1], moe_b_re[1])
    x1, x1b, wf, cnt = _mix_out(x2, o.reshape(TOKENS, -1), c_w_o[0].astype(BF16), vec(ln1_g[1]), vec(ln1_b[1]),
                                wr, br)
    x2 = moe_layer(1, x1b, x1, wf, cnt)
    return x2.reshape(BATCH, SEQ, D_MODEL)
```

```python
import functools
import math

import jax
import jax.numpy as jnp
from jax import lax
from jax.experimental import pallas as pl
from jax.experimental.pallas import tpu as pltpu

F32 = jnp.float32
BF16 = jnp.bfloat16

D_MODEL = 1024
BATCH = 16
SEQ = 2048
TOKENS = BATCH * SEQ
DEPTH = 2
DN_ALPHA = (2.0 * DEPTH) ** 0.25
LN_EPS = 1e-5
NEG = -1e30
LOG2E = math.log2(math.e)
LANES = 128
HALF = LANES // 2

GM_WIDTH = 512
GM_GROUPS = 4
GM_CHUNK = 128

MLA_HEADS = 8
MLA_NOPE = 64
MLA_ROPE = 32
MLA_V = 64
MLA_Q_RANK = 256
MLA_KV_RANK = 128
ROPE_BASE = 10000.0
MLA_SCALE = (MLA_NOPE + MLA_ROPE) ** -0.5

NSA_HEADS = 16
NSA_GROUPS = 2
NSA_HPG = 8
NSA_DH = 64
NSA_CMP_LEN = 32
NSA_CMP_STRIDE = 16
NSA_CMP_HIDDEN = 256
NSA_SEL_LEN = 64
NSA_TOPK = 8
NSA_WINDOW = 512
NSA_NSEL = SEQ // NSA_SEL_LEN
NSA_NCMP = SEQ // NSA_CMP_STRIDE
NSA_FORCE = 1e4
NSA_SCALE = NSA_DH ** -0.5
C_MIX = NSA_HEADS * NSA_DH

MOE_GROUPS = 4
MOE_EPG = 8
MOE_EXPERTS = 32
MOE_HIDDEN = 256

TM_PROJ = 512
TQ_MLA = 256
TQ_NSA = 256
TK_ATT = 256
WIN_SUB = LANES
WIN_BAND = NSA_WINDOW + WIN_SUB
MLA_VT_ROWS = LANES + 16
TM_MOE = 512
EXPERTS_VMEM_BYTES = (2 * 3 * MOE_EPG * D_MODEL * MOE_HIDDEN * 2 + 2 * 2 * TM_MOE * D_MODEL * 2
                      + 2 * MOE_EPG * TM_MOE * MOE_HIDDEN * 4 + TM_MOE * MOE_EPG * MOE_HIDDEN * 2
                      + 2 * TM_MOE * D_MODEL * 4)

GSEL_LANE = MOE_EXPERTS
GRAN = 16
TBL_W = 3 * MOE_GROUPS
ROWS_LOCAL = 640
ROWS_SORTED = TM_MOE * (TOKENS // TM_MOE + MOE_GROUPS
                        + -(-(TOKENS // TM_PROJ) * MOE_GROUPS * (GRAN - 1) // TM_MOE))


def _dot(a, b):
    return jnp.dot(a, b, preferred_element_type=F32)


def _dot_nt(a, b):
    return lax.dot_general(a, b, (((1,), (1,)), ((), ())), preferred_element_type=F32)


def _gelu(x):
    return 0.5 * x * (1.0 + jnp.tanh(math.sqrt(2.0 / math.pi) * (x + 0.044715 * (x * x * x))))


def _layer_norm(x, g, b):
    mu = jnp.mean(x, axis=-1, keepdims=True)
    xc = x - mu
    var = jnp.mean(xc * xc, axis=-1, keepdims=True)
    return xc * lax.rsqrt(var + LN_EPS) * g + b


def _rms_norm(x, g):
    return x * lax.rsqrt(jnp.mean(x * x, axis=-1, keepdims=True) + LN_EPS) * g


def _store_transposed(ref, blocks):
    extra, width = ref.shape[2] - LANES, ref.shape[3]
    if extra:
        ones_rows = jnp.where(lax.broadcasted_iota(jnp.int32, (extra, width), 0) == 0, 1.0, 0.0)
    for n, blk in enumerate(blocks):
        t = blk.T
        for c in range(t.shape[1] // width):
            chunk = t[:, c * width:(c + 1) * width]
            if extra:
                chunk = jnp.concatenate([chunk, ones_rows], axis=0)
            ref[n, c] = chunk.astype(ref.dtype)


def _ab_in_kernel(x_ref, pos_ref, win_ref, lng_ref, lnb_ref, qg_ref, kvg_ref, wq_ref, wkv_ref,
                  fc_ref, gu_ref, vn_ref, q_ref, k_ref, vt_ref):
    h = _dot(x_ref[...].astype(BF16), win_ref[...])
    gu_ref[...] = _gelu(h[:, 0:512]).astype(BF16)
    vn_ref[...] = _layer_norm(_gelu(h[:, 512:1024]), lng_ref[...], lnb_ref[...]).astype(BF16)

    tm = x_ref.shape[0]
    ang = fc_ref[...] * pos_ref[0].astype(F32)
    cos_t, sin_t = jnp.cos(ang), jnp.sin(ang)
    ones_t, zeros_t = jnp.ones((MLA_NOPE, tm), F32), jnp.zeros((MLA_NOPE, tm), F32)
    pad = LANES - MLA_NOPE - MLA_ROPE
    cc = jnp.concatenate([ones_t, cos_t, cos_t, ones_t[:pad]], axis=0).T
    ss = jnp.concatenate([zeros_t, -sin_t, sin_t, zeros_t[:pad]], axis=0).T

    cqn = _rms_norm(h[:, 1024:1280], qg_ref[...]).astype(BF16)
    qq = _dot(cqn, wq_ref[...])
    for hd in range(MLA_HEADS):
        lo, hi = hd * LANES, (hd + 1) * LANES
        q_ref[:, lo:hi] = ((qq[:, lo:hi] * cc + qq[:, 1024 + lo:1024 + hi] * ss) * (MLA_SCALE * LOG2E)).astype(BF16)

    ckvn = _rms_norm(h[:, 1280:1408], kvg_ref[...]).astype(BF16)
    kv = _dot(ckvn, wkv_ref[...])
    k_rope = h[:, 1408:1536] * cc + h[:, 1536:1664] * ss
    for hd in range(MLA_HEADS):
        lo, hi = hd * LANES, (hd + 1) * LANES
        k_ref[:, lo:hi] = (kv[:, lo:hi] + k_rope).astype(BF16)
    _store_transposed(vt_ref, [kv[:, 1024 + p * LANES:1024 + (p + 1) * LANES] for p in range(MLA_HEADS // 2)])


def _ab_in(x2, pos3, win, lng, lnb, qg, kvg, wq, wkv, fc):
    tm = TM_PROJ
    row = lambda n: pl.BlockSpec((tm, n), lambda i: (i, 0))
    full = lambda a: pl.BlockSpec(a.shape, lambda i: (0,) * a.ndim)
    return pl.pallas_call(
        _ab_in_kernel,
        grid=(TOKENS // tm,),
        in_specs=[row(D_MODEL), pl.BlockSpec((1, 1, tm), lambda i: (i, 0, 0)), full(win), full(lng), full(lnb),
                  full(qg), full(kvg), full(wq), full(wkv), full(fc)],
        out_specs=[row(512), row(512), row(1024), row(1024),
                   pl.BlockSpec((MLA_HEADS // 2, tm // TK_ATT, MLA_VT_ROWS, TK_ATT), lambda i: (0, i, 0, 0))],
        out_shape=[jax.ShapeDtypeStruct((TOKENS, 512), BF16), jax.ShapeDtypeStruct((TOKENS, 512), BF16),
                   jax.ShapeDtypeStruct((TOKENS, 1024), BF16), jax.ShapeDtypeStruct((TOKENS, 1024), BF16),
                   jax.ShapeDtypeStruct((MLA_HEADS // 2, TOKENS // TK_ATT, MLA_VT_ROWS, TK_ATT), BF16)],
        compiler_params=pltpu.CompilerParams(dimension_semantics=("arbitrary",)),
        name="ab_in",
    )(x2, pos3, win, lng, lnb, qg, kvg, wq, wkv, fc)


def _mla_attn_kernel(q_ref, k_ref, vt_ref, o_ref, m_ref, l_ref, acc_ref):
    tq, tk = TQ_MLA, TK_ATT
    qi = pl.program_id(1)
    krow = lax.broadcasted_iota(jnp.int32, (tk, tq), 0)
    qcol = lax.broadcasted_iota(jnp.int32, (tk, tq), 1)
    top = lax.broadcasted_iota(jnp.int32, (LANES, tq), 0) < HALF
    m_ref[...] = jnp.full(m_ref.shape, NEG, F32)
    l_ref[...] = jnp.zeros(l_ref.shape, F32)
    acc_ref[...] = jnp.zeros(acc_ref.shape, F32)

    def tile(j, masked, chunks=1):
        r0 = pl.multiple_of(j * tk, tk)
        scores = [_dot_nt(k_ref[0, pl.ds(r0, chunks * tk), h * LANES:(h + 1) * LANES],
                          q_ref[0, :, h * LANES:(h + 1) * LANES])
                  for h in range(MLA_HEADS)]
        for pr in range(MLA_HEADS // 2):
            probs, alphas = [], []
            for h in (2 * pr, 2 * pr + 1):
                s = jnp.where(krow <= qcol, scores[h], NEG) if masked else scores[h]
                m_old = m_ref[h]
                m_new = jnp.maximum(m_old, jnp.max(s, axis=0, keepdims=True))
                alphas.append(jnp.exp2(m_old - m_new))
                probs.append(jnp.exp2(s - m_new).astype(BF16))
                m_ref[h] = m_new
            vt = jnp.concatenate([vt_ref[pr, j + c] for c in range(chunks)], axis=1)
            pv = _dot(vt, jnp.concatenate(probs, axis=1))
            for n, hh in enumerate((2 * pr, 2 * pr + 1)):
                l_ref[hh] = alphas[n] * l_ref[hh] + pv[LANES:LANES + 1, n * tq:(n + 1) * tq]
            a = jnp.where(top, alphas[0], alphas[1])
            acc_ref[pr] = a * acc_ref[pr] + jnp.where(top, pv[:LANES, :tq], pv[:LANES, tq:])

    def body(j, c):
        tile(2 * j, False, chunks=2)
        return c

    lax.fori_loop(0, qi // 2, body, 0)

    @pl.when(qi % 2 == 1)
    def _():
        tile(qi - 1, False)

    tile(qi, True)
    for pr in range(MLA_HEADS // 2):
        l = jnp.where(top, l_ref[2 * pr], l_ref[2 * pr + 1])
        o_ref[0, :, pr * LANES:(pr + 1) * LANES] = (acc_ref[pr] / l).T.astype(BF16)


def _mla_attn(q3, k3, vt):
    tq = TQ_MLA
    n_chunks = SEQ // TK_ATT
    return pl.pallas_call(
        _mla_attn_kernel,
        grid=(BATCH, SEQ // tq),
        in_specs=[pl.BlockSpec((1, tq, MLA_HEADS * LANES), lambda b, i: (b, i, 0)),
                  pl.BlockSpec((1, SEQ, MLA_HEADS * LANES), lambda b, i: (b, 0, 0)),
                  pl.BlockSpec((MLA_HEADS // 2, n_chunks, MLA_VT_ROWS, TK_ATT), lambda b, i: (0, b, 0, 0))],
        out_specs=pl.BlockSpec((1, tq, MLA_HEADS * MLA_V), lambda b, i: (b, i, 0)),
        out_shape=jax.ShapeDtypeStruct((BATCH, SEQ, MLA_HEADS * MLA_V), BF16),
        scratch_shapes=[pltpu.VMEM((MLA_HEADS, 1, tq), F32), pltpu.VMEM((MLA_HEADS, 1, tq), F32),
                        pltpu.VMEM((MLA_HEADS // 2, LANES, tq), F32)],
        compiler_params=pltpu.CompilerParams(dimension_semantics=("arbitrary",) * 2),
        name="mla_attn",
    )(q3, k3, vt)


def _router(x1, wr, br):
    tm = x1.shape[0]
    x_hi = x1.astype(BF16)
    x_lo = (x1 - x_hi.astype(F32)).astype(BF16)
    parts = _dot(jnp.concatenate([x_hi, x_lo], axis=0), wr)
    logits = (parts[:tm, :LANES] + (parts[:tm, LANES:] + parts[tm:, :LANES]) + parts[tm:, LANES:]) + br
    lane = lax.broadcasted_iota(jnp.int32, (tm, LANES), 1).astype(F32)
    big = 1e6
    is_g = (lane >= MOE_EXPERTS) & (lane < MOE_EXPERTS + MOE_GROUPS)
    gl = jnp.where(is_g, logits, NEG)
    gmax = jnp.max(gl, axis=-1, keepdims=True)
    g_sel = jnp.min(jnp.where(is_g & (gl == gmax), lane, big), axis=-1, keepdims=True) - MOE_EXPERTS
    g_w = 1.0 / jnp.sum(jnp.where(is_g, jnp.exp(gl - gmax), 0.0), axis=-1, keepdims=True)
    in_grp = (lane >= g_sel * MOE_EPG) & (lane < (g_sel + 1) * MOE_EPG)
    el = jnp.where(in_grp, logits, NEG)
    emax = jnp.max(el, axis=-1, keepdims=True)
    ee = jnp.where(in_grp, jnp.exp(el - emax), 0.0)
    pe = ee / jnp.sum(ee, axis=-1, keepdims=True)
    p1 = jnp.max(pe, axis=-1, keepdims=True)
    i1 = jnp.min(jnp.where(in_grp & (pe == p1), lane, big), axis=-1, keepdims=True)
    rest = in_grp & (lane != i1)
    pr = jnp.where(rest, pe, -1.0)
    p2 = jnp.max(pr, axis=-1, keepdims=True)
    i2 = jnp.min(jnp.where(rest & (pr == p2), lane, big), axis=-1, keepdims=True)
    tot = p1 + p2
    wf = jnp.where(lane == i1, p1 / tot * g_w, jnp.where(lane == i2, p2 / tot * g_w, 0.0))
    wf = jnp.where(lane == GSEL_LANE, g_sel, wf)
    cnt = jnp.sum(jnp.where(lane == g_sel, 1.0, 0.0), axis=0, keepdims=True)
    return wf, cnt


def _mix_out_kernel(*refs, gmlp):
    if gmlp:
        (x_ref, gu_ref, vn_ref, ws_ref, bs_ref, yb_ref, wo_ref, g_ref, b_ref, wr_ref, br_ref,
         x1_ref, x1b_ref, wf_ref, cnt_ref, ya_ref) = refs
        tm = x_ref.shape[0]
        r = lax.broadcasted_iota(jnp.int32, (GM_CHUNK, GM_CHUNK), 0)
        c = lax.broadcasted_iota(jnp.int32, (GM_CHUNK, GM_CHUNK), 1)
        for g in range(GM_GROUPS):
            ws = jnp.where(r >= c, ws_ref[g], 0.0).astype(BF16)
            bias = bs_ref[:, g:g + 1]
            for ch in range(tm // GM_CHUNK):
                rows = slice(ch * GM_CHUNK, (ch + 1) * GM_CHUNK)
                cols = slice(g * LANES, (g + 1) * LANES)
                s = _dot(ws, vn_ref[rows, cols]) + bias
                ya_ref[rows, cols] = (gu_ref[rows, cols].astype(F32) * s).astype(BF16)
        mix = _dot(ya_ref[...], wo_ref[0:GM_WIDTH, :]) + _dot(yb_ref[...], wo_ref[GM_WIDTH:, :])
    else:
        x_ref, y_ref, wo_ref, g_ref, b_ref, wr_ref, br_ref, x1_ref, x1b_ref, wf_ref, cnt_ref = refs
        mix = _dot(y_ref[...], wo_ref[...])
    x1 = _layer_norm(DN_ALPHA * x_ref[...] + mix, g_ref[...], b_ref[...])
    x1_ref[...] = x1
    x1b_ref[...] = x1.astype(BF16)
    wf, cnt = _router(x1, wr_ref[...], br_ref[...])
    wf_ref[...] = wf
    cnt_ref[0] = jnp.broadcast_to(cnt, cnt_ref.shape[1:])


def _mix_out(x2, ys, wo, g, b, wr, br, gm=None):
    tm = TM_PROJ
    row = lambda n: pl.BlockSpec((tm, n), lambda i: (i, 0))
    full = lambda a: pl.BlockSpec(a.shape, lambda i: (0,) * a.ndim)
    if gm is not None:
        gu, vn, ws, bs = gm
        args = (x2, gu, vn, ws, bs, ys, wo, g, b, wr, br)
        in_specs = [row(D_MODEL), row(512), row(512), full(ws), full(bs), row(512), full(wo),
                    full(g), full(b), full(wr), full(br)]
        scratch = [pltpu.VMEM((tm, GM_WIDTH), BF16)]
    else:
        args = (x2, ys, wo, g, b, wr, br)
        in_specs = [row(D_MODEL), row(C_MIX), full(wo), full(g), full(b), full(wr), full(br)]
        scratch = []
    return pl.pallas_call(
        functools.partial(_mix_out_kernel, gmlp=gm is not None),
        grid=(TOKENS // tm,),
        in_specs=in_specs,
        out_specs=[row(D_MODEL), row(D_MODEL), row(LANES), pl.BlockSpec((1, 8, LANES), lambda i: (i, 0, 0))],
        out_shape=[jax.ShapeDtypeStruct((TOKENS, D_MODEL), F32), jax.ShapeDtypeStruct((TOKENS, D_MODEL), BF16),
                   jax.ShapeDtypeStruct((TOKENS, LANES), F32),
                   jax.ShapeDtypeStruct((TOKENS // tm, 8, LANES), F32)],
        scratch_shapes=scratch,
        compiler_params=pltpu.CompilerParams(dimension_semantics=("arbitrary",)),
        name="mix_out_gmlp" if gm is not None else "mix_out",
    )(*args)


def _group_dest(wf, ltri_ref, ustr_ref):
    tm = wf.shape[0]
    lane = lax.broadcasted_iota(jnp.int32, (tm, LANES), 1).astype(F32)
    onehot = jnp.where(lane == wf[:, GSEL_LANE:GSEL_LANE + 1], 1.0, 0.0)
    before = _dot(ltri_ref[...], onehot.astype(BF16))
    cnt = jnp.sum(onehot, axis=0, keepdims=True)
    gran = jnp.floor((cnt + (GRAN - 1)) * (1.0 / GRAN))
    start = _dot(jnp.broadcast_to(gran, (8, LANES)).astype(BF16), ustr_ref[...])[0:1]
    return jnp.sum(onehot * (GRAN * start + before), axis=-1, keepdims=True)


def _granule_copies(tbl_ref, tile, vmem_bufs, hbm_refs, sems, to_hbm, act):
    for g in range(MOE_GROUPS):
        n = tbl_ref[tile * TBL_W + g]
        loc = tbl_ref[tile * TBL_W + MOE_GROUPS + g]
        glb = tbl_ref[tile * TBL_W + 2 * MOE_GROUPS + g]

        def body(k, c, loc=loc, glb=glb):
            lo = pl.multiple_of((loc + k) * GRAN, GRAN)
            hi = pl.multiple_of((glb + k) * GRAN, GRAN)
            for idx, (vb, hb) in enumerate(zip(vmem_bufs, hbm_refs)):
                v_sl, h_sl = vb.at[pl.ds(lo, GRAN)], hb.at[pl.ds(hi, GRAN)]
                src, dst = (v_sl, h_sl) if to_hbm else (h_sl, v_sl)
                act(pltpu.make_async_copy(src, dst, sems[idx]))
            return c

        lax.fori_loop(0, n, body, 0)


def _zero_fill_copies(tbl_ref, zero_bufs, hbm_refs, sems, act):
    tail = (TOKENS // TM_PROJ) * TBL_W
    for n in range(MOE_GROUPS + 1):
        first = tbl_ref[tail + 2 * n]

        def body(k, c, first=first):
            hi = pl.multiple_of((first + k) * GRAN, GRAN)
            for idx, (zb, hb) in enumerate(zip(zero_bufs, hbm_refs)):
                act(pltpu.make_async_copy(zb, hb.at[pl.ds(hi, GRAN)], sems[idx]))
            return c

        lax.fori_loop(0, tbl_ref[tail + 2 * n + 1], body, 0)


def _dispatch_kernel(tbl_ref, xb_ref, wf_ref, ltri_ref, ustr_ref, xs_out, ws_out, xbuf, wbuf, zx, zw, sems):
    tile = pl.program_id(0)
    slot = tile % 2
    tm = xb_ref.shape[0]
    wf = wf_ref[...]
    dest = _group_dest(wf, ltri_ref, ustr_ref)
    dest_row = jnp.broadcast_to(dest, (tm, LANES)).T[0:1]
    r = lax.broadcasted_iota(jnp.int32, (ROWS_LOCAL, tm), 0).astype(F32)
    perm = jnp.where(r == dest_row, 1.0, 0.0).astype(BF16)
    xbuf[slot] = _dot(perm, xb_ref[...]).astype(BF16)
    hi = wf.astype(BF16)
    r1 = wf - hi.astype(F32)
    mid = r1.astype(BF16)
    lo = (r1 - mid.astype(F32)).astype(BF16)
    pieces = _dot(perm, jnp.concatenate([hi, mid, lo], axis=1))
    wbuf[slot] = pieces[:, :LANES] + pieces[:, LANES:2 * LANES] + pieces[:, 2 * LANES:]

    def copies(t, s, act):
        _granule_copies(tbl_ref, t, (xbuf.at[s], wbuf.at[s]), (xs_out, ws_out), (sems.at[s, 0], sems.at[s, 1]),
                        True, act)

    copies(tile, slot, lambda c: c.start())

    @pl.when(tile > 0)
    def _():
        copies(tile - 1, 1 - slot, lambda c: c.wait())

    @pl.when(tile == pl.num_programs(0) - 1)
    def _():
        zx[...] = jnp.zeros_like(zx)
        zw[...] = jnp.zeros_like(zw)
        fill = functools.partial(_zero_fill_copies, tbl_ref, (zx, zw), (xs_out, ws_out), (sems.at[2, 0], sems.at[2, 1]))
        fill(lambda c: c.start())
        copies(tile, slot, lambda c: c.wait())
        fill(lambda c: c.wait())


def _dispatch(tbl, x1b, wf, ltri, ustr):
    tm = TM_PROJ
    row = lambda n: pl.BlockSpec((tm, n), lambda i, t: (i, 0))
    full = lambda a: pl.BlockSpec(a.shape, lambda i, t: (0,) * a.ndim)
    anyspace = pl.BlockSpec(memory_space=pl.ANY)
    return pl.pallas_call(
        _dispatch_kernel,
        grid_spec=pltpu.PrefetchScalarGridSpec(
            num_scalar_prefetch=1, grid=(TOKENS // tm,),
            in_specs=[row(D_MODEL), row(LANES), full(ltri), full(ustr)],
            out_specs=[anyspace, anyspace],
            scratch_shapes=[pltpu.VMEM((2, ROWS_LOCAL, D_MODEL), BF16), pltpu.VMEM((2, ROWS_LOCAL, LANES), F32),
                            pltpu.VMEM((GRAN, D_MODEL), BF16), pltpu.VMEM((GRAN, LANES), F32),
                            pltpu.SemaphoreType.DMA((3, 2))]),
        out_shape=[jax.ShapeDtypeStruct((ROWS_SORTED, D_MODEL), BF16),
                   jax.ShapeDtypeStruct((ROWS_SORTED, LANES), F32)],
        compiler_params=pltpu.CompilerParams(dimension_semantics=("arbitrary",)),
        name="moe_dispatch",
    )(tbl, x1b, wf, ltri, ustr)


def _experts_kernel(gid_ref, valid_ref, xs_ref, ws_ref, wg_ref, wu_ref, wd_ref, y_ref):
    i = pl.program_id(0)

    @pl.when(valid_ref[i] == 1)
    def _():
        x = xs_ref[...]
        gates = [_dot(x, wg_ref[e]) for e in range(MOE_EPG)]
        ups = [_dot(x, wu_ref[e]) for e in range(MOE_EPG)]
        ws = ws_ref[...]
        lane = lax.broadcasted_iota(jnp.int32, ws.shape, 1)
        hidden = []
        for e in range(MOE_EPG):
            w_tok = jnp.sum(jnp.where(lane == gid_ref[i] * MOE_EPG + e, ws, 0.0), axis=-1, keepdims=True)
            hidden.append((gates[e] * jax.nn.sigmoid(gates[e]) * ups[e] * w_tok).astype(BF16))
        wd = wd_ref[...].reshape(MOE_EPG * MOE_HIDDEN, D_MODEL)
        y_ref[...] = _dot(jnp.concatenate(hidden, axis=1), wd).astype(BF16)

    @pl.when(valid_ref[i] == 0)
    def _():
        y_ref[...] = jnp.zeros_like(y_ref)


def _experts(gid, valid, xs, ws, wg, wu, wd):
    tm = TM_MOE
    row = lambda n: pl.BlockSpec((tm, n), lambda i, gid, valid: (i, 0))
    wspec = lambda a, b: pl.BlockSpec((MOE_EPG, a, b), lambda i, gid, valid: (gid[i], 0, 0))
    return pl.pallas_call(
        _experts_kernel,
        grid_spec=pltpu.PrefetchScalarGridSpec(
            num_scalar_prefetch=2, grid=(ROWS_SORTED // tm,),
            in_specs=[row(D_MODEL), row(LANES), wspec(D_MODEL, MOE_HIDDEN), wspec(D_MODEL, MOE_HIDDEN),
                      wspec(MOE_HIDDEN, D_MODEL)],
            out_specs=row(D_MODEL)),
        out_shape=jax.ShapeDtypeStruct((ROWS_SORTED, D_MODEL), BF16),
        compiler_params=pltpu.CompilerParams(dimension_semantics=("arbitrary",),
                                             vmem_limit_bytes=EXPERTS_VMEM_BYTES),
        name="moe_experts",
    )(gid, valid, xs, ws, wg, wu, wd)


def _combine_kernel(tbl_ref, wf_ref, x1_ref, ltri_ref, ustr_ref, g_ref, b_ref, y_hbm, o_ref, ybuf, sems):
    tile = pl.program_id(0)
    slot = tile % 2
    tm = x1_ref.shape[0]

    def copies(t, s, act):
        _granule_copies(tbl_ref, t, (ybuf.at[s],), (y_hbm,), (sems.at[s],), False, act)

    def fetch(t, s):
        ybuf[s] = jnp.zeros(ybuf.shape[1:], ybuf.dtype)
        copies(t, s, lambda c: c.start())

    @pl.when(tile == 0)
    def _():
        fetch(tile, slot)

    @pl.when(tile + 1 < pl.num_programs(0))
    def _():
        fetch(tile + 1, 1 - slot)

    dest = _group_dest(wf_ref[...], ltri_ref, ustr_ref)
    c = lax.broadcasted_iota(jnp.int32, (tm, ROWS_LOCAL), 1).astype(F32)
    unperm = jnp.where(c == dest, 1.0, 0.0).astype(BF16)
    copies(tile, slot, lambda c: c.wait())
    y_sorted = ybuf[slot]
    halves = (slice(0, tm // 2), slice(tm // 2, tm))
    ffn = [_dot(unperm[rows], y_sorted) for rows in halves]
    for rows, part in zip(halves, ffn):
        o_ref[rows, :] = _layer_norm(DN_ALPHA * x1_ref[rows, :] + part, g_ref[...], b_ref[...])


def _combine(tbl, wf, x1, ltri, ustr, g, b, y):
    tm = TM_PROJ
    row = lambda n: pl.BlockSpec((tm, n), lambda i, t: (i, 0))
    full = lambda a: pl.BlockSpec(a.shape, lambda i, t: (0,) * a.ndim)
    return pl.pallas_call(
        _combine_kernel,
        grid_spec=pltpu.PrefetchScalarGridSpec(
            num_scalar_prefetch=1, grid=(TOKENS // tm,),
            in_specs=[row(LANES), row(D_MODEL), full(ltri), full(ustr), full(g), full(b),
                      pl.BlockSpec(memory_space=pl.ANY)],
            out_specs=row(D_MODEL),
            scratch_shapes=[pltpu.VMEM((2, ROWS_LOCAL, D_MODEL), BF16), pltpu.SemaphoreType.DMA((2,))]),
        out_shape=jax.ShapeDtypeStruct((TOKENS, D_MODEL), F32),
        compiler_params=pltpu.CompilerParams(dimension_semantics=("arbitrary",)),
        name="moe_combine",
    )(tbl, wf, x1, ltri, ustr, g, b, y)


def _routing_tables(cnt):
    n_tiles = cnt.shape[0]
    c = cnt[:, 0, :MOE_GROUPS].astype(jnp.int32)
    gran = (c + GRAN - 1) // GRAN
    local = jnp.cumsum(gran, axis=1) - gran
    per_tile = TM_MOE // GRAN
    tiles_g = (jnp.sum(gran, axis=0) + per_tile - 1) // per_tile
    ends = jnp.cumsum(tiles_g)
    base = (ends - tiles_g) * per_tile
    glob = base[None, :] + jnp.cumsum(gran, axis=0) - gran
    used = jnp.sum(gran, axis=0)
    pad_first = jnp.concatenate([base + used, ends[-1:] * per_tile])
    pad_count = jnp.concatenate([tiles_g * per_tile - used, ROWS_SORTED // GRAN - ends[-1:] * per_tile])
    tail = jnp.stack([pad_first, pad_count], axis=1).reshape(-1)
    tbl = jnp.concatenate([jnp.concatenate([gran, local, glob], axis=1).reshape(n_tiles * TBL_W), tail])
    idx = jnp.arange(ROWS_SORTED // TM_MOE)
    gid = jnp.minimum(jnp.sum(idx[:, None] >= ends[None, :], axis=1), MOE_GROUPS - 1).astype(jnp.int32)
    valid = (idx < ends[-1]).astype(jnp.int32)
    return tbl, gid, valid


def _sort_tables():
    t = jnp.arange(TM_PROJ)
    ltri = (t[None, :] < t[:, None]).astype(BF16)
    l = jnp.arange(LANES)
    ustr = (l[:, None] < l[None, :]).astype(BF16)
    return ltri, ustr


def _dup_halves(t):
    lane = lax.broadcasted_iota(jnp.int32, t.shape, 1)
    r = pltpu.roll(t, HALF, 1)
    return jnp.where(lane < HALF, t, r), jnp.where(lane < HALF, r, t)


def _nsa_in_kernel(x_ref, w_ref, gb_ref, q_ref, kc_ref, vc_ref, ks_ref, vst_ref, kw_ref, vwt_ref, gate_ref):
    h = _dot(x_ref[...].astype(BF16), w_ref[...])
    q_ref[...] = (h[:, 0:C_MIX] * (NSA_SCALE * LOG2E)).astype(BF16)
    kc_ref[...] = h[:, 1024:1152].astype(BF16)
    vc_ref[...] = h[:, 1152:1280].astype(BF16)
    tm = h.shape[0]
    lane = lax.broadcasted_iota(jnp.int32, (tm, LANES), 1)
    for g, d in enumerate(_dup_halves(h[:, 1536:1664])):
        kw_ref[:, g * LANES:(g + 1) * LANES] = d.astype(BF16)
    pos = (pl.program_id(0) * tm) % SEQ + lax.broadcasted_iota(jnp.int32, (tm, LANES), 0)
    block_onehot = jnp.where(lane == pos // NSA_SEL_LEN, 1.0, 0.0).astype(BF16)
    for g, d in enumerate(_dup_halves(h[:, 1280:1408])):
        ks_ref[:, 2 * g * LANES:(2 * g + 1) * LANES] = d.astype(BF16)
        ks_ref[:, (2 * g + 1) * LANES:(2 * g + 2) * LANES] = block_onehot
    for idx, ref in ((1, vst_ref), (3, vwt_ref)):
        tail = jnp.where(lane == HALF, 1.0, 0.0)
        _store_transposed(ref, [jnp.where(lane < HALF, d, tail)
                                for d in _dup_halves(h[:, 1280 + idx * LANES:1280 + (idx + 1) * LANES])])
    for g in range(NSA_GROUPS):
        gate_ref[g] = jax.nn.sigmoid(h[:, 1792 + g * LANES:1792 + (g + 1) * LANES] + gb_ref[g])


def _nsa_in(x2, w, gb):
    tm = TM_PROJ
    row = lambda n: pl.BlockSpec((tm, n), lambda i: (i, 0))
    full = lambda a: pl.BlockSpec(a.shape, lambda i: (0,) * a.ndim)
    sd = jax.ShapeDtypeStruct
    vt_spec = lambda w: pl.BlockSpec((NSA_GROUPS, tm // w, LANES, w), lambda i: (0, i, 0, 0))
    vt_shape = lambda w: sd((NSA_GROUPS, TOKENS // w, LANES, w), BF16)
    return pl.pallas_call(
        _nsa_in_kernel,
        grid=(TOKENS // tm,),
        in_specs=[row(D_MODEL), full(w), full(gb)],
        out_specs=[row(C_MIX), row(LANES), row(LANES), row(4 * LANES), vt_spec(TK_ATT), row(2 * LANES),
                   vt_spec(WIN_SUB), pl.BlockSpec((NSA_GROUPS, tm, LANES), lambda i: (0, i, 0))],
        out_shape=[sd((TOKENS, C_MIX), BF16), sd((TOKENS, LANES), BF16), sd((TOKENS, LANES), BF16),
                   sd((TOKENS, 4 * LANES), BF16), vt_shape(TK_ATT), sd((TOKENS, 2 * LANES), BF16), vt_shape(WIN_SUB),
                   sd((NSA_GROUPS, TOKENS, LANES), F32)],
        compiler_params=pltpu.CompilerParams(dimension_semantics=("arbitrary",)),
        name="nsa_in",
    )(x2, w, gb)


def _compress_kernel(kc_ref, vc_ref, pk_ref, pv_ref, wk1_ref, wv1_ref, wk2_ref, wv2_ref, ko_ref, vo_ref):
    for a_ref, p_ref, w1_ref, w2_ref, o_ref in ((kc_ref, pk_ref, wk1_ref, wk2_ref, ko_ref),
                                                (vc_ref, pv_ref, wv1_ref, wv2_ref, vo_ref)):
        a = a_ref[0].astype(F32)
        a0 = (a + p_ref[0]).astype(BF16)
        a1 = (a + p_ref[1]).astype(BF16)
        outs = []
        for g in range(NSA_GROUPS):
            first = _dot(a0, w1_ref[g])
            second = _dot(a1, w1_ref[NSA_GROUPS + g])
            hid = first + pltpu.roll(second, NSA_NCMP - 1, 0)
            outs.append(_dot(_gelu(hid).astype(BF16), w2_ref[...]))
        if o_ref is ko_ref:
            o_ref[0] = jnp.concatenate(outs, axis=1).astype(BF16)
        else:
            for g in range(NSA_GROUPS):
                o_ref[0, g] = outs[g].T.astype(BF16)


def _compress(kc_r, vc_r, pk, pv, wk1, wv1, wk2, wv2):
    blk = pl.BlockSpec((1, NSA_NCMP, NSA_CMP_STRIDE * LANES), lambda b: (b, 0, 0))
    full = lambda a: pl.BlockSpec(a.shape, lambda b: (0,) * a.ndim)
    sd = jax.ShapeDtypeStruct
    return pl.pallas_call(
        _compress_kernel,
        grid=(BATCH,),
        in_specs=[blk, blk, full(pk), full(pv), full(wk1), full(wv1), full(wk2), full(wv2)],
        out_specs=[pl.BlockSpec((1, NSA_NCMP, 2 * LANES), lambda b: (b, 0, 0)),
                   pl.BlockSpec((1, NSA_GROUPS, LANES, NSA_NCMP), lambda b: (b, 0, 0, 0))],
        out_shape=[sd((BATCH, NSA_NCMP, 2 * LANES), BF16), sd((BATCH, NSA_GROUPS, LANES, NSA_NCMP), BF16)],
        compiler_params=pltpu.CompilerParams(dimension_semantics=("arbitrary",)),
        name="nsa_compress",
    )(kc_r, vc_r, pk, pv, wk1, wv1, wk2, wv2)


def _nsa_attn_kernel(q_ref, kc_ref, vct_ref, ks_ref, vst_ref, kw_ref, vwt_ref, gate_ref, cover_ref,
                     o_ref, m_ref, acc_ref, sa_ref, sb_ref):
    tq, tk, hpg = TQ_NSA, TK_ATT, NSA_HPG
    qi = pl.program_id(2)
    q0 = qi * tq
    lane = lax.broadcasted_iota(jnp.int32, (tq, LANES), 1)
    t_tok = q0 + lax.broadcasted_iota(jnp.int32, (1, tq), 1)
    head = lambda x, i: x[:, i * tq:(i + 1) * tq]
    heads = range(hpg)

    parts = []
    for p in range(hpg // 2):
        qp = q_ref[0, :, p * LANES:(p + 1) * LANES]
        zero = jnp.zeros_like(qp)
        parts.append(jnp.where(lane < HALF, qp, zero))
        parts.append(jnp.where(lane < HALF, zero, qp))
    qs = jnp.concatenate(parts, axis=0)

    kb_sub = lax.broadcasted_iota(jnp.int32, (WIN_BAND, 1), 0)

    def window_scores(part):
        q_lo = q0 + part * WIN_SUB
        start = pl.multiple_of(jnp.maximum(q_lo - NSA_WINDOW, 0), WIN_SUB)
        qs_part = jnp.concatenate([qs[i * tq + part * WIN_SUB:i * tq + (part + 1) * WIN_SUB] for i in heads], axis=0)
        return q_lo, start, _dot_nt(kw_ref[0, pl.ds(start, WIN_BAND), :], qs_part)

    def window_finish(q_lo, start, s_w):
        kpos = start + kb_sub
        t_part = q_lo + lax.broadcasted_iota(jnp.int32, (1, WIN_SUB), 1)
        in_win = (kpos <= t_part) & (kpos > t_part - NSA_WINDOW)
        vw_band = jnp.concatenate([vwt_ref[0, start // WIN_SUB + c] for c in range(WIN_BAND // WIN_SUB)], axis=1)
        out = []
        for i0 in range(0, hpg, 2):
            e_w = []
            for i in (i0, i0 + 1):
                sm = jnp.where(in_win, s_w[:, i * WIN_SUB:(i + 1) * WIN_SUB], NEG)
                e_w.append(jnp.exp2((sm - jnp.max(sm, axis=0, keepdims=True)).astype(BF16)))
            o_pair = _dot(vw_band, jnp.concatenate(e_w, axis=1))
            out += [o_pair[:, :WIN_SUB], o_pair[:, WIN_SUB:]]
        return out

    s_c = _dot_nt(kc_ref[0], qs)
    win0 = window_scores(0)
    n_sub = lax.broadcasted_iota(jnp.int32, (NSA_NCMP, 1), 0)
    vis = t_tok >= n_sub * NSA_CMP_STRIDE + (NSA_CMP_LEN - 1)
    sees_any = t_tok >= NSA_CMP_LEN - 1
    p_sum = jnp.zeros((NSA_NCMP, tq), F32)
    p_c = []
    for i in heads:
        sm = jnp.where(vis, head(s_c, i), NEG)
        e = jnp.exp2(sm - jnp.max(sm, axis=0, keepdims=True))
        p = e * jnp.where(sees_any, 1.0 / jnp.sum(e, axis=0, keepdims=True), 0.0)
        p_sum = p_sum + p
        p_c.append(p.astype(BF16))
    o_c = _dot(vct_ref[0, 0], jnp.concatenate(p_c, axis=1))

    imp = jnp.dot(cover_ref[...], p_sum, preferred_element_type=F32,
                  precision=lax.Precision.HIGHEST)[0:NSA_NSEL]
    o_w_parts = [window_finish(*win0)]
    o_w_parts += [window_finish(*window_scores(part)) for part in range(1, tq // WIN_SUB)]
    jj = lax.broadcasted_iota(jnp.int32, (NSA_NSEL, 1), 0)
    tb = t_tok // NSA_SEL_LEN
    forced = (jj == 0) | (jj == tb) | (jj == tb - 1)
    score = jnp.where(forced, NSA_FORCE, jnp.where(jj <= tb, imp, -NSA_FORCE))
    sub = 8
    rows = [score[b * sub:(b + 1) * sub] for b in range(NSA_NSEL // sub)]
    ranks = [jnp.zeros((sub, tq), jnp.int32) for _ in rows]
    j_in = lax.broadcasted_iota(jnp.int32, (sub, 1), 0)
    for i in range(NSA_NSEL):
        si = score[i:i + 1, :]
        for b, blk in enumerate(rows):
            if b < i // sub:
                beats = si > blk
            elif b > i // sub:
                beats = si >= blk
            else:
                beats = (si > blk) | ((si == blk) & (j_in > i % sub))
            ranks[b] = ranks[b] + beats.astype(jnp.int32)
    rank = jnp.concatenate(ranks, axis=0)
    sel_bias = jnp.where(rank < NSA_TOPK, 0.0, NEG)
    bias_rows = jnp.concatenate([sel_bias, jnp.zeros((LANES - NSA_NSEL, tq), F32)], axis=0).T.astype(BF16)
    qs_sel = jnp.concatenate([qs, jnp.concatenate([bias_rows] * hpg, axis=0)], axis=1)

    m_ref[...] = jnp.full(m_ref.shape, NEG, F32)
    acc_ref[...] = jnp.zeros(acc_ref.shape, F32)
    k_sub = lax.broadcasted_iota(jnp.int32, (tk, 1), 0)
    diag = q0 // tk
    causal = diag * tk + k_sub <= t_tok
    per_dot = max(1, 2 * LANES // tq)

    def scores_into(ref, kt):
        ref[...] = _dot_nt(ks_ref[0, pl.ds(pl.multiple_of(kt * tk, tk), tk), :], qs_sel)

    def process(ref, kt, diagonal):
        s = ref[...]
        vt = vst_ref[0, kt]
        for i0 in range(0, hpg, per_dot):
            cols = slice(i0 * tq, (i0 + per_dot) * tq)
            probs, alphas = [], []
            for i in range(i0, i0 + per_dot):
                for part in range(tq // LANES):
                    sub = slice(part * LANES, (part + 1) * LANES)
                    sm = s[:, i * tq + part * LANES:i * tq + (part + 1) * LANES]
                    if diagonal:
                        sm = jnp.where(causal[:, sub], sm, NEG)
                    m_old = m_ref[i, :, sub]
                    m_new = jnp.maximum(m_old, jnp.max(sm, axis=0, keepdims=True))
                    m_ref[i, :, sub] = m_new
                    probs.append(jnp.exp2((sm - m_new).astype(BF16)))
                    alphas.append(jnp.exp2(m_old - m_new))
            pv = _dot(vt, jnp.concatenate(probs, axis=1))
            acc_ref[:, cols] = jnp.concatenate(alphas, axis=1) * acc_ref[:, cols] + pv

    odd = diag % 2
    scores_into(sa_ref, 0)

    @pl.when(odd == 1)
    def _():
        process(sa_ref, 0, False)
        scores_into(sa_ref, 1)

    def pair(k, c):
        t0 = odd + 2 * k
        scores_into(sb_ref, t0 + 1)
        process(sa_ref, t0, False)
        scores_into(sa_ref, t0 + 2)
        process(sb_ref, t0 + 1, False)
        return c

    lax.fori_loop(0, diag // 2, pair, 0)
    process(sa_ref, diag, True)

    o_w =[jnp.concatenate([part[i] for part in o_w_parts], axis=1) for i in heads]

    gt = gate_ref[0, 0].T
    o_s = acc_ref[...]
    outs = []
    for i in heads:
        c_i, s_i, w_i = head(o_c, i), head(o_s, i), o_w[i]
        outs.append(gt[i:i + 1] * c_i[:HALF]
                    + gt[hpg + i:hpg + i + 1] / s_i[HALF:HALF + 1] * s_i[:HALF]
                    + gt[2 * hpg + i:2 * hpg + i + 1] / w_i[HALF:HALF + 1] * w_i[:HALF])
    for p in range(hpg // 2):
        pair = jnp.concatenate([outs[2 * p], outs[2 * p + 1]], axis=0)
        o_ref[0, :, p * LANES:(p + 1) * LANES] = pair.T.astype(BF16)


def _nsa_attn(q3, kc2, vct, ks3, vst, kw3, vwt, gates4, cover_t):
    tq = TQ_NSA
    half_w = NSA_HPG * NSA_DH
    kv = pl.BlockSpec((1, SEQ, LANES), lambda b, g, i: (b, 0, g))
    vt = lambda w: pl.BlockSpec((1, SEQ // w, LANES, w), lambda b, g, i: (g, b, 0, 0))
    full = lambda a: pl.BlockSpec(a.shape, lambda b, g, i: (0,) * a.ndim)
    return pl.pallas_call(
        _nsa_attn_kernel,
        grid=(BATCH, NSA_GROUPS, SEQ // tq),
        in_specs=[pl.BlockSpec((1, tq, half_w), lambda b, g, i: (b, i, g)),
                  pl.BlockSpec((1, NSA_NCMP, LANES), lambda b, g, i: (b, 0, g)),
                  pl.BlockSpec((1, 1, LANES, NSA_NCMP), lambda b, g, i: (b, g, 0, 0)),
                  pl.BlockSpec((1, SEQ, 2 * LANES), lambda b, g, i: (b, 0, g)), vt(TK_ATT), kv, vt(WIN_SUB),
                  pl.BlockSpec((1, 1, tq, LANES), lambda b, g, i: (g, b, i, 0)),
                  full(cover_t)],
        out_specs=pl.BlockSpec((1, tq, half_w), lambda b, g, i: (b, i, g)),
        out_shape=jax.ShapeDtypeStruct((BATCH, SEQ, C_MIX), BF16),
        scratch_shapes=[pltpu.VMEM((NSA_HPG, 1, tq), F32),
                        pltpu.VMEM((LANES, NSA_HPG * tq), F32),
                        pltpu.VMEM((TK_ATT, NSA_HPG * tq), F32), pltpu.VMEM((TK_ATT, NSA_HPG * tq), F32)],
        compiler_params=pltpu.CompilerParams(dimension_semantics=("arbitrary",) * 3),
        name="nsa_attn",
    )(q3, kc2, vct, ks3, vst, kw3, vwt, gates4, cover_t)


def _rope_tables():
    half = MLA_ROPE // 2
    freq = jnp.exp(-math.log(ROPE_BASE) * jnp.arange(half, dtype=F32) / half)
    return freq[:, None]


def _swap_halves(w):
    half = w.shape[-1] // 2
    return jnp.concatenate([w[..., half:], w[..., :half]], axis=-1)


def _pad_last(w, n):
    return jnp.pad(w, [(0, 0)] * (w.ndim - 1) + [(0, n - w.shape[-1])])


def _ab_weights(w_in, w_uq, w_uk, w_uv):
    w_kr = w_in[:, 1408:1440]
    place = lambda w: jnp.pad(w, ((0, 0), (MLA_NOPE, LANES - MLA_NOPE - MLA_ROPE)))
    win = jnp.concatenate([w_in[:, :1408], place(w_kr), place(_swap_halves(w_kr))], axis=1).astype(BF16)
    uq = w_uq.reshape(MLA_Q_RANK, MLA_HEADS, MLA_NOPE + MLA_ROPE)
    nope, rp = uq[..., :MLA_NOPE], uq[..., MLA_NOPE:]
    q_pad = _pad_last(jnp.concatenate([nope, rp], -1), LANES).reshape(MLA_Q_RANK, MLA_HEADS * LANES)
    q_sw = _pad_last(jnp.concatenate([jnp.zeros_like(nope), _swap_halves(rp)], -1), LANES)
    wq = jnp.concatenate([q_pad, q_sw.reshape(MLA_Q_RANK, MLA_HEADS * LANES)], axis=1).astype(BF16)
    k_pad = _pad_last(w_uk.reshape(MLA_KV_RANK, MLA_HEADS, MLA_NOPE), LANES).reshape(MLA_KV_RANK, -1)
    wkv = jnp.concatenate([k_pad, w_uv], axis=1).astype(BF16)
    return win, wq, wkv


def _router_weights(w_rg, b_rg, w_re, b_re):
    wr = _pad_last(jnp.concatenate([w_re, w_rg], axis=1), LANES)
    wr_hi = wr.astype(BF16)
    wr_lo = (wr - wr_hi.astype(F32)).astype(BF16)
    br = _pad_last(jnp.concatenate([b_re, b_rg])[None, :], LANES)
    return jnp.concatenate([wr_hi, wr_lo], axis=1), br


def _nsa_in_weights(w_in, gate_b):
    g_cols = w_in[:, C_MIX + 768:].reshape(D_MODEL, 3, NSA_GROUPS, NSA_HPG)
    g_blocks = [_pad_last(g_cols[:, :, g, :].reshape(D_MODEL, 3 * NSA_HPG), LANES) for g in range(NSA_GROUPS)]
    w = jnp.concatenate([w_in[:, :C_MIX + 768]] + g_blocks, axis=1).astype(BF16)
    gb = gate_b.reshape(3, NSA_GROUPS, NSA_HPG)
    gb = jnp.stack([_pad_last(gb[:, g, :].reshape(1, 3 * NSA_HPG), LANES) for g in range(NSA_GROUPS)])
    return w, gb


def _compress_weights(pos, w1, w2):
    w1r = w1.reshape(2, NSA_CMP_STRIDE, NSA_DH, NSA_CMP_HIDDEN)
    zero = jnp.zeros_like(w1r)
    per_g = []
    for g in range(NSA_GROUPS):
        parts = [w1r if gg == g else zero for gg in range(NSA_GROUPS)]
        per_g.append(jnp.stack(parts, axis=2).reshape(2, NSA_CMP_STRIDE * LANES, NSA_CMP_HIDDEN))
    w1x = jnp.stack(per_g, axis=1).reshape(2 * NSA_GROUPS, NSA_CMP_STRIDE * LANES, NSA_CMP_HIDDEN)
    posr = pos.reshape(2, NSA_CMP_STRIDE, 1, NSA_DH)
    posx = jnp.broadcast_to(posr, (2, NSA_CMP_STRIDE, NSA_GROUPS, NSA_DH)).reshape(2, 1, NSA_CMP_STRIDE * LANES)
    w2x = jnp.concatenate([w2, w2], axis=1)
    return posx, w1x.astype(BF16), w2x.astype(BF16)


def _selection_tables():
    n = jnp.arange(LANES)[:, None]
    j = jnp.arange(LANES)[None, :]
    c0 = n * NSA_CMP_STRIDE
    s0 = j * NSA_SEL_LEN
    cover = ((c0 < s0 + NSA_SEL_LEN) & (c0 + NSA_CMP_LEN > s0) & (n < NSA_NCMP - 1) & (j < NSA_NSEL))
    return jnp.transpose(cover).astype(F32)


def kernel(x, positions, ab_w_in, ab_gm_ln_g, ab_gm_ln_b, ab_gm_ws, ab_gm_bs, ab_mla_q_norm,
           ab_mla_kv_norm, ab_mla_w_uq, ab_mla_w_uk, ab_mla_w_uv, ab_w_o, c_w_in, c_cmp_pos, c_w_ck1,
           c_w_ck2, c_w_cv1, c_w_cv2, c_gate_b, c_w_o, moe_w_rg, moe_b_rg, moe_w_re, moe_b_re,
           moe_w_gate, moe_w_up, moe_w_down, ln1_g, ln1_b, ln2_g, ln2_b):
    x2 = x.reshape(TOKENS, D_MODEL)
    pos3 = positions.reshape(TOKENS // TM_PROJ, 1, TM_PROJ)
    vec = lambda a: a[None, :]

    ltri, ustr = _sort_tables()

    def moe_layer(layer, x1b, x1, wf, cnt):
        tbl, gid, valid = _routing_tables(cnt)
        xs, ws = _dispatch(tbl, x1b, wf, ltri, ustr)
        y = _experts(gid, valid, xs, ws, moe_w_gate[layer].astype(BF16), moe_w_up[layer].astype(BF16),
                     moe_w_down[layer].astype(BF16))
        return _combine(tbl, wf, x1, ltri, ustr, vec(ln2_g[layer]), vec(ln2_b[layer]), y)

    win, wq, wkv = _ab_weights(ab_w_in[0], ab_mla_w_uq[0], ab_mla_w_uk[0], ab_mla_w_uv[0])
    gu, vn, q, k, vt = _ab_in(x2, pos3, win, vec(ab_gm_ln_g[0]), vec(ab_gm_ln_b[0]), vec(ab_mla_q_norm[0]),
                              vec(ab_mla_kv_norm[0]), wq, wkv, _rope_tables())
    yb = _mla_attn(q.reshape(BATCH, SEQ, -1), k.reshape(BATCH, SEQ, -1), vt)
    wr, br = _router_weights(moe_w_rg[0], moe_b_rg[0], moe_w_re[0], moe_b_re[0])
    x1, x1b, wf, cnt = _mix_out(x2, yb.reshape(TOKENS, -1), ab_w_o[0].astype(BF16), vec(ln1_g[0]),
                                vec(ln1_b[0]), wr, br, gm=(gu, vn, ab_gm_ws[0], jnp.transpose(ab_gm_bs[0])))
    x2 = moe_layer(0, x1b, x1, wf, cnt)

    w_nsa, gb = _nsa_in_weights(c_w_in[0], c_gate_b[0])
    q, kc, vc, ks, vst, kw, vwt, gates = _nsa_in(x2, w_nsa, gb)
    pk, wk1, wk2 = _compress_weights(c_cmp_pos[0, 0], c_w_ck1[0], c_w_ck2[0])
    pv, wv1, wv2 = _compress_weights(c_cmp_pos[0, 1], c_w_cv1[0], c_w_cv2[0])
    blocks = lambda a: a.reshape(BATCH, NSA_NCMP, NSA_CMP_STRIDE * LANES)
    kc2, vct = _compress(blocks(kc), blocks(vc), pk, pv, wk1, wv1, wk2, wv2)
    b3 = lambda a: a.reshape(BATCH, SEQ, -1)
    o = _nsa_attn(b3(q), kc2, vct, b3(ks), vst, b3(kw), vwt,
                  gates.reshape(NSA_GROUPS, BATCH, SEQ, LANES), _selection_tables())
    wr, br = _router_weights(moe_w_rg[1], moe_b_rg[1], moe_w_re[1], moe_b_re[1])
    x1, x1b, wf, cnt = _mix_out(x2, o.reshape(TOKENS, -1), c_w_o[0].astype(BF16), vec(ln1_g[1]), vec(ln1_b[1]),
                                wr, br)
    x2 = moe_layer(1, x1b, x1, wf, cnt)
    return x2.reshape(BATCH, SEQ, D_MODEL)
```

```python
import functools
import math

import jax
import jax.numpy as jnp
from jax import lax
from jax.experimental import pallas as pl
from jax.experimental.pallas import tpu as pltpu

F32 = jnp.float32
BF16 = jnp.bfloat16

D_MODEL = 1024
BATCH = 16
SEQ = 2048
TOKENS = BATCH * SEQ
DEPTH = 2
DN_ALPHA = (2.0 * DEPTH) ** 0.25
LN_EPS = 1e-5
NEG = -1e30
LOG2E = math.log2(math.e)
LANES = 128
HALF = LANES // 2

GM_WIDTH = 512
GM_GROUPS = 4
GM_CHUNK = 128

MLA_HEADS = 8
MLA_NOPE = 64
MLA_ROPE = 32
MLA_V = 64
MLA_Q_RANK = 256
MLA_KV_RANK = 128
ROPE_BASE = 10000.0
MLA_SCALE = (MLA_NOPE + MLA_ROPE) ** -0.5

NSA_HEADS = 16
NSA_GROUPS = 2
NSA_HPG = 8
NSA_DH = 64
NSA_CMP_LEN = 32
NSA_CMP_STRIDE = 16
NSA_CMP_HIDDEN = 256
NSA_SEL_LEN = 64
NSA_TOPK = 8
NSA_WINDOW = 512
NSA_NSEL = SEQ // NSA_SEL_LEN
NSA_NCMP = SEQ // NSA_CMP_STRIDE
NSA_FORCE = 1e4
NSA_SCALE = NSA_DH ** -0.5
C_MIX = NSA_HEADS * NSA_DH

MOE_GROUPS = 4
MOE_EPG = 8
MOE_EXPERTS = 32
MOE_HIDDEN = 256

TM_PROJ = 512
TQ_MLA = 256
TQ_NSA = 256
TK_ATT = 256
WIN_SUB = LANES
WIN_BAND = NSA_WINDOW + WIN_SUB
MLA_VT_ROWS = LANES + 16
TM_MOE = 512
EXPERTS_VMEM_BYTES = (2 * 3 * MOE_EPG * D_MODEL * MOE_HIDDEN * 2 + 2 * 2 * TM_MOE * D_MODEL * 2
                      + 2 * MOE_EPG * TM_MOE * MOE_HIDDEN * 4 + TM_MOE * MOE_EPG * MOE_HIDDEN * 2
                      + 2 * TM_MOE * D_MODEL * 4)

GSEL_LANE = MOE_EXPERTS
GRAN = 16
TBL_W = 3 * MOE_GROUPS
ROWS_LOCAL = 640
ROWS_SORTED = TM_MOE * (TOKENS // TM_MOE + MOE_GROUPS
                        + -(-(TOKENS // TM_PROJ) * MOE_GROUPS * (GRAN - 1) // TM_MOE))


def _dot(a, b):
    return jnp.dot(a, b, preferred_element_type=F32)


def _dot_nt(a, b):
    return lax.dot_general(a, b, (((1,), (1,)), ((), ())), preferred_element_type=F32)


def _gelu(x):
    return 0.5 * x * (1.0 + jnp.tanh(math.sqrt(2.0 / math.pi) * (x + 0.044715 * (x * x * x))))


def _layer_norm(x, g, b):
    mu = jnp.mean(x, axis=-1, keepdims=True)
    xc = x - mu
    var = jnp.mean(xc * xc, axis=-1, keepdims=True)
    return xc * lax.rsqrt(var + LN_EPS) * g + b


def _rms_norm(x, g):
    return x * lax.rsqrt(jnp.mean(x * x, axis=-1, keepdims=True) + LN_EPS) * g


def _store_transposed(ref, blocks):
    extra, width = ref.shape[2] - LANES, ref.shape[3]
    if extra:
        ones_rows = jnp.where(lax.broadcasted_iota(jnp.int32, (extra, width), 0) == 0, 1.0, 0.0)
    for n, blk in enumerate(blocks):
        t = blk.T
        for c in range(t.shape[1] // width):
            chunk = t[:, c * width:(c + 1) * width]
            if extra:
                chunk = jnp.concatenate([chunk, ones_rows], axis=0)
            ref[n, c] = chunk.astype(ref.dtype)


def _ab_in_kernel(x_ref, pos_ref, win_ref, lng_ref, lnb_ref, qg_ref, kvg_ref, wq_ref, wkv_ref,
                  fc_ref, gu_ref, vn_ref, q_ref, k_ref, vt_ref):
    h = _dot(x_ref[...].astype(BF16), win_ref[...])
    gu_ref[...] = _gelu(h[:, 0:512]).astype(BF16)
    vn_ref[...] = _layer_norm(_gelu(h[:, 512:1024]), lng_ref[...], lnb_ref[...]).astype(BF16)

    tm = x_ref.shape[0]
    ang = fc_ref[...] * pos_ref[0].astype(F32)
    cos_t, sin_t = jnp.cos(ang), jnp.sin(ang)
    ones_t, zeros_t = jnp.ones((MLA_NOPE, tm), F32), jnp.zeros((MLA_NOPE, tm), F32)
    pad = LANES - MLA_NOPE - MLA_ROPE
    cc = jnp.concatenate([ones_t, cos_t, cos_t, ones_t[:pad]], axis=0).T
    ss = jnp.concatenate([zeros_t, -sin_t, sin_t, zeros_t[:pad]], axis=0).T

    cqn = _rms_norm(h[:, 1024:1280], qg_ref[...]).astype(BF16)
    qq = _dot(cqn, wq_ref[...])
    for hd in range(MLA_HEADS):
        lo, hi = hd * LANES, (hd + 1) * LANES
        q_ref[:, lo:hi] = ((qq[:, lo:hi] * cc + qq[:, 1024 + lo:1024 + hi] * ss) * (MLA_SCALE * LOG2E)).astype(BF16)

    ckvn = _rms_norm(h[:, 1280:1408], kvg_ref[...]).astype(BF16)
    kv = _dot(ckvn, wkv_ref[...])
    k_rope = h[:, 1408:1536] * cc + h[:, 1536:1664] * ss
    for hd in range(MLA_HEADS):
        lo, hi = hd * LANES, (hd + 1) * LANES
        k_ref[:, lo:hi] = (kv[:, lo:hi] + k_rope).astype(BF16)
    _store_transposed(vt_ref, [kv[:, 1024 + p * LANES:1024 + (p + 1) * LANES] for p in range(MLA_HEADS // 2)])


def _ab_in(x2, pos3, win, lng, lnb, qg, kvg, wq, wkv, fc):
    tm = TM_PROJ
    row = lambda n: pl.BlockSpec((tm, n), lambda i: (i, 0))
    full = lambda a: pl.BlockSpec(a.shape, lambda i: (0,) * a.ndim)
    return pl.pallas_call(
        _ab_in_kernel,
        grid=(TOKENS // tm,),
        in_specs=[row(D_MODEL), pl.BlockSpec((1, 1, tm), lambda i: (i, 0, 0)), full(win), full(lng), full(lnb),
                  full(qg), full(kvg), full(wq), full(wkv), full(fc)],
        out_specs=[row(512), row(512), row(1024), row(1024),
                   pl.BlockSpec((MLA_HEADS // 2, tm // TK_ATT, MLA_VT_ROWS, TK_ATT), lambda i: (0, i, 0, 0))],
        out_shape=[jax.ShapeDtypeStruct((TOKENS, 512), BF16), jax.ShapeDtypeStruct((TOKENS, 512), BF16),
                   jax.ShapeDtypeStruct((TOKENS, 1024), BF16), jax.ShapeDtypeStruct((TOKENS, 1024), BF16),
                   jax.ShapeDtypeStruct((MLA_HEADS // 2, TOKENS // TK_ATT, MLA_VT_ROWS, TK_ATT), BF16)],
        compiler_params=pltpu.CompilerParams(dimension_semantics=("arbitrary",)),
        name="ab_in",
    )(x2, pos3, win, lng, lnb, qg, kvg, wq, wkv, fc)


def _mla_attn_kernel(q_ref, k_ref, vt_ref, o_ref, m_ref, l_ref, acc_ref):
    tq, tk = TQ_MLA, TK_ATT
    qi = pl.program_id(1)
    krow = lax.broadcasted_iota(jnp.int32, (tk, tq), 0)
    qcol = lax.broadcasted_iota(jnp.int32, (tk, tq), 1)
    top = lax.broadcasted_iota(jnp.int32, (LANES, tq), 0) < HALF
    m_ref[...] = jnp.full(m_ref.shape, NEG, F32)
    l_ref[...] = jnp.zeros(l_ref.shape, F32)
    acc_ref[...] = jnp.zeros(acc_ref.shape, F32)

    def tile(j, masked, chunks=1):
        r0 = pl.multiple_of(j * tk, tk)
        scores = [_dot_nt(k_ref[0, pl.ds(r0, chunks * tk), h * LANES:(h + 1) * LANES],
                          q_ref[0, :, h * LANES:(h + 1) * LANES])
                  for h in range(MLA_HEADS)]
        for pr in range(MLA_HEADS // 2):
            probs, alphas = [], []
            for h in (2 * pr, 2 * pr + 1):
                s = jnp.where(krow <= qcol, scores[h], NEG) if masked else scores[h]
                m_old = m_ref[h]
                m_new = jnp.maximum(m_old, jnp.max(s, axis=0, keepdims=True))
                alphas.append(jnp.exp2(m_old - m_new))
                probs.append(jnp.exp2(s - m_new).astype(BF16))
                m_ref[h] = m_new
            vt = jnp.concatenate([vt_ref[pr, j + c] for c in range(chunks)], axis=1)
            pv = _dot(vt, jnp.concatenate(probs, axis=1))
            for n, hh in enumerate((2 * pr, 2 * pr + 1)):
                l_ref[hh] = alphas[n] * l_ref[hh] + pv[LANES:LANES + 1, n * tq:(n + 1) * tq]
            a = jnp.where(top, alphas[0], alphas[1])
            acc_ref[pr] = a * acc_ref[pr] + jnp.where(top, pv[:LANES, :tq], pv[:LANES, tq:])

    def body(j, c):
        tile(2 * j, False, chunks=2)
        return c

    lax.fori_loop(0, qi // 2, body, 0)

    @pl.when(qi % 2 == 1)
    def _():
        tile(qi - 1, False)

    tile(qi, True)
    for pr in range(MLA_HEADS // 2):
        l = jnp.where(top, l_ref[2 * pr], l_ref[2 * pr + 1])
        o_ref[0, :, pr * LANES:(pr + 1) * LANES] = (acc_ref[pr] / l).T.astype(BF16)


def _mla_attn(q3, k3, vt):
    tq = TQ_MLA
    n_chunks = SEQ // TK_ATT
    return pl.pallas_call(
        _mla_attn_kernel,
        grid=(BATCH, SEQ // tq),
        in_specs=[pl.BlockSpec((1, tq, MLA_HEADS * LANES), lambda b, i: (b, i, 0)),
                  pl.BlockSpec((1, SEQ, MLA_HEADS * LANES), lambda b, i: (b, 0, 0)),
                  pl.BlockSpec((MLA_HEADS // 2, n_chunks, MLA_VT_ROWS, TK_ATT), lambda b, i: (0, b, 0, 0))],
        out_specs=pl.BlockSpec((1, tq, MLA_HEADS * MLA_V), lambda b, i: (b, i, 0)),
        out_shape=jax.ShapeDtypeStruct((BATCH, SEQ, MLA_HEADS * MLA_V), BF16),
        scratch_shapes=[pltpu.VMEM((MLA_HEADS, 1, tq), F32), pltpu.VMEM((MLA_HEADS, 1, tq), F32),
                        pltpu.VMEM((MLA_HEADS // 2, LANES, tq), F32)],
        compiler_params=pltpu.CompilerParams(dimension_semantics=("arbitrary",) * 2),
        name="mla_attn",
    )(q3, k3, vt)


def _router(x1, wr, br):
    tm = x1.shape[0]
    x_hi = x1.astype(BF16)
    x_lo = (x1 - x_hi.astype(F32)).astype(BF16)
    parts = _dot(jnp.concatenate([x_hi, x_lo], axis=0), wr)
    logits = (parts[:tm, :LANES] + (parts[:tm, LANES:] + parts[tm:, :LANES]) + parts[tm:, LANES:]) + br
    lane = lax.broadcasted_iota(jnp.int32, (tm, LANES), 1).astype(F32)
    big = 1e6
    is_g = (lane >= MOE_EXPERTS) & (lane < MOE_EXPERTS + MOE_GROUPS)
    gl = jnp.where(is_g, logits, NEG)
    gmax = jnp.max(gl, axis=-1, keepdims=True)
    g_sel = jnp.min(jnp.where(is_g & (gl == gmax), lane, big), axis=-1, keepdims=True) - MOE_EXPERTS
    g_w = 1.0 / jnp.sum(jnp.where(is_g, jnp.exp(gl - gmax), 0.0), axis=-1, keepdims=True)
    in_grp = (lane >= g_sel * MOE_EPG) & (lane < (g_sel + 1) * MOE_EPG)
    el = jnp.where(in_grp, logits, NEG)
    emax = jnp.max(el, axis=-1, keepdims=True)
    ee = jnp.where(in_grp, jnp.exp(el - emax), 0.0)
    pe = ee / jnp.sum(ee, axis=-1, keepdims=True)
    p1 = jnp.max(pe, axis=-1, keepdims=True)
    i1 = jnp.min(jnp.where(in_grp & (pe == p1), lane, big), axis=-1, keepdims=True)
    rest = in_grp & (lane != i1)
    pr = jnp.where(rest, pe, -1.0)
    p2 = jnp.max(pr, axis=-1, keepdims=True)
    i2 = jnp.min(jnp.where(rest & (pr == p2), lane, big), axis=-1, keepdims=True)
    tot = p1 + p2
    wf = jnp.where(lane == i1, p1 / tot * g_w, jnp.where(lane == i2, p2 / tot * g_w, 0.0))
    wf = jnp.where(lane == GSEL_LANE, g_sel, wf)
    cnt = jnp.sum(jnp.where(lane == g_sel, 1.0, 0.0), axis=0, keepdims=True)
    return wf, cnt


def _mix_out_kernel(*refs, gmlp):
    if gmlp:
        (x_ref, gu_ref, vn_ref, ws_ref, bs_ref, yb_ref, wo_ref, g_ref, b_ref, wr_ref, br_ref,
         x1_ref, x1b_ref, wf_ref, cnt_ref, ya_ref) = refs
        tm = x_ref.shape[0]
        r = lax.broadcasted_iota(jnp.int32, (GM_CHUNK, GM_CHUNK), 0)
        c = lax.broadcasted_iota(jnp.int32, (GM_CHUNK, GM_CHUNK), 1)
        for g in range(GM_GROUPS):
            ws = jnp.where(r >= c, ws_ref[g], 0.0).astype(BF16)
            bias = bs_ref[:, g:g + 1]
            for ch in range(tm // GM_CHUNK):
                rows = slice(ch * GM_CHUNK, (ch + 1) * GM_CHUNK)
                cols = slice(g * LANES, (g + 1) * LANES)
                s = _dot(ws, vn_ref[rows, cols]) + bias
                ya_ref[rows, cols] = (gu_ref[rows, cols].astype(F32) * s).astype(BF16)
        mix = _dot(ya_ref[...], wo_ref[0:GM_WIDTH, :]) + _dot(yb_ref[...], wo_ref[GM_WIDTH:, :])
    else:
        x_ref, y_ref, wo_ref, g_ref, b_ref, wr_ref, br_ref, x1_ref, x1b_ref, wf_ref, cnt_ref = refs
        mix = _dot(y_ref[...], wo_ref[...])
    x1 = _layer_norm(DN_ALPHA * x_ref[...] + mix, g_ref[...], b_ref[...])
    x1_ref[...] = x1
    x1b_ref[...] = x1.astype(BF16)
    wf, cnt = _router(x1, wr_ref[...], br_ref[...])
    wf_ref[...] = wf
    cnt_ref[0] = jnp.broadcast_to(cnt, cnt_ref.shape[1:])


def _mix_out(x2, ys, wo, g, b, wr, br, gm=None):
    tm = TM_PROJ
    row = lambda n: pl.BlockSpec((tm, n), lambda i: (i, 0))
    full = lambda a: pl.BlockSpec(a.shape, lambda i: (0,) * a.ndim)
    if gm is not None:
        gu, vn, ws, bs = gm
        args = (x2, gu, vn, ws, bs, ys, wo, g, b, wr, br)
        in_specs = [row(D_MODEL), row(512), row(512), full(ws), full(bs), row(512), full(wo),
                    full(g), full(b), full(wr), full(br)]
        scratch = [pltpu.VMEM((tm, GM_WIDTH), BF16)]
    else:
        args = (x2, ys, wo, g, b, wr, br)
        in_specs = [row(D_MODEL), row(C_MIX), full(wo), full(g), full(b), full(wr), full(br)]
        scratch = []
    return pl.pallas_call(
        functools.partial(_mix_out_kernel, gmlp=gm is not None),
        grid=(TOKENS // tm,),
        in_specs=in_specs,
        out_specs=[row(D_MODEL), row(D_MODEL), row(LANES), pl.BlockSpec((1, 8, LANES), lambda i: (i, 0, 0))],
        out_shape=[jax.ShapeDtypeStruct((TOKENS, D_MODEL), F32), jax.ShapeDtypeStruct((TOKENS, D_MODEL), BF16),
                   jax.ShapeDtypeStruct((TOKENS, LANES), F32),
                   jax.ShapeDtypeStruct((TOKENS // tm, 8, LANES), F32)],
        scratch_shapes=scratch,
        compiler_params=pltpu.CompilerParams(dimension_semantics=("arbitrary",)),
        name="mix_out_gmlp" if gm is not None else "mix_out",
    )(*args)


def _group_dest(wf, ltri_ref, ustr_ref):
    tm = wf.shape[0]
    lane = lax.broadcasted_iota(jnp.int32, (tm, LANES), 1).astype(F32)
    onehot = jnp.where(lane == wf[:, GSEL_LANE:GSEL_LANE + 1], 1.0, 0.0)
    before = _dot(ltri_ref[...], onehot.astype(BF16))
    cnt = jnp.sum(onehot, axis=0, keepdims=True)
    gran = jnp.floor((cnt + (GRAN - 1)) * (1.0 / GRAN))
    start = _dot(jnp.broadcast_to(gran, (8, LANES)).astype(BF16), ustr_ref[...])[0:1]
    return jnp.sum(onehot * (GRAN * start + before), axis=-1, keepdims=True)


def _granule_copies(tbl_ref, tile, vmem_bufs, hbm_refs, sems, to_hbm, act):
    for g in range(MOE_GROUPS):
        n = tbl_ref[tile * TBL_W + g]
        loc = tbl_ref[tile * TBL_W + MOE_GROUPS + g]
        glb = tbl_ref[tile * TBL_W + 2 * MOE_GROUPS + g]

        def body(k, c, loc=loc, glb=glb):
            lo = pl.multiple_of((loc + k) * GRAN, GRAN)
            hi = pl.multiple_of((glb + k) * GRAN, GRAN)
            for idx, (vb, hb) in enumerate(zip(vmem_bufs, hbm_refs)):
                v_sl, h_sl = vb.at[pl.ds(lo, GRAN)], hb.at[pl.ds(hi, GRAN)]
                src, dst = (v_sl, h_sl) if to_hbm else (h_sl, v_sl)
                act(pltpu.make_async_copy(src, dst, sems[idx]))
            return c

        lax.fori_loop(0, n, body, 0)


def _zero_fill_copies(tbl_ref, zero_bufs, hbm_refs, sems, act):
    tail = (TOKENS // TM_PROJ) * TBL_W
    for n in range(MOE_GROUPS + 1):
        first = tbl_ref[tail + 2 * n]

        def body(k, c, first=first):
            hi = pl.multiple_of((first + k) * GRAN, GRAN)
            for idx, (zb, hb) in enumerate(zip(zero_bufs, hbm_refs)):
                act(pltpu.make_async_copy(zb, hb.at[pl.ds(hi, GRAN)], sems[idx]))
            return c

        lax.fori_loop(0, tbl_ref[tail + 2 * n + 1], body, 0)


def _dispatch_kernel(tbl_ref, xb_ref, wf_ref, ltri_ref, ustr_ref, xs_out, ws_out, xbuf, wbuf, zx, zw, sems):
    tile = pl.program_id(0)
    slot = tile % 2
    tm = xb_ref.shape[0]
    wf = wf_ref[...]
    dest = _group_dest(wf, ltri_ref, ustr_ref)
    dest_row = jnp.broadcast_to(dest, (tm, LANES)).T[0:1]
    r = lax.broadcasted_iota(jnp.int32, (ROWS_LOCAL, tm), 0).astype(F32)
    perm = jnp.where(r == dest_row, 1.0, 0.0).astype(BF16)
    xbuf[slot] = _dot(perm, xb_ref[...]).astype(BF16)
    hi = wf.astype(BF16)
    r1 = wf - hi.astype(F32)
    mid = r1.astype(BF16)
    lo = (r1 - mid.astype(F32)).astype(BF16)
    pieces = _dot(perm, jnp.concatenate([hi, mid, lo], axis=1))
    wbuf[slot] = pieces[:, :LANES] + pieces[:, LANES:2 * LANES] + pieces[:, 2 * LANES:]

    def copies(t, s, act):
        _granule_copies(tbl_ref, t, (xbuf.at[s], wbuf.at[s]), (xs_out, ws_out), (sems.at[s, 0], sems.at[s, 1]),
                        True, act)

    copies(tile, slot, lambda c: c.start())

    @pl.when(tile > 0)
    def _():
        copies(tile - 1, 1 - slot, lambda c: c.wait())

    @pl.when(tile == pl.num_programs(0) - 1)
    def _():
        zx[...] = jnp.zeros_like(zx)
        zw[...] = jnp.zeros_like(zw)
        fill = functools.partial(_zero_fill_copies, tbl_ref, (zx, zw), (xs_out, ws_out), (sems.at[2, 0], sems.at[2, 1]))
        fill(lambda c: c.start())
        copies(tile, slot, lambda c: c.wait())
        fill(lambda c: c.wait())


def _dispatch(tbl, x1b, wf, ltri, ustr):
    tm = TM_PROJ
    row = lambda n: pl.BlockSpec((tm, n), lambda i, t: (i, 0))
    full = lambda a: pl.BlockSpec(a.shape, lambda i, t: (0,) * a.ndim)
    anyspace = pl.BlockSpec(memory_space=pl.ANY)
    return pl.pallas_call(
        _dispatch_kernel,
        grid_spec=pltpu.PrefetchScalarGridSpec(
            num_scalar_prefetch=1, grid=(TOKENS // tm,),
            in_specs=[row(D_MODEL), row(LANES), full(ltri), full(ustr)],
            out_specs=[anyspace, anyspace],
            scratch_shapes=[pltpu.VMEM((2, ROWS_LOCAL, D_MODEL), BF16), pltpu.VMEM((2, ROWS_LOCAL, LANES), F32),
                            pltpu.VMEM((GRAN, D_MODEL), BF16), pltpu.VMEM((GRAN, LANES), F32),
                            pltpu.SemaphoreType.DMA((3, 2))]),
        out_shape=[jax.ShapeDtypeStruct((ROWS_SORTED, D_MODEL), BF16),
                   jax.ShapeDtypeStruct((ROWS_SORTED, LANES), F32)],
        compiler_params=pltpu.CompilerParams(dimension_semantics=("arbitrary",)),
        name="moe_dispatch",
    )(tbl, x1b, wf, ltri, ustr)


def _experts_kernel(gid_ref, valid_ref, xs_ref, ws_ref, wg_ref, wu_ref, wd_ref, y_ref):
    i = pl.program_id(0)

    @pl.when(valid_ref[i] == 1)
    def _():
        x = xs_ref[...]
        gates = [_dot(x, wg_ref[e]) for e in range(MOE_EPG)]
        ups = [_dot(x, wu_ref[e]) for e in range(MOE_EPG)]
        ws = ws_ref[...]
        lane = lax.broadcasted_iota(jnp.int32, ws.shape, 1)
        hidden = []
        for e in range(MOE_EPG):
            w_tok = jnp.sum(jnp.where(lane == gid_ref[i] * MOE_EPG + e, ws, 0.0), axis=-1, keepdims=True)
            hidden.append((gates[e] * jax.nn.sigmoid(gates[e]) * ups[e] * w_tok).astype(BF16))
        wd = wd_ref[...].reshape(MOE_EPG * MOE_HIDDEN, D_MODEL)
        y_ref[...] = _dot(jnp.concatenate(hidden, axis=1), wd).astype(BF16)

    @pl.when(valid_ref[i] == 0)
    def _():
        y_ref[...] = jnp.zeros_like(y_ref)


def _experts(gid, valid, xs, ws, wg, wu, wd):
    tm = TM_MOE
    row = lambda n: pl.BlockSpec((tm, n), lambda i, gid, valid: (i, 0))
    wspec = lambda a, b: pl.BlockSpec((MOE_EPG, a, b), lambda i, gid, valid: (gid[i], 0, 0))
    return pl.pallas_call(
        _experts_kernel,
        grid_spec=pltpu.PrefetchScalarGridSpec(
            num_scalar_prefetch=2, grid=(ROWS_SORTED // tm,),
            in_specs=[row(D_MODEL), row(LANES), wspec(D_MODEL, MOE_HIDDEN), wspec(D_MODEL, MOE_HIDDEN),
                      wspec(MOE_HIDDEN, D_MODEL)],
            out_specs=row(D_MODEL)),
        out_shape=jax.ShapeDtypeStruct((ROWS_SORTED, D_MODEL), BF16),
        compiler_params=pltpu.CompilerParams(dimension_semantics=("arbitrary",),
                                             vmem_limit_bytes=EXPERTS_VMEM_BYTES),
        name="moe_experts",
    )(gid, valid, xs, ws, wg, wu, wd)


def _combine_kernel(tbl_ref, wf_ref, x1_ref, ltri_ref, ustr_ref, g_ref, b_ref, y_hbm, o_ref, ybuf, sems):
    tile = pl.program_id(0)
    slot = tile % 2
    tm = x1_ref.shape[0]

    def copies(t, s, act):
        _granule_copies(tbl_ref, t, (ybuf.at[s],), (y_hbm,), (sems.at[s],), False, act)

    def fetch(t, s):
        ybuf[s] = jnp.zeros(ybuf.shape[1:], ybuf.dtype)
        copies(t, s, lambda c: c.start())

    @pl.when(tile == 0)
    def _():
        fetch(tile, slot)

    @pl.when(tile + 1 < pl.num_programs(0))
    def _():
        fetch(tile + 1, 1 - slot)

    dest = _group_dest(wf_ref[...], ltri_ref, ustr_ref)
    c = lax.broadcasted_iota(jnp.int32, (tm, ROWS_LOCAL), 1).astype(F32)
    unperm = jnp.where(c == dest, 1.0, 0.0).astype(BF16)
    copies(tile, slot, lambda c: c.wait())
    y_sorted = ybuf[slot]
    halves = (slice(0, tm // 2), slice(tm // 2, tm))
    ffn = [_dot(unperm[rows], y_sorted) for rows in halves]
    for rows, part in zip(halves, ffn):
        o_ref[rows, :] = _layer_norm(DN_ALPHA * x1_ref[rows, :] + part, g_ref[...], b_ref[...])


def _combine(tbl, wf, x1, ltri, ustr, g, b, y):
    tm = TM_PROJ
    row = lambda n: pl.BlockSpec((tm, n), lambda i, t: (i, 0))
    full = lambda a: pl.BlockSpec(a.shape, lambda i, t: (0,) * a.ndim)
    return pl.pallas_call(
        _combine_kernel,
        grid_spec=pltpu.PrefetchScalarGridSpec(
            num_scalar_prefetch=1, grid=(TOKENS // tm,),
            in_specs=[row(LANES), row(D_MODEL), full(ltri), full(ustr), full(g), full(b),
                      pl.BlockSpec(memory_space=pl.ANY)],
            out_specs=row(D_MODEL),
            scratch_shapes=[pltpu.VMEM((2, ROWS_LOCAL, D_MODEL), BF16), pltpu.SemaphoreType.DMA((2,))]),
        out_shape=jax.ShapeDtypeStruct((TOKENS, D_MODEL), F32),
        compiler_params=pltpu.CompilerParams(dimension_semantics=("arbitrary",)),
        name="moe_combine",
    )(tbl, wf, x1, ltri, ustr, g, b, y)


def _routing_tables(cnt):
    n_tiles = cnt.shape[0]
    c = cnt[:, 0, :MOE_GROUPS].astype(jnp.int32)
    gran = (c + GRAN - 1) // GRAN
    local = jnp.cumsum(gran, axis=1) - gran
    per_tile = TM_MOE // GRAN
    tiles_g = (jnp.sum(gran, axis=0) + per_tile - 1) // per_tile
    ends = jnp.cumsum(tiles_g)
    base = (ends - tiles_g) * per_tile
    glob = base[None, :] + jnp.cumsum(gran, axis=0) - gran
    used = jnp.sum(gran, axis=0)
    pad_first = jnp.concatenate([base + used, ends[-1:] * per_tile])
    pad_count = jnp.concatenate([tiles_g * per_tile - used, ROWS_SORTED // GRAN - ends[-1:] * per_tile])
    tail = jnp.stack([pad_first, pad_count], axis=1).reshape(-1)
    tbl = jnp.concatenate([jnp.concatenate([gran, local, glob], axis=1).reshape(n_tiles * TBL_W), tail])
    idx = jnp.arange(ROWS_SORTED // TM_MOE)
    gid = jnp.minimum(jnp.sum(idx[:, None] >= ends[None, :], axis=1), MOE_GROUPS - 1).astype(jnp.int32)
    valid = (idx < ends[-1]).astype(jnp.int32)
    return tbl, gid, valid


def _sort_tables():
    t = jnp.arange(TM_PROJ)
    ltri = (t[None, :] < t[:, None]).astype(BF16)
    l = jnp.arange(LANES)
    ustr = (l[:, None] < l[None, :]).astype(BF16)
    return ltri, ustr


def _dup_halves(t):
    lane = lax.broadcasted_iota(jnp.int32, t.shape, 1)
    r = pltpu.roll(t, HALF, 1)
    return jnp.where(lane < HALF, t, r), jnp.where(lane < HALF, r, t)


def _nsa_in_kernel(x_ref, w_ref, gb_ref, q_ref, kc_ref, vc_ref, ks_ref, vst_ref, kw_ref, vwt_ref, gate_ref):
    h = _dot(x_ref[...].astype(BF16), w_ref[...])
    q_ref[...] = (h[:, 0:C_MIX] * (NSA_SCALE * LOG2E)).astype(BF16)
    kc_ref[...] = h[:, 1024:1152].astype(BF16)
    vc_ref[...] = h[:, 1152:1280].astype(BF16)
    tm = h.shape[0]
    lane = lax.broadcasted_iota(jnp.int32, (tm, LANES), 1)
    for g, d in enumerate(_dup_halves(h[:, 1536:1664])):
        kw_ref[:, g * LANES:(g + 1) * LANES] = d.astype(BF16)
    pos = (pl.program_id(0) * tm) % SEQ + lax.broadcasted_iota(jnp.int32, (tm, LANES), 0)
    block_onehot = jnp.where(lane == pos // NSA_SEL_LEN, 1.0, 0.0).astype(BF16)
    for g, d in enumerate(_dup_halves(h[:, 1280:1408])):
        ks_ref[:, 2 * g * LANES:(2 * g + 1) * LANES] = d.astype(BF16)
        ks_ref[:, (2 * g + 1) * LANES:(2 * g + 2) * LANES] = block_onehot
    for idx, ref in ((1, vst_ref), (3, vwt_ref)):
        tail = jnp.where(lane == HALF, 1.0, 0.0)
        _store_transposed(ref, [jnp.where(lane < HALF, d, tail)
                                for d in _dup_halves(h[:, 1280 + idx * LANES:1280 + (idx + 1) * LANES])])
    for g in range(NSA_GROUPS):
        gate_ref[g] = jax.nn.sigmoid(h[:, 1792 + g * LANES:1792 + (g + 1) * LANES] + gb_ref[g])


def _nsa_in(x2, w, gb):
    tm = TM_PROJ
    row = lambda n: pl.BlockSpec((tm, n), lambda i: (i, 0))
    full = lambda a: pl.BlockSpec(a.shape, lambda i: (0,) * a.ndim)
    sd = jax.ShapeDtypeStruct
    vt_spec = lambda w: pl.BlockSpec((NSA_GROUPS, tm // w, LANES, w), lambda i: (0, i, 0, 0))
    vt_shape = lambda w: sd((NSA_GROUPS, TOKENS // w, LANES, w), BF16)
    return pl.pallas_call(
        _nsa_in_kernel,
        grid=(TOKENS // tm,),
        in_specs=[row(D_MODEL), full(w), full(gb)],
        out_specs=[row(C_MIX), row(LANES), row(LANES), row(4 * LANES), vt_spec(TK_ATT), row(2 * LANES),
                   vt_spec(WIN_SUB), pl.BlockSpec((NSA_GROUPS, tm, LANES), lambda i: (0, i, 0))],
        out_shape=[sd((TOKENS, C_MIX), BF16), sd((TOKENS, LANES), BF16), sd((TOKENS, LANES), BF16),
                   sd((TOKENS, 4 * LANES), BF16), vt_shape(TK_ATT), sd((TOKENS, 2 * LANES), BF16), vt_shape(WIN_SUB),
                   sd((NSA_GROUPS, TOKENS, LANES), F32)],
        compiler_params=pltpu.CompilerParams(dimension_semantics=("arbitrary",)),
        name="nsa_in",
    )(x2, w, gb)


def _compress_kernel(kc_ref, vc_ref, pk_ref, pv_ref, wk1_ref, wv1_ref, wk2_ref, wv2_ref, ko_ref, vo_ref):
    for a_ref, p_ref, w1_ref, w2_ref, o_ref in ((kc_ref, pk_ref, wk1_ref, wk2_ref, ko_ref),
                                                (vc_ref, pv_ref, wv1_ref, wv2_ref, vo_ref)):
        a = a_ref[0].astype(F32)
        a0 = (a + p_ref[0]).astype(BF16)
        a1 = (a + p_ref[1]).astype(BF16)
        outs = []
        for g in range(NSA_GROUPS):
            first = _dot(a0, w1_ref[g])
            second = _dot(a1, w1_ref[NSA_GROUPS + g])
            hid = first + pltpu.roll(second, NSA_NCMP - 1, 0)
            outs.append(_dot(_gelu(hid).astype(BF16), w2_ref[...]))
        if o_ref is ko_ref:
            o_ref[0] = jnp.concatenate(outs, axis=1).astype(BF16)
        else:
            for g in range(NSA_GROUPS):
                o_ref[0, g] = outs[g].T.astype(BF16)


def _compress(kc_r, vc_r, pk, pv, wk1, wv1, wk2, wv2):
    blk = pl.BlockSpec((1, NSA_NCMP, NSA_CMP_STRIDE * LANES), lambda b: (b, 0, 0))
    full = lambda a: pl.BlockSpec(a.shape, lambda b: (0,) * a.ndim)
    sd = jax.ShapeDtypeStruct
    return pl.pallas_call(
        _compress_kernel,
        grid=(BATCH,),
        in_specs=[blk, blk, full(pk), full(pv), full(wk1), full(wv1), full(wk2), full(wv2)],
        out_specs=[pl.BlockSpec((1, NSA_NCMP, 2 * LANES), lambda b: (b, 0, 0)),
                   pl.BlockSpec((1, NSA_GROUPS, LANES, NSA_NCMP), lambda b: (b, 0, 0, 0))],
        out_shape=[sd((BATCH, NSA_NCMP, 2 * LANES), BF16), sd((BATCH, NSA_GROUPS, LANES, NSA_NCMP), BF16)],
        compiler_params=pltpu.CompilerParams(dimension_semantics=("arbitrary",)),
        name="nsa_compress",
    )(kc_r, vc_r, pk, pv, wk1, wv1, wk2, wv2)


def _nsa_attn_kernel(q_ref, kc_ref, vct_ref, ks_ref, vst_ref, kw_ref, vwt_ref, gate_ref, cover_ref,
                     o_ref, m_ref, acc_ref):
    tq, tk, hpg = TQ_NSA, TK_ATT, NSA_HPG
    qi = pl.program_id(2)
    q0 = qi * tq
    lane = lax.broadcasted_iota(jnp.int32, (tq, LANES), 1)
    t_tok = q0 + lax.broadcasted_iota(jnp.int32, (1, tq), 1)
    head = lambda x, i: x[:, i * tq:(i + 1) * tq]
    heads = range(hpg)

    parts = []
    for p in range(hpg // 2):
        qp = q_ref[0, :, p * LANES:(p + 1) * LANES]
        zero = jnp.zeros_like(qp)
        parts.append(jnp.where(lane < HALF, qp, zero))
        parts.append(jnp.where(lane < HALF, zero, qp))
    qs = jnp.concatenate(parts, axis=0)

    kb_sub = lax.broadcasted_iota(jnp.int32, (WIN_BAND, 1), 0)

    def window_scores(part):
        q_lo = q0 + part * WIN_SUB
        start = pl.multiple_of(jnp.maximum(q_lo - NSA_WINDOW, 0), WIN_SUB)
        qs_part = jnp.concatenate([qs[i * tq + part * WIN_SUB:i * tq + (part + 1) * WIN_SUB] for i in heads], axis=0)
        return q_lo, start, _dot_nt(kw_ref[0, pl.ds(start, WIN_BAND), :], qs_part)

    def window_finish(q_lo, start, s_w):
        kpos = start + kb_sub
        t_part = q_lo + lax.broadcasted_iota(jnp.int32, (1, WIN_SUB), 1)
        in_win = (kpos <= t_part) & (kpos > t_part - NSA_WINDOW)
        vw_band = jnp.concatenate([vwt_ref[0, start // WIN_SUB + c] for c in range(WIN_BAND // WIN_SUB)], axis=1)
        out = []
        for i0 in range(0, hpg, 2):
            e_w = []
            for i in (i0, i0 + 1):
                sm = jnp.where(in_win, s_w[:, i * WIN_SUB:(i + 1) * WIN_SUB], NEG)
                e_w.append(jnp.exp2((sm - jnp.max(sm, axis=0, keepdims=True)).astype(BF16)))
            o_pair = _dot(vw_band, jnp.concatenate(e_w, axis=1))
            out += [o_pair[:, :WIN_SUB], o_pair[:, WIN_SUB:]]
        return out

    s_c = _dot_nt(kc_ref[0], qs)
    win0 = window_scores(0)
    n_sub = lax.broadcasted_iota(jnp.int32, (NSA_NCMP, 1), 0)
    vis = t_tok >= n_sub * NSA_CMP_STRIDE + (NSA_CMP_LEN - 1)
    sees_any = t_tok >= NSA_CMP_LEN - 1
    p_sum = jnp.zeros((NSA_NCMP, tq), F32)
    p_c = []
    for i in heads:
        sm = jnp.where(vis, head(s_c, i), NEG)
        e = jnp.exp2(sm - jnp.max(sm, axis=0, keepdims=True))
        p = e * jnp.where(sees_any, 1.0 / jnp.sum(e, axis=0, keepdims=True), 0.0)
        p_sum = p_sum + p
        p_c.append(p.astype(BF16))
    o_c = _dot(vct_ref[0, 0], jnp.concatenate(p_c, axis=1))

    imp = jnp.dot(cover_ref[...], p_sum, preferred_element_type=F32,
                  precision=lax.Precision.HIGHEST)[0:NSA_NSEL]
    o_w_parts = [window_finish(*win0)]
    o_w_parts += [window_finish(*window_scores(part)) for part in range(1, tq // WIN_SUB)]
    jj = lax.broadcasted_iota(jnp.int32, (NSA_NSEL, 1), 0)
    tb = t_tok // NSA_SEL_LEN
    forced = (jj == 0) | (jj == tb) | (jj == tb - 1)
    score = jnp.where(forced, NSA_FORCE, jnp.where(jj <= tb, imp, -NSA_FORCE))
    sub = 8
    rows = [score[b * sub:(b + 1) * sub] for b in range(NSA_NSEL // sub)]
    ranks = [jnp.zeros((sub, tq), jnp.int32) for _ in rows]
    j_in = lax.broadcasted_iota(jnp.int32, (sub, 1), 0)
    for i in range(NSA_NSEL):
        si = score[i:i + 1, :]
        for b, blk in enumerate(rows):
            if b < i // sub:
                beats = si > blk
            elif b > i // sub:
                beats = si >= blk
            else:
                beats = (si > blk) | ((si == blk) & (j_in > i % sub))
            ranks[b] = ranks[b] + beats.astype(jnp.int32)
    rank = jnp.concatenate(ranks, axis=0)
    sel_bias = jnp.where(rank < NSA_TOPK, 0.0, NEG)
    bias_rows = jnp.concatenate([sel_bias, jnp.zeros((LANES - NSA_NSEL, tq), F32)], axis=0).T.astype(BF16)
    qs_sel = jnp.concatenate([qs, jnp.concatenate([bias_rows] * hpg, axis=0)], axis=1)

    m_ref[...] = jnp.full(m_ref.shape, NEG, F32)
    acc_ref[...] = jnp.zeros(acc_ref.shape, F32)
    k_sub = lax.broadcasted_iota(jnp.int32, (tk, 1), 0)
    diag = q0 // tk
    causal = diag * tk + k_sub <= t_tok
    per_dot = max(1, 2 * LANES // tq)

    def process(kt, diagonal, chunks=1):
        s = _dot_nt(ks_ref[0, pl.ds(pl.multiple_of(kt * tk, tk), chunks * tk), :], qs_sel)
        vt = jnp.concatenate([vst_ref[0, kt + c] for c in range(chunks)], axis=1)
        for i0 in range(0, hpg, per_dot):
            cols = slice(i0 * tq, (i0 + per_dot) * tq)
            probs, alphas = [], []
            for i in range(i0, i0 + per_dot):
                for part in range(tq // LANES):
                    sub = slice(part * LANES, (part + 1) * LANES)
                    sm = s[:, i * tq + part * LANES:i * tq + (part + 1) * LANES]
                    if diagonal:
                        sm = jnp.where(causal[:, sub], sm, NEG)
                    m_old = m_ref[i, :, sub]
                    m_new = jnp.maximum(m_old, jnp.max(sm, axis=0, keepdims=True))
                    m_ref[i, :, sub] = m_new
                    probs.append(jnp.exp2((sm - m_new).astype(BF16)))
                    alphas.append(jnp.exp2(m_old - m_new))
            pv = _dot(vt, jnp.concatenate(probs, axis=1))
            acc_ref[:, cols] = jnp.concatenate(alphas, axis=1) * acc_ref[:, cols] + pv

    def pair(k, c):
        process(2 * k, False, chunks=2)
        return c

    lax.fori_loop(0, diag // 2, pair, 0)

    @pl.when(diag % 2 == 1)
    def _():
        process(diag - 1, False)

    process(diag, True)

    o_w = [jnp.concatenate([part[i] for part in o_w_parts], axis=1) for i in heads]

    gt = gate_ref[0, 0].T
    o_s = acc_ref[...]
    outs = []
    for i in heads:
        c_i, s_i, w_i = head(o_c, i), head(o_s, i), o_w[i]
        outs.append(gt[i:i + 1] * c_i[:HALF]
                    + gt[hpg + i:hpg + i + 1] / s_i[HALF:HALF + 1] * s_i[:HALF]
                    + gt[2 * hpg + i:2 * hpg + i + 1] / w_i[HALF:HALF + 1] * w_i[:HALF])
    for p in range(hpg // 2):
        pair = jnp.concatenate([outs[2 * p], outs[2 * p + 1]], axis=0)
        o_ref[0, :, p * LANES:(p + 1) * LANES] = pair.T.astype(BF16)


def _nsa_attn(q3, kc2, vct, ks3, vst, kw3, vwt, gates4, cover_t):
    tq = TQ_NSA
    half_w = NSA_HPG * NSA_DH
    kv = pl.BlockSpec((1, SEQ, LANES), lambda b, g, i: (b, 0, g))
    vt = lambda w: pl.BlockSpec((1, SEQ // w, LANES, w), lambda b, g, i: (g, b, 0, 0))
    full = lambda a: pl.BlockSpec(a.shape, lambda b, g, i: (0,) * a.ndim)
    return pl.pallas_call(
        _nsa_attn_kernel,
        grid=(BATCH, NSA_GROUPS, SEQ // tq),
        in_specs=[pl.BlockSpec((1, tq, half_w), lambda b, g, i: (b, i, g)),
                  pl.BlockSpec((1, NSA_NCMP, LANES), lambda b, g, i: (b, 0, g)),
                  pl.BlockSpec((1, 1, LANES, NSA_NCMP), lambda b, g, i: (b, g, 0, 0)),
                  pl.BlockSpec((1, SEQ, 2 * LANES), lambda b, g, i: (b, 0, g)), vt(TK_ATT), kv, vt(WIN_SUB),
                  pl.BlockSpec((1, 1, tq, LANES), lambda b, g, i: (g, b, i, 0)),
                  full(cover_t)],
        out_specs=pl.BlockSpec((1, tq, half_w), lambda b, g, i: (b, i, g)),
        out_shape=jax.ShapeDtypeStruct((BATCH, SEQ, C_MIX), BF16),
        scratch_shapes=[pltpu.VMEM((NSA_HPG, 1, tq), F32), pltpu.VMEM((LANES, NSA_HPG * tq), F32)],
        compiler_params=pltpu.CompilerParams(dimension_semantics=("arbitrary",) * 3),
        name="nsa_attn",
    )(q3, kc2, vct, ks3, vst, kw3, vwt, gates4, cover_t)


def _rope_tables():
    half = MLA_ROPE // 2
    freq = jnp.exp(-math.log(ROPE_BASE) * jnp.arange(half, dtype=F32) / half)
    return freq[:, None]


def _swap_halves(w):
    half = w.shape[-1] // 2
    return jnp.concatenate([w[..., half:], w[..., :half]], axis=-1)


def _pad_last(w, n):
    return jnp.pad(w, [(0, 0)] * (w.ndim - 1) + [(0, n - w.shape[-1])])


def _ab_weights(w_in, w_uq, w_uk, w_uv):
    w_kr = w_in[:, 1408:1440]
    place = lambda w: jnp.pad(w, ((0, 0), (MLA_NOPE, LANES - MLA_NOPE - MLA_ROPE)))
    win = jnp.concatenate([w_in[:, :1408], place(w_kr), place(_swap_halves(w_kr))], axis=1).astype(BF16)
    uq = w_uq.reshape(MLA_Q_RANK, MLA_HEADS, MLA_NOPE + MLA_ROPE)
    nope, rp = uq[..., :MLA_NOPE], uq[..., MLA_NOPE:]
    q_pad = _pad_last(jnp.concatenate([nope, rp], -1), LANES).reshape(MLA_Q_RANK, MLA_HEADS * LANES)
    q_sw = _pad_last(jnp.concatenate([jnp.zeros_like(nope), _swap_halves(rp)], -1), LANES)
    wq = jnp.concatenate([q_pad, q_sw.reshape(MLA_Q_RANK, MLA_HEADS * LANES)], axis=1).astype(BF16)
    k_pad = _pad_last(w_uk.reshape(MLA_KV_RANK, MLA_HEADS, MLA_NOPE), LANES).reshape(MLA_KV_RANK, -1)
    wkv = jnp.concatenate([k_pad, w_uv], axis=1).astype(BF16)
    return win, wq, wkv


def _router_weights(w_rg, b_rg, w_re, b_re):
    wr = _pad_last(jnp.concatenate([w_re, w_rg], axis=1), LANES)
    wr_hi = wr.astype(BF16)
    wr_lo = (wr - wr_hi.astype(F32)).astype(BF16)
    br = _pad_last(jnp.concatenate([b_re, b_rg])[None, :], LANES)
    return jnp.concatenate([wr_hi, wr_lo], axis=1), br


def _nsa_in_weights(w_in, gate_b):
    g_cols = w_in[:, C_MIX + 768:].reshape(D_MODEL, 3, NSA_GROUPS, NSA_HPG)
    g_blocks = [_pad_last(g_cols[:, :, g, :].reshape(D_MODEL, 3 * NSA_HPG), LANES) for g in range(NSA_GROUPS)]
    w = jnp.concatenate([w_in[:, :C_MIX + 768]] + g_blocks, axis=1).astype(BF16)
    gb = gate_b.reshape(3, NSA_GROUPS, NSA_HPG)
    gb = jnp.stack([_pad_last(gb[:, g, :].reshape(1, 3 * NSA_HPG), LANES) for g in range(NSA_GROUPS)])
    return w, gb


def _compress_weights(pos, w1, w2):
    w1r = w1.reshape(2, NSA_CMP_STRIDE, NSA_DH, NSA_CMP_HIDDEN)
    zero = jnp.zeros_like(w1r)
    per_g = []
    for g in range(NSA_GROUPS):
        parts = [w1r if gg == g else zero for gg in range(NSA_GROUPS)]
        per_g.append(jnp.stack(parts, axis=2).reshape(2, NSA_CMP_STRIDE * LANES, NSA_CMP_HIDDEN))
    w1x = jnp.stack(per_g, axis=1).reshape(2 * NSA_GROUPS, NSA_CMP_STRIDE * LANES, NSA_CMP_HIDDEN)
    posr = pos.reshape(2, NSA_CMP_STRIDE, 1, NSA_DH)
    posx = jnp.broadcast_to(posr, (2, NSA_CMP_STRIDE, NSA_GROUPS, NSA_DH)).reshape(2, 1, NSA_CMP_STRIDE * LANES)
    w2x = jnp.concatenate([w2, w2], axis=1)
    return posx, w1x.astype(BF16), w2x.astype(BF16)


def _selection_tables():
    n = jnp.arange(LANES)[:, None]
    j = jnp.arange(LANES)[None, :]
    c0 = n * NSA_CMP_STRIDE
    s0 = j * NSA_SEL_LEN
    cover = ((c0 < s0 + NSA_SEL_LEN) & (c0 + NSA_CMP_LEN > s0) & (n < NSA_NCMP - 1) & (j < NSA_NSEL))
    return jnp.transpose(cover).astype(F32)


def kernel(x, positions, ab_w_in, ab_gm_ln_g, ab_gm_ln_b, ab_gm_ws, ab_gm_bs, ab_mla_q_norm,
           ab_mla_kv_norm, ab_mla_w_uq, ab_mla_w_uk, ab_mla_w_uv, ab_w_o, c_w_in, c_cmp_pos, c_w_ck1,
           c_w_ck2, c_w_cv1, c_w_cv2, c_gate_b, c_w_o, moe_w_rg, moe_b_rg, moe_w_re, moe_b_re,
           moe_w_gate, moe_w_up, moe_w_down, ln1_g, ln1_b, ln2_g, ln2_b):
    x2 = x.reshape(TOKENS, D_MODEL)
    pos3 = positions.reshape(TOKENS // TM_PROJ, 1, TM_PROJ)
    vec = lambda a: a[None, :]

    ltri, ustr = _sort_tables()

    def moe_layer(layer, x1b, x1, wf, cnt):
        tbl, gid, valid = _routing_tables(cnt)
        xs, ws = _dispatch(tbl, x1b, wf, ltri, ustr)
        y = _experts(gid, valid, xs, ws, moe_w_gate[layer].astype(BF16), moe_w_up[layer].astype(BF16),
                     moe_w_down[layer].astype(BF16))
        return _combine(tbl, wf, x1, ltri, ustr, vec(ln2_g[layer]), vec(ln2_b[layer]), y)

    win, wq, wkv = _ab_weights(ab_w_in[0], ab_mla_w_uq[0], ab_mla_w_uk[0], ab_mla_w_uv[0])
    gu, vn, q, k, vt = _ab_in(x2, pos3, win, vec(ab_gm_ln_g[0]), vec(ab_gm_ln_b[0]), vec(ab_mla_q_norm[0]),
                              vec(ab_mla_kv_norm[0]), wq, wkv, _rope_tables())
    yb = _mla_attn(q.reshape(BATCH, SEQ, -1), k.reshape(BATCH, SEQ, -1), vt)
    wr, br = _router_weights(moe_w_rg[0], moe_b_rg[0], moe_w_re[0], moe_b_re[0])
    x1, x1b, wf, cnt = _mix_out(x2, yb.reshape(TOKENS, -1), ab_w_o[0].astype(BF16), vec(ln1_g[0]),
                                vec(ln1_b[0]), wr, br, gm=(gu, vn, ab_gm_ws[0], jnp.transpose(ab_gm_bs[0])))
    x2 = moe_layer(0, x1b, x1, wf, cnt)

    w_nsa, gb = _nsa_in_weights(c_w_in[0], c_gate_b[0])
    q, kc, vc, ks, vst, kw, vwt, gates = _nsa_in(x2, w_nsa, gb)
    pk, wk1, wk2 = _compress_weights(c_cmp_pos[0, 0], c_w_ck1[0], c_w_ck2[0])
    pv, wv1, wv2 = _compress_weights(c_cmp_pos[0, 1], c_w_cv1[0], c_w_cv2[0])
    blocks = lambda a: a.reshape(BATCH, NSA_NCMP, NSA_CMP_STRIDE * LANES)
    kc2, vct = _compress(blocks(kc), blocks(vc), pk, pv, wk1, wv1, wk2, wv2)
    b3 = lambda a: a.reshape(BATCH, SEQ, -1)
    o = _nsa_attn(b3(q), kc2, vct, b3(ks), vst, b3(kw), vwt,
                  gates.reshape(NSA_GROUPS, BATCH, SEQ, LANES), _selection_tables())
    wr, br = _router_weights(moe_w_rg[1], moe_b_rg[1], moe_w_re[1], moe_b_re[1])
    x1, x1b, wf, cnt = _mix_out(x2, o.reshape(TOKENS, -1), c_w_o[0].astype(BF16), vec(ln1_g[1]), vec(ln1_b[1]),
                                wr, br)
    x2 = moe_layer(1, x1b, x1, wf, cnt)
    return x2.reshape(BATCH, SEQ, D_MODEL)
```

```python
import functools
import math

import jax
import jax.numpy as jnp
from jax import lax
from jax.experimental import pallas as pl
from jax.experimental.pallas import tpu as pltpu

F32 = jnp.float32
BF16 = jnp.bfloat16

D_MODEL = 1024
BATCH = 16
SEQ = 2048
TOKENS = BATCH * SEQ
DEPTH = 2
DN_ALPHA = (2.0 * DEPTH) ** 0.25
LN_EPS = 1e-5
NEG = -1e30
LOG2E = math.log2(math.e)
LANES = 128
HALF = LANES // 2

GM_WIDTH = 512
GM_GROUPS = 4
GM_CHUNK = 128

MLA_HEADS = 8
MLA_NOPE = 64
MLA_ROPE = 32
MLA_V = 64
MLA_Q_RANK = 256
MLA_KV_RANK = 128
ROPE_BASE = 10000.0
MLA_SCALE = (MLA_NOPE + MLA_ROPE) ** -0.5

NSA_HEADS = 16
NSA_GROUPS = 2
NSA_HPG = 8
NSA_DH = 64
NSA_CMP_LEN = 32
NSA_CMP_STRIDE = 16
NSA_CMP_HIDDEN = 256
NSA_SEL_LEN = 64
NSA_TOPK = 8
NSA_WINDOW = 512
NSA_NSEL = SEQ // NSA_SEL_LEN
NSA_NCMP = SEQ // NSA_CMP_STRIDE
NSA_FORCE = 1e4
NSA_SCALE = NSA_DH ** -0.5
C_MIX = NSA_HEADS * NSA_DH

MOE_GROUPS = 4
MOE_EPG = 8
MOE_EXPERTS = 32
MOE_HIDDEN = 256

TM_PROJ = 512
TQ_MLA = 256
TQ_NSA = 256
TK_ATT = 256
WIN_SUB = LANES
WIN_BAND = NSA_WINDOW + WIN_SUB
MLA_VT_ROWS = LANES + 16
TM_MOE = 512
EXPERTS_VMEM_BYTES = (2 * 3 * MOE_EPG * D_MODEL * MOE_HIDDEN * 2 + 2 * 2 * TM_MOE * D_MODEL * 2
                      + 2 * MOE_EPG * TM_MOE * MOE_HIDDEN * 4 + TM_MOE * MOE_EPG * MOE_HIDDEN * 2
                      + 2 * TM_MOE * D_MODEL * 4)

GSEL_LANE = MOE_EXPERTS
GRAN = 16
TBL_W = 3 * MOE_GROUPS
ROWS_LOCAL = 640
ROWS_SORTED = TM_MOE * (TOKENS // TM_MOE + MOE_GROUPS
                        + -(-(TOKENS // TM_PROJ) * MOE_GROUPS * (GRAN - 1) // TM_MOE))


def _dot(a, b):
    return jnp.dot(a, b, preferred_element_type=F32)


def _dot_nt(a, b):
    return lax.dot_general(a, b, (((1,), (1,)), ((), ())), preferred_element_type=F32)


def _gelu(x):
    return 0.5 * x * (1.0 + jnp.tanh(math.sqrt(2.0 / math.pi) * (x + 0.044715 * (x * x * x))))


def _layer_norm(x, g, b):
    mu = jnp.mean(x, axis=-1, keepdims=True)
    xc = x - mu
    var = jnp.mean(xc * xc, axis=-1, keepdims=True)
    return xc * lax.rsqrt(var + LN_EPS) * g + b


def _rms_norm(x, g):
    return x * lax.rsqrt(jnp.mean(x * x, axis=-1, keepdims=True) + LN_EPS) * g


def _store_transposed(ref, blocks):
    extra, width = ref.shape[2] - LANES, ref.shape[3]
    if extra:
        ones_rows = jnp.where(lax.broadcasted_iota(jnp.int32, (extra, width), 0) == 0, 1.0, 0.0)
    for n, blk in enumerate(blocks):
        t = blk.T
        for c in range(t.shape[1] // width):
            chunk = t[:, c * width:(c + 1) * width]
            if extra:
                chunk = jnp.concatenate([chunk, ones_rows], axis=0)
            ref[n, c] = chunk.astype(ref.dtype)


def _ab_in_kernel(x_ref, pos_ref, win_ref, lng_ref, lnb_ref, qg_ref, kvg_ref, wq_ref, wkv_ref,
                  fc_ref, gu_ref, vn_ref, q_ref, k_ref, vt_ref):
    h = _dot(x_ref[...].astype(BF16), win_ref[...])
    gu_ref[...] = _gelu(h[:, 0:512]).astype(BF16)
    vn_ref[...] = _layer_norm(_gelu(h[:, 512:1024]), lng_ref[...], lnb_ref[...]).astype(BF16)

    tm = x_ref.shape[0]
    ang = fc_ref[...] * pos_ref[0].astype(F32)
    cos_t, sin_t = jnp.cos(ang), jnp.sin(ang)
    ones_t, zeros_t = jnp.ones((MLA_NOPE, tm), F32), jnp.zeros((MLA_NOPE, tm), F32)
    pad = LANES - MLA_NOPE - MLA_ROPE
    cc = jnp.concatenate([ones_t, cos_t, cos_t, ones_t[:pad]], axis=0).T
    ss = jnp.concatenate([zeros_t, -sin_t, sin_t, zeros_t[:pad]], axis=0).T

    cqn = _rms_norm(h[:, 1024:1280], qg_ref[...]).astype(BF16)
    qq = _dot(cqn, wq_ref[...])
    for hd in range(MLA_HEADS):
        lo, hi = hd * LANES, (hd + 1) * LANES
        q_ref[:, lo:hi] = ((qq[:, lo:hi] * cc + qq[:, 1024 + lo:1024 + hi] * ss) * (MLA_SCALE * LOG2E)).astype(BF16)

    ckvn = _rms_norm(h[:, 1280:1408], kvg_ref[...]).astype(BF16)
    kv = _dot(ckvn, wkv_ref[...])
    k_rope = h[:, 1408:1536] * cc + h[:, 1536:1664] * ss
    for hd in range(MLA_HEADS):
        lo, hi = hd * LANES, (hd + 1) * LANES
        k_ref[:, lo:hi] = (kv[:, lo:hi] + k_rope).astype(BF16)
    _store_transposed(vt_ref, [kv[:, 1024 + p * LANES:1024 + (p + 1) * LANES] for p in range(MLA_HEADS // 2)])


def _ab_in(x2, pos3, win, lng, lnb, qg, kvg, wq, wkv, fc):
    tm = TM_PROJ
    row = lambda n: pl.BlockSpec((tm, n), lambda i: (i, 0))
    full = lambda a: pl.BlockSpec(a.shape, lambda i: (0,) * a.ndim)
    return pl.pallas_call(
        _ab_in_kernel,
        grid=(TOKENS // tm,),
        in_specs=[row(D_MODEL), pl.BlockSpec((1, 1, tm), lambda i: (i, 0, 0)), full(win), full(lng), full(lnb),
                  full(qg), full(kvg), full(wq), full(wkv), full(fc)],
        out_specs=[row(512), row(512), row(1024), row(1024),
                   pl.BlockSpec((MLA_HEADS // 2, tm // TK_ATT, MLA_VT_ROWS, TK_ATT), lambda i: (0, i, 0, 0))],
        out_shape=[jax.ShapeDtypeStruct((TOKENS, 512), BF16), jax.ShapeDtypeStruct((TOKENS, 512), BF16),
                   jax.ShapeDtypeStruct((TOKENS, 1024), BF16), jax.ShapeDtypeStruct((TOKENS, 1024), BF16),
                   jax.ShapeDtypeStruct((MLA_HEADS // 2, TOKENS // TK_ATT, MLA_VT_ROWS, TK_ATT), BF16)],
        compiler_params=pltpu.CompilerParams(dimension_semantics=("arbitrary",)),
        name="ab_in",
    )(x2, pos3, win, lng, lnb, qg, kvg, wq, wkv, fc)


def _mla_attn_kernel(q_ref, k_ref, vt_ref, o_ref, m_ref, l_ref, acc_ref):
    tq, tk = TQ_MLA, TK_ATT
    qi = pl.program_id(1)
    krow = lax.broadcasted_iota(jnp.int32, (tk, tq), 0)
    qcol = lax.broadcasted_iota(jnp.int32, (tk, tq), 1)
    top = lax.broadcasted_iota(jnp.int32, (LANES, tq), 0) < HALF
    m_ref[...] = jnp.full(m_ref.shape, NEG, F32)
    l_ref[...] = jnp.zeros(l_ref.shape, F32)
    acc_ref[...] = jnp.zeros(acc_ref.shape, F32)

    def tile(j, masked, chunks=1):
        r0 = pl.multiple_of(j * tk, tk)
        scores = [_dot_nt(k_ref[0, pl.ds(r0, chunks * tk), h * LANES:(h + 1) * LANES],
                          q_ref[0, :, h * LANES:(h + 1) * LANES])
                  for h in range(MLA_HEADS)]
        for pr in range(MLA_HEADS // 2):
            probs, alphas = [], []
            for h in (2 * pr, 2 * pr + 1):
                s = jnp.where(krow <= qcol, scores[h], NEG) if masked else scores[h]
                m_old = m_ref[h]
                m_new = jnp.maximum(m_old, jnp.max(s, axis=0, keepdims=True))
                alphas.append(jnp.exp2(m_old - m_new))
                probs.append(jnp.exp2(s - m_new).astype(BF16))
                m_ref[h] = m_new
            vt = jnp.concatenate([vt_ref[pr, j + c] for c in range(chunks)], axis=1)
            pv = _dot(vt, jnp.concatenate(probs, axis=1))
            for n, hh in enumerate((2 * pr, 2 * pr + 1)):
                l_ref[hh] = alphas[n] * l_ref[hh] + pv[LANES:LANES + 1, n * tq:(n + 1) * tq]
            a = jnp.where(top, alphas[0], alphas[1])
            acc_ref[pr] = a * acc_ref[pr] + jnp.where(top, pv[:LANES, :tq], pv[:LANES, tq:])

    def body(j, c):
        tile(2 * j, False, chunks=2)
        return c

    lax.fori_loop(0, qi // 2, body, 0)

    @pl.when(qi % 2 == 1)
    def _():
        tile(qi - 1, False)

    tile(qi, True)
    for pr in range(MLA_HEADS // 2):
        l = jnp.where(top, l_ref[2 * pr], l_ref[2 * pr + 1])
        o_ref[0, :, pr * LANES:(pr + 1) * LANES] = (acc_ref[pr] / l).T.astype(BF16)


def _mla_attn(q3, k3, vt):
    tq = TQ_MLA
    n_chunks = SEQ // TK_ATT
    return pl.pallas_call(
        _mla_attn_kernel,
        grid=(BATCH, SEQ // tq),
        in_specs=[pl.BlockSpec((1, tq, MLA_HEADS * LANES), lambda b, i: (b, i, 0)),
                  pl.BlockSpec((1, SEQ, MLA_HEADS * LANES), lambda b, i: (b, 0, 0)),
                  pl.BlockSpec((MLA_HEADS // 2, n_chunks, MLA_VT_ROWS, TK_ATT), lambda b, i: (0, b, 0, 0))],
        out_specs=pl.BlockSpec((1, tq, MLA_HEADS * MLA_V), lambda b, i: (b, i, 0)),
        out_shape=jax.ShapeDtypeStruct((BATCH, SEQ, MLA_HEADS * MLA_V), BF16),
        scratch_shapes=[pltpu.VMEM((MLA_HEADS, 1, tq), F32), pltpu.VMEM((MLA_HEADS, 1, tq), F32),
                        pltpu.VMEM((MLA_HEADS // 2, LANES, tq), F32)],
        compiler_params=pltpu.CompilerParams(dimension_semantics=("arbitrary",) * 2),
        name="mla_attn",
    )(q3, k3, vt)


def _router(x1, wr, br):
    tm = x1.shape[0]
    x_hi = x1.astype(BF16)
    x_lo = (x1 - x_hi.astype(F32)).astype(BF16)
    parts = _dot(jnp.concatenate([x_hi, x_lo], axis=0), wr)
    logits = (parts[:tm, :LANES] + (parts[:tm, LANES:] + parts[tm:, :LANES]) + parts[tm:, LANES:]) + br
    lane = lax.broadcasted_iota(jnp.int32, (tm, LANES), 1).astype(F32)
    big = 1e6
    is_g = (lane >= MOE_EXPERTS) & (lane < MOE_EXPERTS + MOE_GROUPS)
    gl = jnp.where(is_g, logits, NEG)
    gmax = jnp.max(gl, axis=-1, keepdims=True)
    g_sel = jnp.min(jnp.where(is_g & (gl == gmax), lane, big), axis=-1, keepdims=True) - MOE_EXPERTS
    g_w = 1.0 / jnp.sum(jnp.where(is_g, jnp.exp(gl - gmax), 0.0), axis=-1, keepdims=True)
    in_grp = (lane >= g_sel * MOE_EPG) & (lane < (g_sel + 1) * MOE_EPG)
    el = jnp.where(in_grp, logits, NEG)
    emax = jnp.max(el, axis=-1, keepdims=True)
    ee = jnp.where(in_grp, jnp.exp(el - emax), 0.0)
    pe = ee / jnp.sum(ee, axis=-1, keepdims=True)
    p1 = jnp.max(pe, axis=-1, keepdims=True)
    i1 = jnp.min(jnp.where(in_grp & (pe == p1), lane, big), axis=-1, keepdims=True)
    rest = in_grp & (lane != i1)
    pr = jnp.where(rest, pe, -1.0)
    p2 = jnp.max(pr, axis=-1, keepdims=True)
    i2 = jnp.min(jnp.where(rest & (pr == p2), lane, big), axis=-1, keepdims=True)
    tot = p1 + p2
    wf = jnp.where(lane == i1, p1 / tot * g_w, jnp.where(lane == i2, p2 / tot * g_w, 0.0))
    wf = jnp.where(lane == GSEL_LANE, g_sel, wf)
    cnt = jnp.sum(jnp.where(lane == g_sel, 1.0, 0.0), axis=0, keepdims=True)
    return wf, cnt


def _mix_out_kernel(*refs, gmlp):
    if gmlp:
        (x_ref, gu_ref, vn_ref, ws_ref, bs_ref, yb_ref, wo_ref, g_ref, b_ref, wr_ref, br_ref,
         x1_ref, x1b_ref, wf_ref, cnt_ref, ya_ref) = refs
        tm = x_ref.shape[0]
        r = lax.broadcasted_iota(jnp.int32, (GM_CHUNK, GM_CHUNK), 0)
        c = lax.broadcasted_iota(jnp.int32, (GM_CHUNK, GM_CHUNK), 1)
        for g in range(GM_GROUPS):
            ws = jnp.where(r >= c, ws_ref[g], 0.0).astype(BF16)
            bias = bs_ref[:, g:g + 1]
            for ch in range(tm // GM_CHUNK):
                rows = slice(ch * GM_CHUNK, (ch + 1) * GM_CHUNK)
                cols = slice(g * LANES, (g + 1) * LANES)
                s = _dot(ws, vn_ref[rows, cols]) + bias
                ya_ref[rows, cols] = (gu_ref[rows, cols].astype(F32) * s).astype(BF16)
        mix = _dot(ya_ref[...], wo_ref[0:GM_WIDTH, :]) + _dot(yb_ref[...], wo_ref[GM_WIDTH:, :])
    else:
        x_ref, y_ref, wo_ref, g_ref, b_ref, wr_ref, br_ref, x1_ref, x1b_ref, wf_ref, cnt_ref = refs
        mix = _dot(y_ref[...], wo_ref[...])
    x1 = _layer_norm(DN_ALPHA * x_ref[...] + mix, g_ref[...], b_ref[...])
    x1_ref[...] = x1
    x1b_ref[...] = x1.astype(BF16)
    wf, cnt = _router(x1, wr_ref[...], br_ref[...])
    wf_ref[...] = wf
    cnt_ref[0] = jnp.broadcast_to(cnt, cnt_ref.shape[1:])


def _mix_out(x2, ys, wo, g, b, wr, br, gm=None):
    tm = TM_PROJ
    row = lambda n: pl.BlockSpec((tm, n), lambda i: (i, 0))
    full = lambda a: pl.BlockSpec(a.shape, lambda i: (0,) * a.ndim)
    if gm is not None:
        gu, vn, ws, bs = gm
        args = (x2, gu, vn, ws, bs, ys, wo, g, b, wr, br)
        in_specs = [row(D_MODEL), row(512), row(512), full(ws), full(bs), row(512), full(wo),
                    full(g), full(b), full(wr), full(br)]
        scratch = [pltpu.VMEM((tm, GM_WIDTH), BF16)]
    else:
        args = (x2, ys, wo, g, b, wr, br)
        in_specs = [row(D_MODEL), row(C_MIX), full(wo), full(g), full(b), full(wr), full(br)]
        scratch = []
    return pl.pallas_call(
        functools.partial(_mix_out_kernel, gmlp=gm is not None),
        grid=(TOKENS // tm,),
        in_specs=in_specs,
        out_specs=[row(D_MODEL), row(D_MODEL), row(LANES), pl.BlockSpec((1, 8, LANES), lambda i: (i, 0, 0))],
        out_shape=[jax.ShapeDtypeStruct((TOKENS, D_MODEL), F32), jax.ShapeDtypeStruct((TOKENS, D_MODEL), BF16),
                   jax.ShapeDtypeStruct((TOKENS, LANES), F32),
                   jax.ShapeDtypeStruct((TOKENS // tm, 8, LANES), F32)],
        scratch_shapes=scratch,
        compiler_params=pltpu.CompilerParams(dimension_semantics=("arbitrary",)),
        name="mix_out_gmlp" if gm is not None else "mix_out",
    )(*args)


def _group_dest(wf, ltri_ref, ustr_ref):
    tm = wf.shape[0]
    lane = lax.broadcasted_iota(jnp.int32, (tm, LANES), 1).astype(F32)
    onehot = jnp.where(lane == wf[:, GSEL_LANE:GSEL_LANE + 1], 1.0, 0.0)
    before = _dot(ltri_ref[...], onehot.astype(BF16))
    cnt = jnp.sum(onehot, axis=0, keepdims=True)
    gran = jnp.floor((cnt + (GRAN - 1)) * (1.0 / GRAN))
    start = _dot(jnp.broadcast_to(gran, (8, LANES)).astype(BF16), ustr_ref[...])[0:1]
    return jnp.sum(onehot * (GRAN * start + before), axis=-1, keepdims=True)


def _granule_copies(tbl_ref, tile, vmem_bufs, hbm_refs, sems, to_hbm, act):
    for g in range(MOE_GROUPS):
        n = tbl_ref[tile * TBL_W + g]
        loc = tbl_ref[tile * TBL_W + MOE_GROUPS + g]
        glb = tbl_ref[tile * TBL_W + 2 * MOE_GROUPS + g]

        def body(k, c, loc=loc, glb=glb):
            lo = pl.multiple_of((loc + k) * GRAN, GRAN)
            hi = pl.multiple_of((glb + k) * GRAN, GRAN)
            for idx, (vb, hb) in enumerate(zip(vmem_bufs, hbm_refs)):
                v_sl, h_sl = vb.at[pl.ds(lo, GRAN)], hb.at[pl.ds(hi, GRAN)]
                src, dst = (v_sl, h_sl) if to_hbm else (h_sl, v_sl)
                act(pltpu.make_async_copy(src, dst, sems[idx]), idx)
            return c

        lax.fori_loop(0, n, body, 0)


def _zero_fill_copies(tbl_ref, zero_bufs, hbm_refs, sems, act):
    tail = (TOKENS // TM_PROJ) * TBL_W
    for n in range(MOE_GROUPS + 1):
        first = tbl_ref[tail + 2 * n]

        def body(k, c, first=first):
            hi = pl.multiple_of((first + k) * GRAN, GRAN)
            for idx, (zb, hb) in enumerate(zip(zero_bufs, hbm_refs)):
                act(pltpu.make_async_copy(zb, hb.at[pl.ds(hi, GRAN)], sems[idx]), idx)
            return c

        lax.fori_loop(0, tbl_ref[tail + 2 * n + 1], body, 0)


def _dispatch_kernel(tbl_ref, xb_ref, wf_ref, ltri_ref, ustr_ref, xs_out, ws_out, xbuf, wbuf, zx, zw, sems):
    tile = pl.program_id(0)
    slot = tile % 2
    tm = xb_ref.shape[0]
    wf = wf_ref[...]
    dest = _group_dest(wf, ltri_ref, ustr_ref)
    dest_row = jnp.broadcast_to(dest, (tm, LANES)).T[0:1]
    r = lax.broadcasted_iota(jnp.int32, (ROWS_LOCAL, tm), 0).astype(F32)
    perm = jnp.where(r == dest_row, 1.0, 0.0).astype(BF16)
    xbuf[slot] = _dot(perm, xb_ref[...]).astype(BF16)
    hi = wf.astype(BF16)
    r1 = wf - hi.astype(F32)
    mid = r1.astype(BF16)
    lo = (r1 - mid.astype(F32)).astype(BF16)
    pieces = _dot(perm, jnp.concatenate([hi, mid, lo], axis=1))
    wbuf[slot] = pieces[:, :LANES] + pieces[:, LANES:2 * LANES] + pieces[:, 2 * LANES:]

    def copies(t, s, act):
        _granule_copies(tbl_ref, t, (xbuf.at[s], wbuf.at[s]), (xs_out, ws_out), (sems.at[s, 0], sems.at[s, 1]),
                        True, act)

    start = lambda c, k: c.start(priority=k)
    wait = lambda c, k: c.wait()
    copies(tile, slot, start)

    @pl.when(tile > 0)
    def _():
        copies(tile - 1, 1 - slot, wait)

    @pl.when(tile == pl.num_programs(0) - 1)
    def _():
        zx[...] = jnp.zeros_like(zx)
        zw[...] = jnp.zeros_like(zw)
        fill = functools.partial(_zero_fill_copies, tbl_ref, (zx, zw), (xs_out, ws_out), (sems.at[2, 0], sems.at[2, 1]))
        fill(start)
        copies(tile, slot, wait)
        fill(wait)


def _dispatch(tbl, x1b, wf, ltri, ustr):
    tm = TM_PROJ
    row = lambda n: pl.BlockSpec((tm, n), lambda i, t: (i, 0))
    full = lambda a: pl.BlockSpec(a.shape, lambda i, t: (0,) * a.ndim)
    anyspace = pl.BlockSpec(memory_space=pl.ANY)
    return pl.pallas_call(
        _dispatch_kernel,
        grid_spec=pltpu.PrefetchScalarGridSpec(
            num_scalar_prefetch=1, grid=(TOKENS // tm,),
            in_specs=[row(D_MODEL), row(LANES), full(ltri), full(ustr)],
            out_specs=[anyspace, anyspace],
            scratch_shapes=[pltpu.VMEM((2, ROWS_LOCAL, D_MODEL), BF16), pltpu.VMEM((2, ROWS_LOCAL, LANES), F32),
                            pltpu.VMEM((GRAN, D_MODEL), BF16), pltpu.VMEM((GRAN, LANES), F32),
                            pltpu.SemaphoreType.DMA((3, 2))]),
        out_shape=[jax.ShapeDtypeStruct((ROWS_SORTED, D_MODEL), BF16),
                   jax.ShapeDtypeStruct((ROWS_SORTED, LANES), F32)],
        compiler_params=pltpu.CompilerParams(dimension_semantics=("arbitrary",)),
        name="moe_dispatch",
    )(tbl, x1b, wf, ltri, ustr)


def _experts_kernel(gid_ref, valid_ref, xs_ref, ws_ref, wg_ref, wu_ref, wd_ref, y_ref):
    i = pl.program_id(0)

    @pl.when(valid_ref[i] == 1)
    def _():
        x = xs_ref[...]
        gates = [_dot(x, wg_ref[e]) for e in range(MOE_EPG)]
        ups = [_dot(x, wu_ref[e]) for e in range(MOE_EPG)]
        ws = ws_ref[...]
        lane = lax.broadcasted_iota(jnp.int32, ws.shape, 1)
        hidden = []
        for e in range(MOE_EPG):
            w_tok = jnp.sum(jnp.where(lane == gid_ref[i] * MOE_EPG + e, ws, 0.0), axis=-1, keepdims=True)
            hidden.append((gates[e] * jax.nn.sigmoid(gates[e]) * ups[e] * w_tok).astype(BF16))
        wd = wd_ref[...].reshape(MOE_EPG * MOE_HIDDEN, D_MODEL)
        y_ref[...] = _dot(jnp.concatenate(hidden, axis=1), wd).astype(BF16)

    @pl.when(valid_ref[i] == 0)
    def _():
        y_ref[...] = jnp.zeros_like(y_ref)


def _experts(gid, valid, xs, ws, wg, wu, wd):
    tm = TM_MOE
    row = lambda n: pl.BlockSpec((tm, n), lambda i, gid, valid: (i, 0))
    wspec = lambda a, b: pl.BlockSpec((MOE_EPG, a, b), lambda i, gid, valid: (gid[i], 0, 0))
    return pl.pallas_call(
        _experts_kernel,
        grid_spec=pltpu.PrefetchScalarGridSpec(
            num_scalar_prefetch=2, grid=(ROWS_SORTED // tm,),
            in_specs=[row(D_MODEL), row(LANES), wspec(D_MODEL, MOE_HIDDEN), wspec(D_MODEL, MOE_HIDDEN),
                      wspec(MOE_HIDDEN, D_MODEL)],
            out_specs=row(D_MODEL)),
        out_shape=jax.ShapeDtypeStruct((ROWS_SORTED, D_MODEL), BF16),
        compiler_params=pltpu.CompilerParams(dimension_semantics=("arbitrary",),
                                             vmem_limit_bytes=EXPERTS_VMEM_BYTES),
        name="moe_experts",
    )(gid, valid, xs, ws, wg, wu, wd)


def _combine_kernel(tbl_ref, wf_ref, x1_ref, ltri_ref, ustr_ref, g_ref, b_ref, y_hbm, o_ref, ybuf, sems):
    tile = pl.program_id(0)
    slot = tile % 2
    tm = x1_ref.shape[0]

    def copies(t, s, act):
        _granule_copies(tbl_ref, t, (ybuf.at[s],), (y_hbm,), (sems.at[s],), False, act)

    def fetch(t, s):
        ybuf[s] = jnp.zeros(ybuf.shape[1:], ybuf.dtype)
        copies(t, s, lambda c, k: c.start(priority=1))

    @pl.when(tile == 0)
    def _():
        fetch(tile, slot)

    @pl.when(tile + 1 < pl.num_programs(0))
    def _():
        fetch(tile + 1, 1 - slot)

    dest = _group_dest(wf_ref[...], ltri_ref, ustr_ref)
    c = lax.broadcasted_iota(jnp.int32, (tm, ROWS_LOCAL), 1).astype(F32)
    unperm = jnp.where(c == dest, 1.0, 0.0).astype(BF16)
    copies(tile, slot, lambda c, k: c.wait())
    y_sorted = ybuf[slot]
    halves = (slice(0, tm // 2), slice(tm // 2, tm))
    ffn = [_dot(unperm[rows], y_sorted) for rows in halves]
    for rows, part in zip(halves, ffn):
        o_ref[rows, :] = _layer_norm(DN_ALPHA * x1_ref[rows, :] + part, g_ref[...], b_ref[...])


def _combine(tbl, wf, x1, ltri, ustr, g, b, y):
    tm = TM_PROJ
    row = lambda n: pl.BlockSpec((tm, n), lambda i, t: (i, 0))
    full = lambda a: pl.BlockSpec(a.shape, lambda i, t: (0,) * a.ndim)
    return pl.pallas_call(
        _combine_kernel,
        grid_spec=pltpu.PrefetchScalarGridSpec(
            num_scalar_prefetch=1, grid=(TOKENS // tm,),
            in_specs=[row(LANES), row(D_MODEL), full(ltri), full(ustr), full(g), full(b),
                      pl.BlockSpec(memory_space=pl.ANY)],
            out_specs=row(D_MODEL),
            scratch_shapes=[pltpu.VMEM((2, ROWS_LOCAL, D_MODEL), BF16), pltpu.SemaphoreType.DMA((2,))]),
        out_shape=jax.ShapeDtypeStruct((TOKENS, D_MODEL), F32),
        compiler_params=pltpu.CompilerParams(dimension_semantics=("arbitrary",)),
        name="moe_combine",
    )(tbl, wf, x1, ltri, ustr, g, b, y)


def _routing_tables(cnt):
    n_tiles = cnt.shape[0]
    c = cnt[:, 0, :MOE_GROUPS].astype(jnp.int32)
    gran = (c + GRAN - 1) // GRAN
    local = jnp.cumsum(gran, axis=1) - gran
    per_tile = TM_MOE // GRAN
    tiles_g = (jnp.sum(gran, axis=0) + per_tile - 1) // per_tile
    ends = jnp.cumsum(tiles_g)
    base = (ends - tiles_g) * per_tile
    glob = base[None, :] + jnp.cumsum(gran, axis=0) - gran
    used = jnp.sum(gran, axis=0)
    pad_first = jnp.concatenate([base + used, ends[-1:] * per_tile])
    pad_count = jnp.concatenate([tiles_g * per_tile - used, ROWS_SORTED // GRAN - ends[-1:] * per_tile])
    tail = jnp.stack([pad_first, pad_count], axis=1).reshape(-1)
    tbl = jnp.concatenate([jnp.concatenate([gran, local, glob], axis=1).reshape(n_tiles * TBL_W), tail])
    idx = jnp.arange(ROWS_SORTED // TM_MOE)
    gid = jnp.minimum(jnp.sum(idx[:, None] >= ends[None, :], axis=1), MOE_GROUPS - 1).astype(jnp.int32)
    valid = (idx < ends[-1]).astype(jnp.int32)
    return tbl, gid, valid


def _sort_tables():
    t = jnp.arange(TM_PROJ)
    ltri = (t[None, :] < t[:, None]).astype(BF16)
    l = jnp.arange(LANES)
    ustr = (l[:, None] < l[None, :]).astype(BF16)
    return ltri, ustr


def _dup_halves(t):
    lane = lax.broadcasted_iota(jnp.int32, t.shape, 1)
    r = pltpu.roll(t, HALF, 1)
    return jnp.where(lane < HALF, t, r), jnp.where(lane < HALF, r, t)


def _nsa_in_kernel(x_ref, w_ref, gb_ref, q_ref, kc_ref, vc_ref, ks_ref, vst_ref, kw_ref, vwt_ref, gate_ref):
    h = _dot(x_ref[...].astype(BF16), w_ref[...])
    q_ref[...] = (h[:, 0:C_MIX] * (NSA_SCALE * LOG2E)).astype(BF16)
    kc_ref[...] = h[:, 1024:1152].astype(BF16)
    vc_ref[...] = h[:, 1152:1280].astype(BF16)
    tm = h.shape[0]
    lane = lax.broadcasted_iota(jnp.int32, (tm, LANES), 1)
    for g, d in enumerate(_dup_halves(h[:, 1536:1664])):
        kw_ref[:, g * LANES:(g + 1) * LANES] = d.astype(BF16)
    pos = (pl.program_id(0) * tm) % SEQ + lax.broadcasted_iota(jnp.int32, (tm, LANES), 0)
    block_onehot = jnp.where(lane == pos // NSA_SEL_LEN, 1.0, 0.0).astype(BF16)
    for g, d in enumerate(_dup_halves(h[:, 1280:1408])):
        ks_ref[:, 2 * g * LANES:(2 * g + 1) * LANES] = d.astype(BF16)
        ks_ref[:, (2 * g + 1) * LANES:(2 * g + 2) * LANES] = block_onehot
    for idx, ref in ((1, vst_ref), (3, vwt_ref)):
        tail = jnp.where(lane == HALF, 1.0, 0.0)
        _store_transposed(ref, [jnp.where(lane < HALF, d, tail)
                                for d in _dup_halves(h[:, 1280 + idx * LANES:1280 + (idx + 1) * LANES])])
    for g in range(NSA_GROUPS):
        gate_ref[g] = jax.nn.sigmoid(h[:, 1792 + g * LANES:1792 + (g + 1) * LANES] + gb_ref[g])


def _nsa_in(x2, w, gb):
    tm = TM_PROJ
    row = lambda n: pl.BlockSpec((tm, n), lambda i: (i, 0))
    full = lambda a: pl.BlockSpec(a.shape, lambda i: (0,) * a.ndim)
    sd = jax.ShapeDtypeStruct
    vt_spec = lambda w: pl.BlockSpec((NSA_GROUPS, tm // w, LANES, w), lambda i: (0, i, 0, 0))
    vt_shape = lambda w: sd((NSA_GROUPS, TOKENS // w, LANES, w), BF16)
    return pl.pallas_call(
        _nsa_in_kernel,
        grid=(TOKENS // tm,),
        in_specs=[row(D_MODEL), full(w), full(gb)],
        out_specs=[row(C_MIX), row(LANES), row(LANES), row(4 * LANES), vt_spec(TK_ATT), row(2 * LANES),
                   vt_spec(WIN_SUB), pl.BlockSpec((NSA_GROUPS, tm, LANES), lambda i: (0, i, 0))],
        out_shape=[sd((TOKENS, C_MIX), BF16), sd((TOKENS, LANES), BF16), sd((TOKENS, LANES), BF16),
                   sd((TOKENS, 4 * LANES), BF16), vt_shape(TK_ATT), sd((TOKENS, 2 * LANES), BF16), vt_shape(WIN_SUB),
                   sd((NSA_GROUPS, TOKENS, LANES), F32)],
        compiler_params=pltpu.CompilerParams(dimension_semantics=("arbitrary",)),
        name="nsa_in",
    )(x2, w, gb)


def _compress_kernel(kc_ref, vc_ref, pk_ref, pv_ref, wk1_ref, wv1_ref, wk2_ref, wv2_ref, ko_ref, vo_ref):
    for a_ref, p_ref, w1_ref, w2_ref, o_ref in ((kc_ref, pk_ref, wk1_ref, wk2_ref, ko_ref),
                                                (vc_ref, pv_ref, wv1_ref, wv2_ref, vo_ref)):
        a = a_ref[0].astype(F32)
        a0 = (a + p_ref[0]).astype(BF16)
        a1 = (a + p_ref[1]).astype(BF16)
        outs = []
        for g in range(NSA_GROUPS):
            first = _dot(a0, w1_ref[g])
            second = _dot(a1, w1_ref[NSA_GROUPS + g])
            hid = first + pltpu.roll(second, NSA_NCMP - 1, 0)
            outs.append(_dot(_gelu(hid).astype(BF16), w2_ref[...]))
        if o_ref is ko_ref:
            o_ref[0] = jnp.concatenate(outs, axis=1).astype(BF16)
        else:
            for g in range(NSA_GROUPS):
                o_ref[0, g] = outs[g].T.astype(BF16)


def _compress(kc_r, vc_r, pk, pv, wk1, wv1, wk2, wv2):
    blk = pl.BlockSpec((1, NSA_NCMP, NSA_CMP_STRIDE * LANES), lambda b: (b, 0, 0))
    full = lambda a: pl.BlockSpec(a.shape, lambda b: (0,) * a.ndim)
    sd = jax.ShapeDtypeStruct
    return pl.pallas_call(
        _compress_kernel,
        grid=(BATCH,),
        in_specs=[blk, blk, full(pk), full(pv), full(wk1), full(wv1), full(wk2), full(wv2)],
        out_specs=[pl.BlockSpec((1, NSA_NCMP, 2 * LANES), lambda b: (b, 0, 0)),
                   pl.BlockSpec((1, NSA_GROUPS, LANES, NSA_NCMP), lambda b: (b, 0, 0, 0))],
        out_shape=[sd((BATCH, NSA_NCMP, 2 * LANES), BF16), sd((BATCH, NSA_GROUPS, LANES, NSA_NCMP), BF16)],
        compiler_params=pltpu.CompilerParams(dimension_semantics=("arbitrary",)),
        name="nsa_compress",
    )(kc_r, vc_r, pk, pv, wk1, wv1, wk2, wv2)


def _nsa_attn_kernel(q_ref, kc_ref, vct_ref, ks_ref, vst_ref, kw_ref, vwt_ref, gate_ref, cover_ref,
                     o_ref, m_ref, acc_ref):
    tq, tk, hpg = TQ_NSA, TK_ATT, NSA_HPG
    qi = pl.program_id(2)
    q0 = qi * tq
    lane = lax.broadcasted_iota(jnp.int32, (tq, LANES), 1)
    t_tok = q0 + lax.broadcasted_iota(jnp.int32, (1, tq), 1)
    head = lambda x, i: x[:, i * tq:(i + 1) * tq]
    heads = range(hpg)

    parts = []
    for p in range(hpg // 2):
        qp = q_ref[0, :, p * LANES:(p + 1) * LANES]
        zero = jnp.zeros_like(qp)
        parts.append(jnp.where(lane < HALF, qp, zero))
        parts.append(jnp.where(lane < HALF, zero, qp))
    qs = jnp.concatenate(parts, axis=0)

    kb_sub = lax.broadcasted_iota(jnp.int32, (WIN_BAND, 1), 0)

    def window_scores(part):
        q_lo = q0 + part * WIN_SUB
        start = pl.multiple_of(jnp.maximum(q_lo - NSA_WINDOW, 0), WIN_SUB)
        qs_part = jnp.concatenate([qs[i * tq + part * WIN_SUB:i * tq + (part + 1) * WIN_SUB] for i in heads], axis=0)
        return q_lo, start, _dot_nt(kw_ref[0, pl.ds(start, WIN_BAND), :], qs_part)

    def window_finish(q_lo, start, s_w):
        kpos = start + kb_sub
        t_part = q_lo + lax.broadcasted_iota(jnp.int32, (1, WIN_SUB), 1)
        in_win = (kpos <= t_part) & (kpos > t_part - NSA_WINDOW)
        vw_band = jnp.concatenate([vwt_ref[0, start // WIN_SUB + c] for c in range(WIN_BAND // WIN_SUB)], axis=1)
        out = []
        for i0 in range(0, hpg, 2):
            e_w = []
            for i in (i0, i0 + 1):
                sm = jnp.where(in_win, s_w[:, i * WIN_SUB:(i + 1) * WIN_SUB], NEG)
                e_w.append(jnp.exp2((sm - jnp.max(sm, axis=0, keepdims=True)).astype(BF16)))
            o_pair = _dot(vw_band, jnp.concatenate(e_w, axis=1))
            out += [o_pair[:, :WIN_SUB], o_pair[:, WIN_SUB:]]
        return out

    s_c = _dot_nt(kc_ref[0], qs)
    win0 = window_scores(0)
    n_sub = lax.broadcasted_iota(jnp.int32, (NSA_NCMP, 1), 0)
    vis = t_tok >= n_sub * NSA_CMP_STRIDE + (NSA_CMP_LEN - 1)
    sees_any = t_tok >= NSA_CMP_LEN - 1
    p_sum = jnp.zeros((NSA_NCMP, tq), F32)
    p_c = []
    for i in heads:
        sm = jnp.where(vis, head(s_c, i), NEG)
        e = jnp.exp2(sm - jnp.max(sm, axis=0, keepdims=True))
        p = e * jnp.where(sees_any, 1.0 / jnp.sum(e, axis=0, keepdims=True), 0.0)
        p_sum = p_sum + p
        p_c.append(p.astype(BF16))
    o_c = _dot(vct_ref[0, 0], jnp.concatenate(p_c, axis=1))

    imp = jnp.dot(cover_ref[...], p_sum, preferred_element_type=F32,
                  precision=lax.Precision.HIGHEST)[0:NSA_NSEL]
    o_w_parts = [window_finish(*win0)]
    o_w_parts += [window_finish(*window_scores(part)) for part in range(1, tq // WIN_SUB)]
    jj = lax.broadcasted_iota(jnp.int32, (NSA_NSEL, 1), 0)
    tb = t_tok // NSA_SEL_LEN
    forced = (jj == 0) | (jj == tb) | (jj == tb - 1)
    score = jnp.where(forced, NSA_FORCE, jnp.where(jj <= tb, imp, -NSA_FORCE))
    sub = 8
    rows = [score[b * sub:(b + 1) * sub] for b in range(NSA_NSEL // sub)]
    ranks = [jnp.zeros((sub, tq), jnp.int32) for _ in rows]
    j_in = lax.broadcasted_iota(jnp.int32, (sub, 1), 0)
    for i in range(NSA_NSEL):
        si = score[i:i + 1, :]
        for b, blk in enumerate(rows):
            if b < i // sub:
                beats = si > blk
            elif b > i // sub:
                beats = si >= blk
            else:
                beats = (si > blk) | ((si == blk) & (j_in > i % sub))
            ranks[b] = ranks[b] + beats.astype(jnp.int32)
    rank = jnp.concatenate(ranks, axis=0)
    sel_bias = jnp.where(rank < NSA_TOPK, 0.0, NEG)
    bias_rows = jnp.concatenate([sel_bias, jnp.zeros((LANES - NSA_NSEL, tq), F32)], axis=0).T.astype(BF16)
    qs_sel = jnp.concatenate([qs, jnp.concatenate([bias_rows] * hpg, axis=0)], axis=1)

    m_ref[...] = jnp.full(m_ref.shape, NEG, F32)
    acc_ref[...] = jnp.zeros(acc_ref.shape, F32)
    k_sub = lax.broadcasted_iota(jnp.int32, (tk, 1), 0)
    diag = q0 // tk
    causal = diag * tk + k_sub <= t_tok
    per_dot = max(1, 2 * LANES // tq)

    def process(kt, diagonal, chunks=1):
        s = _dot_nt(ks_ref[0, pl.ds(pl.multiple_of(kt * tk, tk), chunks * tk), :], qs_sel)
        vt = jnp.concatenate([vst_ref[0, kt + c] for c in range(chunks)], axis=1)
        for i0 in range(0, hpg, per_dot):
            cols = slice(i0 * tq, (i0 + per_dot) * tq)
            probs, alphas = [], []
            for i in range(i0, i0 + per_dot):
                for part in range(tq // LANES):
                    sub = slice(part * LANES, (part + 1) * LANES)
                    sm = s[:, i * tq + part * LANES:i * tq + (part + 1) * LANES]
                    if diagonal:
                        sm = jnp.where(causal[:, sub], sm, NEG)
                    m_old = m_ref[i, :, sub]
                    m_new = jnp.maximum(m_old, jnp.max(sm, axis=0, keepdims=True))
                    m_ref[i, :, sub] = m_new
                    probs.append(jnp.exp2((sm - m_new).astype(BF16)))
                    alphas.append(jnp.exp2(m_old - m_new))
            pv = _dot(vt, jnp.concatenate(probs, axis=1))
            acc_ref[:, cols] = jnp.concatenate(alphas, axis=1) * acc_ref[:, cols] + pv

    def pair(k, c):
        process(2 * k, False, chunks=2)
        return c

    lax.fori_loop(0, diag // 2, pair, 0)

    @pl.when(diag % 2 == 1)
    def _():
        process(diag - 1, False)

    process(diag, True)

    o_w = [jnp.concatenate([part[i] for part in o_w_parts], axis=1) for i in heads]

    gt = gate_ref[0, 0].T
    o_s = acc_ref[...]
    outs = []
    for i in heads:
        c_i, s_i, w_i = head(o_c, i), head(o_s, i), o_w[i]
        outs.append(gt[i:i + 1] * c_i[:HALF]
                    + gt[hpg + i:hpg + i + 1] / s_i[HALF:HALF + 1] * s_i[:HALF]
                    + gt[2 * hpg + i:2 * hpg + i + 1] / w_i[HALF:HALF + 1] * w_i[:HALF])
    for p in range(hpg // 2):
        pair = jnp.concatenate([outs[2 * p], outs[2 * p + 1]], axis=0)
        o_ref[0, :, p * LANES:(p + 1) * LANES] = pair.T.astype(BF16)


def _nsa_attn(q3, kc2, vct, ks3, vst, kw3, vwt, gates4, cover_t):
    tq = TQ_NSA
    half_w = NSA_HPG * NSA_DH
    kv = pl.BlockSpec((1, SEQ, LANES), lambda b, g, i: (b, 0, g))
    vt = lambda w: pl.BlockSpec((1, SEQ // w, LANES, w), lambda b, g, i: (g, b, 0, 0))
    full = lambda a: pl.BlockSpec(a.shape, lambda b, g, i: (0,) * a.ndim)
    return pl.pallas_call(
        _nsa_attn_kernel,
        grid=(BATCH, NSA_GROUPS, SEQ // tq),
        in_specs=[pl.BlockSpec((1, tq, half_w), lambda b, g, i: (b, i, g)),
                  pl.BlockSpec((1, NSA_NCMP, LANES), lambda b, g, i: (b, 0, g)),
                  pl.BlockSpec((1, 1, LANES, NSA_NCMP), lambda b, g, i: (b, g, 0, 0)),
                  pl.BlockSpec((1, SEQ, 2 * LANES), lambda b, g, i: (b, 0, g)), vt(TK_ATT), kv, vt(WIN_SUB),
                  pl.BlockSpec((1, 1, tq, LANES), lambda b, g, i: (g, b, i, 0)),
                  full(cover_t)],
        out_specs=pl.BlockSpec((1, tq, half_w), lambda b, g, i: (b, i, g)),
        out_shape=jax.ShapeDtypeStruct((BATCH, SEQ, C_MIX), BF16),
        scratch_shapes=[pltpu.VMEM((NSA_HPG, 1, tq), F32), pltpu.VMEM((LANES, NSA_HPG * tq), F32)],
        compiler_params=pltpu.CompilerParams(dimension_semantics=("arbitrary",) * 3),
        name="nsa_attn",
    )(q3, kc2, vct, ks3, vst, kw3, vwt, gates4, cover_t)


def _rope_tables():
    half = MLA_ROPE // 2
    freq = jnp.exp(-math.log(ROPE_BASE) * jnp.arange(half, dtype=F32) / half)
    return freq[:, None]


def _swap_halves(w):
    half = w.shape[-1] // 2
    return jnp.concatenate([w[..., half:], w[..., :half]], axis=-1)


def _pad_last(w, n):
    return jnp.pad(w, [(0, 0)] * (w.ndim - 1) + [(0, n - w.shape[-1])])


def _ab_weights(w_in, w_uq, w_uk, w_uv):
    w_kr = w_in[:, 1408:1440]
    place = lambda w: jnp.pad(w, ((0, 0), (MLA_NOPE, LANES - MLA_NOPE - MLA_ROPE)))
    win = jnp.concatenate([w_in[:, :1408], place(w_kr), place(_swap_halves(w_kr))], axis=1).astype(BF16)
    uq = w_uq.reshape(MLA_Q_RANK, MLA_HEADS, MLA_NOPE + MLA_ROPE)
    nope, rp = uq[..., :MLA_NOPE], uq[..., MLA_NOPE:]
    q_pad = _pad_last(jnp.concatenate([nope, rp], -1), LANES).reshape(MLA_Q_RANK, MLA_HEADS * LANES)
    q_sw = _pad_last(jnp.concatenate([jnp.zeros_like(nope), _swap_halves(rp)], -1), LANES)
    wq = jnp.concatenate([q_pad, q_sw.reshape(MLA_Q_RANK, MLA_HEADS * LANES)], axis=1).astype(BF16)
    k_pad = _pad_last(w_uk.reshape(MLA_KV_RANK, MLA_HEADS, MLA_NOPE), LANES).reshape(MLA_KV_RANK, -1)
    wkv = jnp.concatenate([k_pad, w_uv], axis=1).astype(BF16)
    return win, wq, wkv


def _router_weights(w_rg, b_rg, w_re, b_re):
    wr = _pad_last(jnp.concatenate([w_re, w_rg], axis=1), LANES)
    wr_hi = wr.astype(BF16)
    wr_lo = (wr - wr_hi.astype(F32)).astype(BF16)
    br = _pad_last(jnp.concatenate([b_re, b_rg])[None, :], LANES)
    return jnp.concatenate([wr_hi, wr_lo], axis=1), br


def _nsa_in_weights(w_in, gate_b):
    g_cols = w_in[:, C_MIX + 768:].reshape(D_MODEL, 3, NSA_GROUPS, NSA_HPG)
    g_blocks = [_pad_last(g_cols[:, :, g, :].reshape(D_MODEL, 3 * NSA_HPG), LANES) for g in range(NSA_GROUPS)]
    w = jnp.concatenate([w_in[:, :C_MIX + 768]] + g_blocks, axis=1).astype(BF16)
    gb = gate_b.reshape(3, NSA_GROUPS, NSA_HPG)
    gb = jnp.stack([_pad_last(gb[:, g, :].reshape(1, 3 * NSA_HPG), LANES) for g in range(NSA_GROUPS)])
    return w, gb


def _compress_weights(pos, w1, w2):
    w1r = w1.reshape(2, NSA_CMP_STRIDE, NSA_DH, NSA_CMP_HIDDEN)
    zero = jnp.zeros_like(w1r)
    per_g = []
    for g in range(NSA_GROUPS):
        parts = [w1r if gg == g else zero for gg in range(NSA_GROUPS)]
        per_g.append(jnp.stack(parts, axis=2).reshape(2, NSA_CMP_STRIDE * LANES, NSA_CMP_HIDDEN))
    w1x = jnp.stack(per_g, axis=1).reshape(2 * NSA_GROUPS, NSA_CMP_STRIDE * LANES, NSA_CMP_HIDDEN)
    posr = pos.reshape(2, NSA_CMP_STRIDE, 1, NSA_DH)
    posx = jnp.broadcast_to(posr, (2, NSA_CMP_STRIDE, NSA_GROUPS, NSA_DH)).reshape(2, 1, NSA_CMP_STRIDE * LANES)
    w2x = jnp.concatenate([w2, w2], axis=1)
    return posx, w1x.astype(BF16), w2x.astype(BF16)


def _selection_tables():
    n = jnp.arange(LANES)[:, None]
    j = jnp.arange(LANES)[None, :]
    c0 = n * NSA_CMP_STRIDE
    s0 = j * NSA_SEL_LEN
    cover = ((c0 < s0 + NSA_SEL_LEN) & (c0 + NSA_CMP_LEN > s0) & (n < NSA_NCMP - 1) & (j < NSA_NSEL))
    return jnp.transpose(cover).astype(F32)


def kernel(x, positions, ab_w_in, ab_gm_ln_g, ab_gm_ln_b, ab_gm_ws, ab_gm_bs, ab_mla_q_norm,
           ab_mla_kv_norm, ab_mla_w_uq, ab_mla_w_uk, ab_mla_w_uv, ab_w_o, c_w_in, c_cmp_pos, c_w_ck1,
           c_w_ck2, c_w_cv1, c_w_cv2, c_gate_b, c_w_o, moe_w_rg, moe_b_rg, moe_w_re, moe_b_re,
           moe_w_gate, moe_w_up, moe_w_down, ln1_g, ln1_b, ln2_g, ln2_b):
    x2 = x.reshape(TOKENS, D_MODEL)
    pos3 = positions.reshape(TOKENS // TM_PROJ, 1, TM_PROJ)
    vec = lambda a: a[None, :]

    ltri, ustr = _sort_tables()

    def moe_layer(layer, x1b, x1, wf, cnt):
        tbl, gid, valid = _routing_tables(cnt)
        xs, ws = _dispatch(tbl, x1b, wf, ltri, ustr)
        y = _experts(gid, valid, xs, ws, moe_w_gate[layer].astype(BF16), moe_w_up[layer].astype(BF16),
                     moe_w_down[layer].astype(BF16))
        return _combine(tbl, wf, x1, ltri, ustr, vec(ln2_g[layer]), vec(ln2_b[layer]), y)

    win, wq, wkv = _ab_weights(ab_w_in[0], ab_mla_w_uq[0], ab_mla_w_uk[0], ab_mla_w_uv[0])
    gu, vn, q, k, vt = _ab_in(x2, pos3, win, vec(ab_gm_ln_g[0]), vec(ab_gm_ln_b[0]), vec(ab_mla_q_norm[0]),
                              vec(ab_mla_kv_norm[0]), wq, wkv, _rope_tables())
    yb = _mla_attn(q.reshape(BATCH, SEQ, -1), k.reshape(BATCH, SEQ, -1), vt)
    wr, br = _router_weights(moe_w_rg[0], moe_b_rg[0], moe_w_re[0], moe_b_re[0])
    x1, x1b, wf, cnt = _mix_out(x2, yb.reshape(TOKENS, -1), ab_w_o[0].astype(BF16), vec(ln1_g[0]),
                                vec(ln1_b[0]), wr, br, gm=(gu, vn, ab_gm_ws[0], jnp.transpose(ab_gm_bs[0])))
    x2 = moe_layer(0, x1b, x1, wf, cnt)

    w_nsa, gb = _nsa_in_weights(c_w_in[0], c_gate_b[0])
    q, kc, vc, ks, vst, kw, vwt, gates = _nsa_in(x2, w_nsa, gb)
    pk, wk1, wk2 = _compress_weights(c_cmp_pos[0, 0], c_w_ck1[0], c_w_ck2[0])
    pv, wv1, wv2 = _compress_weights(c_cmp_pos[0, 1], c_w_cv1[0], c_w_cv2[0])
    blocks = lambda a: a.reshape(BATCH, NSA_NCMP, NSA_CMP_STRIDE * LANES)
    kc2, vct = _compress(blocks(kc), blocks(vc), pk, pv, wk1, wv1, wk2, wv2)
    b3 = lambda a: a.reshape(BATCH, SEQ, -1)
    o = _nsa_attn(b3(q), kc2, vct, b3(ks), vst, b3(kw), vwt,
                  gates.reshape(NSA_GROUPS, BATCH, SEQ, LANES), _selection_tables())
    wr, br = _router_weights(moe_w_rg[1], moe_b_rg[1], moe_w_re[1], moe_b_re[1])
    x1, x1b, wf, cnt = _mix_out(x2, o.reshape(TOKENS, -1), c_w_o[0].astype(BF16), vec(ln1_g[1]), vec(ln1_b[1]),
                                wr, br)
    x2 = moe_layer(1, x1b, x1, wf, cnt)
    return x2.reshape(BATCH, SEQ, D_MODEL)
```

```python
import functools
import math

import jax
import jax.numpy as jnp
from jax import lax
from jax.experimental import pallas as pl
from jax.experimental.pallas import tpu as pltpu

F32 = jnp.float32
BF16 = jnp.bfloat16

D_MODEL = 1024
BATCH = 16
SEQ = 2048
TOKENS = BATCH * SEQ
DEPTH = 2
DN_ALPHA = (2.0 * DEPTH) ** 0.25
LN_EPS = 1e-5
NEG = -1e30
LOG2E = math.log2(math.e)
LANES = 128
HALF = LANES // 2

GM_WIDTH = 512
GM_GROUPS = 4
GM_CHUNK = 128

MLA_HEADS = 8
MLA_NOPE = 64
MLA_ROPE = 32
MLA_V = 64
MLA_Q_RANK = 256
MLA_KV_RANK = 128
ROPE_BASE = 10000.0
MLA_SCALE = (MLA_NOPE + MLA_ROPE) ** -0.5

NSA_HEADS = 16
NSA_GROUPS = 2
NSA_HPG = 8
NSA_DH = 64
NSA_CMP_LEN = 32
NSA_CMP_STRIDE = 16
NSA_CMP_HIDDEN = 256
NSA_SEL_LEN = 64
NSA_TOPK = 8
NSA_WINDOW = 512
NSA_NSEL = SEQ // NSA_SEL_LEN
NSA_NCMP = SEQ // NSA_CMP_STRIDE
NSA_FORCE = 1e4
NSA_SCALE = NSA_DH ** -0.5
C_MIX = NSA_HEADS * NSA_DH

MOE_GROUPS = 4
MOE_EPG = 8
MOE_EXPERTS = 32
MOE_HIDDEN = 256

TM_PROJ = 512
TQ_MLA = 256
TQ_NSA = 256
TK_ATT = 256
WIN_SUB = LANES
WIN_BAND = NSA_WINDOW + WIN_SUB
MLA_VT_ROWS = LANES + 16
TM_MOE = 512
EXPERTS_VMEM_BYTES = (2 * 3 * MOE_EPG * D_MODEL * MOE_HIDDEN * 2 + 2 * 2 * TM_MOE * D_MODEL * 2
                      + 2 * MOE_EPG * TM_MOE * MOE_HIDDEN * 4 + TM_MOE * MOE_EPG * MOE_HIDDEN * 2
                      + 2 * TM_MOE * D_MODEL * 4)

GSEL_LANE = MOE_EXPERTS
GRAN = 16
TBL_W = 3 * MOE_GROUPS
ROWS_LOCAL = 640
ROWS_SORTED = TM_MOE * (TOKENS // TM_MOE + MOE_GROUPS
                        + -(-(TOKENS // TM_PROJ) * MOE_GROUPS * (GRAN - 1) // TM_MOE))


def _dot(a, b):
    return jnp.dot(a, b, preferred_element_type=F32)


def _dot_nt(a, b):
    return lax.dot_general(a, b, (((1,), (1,)), ((), ())), preferred_element_type=F32)


def _gelu(x):
    return 0.5 * x * (1.0 + jnp.tanh(math.sqrt(2.0 / math.pi) * (x + 0.044715 * (x * x * x))))


def _layer_norm(x, g, b):
    mu = jnp.mean(x, axis=-1, keepdims=True)
    xc = x - mu
    var = jnp.mean(xc * xc, axis=-1, keepdims=True)
    return xc * lax.rsqrt(var + LN_EPS) * g + b


def _rms_norm(x, g):
    return x * lax.rsqrt(jnp.mean(x * x, axis=-1, keepdims=True) + LN_EPS) * g


def _store_transposed(ref, blocks):
    extra, width = ref.shape[2] - LANES, ref.shape[3]
    if extra:
        ones_rows = jnp.where(lax.broadcasted_iota(jnp.int32, (extra, width), 0) == 0, 1.0, 0.0)
    for n, blk in enumerate(blocks):
        t = blk.T
        for c in range(t.shape[1] // width):
            chunk = t[:, c * width:(c + 1) * width]
            if extra:
                chunk = jnp.concatenate([chunk, ones_rows], axis=0)
            ref[n, c] = chunk.astype(ref.dtype)


def _ab_in_kernel(x_ref, pos_ref, win_ref, lng_ref, lnb_ref, qg_ref, kvg_ref, wq_ref, wkv_ref,
                  fc_ref, gu_ref, vn_ref, q_ref, k_ref, vt_ref):
    h = _dot(x_ref[...].astype(BF16), win_ref[...])
    gu_ref[...] = _gelu(h[:, 0:512]).astype(BF16)
    vn_ref[...] = _layer_norm(_gelu(h[:, 512:1024]), lng_ref[...], lnb_ref[...]).astype(BF16)

    tm = x_ref.shape[0]
    ang = fc_ref[...] * pos_ref[0].astype(F32)
    cos_t, sin_t = jnp.cos(ang), jnp.sin(ang)
    ones_t, zeros_t = jnp.ones((MLA_NOPE, tm), F32), jnp.zeros((MLA_NOPE, tm), F32)
    pad = LANES - MLA_NOPE - MLA_ROPE
    cc = jnp.concatenate([ones_t, cos_t, cos_t, ones_t[:pad]], axis=0).T
    ss = jnp.concatenate([zeros_t, -sin_t, sin_t, zeros_t[:pad]], axis=0).T

    cqn = _rms_norm(h[:, 1024:1280], qg_ref[...]).astype(BF16)
    qq = _dot(cqn, wq_ref[...])
    for hd in range(MLA_HEADS):
        lo, hi = hd * LANES, (hd + 1) * LANES
        q_ref[:, lo:hi] = ((qq[:, lo:hi] * cc + qq[:, 1024 + lo:1024 + hi] * ss) * (MLA_SCALE * LOG2E)).astype(BF16)

    ckvn = _rms_norm(h[:, 1280:1408], kvg_ref[...]).astype(BF16)
    kv = _dot(ckvn, wkv_ref[...])
    k_rope = h[:, 1408:1536] * cc + h[:, 1536:1664] * ss
    for hd in range(MLA_HEADS):
        lo, hi = hd * LANES, (hd + 1) * LANES
        k_ref[:, lo:hi] = (kv[:, lo:hi] + k_rope).astype(BF16)
    _store_transposed(vt_ref, [kv[:, 1024 + p * LANES:1024 + (p + 1) * LANES] for p in range(MLA_HEADS // 2)])


def _ab_in(x2, pos3, win, lng, lnb, qg, kvg, wq, wkv, fc):
    tm = TM_PROJ
    row = lambda n: pl.BlockSpec((tm, n), lambda i: (i, 0))
    full = lambda a: pl.BlockSpec(a.shape, lambda i: (0,) * a.ndim)
    return pl.pallas_call(
        _ab_in_kernel,
        grid=(TOKENS // tm,),
        in_specs=[row(D_MODEL), pl.BlockSpec((1, 1, tm), lambda i: (i, 0, 0)), full(win), full(lng), full(lnb),
                  full(qg), full(kvg), full(wq), full(wkv), full(fc)],
        out_specs=[row(512), row(512), row(1024), row(1024),
                   pl.BlockSpec((MLA_HEADS // 2, tm // TK_ATT, MLA_VT_ROWS, TK_ATT), lambda i: (0, i, 0, 0))],
        out_shape=[jax.ShapeDtypeStruct((TOKENS, 512), BF16), jax.ShapeDtypeStruct((TOKENS, 512), BF16),
                   jax.ShapeDtypeStruct((TOKENS, 1024), BF16), jax.ShapeDtypeStruct((TOKENS, 1024), BF16),
                   jax.ShapeDtypeStruct((MLA_HEADS // 2, TOKENS // TK_ATT, MLA_VT_ROWS, TK_ATT), BF16)],
        compiler_params=pltpu.CompilerParams(dimension_semantics=("arbitrary",)),
        name="ab_in",
    )(x2, pos3, win, lng, lnb, qg, kvg, wq, wkv, fc)


def _mla_attn_kernel(q_ref, k_ref, vt_ref, o_ref, m_ref, l_ref, acc_ref):
    tq, tk = TQ_MLA, TK_ATT
    qi = pl.program_id(1)
    krow = lax.broadcasted_iota(jnp.int32, (tk, tq), 0)
    qcol = lax.broadcasted_iota(jnp.int32, (tk, tq), 1)
    top = lax.broadcasted_iota(jnp.int32, (LANES, tq), 0) < HALF
    m_ref[...] = jnp.full(m_ref.shape, NEG, F32)
    l_ref[...] = jnp.zeros(l_ref.shape, F32)
    acc_ref[...] = jnp.zeros(acc_ref.shape, F32)

    def tile(j, masked, chunks=1):
        r0 = pl.multiple_of(j * tk, tk)
        scores = [_dot_nt(k_ref[0, pl.ds(r0, chunks * tk), h * LANES:(h + 1) * LANES],
                          q_ref[0, :, h * LANES:(h + 1) * LANES])
                  for h in range(MLA_HEADS)]
        for pr in range(MLA_HEADS // 2):
            probs, alphas = [], []
            for h in (2 * pr, 2 * pr + 1):
                s = jnp.where(krow <= qcol, scores[h], NEG) if masked else scores[h]
                m_old = m_ref[h]
                m_new = jnp.maximum(m_old, jnp.max(s, axis=0, keepdims=True))
                alphas.append(jnp.exp2(m_old - m_new))
                probs.append(jnp.exp2(s - m_new).astype(BF16))
                m_ref[h] = m_new
            vt = jnp.concatenate([vt_ref[pr, j + c] for c in range(chunks)], axis=1)
            pv = _dot(vt, jnp.concatenate(probs, axis=1))
            for n, hh in enumerate((2 * pr, 2 * pr + 1)):
                l_ref[hh] = alphas[n] * l_ref[hh] + pv[LANES:LANES + 1, n * tq:(n + 1) * tq]
            a = jnp.where(top, alphas[0], alphas[1])
            acc_ref[pr] = a * acc_ref[pr] + jnp.where(top, pv[:LANES, :tq], pv[:LANES, tq:])

    def body(j, c):
        tile(2 * j, False, chunks=2)
        return c

    lax.fori_loop(0, qi // 2, body, 0)

    @pl.when(qi % 2 == 1)
    def _():
        tile(qi - 1, False)

    tile(qi, True)
    for pr in range(MLA_HEADS // 2):
        l = jnp.where(top, l_ref[2 * pr], l_ref[2 * pr + 1])
        o_ref[0, :, pr * LANES:(pr + 1) * LANES] = (acc_ref[pr] / l).T.astype(BF16)


def _mla_attn(q3, k3, vt):
    tq = TQ_MLA
    n_chunks = SEQ // TK_ATT
    return pl.pallas_call(
        _mla_attn_kernel,
        grid=(BATCH, SEQ // tq),
        in_specs=[pl.BlockSpec((1, tq, MLA_HEADS * LANES), lambda b, i: (b, i, 0)),
                  pl.BlockSpec((1, SEQ, MLA_HEADS * LANES), lambda b, i: (b, 0, 0)),
                  pl.BlockSpec((MLA_HEADS // 2, n_chunks, MLA_VT_ROWS, TK_ATT), lambda b, i: (0, b, 0, 0))],
        out_specs=pl.BlockSpec((1, tq, MLA_HEADS * MLA_V), lambda b, i: (b, i, 0)),
        out_shape=jax.ShapeDtypeStruct((BATCH, SEQ, MLA_HEADS * MLA_V), BF16),
        scratch_shapes=[pltpu.VMEM((MLA_HEADS, 1, tq), F32), pltpu.VMEM((MLA_HEADS, 1, tq), F32),
                        pltpu.VMEM((MLA_HEADS // 2, LANES, tq), F32)],
        compiler_params=pltpu.CompilerParams(dimension_semantics=("arbitrary",) * 2),
        name="mla_attn",
    )(q3, k3, vt)


def _router(x1, wr, br):
    tm = x1.shape[0]
    x_hi = x1.astype(BF16)
    x_lo = (x1 - x_hi.astype(F32)).astype(BF16)
    parts = _dot(jnp.concatenate([x_hi, x_lo], axis=0), wr)
    logits = (parts[:tm, :LANES] + (parts[:tm, LANES:] + parts[tm:, :LANES]) + parts[tm:, LANES:]) + br
    lane = lax.broadcasted_iota(jnp.int32, (tm, LANES), 1).astype(F32)
    big = 1e6
    is_g = (lane >= MOE_EXPERTS) & (lane < MOE_EXPERTS + MOE_GROUPS)
    gl = jnp.where(is_g, logits, NEG)
    gmax = jnp.max(gl, axis=-1, keepdims=True)
    g_sel = jnp.min(jnp.where(is_g & (gl == gmax), lane, big), axis=-1, keepdims=True) - MOE_EXPERTS
    g_w = 1.0 / jnp.sum(jnp.where(is_g, jnp.exp(gl - gmax), 0.0), axis=-1, keepdims=True)
    in_grp = (lane >= g_sel * MOE_EPG) & (lane < (g_sel + 1) * MOE_EPG)
    el = jnp.where(in_grp, logits, NEG)
    emax = jnp.max(el, axis=-1, keepdims=True)
    ee = jnp.where(in_grp, jnp.exp(el - emax), 0.0)
    pe = ee / jnp.sum(ee, axis=-1, keepdims=True)
    p1 = jnp.max(pe, axis=-1, keepdims=True)
    i1 = jnp.min(jnp.where(in_grp & (pe == p1), lane, big), axis=-1, keepdims=True)
    rest = in_grp & (lane != i1)
    pr = jnp.where(rest, pe, -1.0)
    p2 = jnp.max(pr, axis=-1, keepdims=True)
    i2 = jnp.min(jnp.where(rest & (pr == p2), lane, big), axis=-1, keepdims=True)
    tot = p1 + p2
    wf = jnp.where(lane == i1, p1 / tot * g_w, jnp.where(lane == i2, p2 / tot * g_w, 0.0))
    wf = jnp.where(lane == GSEL_LANE, g_sel, wf)
    cnt = jnp.sum(jnp.where(lane == g_sel, 1.0, 0.0), axis=0, keepdims=True)
    return wf, cnt


def _mix_out_kernel(*refs, gmlp):
    if gmlp:
        (x_ref, gu_ref, vn_ref, ws_ref, bs_ref, yb_ref, wo_ref, g_ref, b_ref, wr_ref, br_ref,
         x1_ref, x1b_ref, wf_ref, cnt_ref, ya_ref) = refs
        tm = x_ref.shape[0]
        r = lax.broadcasted_iota(jnp.int32, (GM_CHUNK, GM_CHUNK), 0)
        c = lax.broadcasted_iota(jnp.int32, (GM_CHUNK, GM_CHUNK), 1)
        for g in range(GM_GROUPS):
            ws = jnp.where(r >= c, ws_ref[g], 0.0).astype(BF16)
            bias = bs_ref[:, g:g + 1]
            for ch in range(tm // GM_CHUNK):
                rows = slice(ch * GM_CHUNK, (ch + 1) * GM_CHUNK)
                cols = slice(g * LANES, (g + 1) * LANES)
                s = _dot(ws, vn_ref[rows, cols]) + bias
                ya_ref[rows, cols] = (gu_ref[rows, cols].astype(F32) * s).astype(BF16)
        mix = _dot(ya_ref[...], wo_ref[0:GM_WIDTH, :]) + _dot(yb_ref[...], wo_ref[GM_WIDTH:, :])
    else:
        x_ref, y_ref, wo_ref, g_ref, b_ref, wr_ref, br_ref, x1_ref, x1b_ref, wf_ref, cnt_ref = refs
        mix = _dot(y_ref[...], wo_ref[...])
    x1 = _layer_norm(DN_ALPHA * x_ref[...] + mix, g_ref[...], b_ref[...])
    x1_ref[...] = x1
    x1b_ref[...] = x1.astype(BF16)
    wf, cnt = _router(x1, wr_ref[...], br_ref[...])
    wf_ref[...] = wf
    cnt_ref[0] = jnp.broadcast_to(cnt, cnt_ref.shape[1:])


def _mix_out(x2, ys, wo, g, b, wr, br, gm=None):
    tm = TM_PROJ
    row = lambda n: pl.BlockSpec((tm, n), lambda i: (i, 0))
    full = lambda a: pl.BlockSpec(a.shape, lambda i: (0,) * a.ndim)
    if gm is not None:
        gu, vn, ws, bs = gm
        args = (x2, gu, vn, ws, bs, ys, wo, g, b, wr, br)
        in_specs = [row(D_MODEL), row(512), row(512), full(ws), full(bs), row(512), full(wo),
                    full(g), full(b), full(wr), full(br)]
        scratch = [pltpu.VMEM((tm, GM_WIDTH), BF16)]
    else:
        args = (x2, ys, wo, g, b, wr, br)
        in_specs = [row(D_MODEL), row(C_MIX), full(wo), full(g), full(b), full(wr), full(br)]
        scratch = []
    return pl.pallas_call(
        functools.partial(_mix_out_kernel, gmlp=gm is not None),
        grid=(TOKENS // tm,),
        in_specs=in_specs,
        out_specs=[row(D_MODEL), row(D_MODEL), row(LANES), pl.BlockSpec((1, 8, LANES), lambda i: (i, 0, 0))],
        out_shape=[jax.ShapeDtypeStruct((TOKENS, D_MODEL), F32), jax.ShapeDtypeStruct((TOKENS, D_MODEL), BF16),
                   jax.ShapeDtypeStruct((TOKENS, LANES), F32),
                   jax.ShapeDtypeStruct((TOKENS // tm, 8, LANES), F32)],
        scratch_shapes=scratch,
        compiler_params=pltpu.CompilerParams(dimension_semantics=("arbitrary",),
                                             allow_input_fusion=[a is wo for a in args]),
        name="mix_out_gmlp" if gm is not None else "mix_out",
    )(*args)


def _group_dest(wf, ltri_ref, ustr_ref):
    tm = wf.shape[0]
    lane = lax.broadcasted_iota(jnp.int32, (tm, LANES), 1).astype(F32)
    onehot = jnp.where(lane == wf[:, GSEL_LANE:GSEL_LANE + 1], 1.0, 0.0)
    before = _dot(ltri_ref[...], onehot.astype(BF16))
    cnt = jnp.sum(onehot, axis=0, keepdims=True)
    gran = jnp.floor((cnt + (GRAN - 1)) * (1.0 / GRAN))
    start = _dot(jnp.broadcast_to(gran, (8, LANES)).astype(BF16), ustr_ref[...])[0:1]
    return jnp.sum(onehot * (GRAN * start + before), axis=-1, keepdims=True)


def _granule_copies(tbl_ref, tile, vmem_bufs, hbm_refs, sems, to_hbm, act):
    for g in range(MOE_GROUPS):
        n = tbl_ref[tile * TBL_W + g]
        loc = tbl_ref[tile * TBL_W + MOE_GROUPS + g]
        glb = tbl_ref[tile * TBL_W + 2 * MOE_GROUPS + g]

        def body(k, c, loc=loc, glb=glb):
            lo = pl.multiple_of((loc + k) * GRAN, GRAN)
            hi = pl.multiple_of((glb + k) * GRAN, GRAN)
            for idx, (vb, hb) in enumerate(zip(vmem_bufs, hbm_refs)):
                v_sl, h_sl = vb.at[pl.ds(lo, GRAN)], hb.at[pl.ds(hi, GRAN)]
                src, dst = (v_sl, h_sl) if to_hbm else (h_sl, v_sl)
                act(pltpu.make_async_copy(src, dst, sems[idx]), idx)
            return c

        lax.fori_loop(0, n, body, 0)


def _zero_fill_copies(tbl_ref, zero_bufs, hbm_refs, sems, act):
    tail = (TOKENS // TM_PROJ) * TBL_W
    for n in range(MOE_GROUPS + 1):
        first = tbl_ref[tail + 2 * n]

        def body(k, c, first=first):
            hi = pl.multiple_of((first + k) * GRAN, GRAN)
            for idx, (zb, hb) in enumerate(zip(zero_bufs, hbm_refs)):
                act(pltpu.make_async_copy(zb, hb.at[pl.ds(hi, GRAN)], sems[idx]), idx)
            return c

        lax.fori_loop(0, tbl_ref[tail + 2 * n + 1], body, 0)


def _dispatch_kernel(tbl_ref, xb_ref, wf_ref, ltri_ref, ustr_ref, xs_out, ws_out, xbuf, wbuf, zx, zw, sems):
    tile = pl.program_id(0)
    slot = tile % 2
    tm = xb_ref.shape[0]
    wf = wf_ref[...]
    dest = _group_dest(wf, ltri_ref, ustr_ref)
    dest_row = jnp.broadcast_to(dest, (tm, LANES)).T[0:1]
    r = lax.broadcasted_iota(jnp.int32, (ROWS_LOCAL, tm), 0).astype(F32)
    perm = jnp.where(r == dest_row, 1.0, 0.0).astype(BF16)
    xbuf[slot] = _dot(perm, xb_ref[...]).astype(BF16)
    hi = wf.astype(BF16)
    r1 = wf - hi.astype(F32)
    mid = r1.astype(BF16)
    lo = (r1 - mid.astype(F32)).astype(BF16)
    pieces = _dot(perm, jnp.concatenate([hi, mid, lo], axis=1))
    wbuf[slot] = pieces[:, :LANES] + pieces[:, LANES:2 * LANES] + pieces[:, 2 * LANES:]

    def copies(t, s, act):
        _granule_copies(tbl_ref, t, (xbuf.at[s], wbuf.at[s]), (xs_out, ws_out), (sems.at[s, 0], sems.at[s, 1]),
                        True, act)

    start = lambda c, k: c.start(priority=k)
    wait = lambda c, k: c.wait()
    copies(tile, slot, start)

    @pl.when(tile > 0)
    def _():
        copies(tile - 1, 1 - slot, wait)

    @pl.when(tile == pl.num_programs(0) - 1)
    def _():
        zx[...] = jnp.zeros_like(zx)
        zw[...] = jnp.zeros_like(zw)
        fill = functools.partial(_zero_fill_copies, tbl_ref, (zx, zw), (xs_out, ws_out), (sems.at[2, 0], sems.at[2, 1]))
        fill(start)
        copies(tile, slot, wait)
        fill(wait)


def _dispatch(tbl, x1b, wf, ltri, ustr):
    tm = TM_PROJ
    row = lambda n: pl.BlockSpec((tm, n), lambda i, t: (i, 0))
    full = lambda a: pl.BlockSpec(a.shape, lambda i, t: (0,) * a.ndim)
    anyspace = pl.BlockSpec(memory_space=pl.ANY)
    return pl.pallas_call(
        _dispatch_kernel,
        grid_spec=pltpu.PrefetchScalarGridSpec(
            num_scalar_prefetch=1, grid=(TOKENS // tm,),
            in_specs=[row(D_MODEL), row(LANES), full(ltri), full(ustr)],
            out_specs=[anyspace, anyspace],
            scratch_shapes=[pltpu.VMEM((2, ROWS_LOCAL, D_MODEL), BF16), pltpu.VMEM((2, ROWS_LOCAL, LANES), F32),
                            pltpu.VMEM((GRAN, D_MODEL), BF16), pltpu.VMEM((GRAN, LANES), F32),
                            pltpu.SemaphoreType.DMA((3, 2))]),
        out_shape=[jax.ShapeDtypeStruct((ROWS_SORTED, D_MODEL), BF16),
                   jax.ShapeDtypeStruct((ROWS_SORTED, LANES), F32)],
        compiler_params=pltpu.CompilerParams(dimension_semantics=("arbitrary",)),
        name="moe_dispatch",
    )(tbl, x1b, wf, ltri, ustr)


def _experts_kernel(gid_ref, valid_ref, xs_ref, ws_ref, wg_ref, wu_ref, wd_ref, y_ref):
    i = pl.program_id(0)

    @pl.when(valid_ref[i] == 1)
    def _():
        x = xs_ref[...]
        gates = [_dot(x, wg_ref[e]) for e in range(MOE_EPG)]
        ups = [_dot(x, wu_ref[e]) for e in range(MOE_EPG)]
        ws = ws_ref[...]
        lane = lax.broadcasted_iota(jnp.int32, ws.shape, 1)
        hidden = []
        for e in range(MOE_EPG):
            w_tok = jnp.sum(jnp.where(lane == gid_ref[i] * MOE_EPG + e, ws, 0.0), axis=-1, keepdims=True)
            hidden.append((gates[e] * jax.nn.sigmoid(gates[e]) * ups[e] * w_tok).astype(BF16))
        wd = wd_ref[...].reshape(MOE_EPG * MOE_HIDDEN, D_MODEL)
        y_ref[...] = _dot(jnp.concatenate(hidden, axis=1), wd).astype(BF16)

    @pl.when(valid_ref[i] == 0)
    def _():
        y_ref[...] = jnp.zeros_like(y_ref)


def _experts(gid, valid, xs, ws, wg, wu, wd):
    tm = TM_MOE
    row = lambda n: pl.BlockSpec((tm, n), lambda i, gid, valid: (i, 0))
    wspec = lambda a, b: pl.BlockSpec((MOE_EPG, a, b), lambda i, gid, valid: (gid[i], 0, 0))
    return pl.pallas_call(
        _experts_kernel,
        grid_spec=pltpu.PrefetchScalarGridSpec(
            num_scalar_prefetch=2, grid=(ROWS_SORTED // tm,),
            in_specs=[row(D_MODEL), row(LANES), wspec(D_MODEL, MOE_HIDDEN), wspec(D_MODEL, MOE_HIDDEN),
                      wspec(MOE_HIDDEN, D_MODEL)],
            out_specs=row(D_MODEL)),
        out_shape=jax.ShapeDtypeStruct((ROWS_SORTED, D_MODEL), BF16),
        compiler_params=pltpu.CompilerParams(dimension_semantics=("arbitrary",),
                                             vmem_limit_bytes=EXPERTS_VMEM_BYTES),
        name="moe_experts",
    )(gid, valid, xs, ws, wg, wu, wd)


def _combine_kernel(tbl_ref, wf_ref, x1_ref, ltri_ref, ustr_ref, g_ref, b_ref, y_hbm, o_ref, ybuf, sems):
    tile = pl.program_id(0)
    slot = tile % 2
    tm = x1_ref.shape[0]

    def copies(t, s, act):
        _granule_copies(tbl_ref, t, (ybuf.at[s],), (y_hbm,), (sems.at[s],), False, act)

    def fetch(t, s):
        ybuf[s] = jnp.zeros(ybuf.shape[1:], ybuf.dtype)
        copies(t, s, lambda c, k: c.start(priority=1))

    @pl.when(tile == 0)
    def _():
        fetch(tile, slot)

    @pl.when(tile + 1 < pl.num_programs(0))
    def _():
        fetch(tile + 1, 1 - slot)

    dest = _group_dest(wf_ref[...], ltri_ref, ustr_ref)
    c = lax.broadcasted_iota(jnp.int32, (tm, ROWS_LOCAL), 1).astype(F32)
    unperm = jnp.where(c == dest, 1.0, 0.0).astype(BF16)
    copies(tile, slot, lambda c, k: c.wait())
    y_sorted = ybuf[slot]
    halves = (slice(0, tm // 2), slice(tm // 2, tm))
    ffn = [_dot(unperm[rows], y_sorted) for rows in halves]
    for rows, part in zip(halves, ffn):
        o_ref[rows, :] = _layer_norm(DN_ALPHA * x1_ref[rows, :] + part, g_ref[...], b_ref[...])


def _combine(tbl, wf, x1, ltri, ustr, g, b, y):
    tm = TM_PROJ
    row = lambda n: pl.BlockSpec((tm, n), lambda i, t: (i, 0))
    full = lambda a: pl.BlockSpec(a.shape, lambda i, t: (0,) * a.ndim)
    return pl.pallas_call(
        _combine_kernel,
        grid_spec=pltpu.PrefetchScalarGridSpec(
            num_scalar_prefetch=1, grid=(TOKENS // tm,),
            in_specs=[row(LANES), row(D_MODEL), full(ltri), full(ustr), full(g), full(b),
                      pl.BlockSpec(memory_space=pl.ANY)],
            out_specs=row(D_MODEL),
            scratch_shapes=[pltpu.VMEM((2, ROWS_LOCAL, D_MODEL), BF16), pltpu.SemaphoreType.DMA((2,))]),
        out_shape=jax.ShapeDtypeStruct((TOKENS, D_MODEL), F32),
        compiler_params=pltpu.CompilerParams(dimension_semantics=("arbitrary",)),
        name="moe_combine",
    )(tbl, wf, x1, ltri, ustr, g, b, y)


def _routing_tables(cnt):
    n_tiles = cnt.shape[0]
    c = cnt[:, 0, :MOE_GROUPS].astype(jnp.int32)
    gran = (c + GRAN - 1) // GRAN
    local = jnp.cumsum(gran, axis=1) - gran
    per_tile = TM_MOE // GRAN
    tiles_g = (jnp.sum(gran, axis=0) + per_tile - 1) // per_tile
    ends = jnp.cumsum(tiles_g)
    base = (ends - tiles_g) * per_tile
    glob = base[None, :] + jnp.cumsum(gran, axis=0) - gran
    used = jnp.sum(gran, axis=0)
    pad_first = jnp.concatenate([base + used, ends[-1:] * per_tile])
    pad_count = jnp.concatenate([tiles_g * per_tile - used, ROWS_SORTED // GRAN - ends[-1:] * per_tile])
    tail = jnp.stack([pad_first, pad_count], axis=1).reshape(-1)
    tbl = jnp.concatenate([jnp.concatenate([gran, local, glob], axis=1).reshape(n_tiles * TBL_W), tail])
    idx = jnp.arange(ROWS_SORTED // TM_MOE)
    gid = jnp.minimum(jnp.sum(idx[:, None] >= ends[None, :], axis=1), MOE_GROUPS - 1).astype(jnp.int32)
    valid = (idx < ends[-1]).astype(jnp.int32)
    return tbl, gid, valid


def _sort_tables():
    t = jnp.arange(TM_PROJ)
    ltri = (t[None, :] < t[:, None]).astype(BF16)
    l = jnp.arange(LANES)
    ustr = (l[:, None] < l[None, :]).astype(BF16)
    return ltri, ustr


def _dup_halves(t):
    lane = lax.broadcasted_iota(jnp.int32, t.shape, 1)
    r = pltpu.roll(t, HALF, 1)
    return jnp.where(lane < HALF, t, r), jnp.where(lane < HALF, r, t)


def _nsa_in_kernel(x_ref, w_ref, gb_ref, q_ref, kc_ref, vc_ref, ks_ref, vst_ref, kw_ref, vwt_ref, gate_ref):
    h = _dot(x_ref[...].astype(BF16), w_ref[...])
    q_ref[...] = (h[:, 0:C_MIX] * (NSA_SCALE * LOG2E)).astype(BF16)
    kc_ref[...] = h[:, 1024:1152].astype(BF16)
    vc_ref[...] = h[:, 1152:1280].astype(BF16)
    tm = h.shape[0]
    lane = lax.broadcasted_iota(jnp.int32, (tm, LANES), 1)
    for g, d in enumerate(_dup_halves(h[:, 1536:1664])):
        kw_ref[:, g * LANES:(g + 1) * LANES] = d.astype(BF16)
    pos = (pl.program_id(0) * tm) % SEQ + lax.broadcasted_iota(jnp.int32, (tm, LANES), 0)
    block_onehot = jnp.where(lane == pos // NSA_SEL_LEN, 1.0, 0.0).astype(BF16)
    for g, d in enumerate(_dup_halves(h[:, 1280:1408])):
        ks_ref[:, 2 * g * LANES:(2 * g + 1) * LANES] = d.astype(BF16)
        ks_ref[:, (2 * g + 1) * LANES:(2 * g + 2) * LANES] = block_onehot
    for idx, ref in ((1, vst_ref), (3, vwt_ref)):
        tail = jnp.where(lane == HALF, 1.0, 0.0)
        _store_transposed(ref, [jnp.where(lane < HALF, d, tail)
                                for d in _dup_halves(h[:, 1280 + idx * LANES:1280 + (idx + 1) * LANES])])
    for g in range(NSA_GROUPS):
        gate_ref[g] = jax.nn.sigmoid(h[:, 1792 + g * LANES:1792 + (g + 1) * LANES] + gb_ref[g])


def _nsa_in(x2, w, gb):
    tm = TM_PROJ
    row = lambda n: pl.BlockSpec((tm, n), lambda i: (i, 0))
    full = lambda a: pl.BlockSpec(a.shape, lambda i: (0,) * a.ndim)
    sd = jax.ShapeDtypeStruct
    vt_spec = lambda w: pl.BlockSpec((NSA_GROUPS, tm // w, LANES, w), lambda i: (0, i, 0, 0))
    vt_shape = lambda w: sd((NSA_GROUPS, TOKENS // w, LANES, w), BF16)
    return pl.pallas_call(
        _nsa_in_kernel,
        grid=(TOKENS // tm,),
        in_specs=[row(D_MODEL), full(w), full(gb)],
        out_specs=[row(C_MIX), row(LANES), row(LANES), row(4 * LANES), vt_spec(TK_ATT), row(2 * LANES),
                   vt_spec(WIN_SUB), pl.BlockSpec((NSA_GROUPS, tm, LANES), lambda i: (0, i, 0))],
        out_shape=[sd((TOKENS, C_MIX), BF16), sd((TOKENS, LANES), BF16), sd((TOKENS, LANES), BF16),
                   sd((TOKENS, 4 * LANES), BF16), vt_shape(TK_ATT), sd((TOKENS, 2 * LANES), BF16), vt_shape(WIN_SUB),
                   sd((NSA_GROUPS, TOKENS, LANES), F32)],
        compiler_params=pltpu.CompilerParams(dimension_semantics=("arbitrary",)),
        name="nsa_in",
    )(x2, w, gb)


def _compress_kernel(kc_ref, vc_ref, pk_ref, pv_ref, wk1_ref, wv1_ref, wk2_ref, wv2_ref, ko_ref, vo_ref):
    for a_ref, p_ref, w1_ref, w2_ref, o_ref in ((kc_ref, pk_ref, wk1_ref, wk2_ref, ko_ref),
                                                (vc_ref, pv_ref, wv1_ref, wv2_ref, vo_ref)):
        a = a_ref[0].astype(F32)
        a0 = (a + p_ref[0]).astype(BF16)
        a1 = (a + p_ref[1]).astype(BF16)
        outs = []
        for g in range(NSA_GROUPS):
            first = _dot(a0, w1_ref[g])
            second = _dot(a1, w1_ref[NSA_GROUPS + g])
            hid = first + pltpu.roll(second, NSA_NCMP - 1, 0)
            outs.append(_dot(_gelu(hid).astype(BF16), w2_ref[...]))
        if o_ref is ko_ref:
            o_ref[0] = jnp.concatenate(outs, axis=1).astype(BF16)
        else:
            for g in range(NSA_GROUPS):
                o_ref[0, g] = outs[g].T.astype(BF16)


def _compress(kc_r, vc_r, pk, pv, wk1, wv1, wk2, wv2):
    blk = pl.BlockSpec((1, NSA_NCMP, NSA_CMP_STRIDE * LANES), lambda b: (b, 0, 0))
    full = lambda a: pl.BlockSpec(a.shape, lambda b: (0,) * a.ndim)
    sd = jax.ShapeDtypeStruct
    return pl.pallas_call(
        _compress_kernel,
        grid=(BATCH,),
        in_specs=[blk, blk, full(pk), full(pv), full(wk1), full(wv1), full(wk2), full(wv2)],
        out_specs=[pl.BlockSpec((1, NSA_NCMP, 2 * LANES), lambda b: (b, 0, 0)),
                   pl.BlockSpec((1, NSA_GROUPS, LANES, NSA_NCMP), lambda b: (b, 0, 0, 0))],
        out_shape=[sd((BATCH, NSA_NCMP, 2 * LANES), BF16), sd((BATCH, NSA_GROUPS, LANES, NSA_NCMP), BF16)],
        compiler_params=pltpu.CompilerParams(dimension_semantics=("arbitrary",)),
        name="nsa_compress",
    )(kc_r, vc_r, pk, pv, wk1, wv1, wk2, wv2)


def _nsa_attn_kernel(q_ref, kc_ref, vct_ref, ks_ref, vst_ref, kw_ref, vwt_ref, gate_ref, cover_ref,
                     o_ref, m_ref, acc_ref):
    tq, tk, hpg = TQ_NSA, TK_ATT, NSA_HPG
    qi = pl.program_id(2)
    q0 = qi * tq
    lane = lax.broadcasted_iota(jnp.int32, (tq, LANES), 1)
    t_tok = q0 + lax.broadcasted_iota(jnp.int32, (1, tq), 1)
    head = lambda x, i: x[:, i * tq:(i + 1) * tq]
    heads = range(hpg)

    parts = []
    for p in range(hpg // 2):
        qp = q_ref[0, :, p * LANES:(p + 1) * LANES]
        zero = jnp.zeros_like(qp)
        parts.append(jnp.where(lane < HALF, qp, zero))
        parts.append(jnp.where(lane < HALF, zero, qp))
    qs = jnp.concatenate(parts, axis=0)

    kb_sub = lax.broadcasted_iota(jnp.int32, (WIN_BAND, 1), 0)

    def window_scores(part):
        q_lo = q0 + part * WIN_SUB
        start = pl.multiple_of(jnp.maximum(q_lo - NSA_WINDOW, 0), WIN_SUB)
        qs_part = jnp.concatenate([qs[i * tq + part * WIN_SUB:i * tq + (part + 1) * WIN_SUB] for i in heads], axis=0)
        return q_lo, start, _dot_nt(kw_ref[0, pl.ds(start, WIN_BAND), :], qs_part)

    def window_finish(q_lo, start, s_w):
        kpos = start + kb_sub
        t_part = q_lo + lax.broadcasted_iota(jnp.int32, (1, WIN_SUB), 1)
        in_win = (kpos <= t_part) & (kpos > t_part - NSA_WINDOW)
        vw_band = jnp.concatenate([vwt_ref[0, start // WIN_SUB + c] for c in range(WIN_BAND // WIN_SUB)], axis=1)
        out = []
        for i0 in range(0, hpg, 2):
            e_w = []
            for i in (i0, i0 + 1):
                sm = jnp.where(in_win, s_w[:, i * WIN_SUB:(i + 1) * WIN_SUB], NEG)
                e_w.append(jnp.exp2((sm - jnp.max(sm, axis=0, keepdims=True)).astype(BF16)))
            o_pair = _dot(vw_band, jnp.concatenate(e_w, axis=1))
            out += [o_pair[:, :WIN_SUB], o_pair[:, WIN_SUB:]]
        return out

    s_c = _dot_nt(kc_ref[0], qs)
    win0 = window_scores(0)
    n_sub = lax.broadcasted_iota(jnp.int32, (NSA_NCMP, 1), 0)
    vis = t_tok >= n_sub * NSA_CMP_STRIDE + (NSA_CMP_LEN - 1)
    sees_any = t_tok >= NSA_CMP_LEN - 1
    p_sum = jnp.zeros((NSA_NCMP, tq), F32)
    p_c = []
    for i in heads:
        sm = jnp.where(vis, head(s_c, i), NEG)
        e = jnp.exp2(sm - jnp.max(sm, axis=0, keepdims=True))
        p = e * jnp.where(sees_any, 1.0 / jnp.sum(e, axis=0, keepdims=True), 0.0)
        p_sum = p_sum + p
        p_c.append(p.astype(BF16))
    o_c = _dot(vct_ref[0, 0], jnp.concatenate(p_c, axis=1))

    imp = jnp.dot(cover_ref[...], p_sum, preferred_element_type=F32,
                  precision=lax.Precision.HIGHEST)[0:NSA_NSEL]
    o_w_parts = [window_finish(*win0)]
    o_w_parts += [window_finish(*window_scores(part)) for part in range(1, tq // WIN_SUB)]
    jj = lax.broadcasted_iota(jnp.int32, (NSA_NSEL, 1), 0)
    tb = t_tok // NSA_SEL_LEN
    forced = (jj == 0) | (jj == tb) | (jj == tb - 1)
    score = jnp.where(forced, NSA_FORCE, jnp.where(jj <= tb, imp, -NSA_FORCE))
    sub = 8
    rows = [score[b * sub:(b + 1) * sub] for b in range(NSA_NSEL // sub)]
    ranks = [jnp.zeros((sub, tq), jnp.int32) for _ in rows]
    j_in = lax.broadcasted_iota(jnp.int32, (sub, 1), 0)
    for i in range(NSA_NSEL):
        si = score[i:i + 1, :]
        for b, blk in enumerate(rows):
            if b < i // sub:
                beats = si > blk
            elif b > i // sub:
                beats = si >= blk
            else:
                beats = (si > blk) | ((si == blk) & (j_in > i % sub))
            ranks[b] = ranks[b] + beats.astype(jnp.int32)
    rank = jnp.concatenate(ranks, axis=0)
    sel_bias = jnp.where(rank < NSA_TOPK, 0.0, NEG)
    bias_rows = jnp.concatenate([sel_bias, jnp.zeros((LANES - NSA_NSEL, tq), F32)], axis=0).T.astype(BF16)
    qs_sel = jnp.concatenate([qs, jnp.concatenate([bias_rows] * hpg, axis=0)], axis=1)

    m_ref[...] = jnp.full(m_ref.shape, NEG, F32)
    acc_ref[...] = jnp.zeros(acc_ref.shape, F32)
    k_sub = lax.broadcasted_iota(jnp.int32, (tk, 1), 0)
    diag = q0 // tk
    causal = diag * tk + k_sub <= t_tok
    per_dot = max(1, 2 * LANES // tq)

    def process(kt, diagonal, chunks=1):
        s = _dot_nt(ks_ref[0, pl.ds(pl.multiple_of(kt * tk, tk), chunks * tk), :], qs_sel)
        vt = jnp.concatenate([vst_ref[0, kt + c] for c in range(chunks)], axis=1)
        for i0 in range(0, hpg, per_dot):
            cols = slice(i0 * tq, (i0 + per_dot) * tq)
            probs, alphas = [], []
            for i in range(i0, i0 + per_dot):
                for part in range(tq // LANES):
                    sub = slice(part * LANES, (part + 1) * LANES)
                    sm = s[:, i * tq + part * LANES:i * tq + (part + 1) * LANES]
                    if diagonal:
                        sm = jnp.where(causal[:, sub], sm, NEG)
                    m_old = m_ref[i, :, sub]
                    m_new = jnp.maximum(m_old, jnp.max(sm, axis=0, keepdims=True))
                    m_ref[i, :, sub] = m_new
                    probs.append(jnp.exp2((sm - m_new).astype(BF16)))
                    alphas.append(jnp.exp2(m_old - m_new))
            pv = _dot(vt, jnp.concatenate(probs, axis=1))
            acc_ref[:, cols] = jnp.concatenate(alphas, axis=1) * acc_ref[:, cols] + pv

    def pair(k, c):
        process(2 * k, False, chunks=2)
        return c

    lax.fori_loop(0, diag // 2, pair, 0)

    @pl.when(diag % 2 == 1)
    def _():
        process(diag - 1, False)

    process(diag, True)

    o_w = [jnp.concatenate([part[i] for part in o_w_parts], axis=1) for i in heads]

    gt = gate_ref[0, 0].T
    o_s = acc_ref[...]
    outs = []
    for i in heads:
        c_i, s_i, w_i = head(o_c, i), head(o_s, i), o_w[i]
        outs.append(gt[i:i + 1] * c_i[:HALF]
                    + gt[hpg + i:hpg + i + 1] / s_i[HALF:HALF + 1] * s_i[:HALF]
                    + gt[2 * hpg + i:2 * hpg + i + 1] / w_i[HALF:HALF + 1] * w_i[:HALF])
    for p in range(hpg // 2):
        pair = jnp.concatenate([outs[2 * p], outs[2 * p + 1]], axis=0)
        o_ref[0, :, p * LANES:(p + 1) * LANES] = pair.T.astype(BF16)


def _nsa_attn(q3, kc2, vct, ks3, vst, kw3, vwt, gates4, cover_t):
    tq = TQ_NSA
    half_w = NSA_HPG * NSA_DH
    kv = pl.BlockSpec((1, SEQ, LANES), lambda b, g, i: (b, 0, g))
    vt = lambda w: pl.BlockSpec((1, SEQ // w, LANES, w), lambda b, g, i: (g, b, 0, 0))
    full = lambda a: pl.BlockSpec(a.shape, lambda b, g, i: (0,) * a.ndim)
    return pl.pallas_call(
        _nsa_attn_kernel,
        grid=(BATCH, NSA_GROUPS, SEQ // tq),
        in_specs=[pl.BlockSpec((1, tq, half_w), lambda b, g, i: (b, i, g)),
                  pl.BlockSpec((1, NSA_NCMP, LANES), lambda b, g, i: (b, 0, g)),
                  pl.BlockSpec((1, 1, LANES, NSA_NCMP), lambda b, g, i: (b, g, 0, 0)),
                  pl.BlockSpec((1, SEQ, 2 * LANES), lambda b, g, i: (b, 0, g)), vt(TK_ATT), kv, vt(WIN_SUB),
                  pl.BlockSpec((1, 1, tq, LANES), lambda b, g, i: (g, b, i, 0)),
                  full(cover_t)],
        out_specs=pl.BlockSpec((1, tq, half_w), lambda b, g, i: (b, i, g)),
        out_shape=jax.ShapeDtypeStruct((BATCH, SEQ, C_MIX), BF16),
        scratch_shapes=[pltpu.VMEM((NSA_HPG, 1, tq), F32), pltpu.VMEM((LANES, NSA_HPG * tq), F32)],
        compiler_params=pltpu.CompilerParams(dimension_semantics=("arbitrary",) * 3),
        name="nsa_attn",
    )(q3, kc2, vct, ks3, vst, kw3, vwt, gates4, cover_t)


def _rope_tables():
    half = MLA_ROPE // 2
    freq = jnp.exp(-math.log(ROPE_BASE) * jnp.arange(half, dtype=F32) / half)
    return freq[:, None]


def _swap_halves(w):
    half = w.shape[-1] // 2
    return jnp.concatenate([w[..., half:], w[..., :half]], axis=-1)


def _pad_last(w, n):
    return jnp.pad(w, [(0, 0)] * (w.ndim - 1) + [(0, n - w.shape[-1])])


def _ab_weights(w_in, w_uq, w_uk, w_uv):
    w_kr = w_in[:, 1408:1440]
    place = lambda w: jnp.pad(w, ((0, 0), (MLA_NOPE, LANES - MLA_NOPE - MLA_ROPE)))
    win = jnp.concatenate([w_in[:, :1408], place(w_kr), place(_swap_halves(w_kr))], axis=1).astype(BF16)
    uq = w_uq.reshape(MLA_Q_RANK, MLA_HEADS, MLA_NOPE + MLA_ROPE)
    nope, rp = uq[..., :MLA_NOPE], uq[..., MLA_NOPE:]
    q_pad = _pad_last(jnp.concatenate([nope, rp], -1), LANES).reshape(MLA_Q_RANK, MLA_HEADS * LANES)
    q_sw = _pad_last(jnp.concatenate([jnp.zeros_like(nope), _swap_halves(rp)], -1), LANES)
    wq = jnp.concatenate([q_pad, q_sw.reshape(MLA_Q_RANK, MLA_HEADS * LANES)], axis=1).astype(BF16)
    k_pad = _pad_last(w_uk.reshape(MLA_KV_RANK, MLA_HEADS, MLA_NOPE), LANES).reshape(MLA_KV_RANK, -1)
    wkv = jnp.concatenate([k_pad, w_uv], axis=1).astype(BF16)
    return win, wq, wkv


def _router_weights(w_rg, b_rg, w_re, b_re):
    wr = _pad_last(jnp.concatenate([w_re, w_rg], axis=1), LANES)
    wr_hi = wr.astype(BF16)
    wr_lo = (wr - wr_hi.astype(F32)).astype(BF16)
    br = _pad_last(jnp.concatenate([b_re, b_rg])[None, :], LANES)
    return jnp.concatenate([wr_hi, wr_lo], axis=1), br


def _nsa_in_weights(w_in, gate_b):
    g_cols = w_in[:, C_MIX + 768:].reshape(D_MODEL, 3, NSA_GROUPS, NSA_HPG)
    g_blocks = [_pad_last(g_cols[:, :, g, :].reshape(D_MODEL, 3 * NSA_HPG), LANES) for g in range(NSA_GROUPS)]
    w = jnp.concatenate([w_in[:, :C_MIX + 768]] + g_blocks, axis=1).astype(BF16)
    gb = gate_b.reshape(3, NSA_GROUPS, NSA_HPG)
    gb = jnp.stack([_pad_last(gb[:, g, :].reshape(1, 3 * NSA_HPG), LANES) for g in range(NSA_GROUPS)])
    return w, gb


def _compress_weights(pos, w1, w2):
    w1r = w1.reshape(2, NSA_CMP_STRIDE, NSA_DH, NSA_CMP_HIDDEN)
    zero = jnp.zeros_like(w1r)
    per_g = []
    for g in range(NSA_GROUPS):
        parts = [w1r if gg == g else zero for gg in range(NSA_GROUPS)]
        per_g.append(jnp.stack(parts, axis=2).reshape(2, NSA_CMP_STRIDE * LANES, NSA_CMP_HIDDEN))
    w1x = jnp.stack(per_g, axis=1).reshape(2 * NSA_GROUPS, NSA_CMP_STRIDE * LANES, NSA_CMP_HIDDEN)
    posr = pos.reshape(2, NSA_CMP_STRIDE, 1, NSA_DH)
    posx = jnp.broadcast_to(posr, (2, NSA_CMP_STRIDE, NSA_GROUPS, NSA_DH)).reshape(2, 1, NSA_CMP_STRIDE * LANES)
    w2x = jnp.concatenate([w2, w2], axis=1)
    return posx, w1x.astype(BF16), w2x.astype(BF16)


def _selection_tables():
    n = jnp.arange(LANES)[:, None]
    j = jnp.arange(LANES)[None, :]
    c0 = n * NSA_CMP_STRIDE
    s0 = j * NSA_SEL_LEN
    cover = ((c0 < s0 + NSA_SEL_LEN) & (c0 + NSA_CMP_LEN > s0) & (n < NSA_NCMP - 1) & (j < NSA_NSEL))
    return jnp.transpose(cover).astype(F32)


def kernel(x, positions, ab_w_in, ab_gm_ln_g, ab_gm_ln_b, ab_gm_ws, ab_gm_bs, ab_mla_q_norm,
           ab_mla_kv_norm, ab_mla_w_uq, ab_mla_w_uk, ab_mla_w_uv, ab_w_o, c_w_in, c_cmp_pos, c_w_ck1,
           c_w_ck2, c_w_cv1, c_w_cv2, c_gate_b, c_w_o, moe_w_rg, moe_b_rg, moe_w_re, moe_b_re,
           moe_w_gate, moe_w_up, moe_w_down, ln1_g, ln1_b, ln2_g, ln2_b):
    x2 = x.reshape(TOKENS, D_MODEL)
    pos3 = positions.reshape(TOKENS // TM_PROJ, 1, TM_PROJ)
    vec = lambda a: a[None, :]

    ltri, ustr = _sort_tables()

    def moe_layer(layer, x1b, x1, wf, cnt):
        tbl, gid, valid = _routing_tables(cnt)
        xs, ws = _dispatch(tbl, x1b, wf, ltri, ustr)
        y = _experts(gid, valid, xs, ws, moe_w_gate[layer].astype(BF16), moe_w_up[layer].astype(BF16),
                     moe_w_down[layer].astype(BF16))
        return _combine(tbl, wf, x1, ltri, ustr, vec(ln2_g[layer]), vec(ln2_b[layer]), y)

    win, wq, wkv = _ab_weights(ab_w_in[0], ab_mla_w_uq[0], ab_mla_w_uk[0], ab_mla_w_uv[0])
    gu, vn, q, k, vt = _ab_in(x2, pos3, win, vec(ab_gm_ln_g[0]), vec(ab_gm_ln_b[0]), vec(ab_mla_q_norm[0]),
                              vec(ab_mla_kv_norm[0]), wq, wkv, _rope_tables())
    yb = _mla_attn(q.reshape(BATCH, SEQ, -1), k.reshape(BATCH, SEQ, -1), vt)
    wr, br = _router_weights(moe_w_rg[0], moe_b_rg[0], moe_w_re[0], moe_b_re[0])
    x1, x1b, wf, cnt = _mix_out(x2, yb.reshape(TOKENS, -1), ab_w_o[0].astype(BF16), vec(ln1_g[0]),
                                vec(ln1_b[0]), wr, br, gm=(gu, vn, ab_gm_ws[0], jnp.transpose(ab_gm_bs[0])))
    x2 = moe_layer(0, x1b, x1, wf, cnt)

    w_nsa, gb = _nsa_in_weights(c_w_in[0], c_gate_b[0])
    q, kc, vc, ks, vst, kw, vwt, gates = _nsa_in(x2, w_nsa, gb)
    pk, wk1, wk2 = _compress_weights(c_cmp_pos[0, 0], c_w_ck1[0], c_w_ck2[0])
    pv, wv1, wv2 = _compress_weights(c_cmp_pos[0, 1], c_w_cv1[0], c_w_cv2[0])
    blocks = lambda a: a.reshape(BATCH, NSA_NCMP, NSA_CMP_STRIDE * LANES)
    kc2, vct = _compress(blocks(kc), blocks(vc), pk, pv, wk1, wv1, wk2, wv2)
    b3 = lambda a: a.reshape(BATCH, SEQ, -1)
    o = _nsa_attn(b3(q), kc2, vct, b3(ks), vst, b3(kw), vwt,
                  gates.reshape(NSA_GROUPS, BATCH, SEQ, LANES), _selection_tables())
    wr, br = _router_weights(moe_w_rg[1], moe_b_rg[1], moe_w_re[1], moe_b_re[1])
    x1, x1b, wf, cnt = _mix_out(x2, o.reshape(TOKENS, -1), c_w_o[0].astype(BF16), vec(ln1_g[1]), vec(ln1_b[1]),
                                wr, br)
    x2 = moe_layer(1, x1b, x1, wf, cnt)
    return x2.reshape(BATCH, SEQ, D_MODEL)
```
